```python
import jax, jax.numpy as jnp
from jax import lax
import numpy as np


D_MODEL = 1024
BATCH = 8
SEQ = 2048
DEPTH = 2

HEAD_DIM = 64
N_HEADS = D_MODEL // HEAD_DIM
N_MIXERS = 2
N_RWKV = (DEPTH + 1) // 2
N_FOX = DEPTH // 2
RWKV_DECAY_LORA = 64
RWKV_AAA_LORA = 64
RWKV_GATE_LORA = 128
GN_EPS = 64e-5
Q_BLOCK = 128
N_EXPERTS = 16
N_GROUPS = 4
EXPERTS_PER_GROUP = N_EXPERTS // N_GROUPS
TOP_K = 2
D_EXPERT = 512
EXPERT_BLOCK = 128
LN_EPS = 1e-5
ALPHA = (2 * DEPTH) ** 0.25
BETA = (8 * DEPTH) ** -0.25

kernel_name = "hybrid_rwkv7_fox_sharedrouter_moe_deepnorm"


def layer_norm(x, g, b):
    xf = x.astype(jnp.float32)
    mu = jnp.mean(xf, -1, keepdims=True)
    var = jnp.mean(jnp.square(xf - mu), -1, keepdims=True)
    return ((xf - mu) * lax.rsqrt(var + LN_EPS)).astype(x.dtype) * g + b


def rwkv7_step(S, inp):
    r_t, w_t, k_t, v_t, kk_t, a_t = inp
    sa = jnp.einsum('bhij,bhj->bhi', S, kk_t)
    S = (S * w_t[:, :, None, :]
         - sa[..., :, None] * (kk_t * a_t)[..., None, :]
         + v_t[..., :, None] * k_t[..., None, :])
    y = jnp.einsum('bhij,bhj->bhi', S, r_t)
    return S, y


def rwkv7_time_mix(x, mix, wr, wk, wv, wo, w0, w1, w2, a0, a1, a2, g1, g2,
                   k_k, k_a, r_k, gn_g, gn_b):
    B, T, D = x.shape
    H, N = N_HEADS, HEAD_DIM
    xx = jnp.pad(x, ((0, 0), (1, 0), (0, 0)))[:, :T] - x
    xr, xw, xk, xv, xa, xg = (x + xx * mix[j] for j in range(6))
    r = xr @ wr
    k = xk @ wk
    v = xv @ wv
    w = -jax.nn.softplus(-(w0 + jnp.tanh(xw @ w1) @ w2)) - 0.5
    decay = jnp.exp(-jnp.exp(w.astype(jnp.float32))).astype(x.dtype)
    a = jax.nn.sigmoid(a0 + (xa @ a1) @ a2)
    g = jax.nn.sigmoid(xg @ g1) @ g2
    kk = (k * k_k).reshape(B, T, H, N)
    kk_norm = jnp.sqrt(jnp.sum(jnp.square(kk.astype(jnp.float32)), -1, keepdims=True))
    kk = kk / jnp.maximum(kk_norm, 1e-12).astype(x.dtype)
    k = k * (1 + (a - 1) * k_a)
    heads = lambda t: t.reshape(B, T, H, N)
    seq_first = lambda t: jnp.swapaxes(t, 0, 1)
    r_h, k_h, v_h = heads(r), heads(k), heads(v)
    S0 = jnp.zeros((B, H, N, N), x.dtype)
    scan_in = tuple(seq_first(t) for t in (r_h, heads(decay), k_h, v_h, kk, heads(a)))
    _, y = lax.scan(rwkv7_step, S0, scan_in)
    y = jnp.swapaxes(y, 0, 1)
    yf = y.astype(jnp.float32)
    mu = jnp.mean(yf, -1, keepdims=True)
    var = jnp.mean(jnp.square(yf - mu), -1, keepdims=True)
    yn = ((yf - mu) * lax.rsqrt(var + GN_EPS)).astype(x.dtype).reshape(B, T, D) * gn_g + gn_b
    bonus = jnp.sum(r_h * k_h * r_k, -1, keepdims=True) * v_h
    y = yn + bonus.reshape(B, T, D)
    return (y * g) @ wo


def forgetting_attention(x, w_in, b_f, wo):
    B, T, D = x.shape
    H, N = N_HEADS, HEAD_DIM
    proj = x @ w_in
    q, k, v = proj[..., :D], proj[..., D:2 * D], proj[..., 2 * D:3 * D]
    log_f = jax.nn.log_sigmoid((proj[..., 3 * D:] + b_f).astype(jnp.float32))
    c = jnp.swapaxes(jnp.cumsum(log_f, axis=1), 1, 2)
    to_heads = lambda t: t.reshape(B, T, H, N).transpose(0, 2, 1, 3)
    q, k, v = to_heads(q), to_heads(k), to_heads(v)
    nb = T // Q_BLOCK
    qb = q.reshape(B, H, nb, Q_BLOCK, N).transpose(2, 0, 1, 3, 4)
    cb = c.reshape(B, H, nb, Q_BLOCK).transpose(2, 0, 1, 3)
    kpos = jnp.arange(T)
    scale = HEAD_DIM ** -0.5

    def attend(args):
        q_blk, c_blk, i = args
        qpos = i * Q_BLOCK + jnp.arange(Q_BLOCK)
        s = jnp.einsum('bhqd,bhkd->bhqk', q_blk, k).astype(jnp.float32) * scale
        s = s + c_blk[..., :, None] - c[:, :, None, :]
        s = jnp.where(kpos[None, :] <= qpos[:, None], s, -jnp.inf)
        p = jax.nn.softmax(s, axis=-1).astype(v.dtype)
        return jnp.einsum('bhqk,bhkd->bhqd', p, v)

    o = lax.map(attend, (qb, cb, jnp.arange(nb)))
    o = o.transpose(1, 0, 3, 2, 4).reshape(B, T, D)
    return o @ wo


def shared_router_moe(h, router_w, router_bias, w_gate, w_up, w_down):
    B, T, D = h.shape
    n_tok = B * T
    xt = h.reshape(n_tok, D)
    s = jax.nn.sigmoid((xt @ router_w).astype(jnp.float32))
    s_sel = (s + router_bias.astype(jnp.float32)).reshape(n_tok, N_GROUPS, EXPERTS_PER_GROUP)
    grp_score = jnp.sum(lax.top_k(s_sel, 2)[0], -1)
    g_star = jnp.argmax(grp_score, -1)
    in_grp = jnp.take_along_axis(s_sel, g_star[:, None, None], axis=1)[:, 0]
    _, loc = lax.top_k(in_grp, TOP_K)
    e_idx = (g_star[:, None] * EXPERTS_PER_GROUP + loc).astype(jnp.int32)
    gate = jnp.take_along_axis(s, e_idx, axis=1)
    gate = (gate / jnp.sum(gate, -1, keepdims=True)).astype(h.dtype)

    n_assign = n_tok * TOP_K
    e_flat = e_idx.reshape(n_assign)
    tok = jnp.arange(n_assign, dtype=jnp.int32) // TOP_K
    order = jnp.argsort(e_flat)
    e_sorted = e_flat[order]
    counts = jnp.zeros((N_EXPERTS,), jnp.int32).at[e_flat].add(1)
    padded = (counts + EXPERT_BLOCK - 1) // EXPERT_BLOCK * EXPERT_BLOCK
    start = jnp.cumsum(counts) - counts
    pend = jnp.cumsum(padded)
    pstart = pend - padded
    dest = pstart[e_sorted] + jnp.arange(n_assign, dtype=jnp.int32) - start[e_sorted]
    n_blocks = (n_assign + N_EXPERTS * (EXPERT_BLOCK - 1) + EXPERT_BLOCK - 1) // EXPERT_BLOCK
    n_rows = n_blocks * EXPERT_BLOCK
    row_tok = jnp.full((n_rows,), n_tok, jnp.int32).at[dest].set(tok[order])
    row_gate = jnp.zeros((n_rows,), h.dtype).at[dest].set(gate.reshape(n_assign)[order])
    block_exp = jnp.minimum(
        jnp.searchsorted(pend, jnp.arange(n_blocks, dtype=jnp.int32) * EXPERT_BLOCK, side='right'),
        N_EXPERTS - 1)
    x_pad = jnp.concatenate([xt, jnp.zeros((1, D), h.dtype)], axis=0)
    xs = x_pad[row_tok].reshape(n_blocks, EXPERT_BLOCK, D)

    def expert_block(args):
        xb, e = args
        hid = jax.nn.silu(xb @ w_gate[e]) * (xb @ w_up[e])
        return hid @ w_down[e]

    ys = lax.map(expert_block, (xs, block_exp)).reshape(n_rows, D)
    out = jax.ops.segment_sum(ys * row_gate[:, None], row_tok, num_segments=n_tok + 1)[:n_tok]
    return out.reshape(B, T, D)


def setup_inputs(seed: int = 0) -> dict:
    key = jax.random.key(seed)
    ks = jax.random.split(key, 32)
    D, H, N, L_A, L_B = D_MODEL, N_HEADS, HEAD_DIM, N_RWKV, N_FOX
    nrm = lambda i, shape, sc: jax.random.normal(ks[i], shape, jnp.float32) * sc
    unif = lambda i, shape: jax.random.uniform(ks[i], shape, jnp.float32)
    inv = D ** -0.5
    v_cols = jnp.concatenate([jnp.ones((2 * D,)), jnp.full((D,), BETA), jnp.ones((H,))]).astype(jnp.float32)
    return {
        "x": nrm(0, (BATCH, SEQ, D), 1.0),
        "rw_mix": unif(1, (L_A, 6, D)),
        "rw_wr": nrm(2, (L_A, D, D), inv),
        "rw_wk": nrm(3, (L_A, D, D), inv),
        "rw_wv": nrm(4, (L_A, D, D), inv * BETA),
        "rw_wo": nrm(5, (L_A, D, D), inv * BETA),
        "rw_w0": -6.0 + 5.0 * unif(6, (L_A, D)),
        "rw_w1": nrm(7, (L_A, D, RWKV_DECAY_LORA), inv),
        "rw_w2": nrm(8, (L_A, RWKV_DECAY_LORA, D), 0.1 * RWKV_DECAY_LORA ** -0.5),
        "rw_a0": nrm(9, (L_A, D), 0.1),
        "rw_a1": nrm(10, (L_A, D, RWKV_AAA_LORA), inv),
        "rw_a2": nrm(11, (L_A, RWKV_AAA_LORA, D), 0.1 * RWKV_AAA_LORA ** -0.5),
        "rw_g1": nrm(12, (L_A, D, RWKV_GATE_LORA), inv),
        "rw_g2": nrm(13, (L_A, RWKV_GATE_LORA, D), RWKV_GATE_LORA ** -0.5),
        "rw_kk": 0.85 + nrm(14, (L_A, D), 0.05),
        "rw_ka": 1.0 + nrm(15, (L_A, D), 0.05),
        "rw_rk": nrm(16, (L_A, H, N), 0.1),
        "rw_gn_g": 1.0 + nrm(17, (L_A, D), 0.02),
        "rw_gn_b": nrm(18, (L_A, D), 0.02),
        "fx_w_in": nrm(19, (L_B, D, 3 * D + H), inv) * v_cols,
        "fx_b_f": 3.0 + nrm(20, (L_B, H), 0.5),
        "fx_wo": nrm(21, (L_B, D, D), inv * BETA),
        "router_w": nrm(22, (D, N_EXPERTS), inv),
        "router_bias": nrm(23, (N_EXPERTS,), 0.01),
        "moe_w_gate": nrm(24, (DEPTH, N_EXPERTS, D, D_EXPERT), inv * BETA),
        "moe_w_up": nrm(25, (DEPTH, N_EXPERTS, D, D_EXPERT), inv * BETA),
        "moe_w_down": nrm(26, (DEPTH, N_EXPERTS, D_EXPERT, D), D_EXPERT ** -0.5 * BETA),
        "ln_g": 1.0 + nrm(27, (DEPTH, 2, D), 0.02),
        "ln_b": nrm(28, (DEPTH, 2, D), 0.02),
    }


def reference(x, rw_mix, rw_wr, rw_wk, rw_wv, rw_wo, rw_w0, rw_w1, rw_w2, rw_a0, rw_a1,
              rw_a2, rw_g1, rw_g2, rw_kk, rw_ka, rw_rk, rw_gn_g, rw_gn_b,
              fx_w_in, fx_b_f, fx_wo, router_w, router_bias,
              moe_w_gate, moe_w_up, moe_w_down, ln_g, ln_b):
    h = x
    for i in range(DEPTH):
        j = i // N_MIXERS
        if i % N_MIXERS == 0:
            mixed = rwkv7_time_mix(h, rw_mix[j], rw_wr[j], rw_wk[j], rw_wv[j], rw_wo[j],
                                   rw_w0[j], rw_w1[j], rw_w2[j], rw_a0[j], rw_a1[j], rw_a2[j],
                                   rw_g1[j], rw_g2[j], rw_kk[j], rw_ka[j], rw_rk[j],
                                   rw_gn_g[j], rw_gn_b[j])
        else:
            mixed = forgetting_attention(h, fx_w_in[j], fx_b_f[j], fx_wo[j])
        h = layer_norm(ALPHA * h + mixed, ln_g[i, 0], ln_b[i, 0])
        ffn = shared_router_moe(h, router_w, router_bias, moe_w_gate[i], moe_w_up[i], moe_w_down[i])
        h = layer_norm(ALPHA * h + ffn, ln_g[i, 1], ln_b[i, 1])
    return h
```

```python
import functools
import math

import jax
import jax.numpy as jnp
from jax import lax
from jax.experimental import pallas as pl
from jax.experimental.pallas import tpu as pltpu

D_MODEL = 1024
HEAD_DIM = 64
N_HEADS = D_MODEL // HEAD_DIM
N_EXPERTS = 16
N_GROUPS = 4
EXPERTS_PER_GROUP = N_EXPERTS // N_GROUPS
D_EXPERT = 512
GN_EPS = 64e-5
LN_EPS = 1e-5
DEPTH = 2
ALPHA = (2 * DEPTH) ** 0.25

V7X_LANES = 128
V7X_VMEM_BYTES = 64 * 2 ** 20

RWKV_CHUNK = 64
PROJ_TILE = 256
ATTN_TILE = 256
MOE_BLOCK = 256
MOE_TOK_TILE = 256

F32 = jnp.float32
BF16 = jnp.bfloat16


def _vmem_limit(n_bytes):
    return int(min(n_bytes + 16 * 2 ** 20, V7X_VMEM_BYTES - 8 * 2 ** 20))


def _bdot(a, b):
    return jnp.dot(a.astype(BF16), b.astype(BF16), preferred_element_type=F32)


def _bdot_nt(a, b):
    return lax.dot_general(a.astype(BF16), b.astype(BF16), (((1,), (1,)), ((), ())),
                           preferred_element_type=F32)


def _bdot_tn(a, b):
    return lax.dot_general(a.astype(BF16), b.astype(BF16), (((0,), (0,)), ((), ())),
                           preferred_element_type=F32)


def _split3(x):
    hi = x.astype(BF16)
    r1 = x - hi.astype(F32)
    mid = r1.astype(BF16)
    lo = (r1 - mid.astype(F32)).astype(BF16)
    return hi, mid, lo


def _layer_norm_rows(x, g, b):
    mu = jnp.mean(x, axis=-1, keepdims=True)
    xc = x - mu
    var = jnp.mean(xc * xc, axis=-1, keepdims=True)
    return xc * lax.rsqrt(var + LN_EPS) * g + b


def _rwkv_proj_kernel(x_ref, xp_ref, mix_ref, wr_ref, wk_ref, wv_ref, w1_ref, w2_ref, a1_ref, a2_ref,
                      g1_ref, g2_ref, w0_ref, a0_ref,
                      r_ref, lw_ref, k_ref, v_ref, a_ref, g_ref, *, tiles_per_seq):
    i = pl.program_id(0)
    x = x_ref[...]
    tt = x.shape[0]
    first = (i % tiles_per_seq) == 0
    prev_row = jnp.where(first, 0.0, xp_ref[7:8, :])
    row = lax.broadcasted_iota(jnp.int32, (tt, 1), 0)
    xprev = jnp.where(row == 0, prev_row, pltpu.roll(x, 1, axis=0))
    xx = xprev - x
    mix = mix_ref[...]
    xr, xw, xk, xv, xa, xg = (x + xx * mix[j:j + 1, :] for j in range(6))
    r_ref[...] = _bdot(xr, wr_ref[...])
    k_ref[...] = _bdot(xk, wk_ref[...])
    v_ref[...] = _bdot(xv, wv_ref[...])
    z = w0_ref[...] + _bdot(jnp.tanh(_bdot(xw, w1_ref[...])), w2_ref[...])
    w = -jax.nn.softplus(-z) - 0.5
    lw_ref[...] = -jnp.exp(w)
    a_ref[...] = jax.nn.sigmoid(a0_ref[...] + _bdot(_bdot(xa, a1_ref[...]), a2_ref[...]))
    g_ref[...] = _bdot(jax.nn.sigmoid(_bdot(xg, g1_ref[...])), g2_ref[...])


def _rwkv_proj(x2, seq_len, mix, wr, wk, wv, w1, w2, a1, a2, g1, g2, w0, a0):
    n_tok, d = x2.shape
    tt = PROJ_TILE
    n_tiles = n_tok // tt
    tiles_per_seq = seq_len // tt
    tile = pl.BlockSpec((tt, d), lambda i: (i, 0))
    prev = pl.BlockSpec((8, d), lambda i: (jnp.maximum(i * (tt // 8) - 1, 0), 0))
    full = lambda arr: pl.BlockSpec(arr.shape, lambda i: (0,) * arr.ndim)
    weights = (mix, wr, wk, wv, w1, w2, a1, a2, g1, g2, w0, a0)
    out_sds = jax.ShapeDtypeStruct((n_tok, d), F32)
    w_bytes = sum(int(w.size) * w.dtype.itemsize for w in weights)
    return pl.pallas_call(
        functools.partial(_rwkv_proj_kernel, tiles_per_seq=tiles_per_seq),
        grid=(n_tiles,),
        in_specs=[tile, prev] + [full(w) for w in weights],
        out_specs=[tile] * 6,
        out_shape=[out_sds] * 6,
        compiler_params=pltpu.CompilerParams(
            dimension_semantics=("parallel",),
            vmem_limit_bytes=_vmem_limit(2 * w_bytes + 2 * 8 * tt * d * 4)),
        name="rwkv_proj",
    )(x2, x2, *weights)


def _rwkv_recur_kernel(r_ref, lw_ref, k_ref, v_ref, a_ref, g_ref, kk_ref, ka_ref, rk_ref, gng_ref, gnb_ref,
                       o_ref, s_ref):
    c = pl.program_id(1)
    C = r_ref.shape[0]
    N = HEAD_DIM

    @pl.when(c == 0)
    def _():
        s_ref[...] = jnp.zeros_like(s_ref)

    lw = lw_ref[...]
    ti = lax.broadcasted_iota(jnp.int32, (C, C), 0)
    tj = lax.broadcasted_iota(jnp.int32, (C, C), 1)
    tril = (tj <= ti).astype(BF16)
    cum = sum(jnp.dot(tril, part, preferred_element_type=F32) for part in _split3(lw))
    rho = cum[C // 2 - 1:C // 2, :]
    last = cum[C - 1:C, :]
    e_q = jnp.exp(cum - rho)
    e_qx = jnp.exp(cum - lw - rho)
    e_k = jnp.exp(rho - cum)
    e_end = jnp.exp(last - cum)
    e_rho = jnp.exp(rho)
    d_end = jnp.exp(last)

    r = r_ref[...]
    k = k_ref[...]
    v = v_ref[...]
    a = a_ref[...]
    kk_raw = k * kk_ref[...]
    k_mod = k * (1.0 + (a - 1.0) * ka_ref[...])
    rkk = r * k_mod * rk_ref[...]

    strict = tj < ti
    incl = tj <= ti
    eye = (ti == tj).astype(F32)

    for h in range(N_HEADS):
        sl = slice(h * N, (h + 1) * N)
        kk_h = kk_raw[:, sl]
        kk_h = kk_h * lax.rsqrt(jnp.maximum(jnp.sum(kk_h * kk_h, axis=-1, keepdims=True), 1e-24))
        a_h = a[:, sl]
        b_h = kk_h * a_h
        km_h = k_mod[:, sl]
        r_h = r[:, sl]
        v_h = v[:, sl]
        kk_q = kk_h * e_qx[:, sl]
        r_q = r_h * e_q[:, sl]
        k_k = km_h * e_k[:, sl]
        b_k = b_h * e_k[:, sl]
        q2 = jnp.concatenate([kk_q, r_q], axis=0)
        a_k = _bdot_nt(q2, k_k)
        a_b = _bdot_nt(q2, b_k)
        a_kk = jnp.where(strict, a_k[:C], 0.0)
        a_rk = jnp.where(incl, a_k[C:], 0.0)
        a_kb = jnp.where(strict, a_b[:C], 0.0)
        a_rb = jnp.where(incl, a_b[C:], 0.0)
        m = -a_kb
        t_inv = eye + m
        p = m
        for _ in range(int(math.log2(C)) - 1):
            p = _bdot(p, p)
            t_inv = t_inv + _bdot(t_inv, p)
        s0 = s_ref[h]
        q2_abs = q2 * e_rho[:, sl]
        qs = _bdot_nt(q2_abs, s0)
        sa = _bdot(t_inv, qs[:C] + _bdot(a_kk, v_h))
        y = qs[C:] + _bdot(a_rk, v_h) - _bdot(a_rb, sa)
        k_d = km_h * e_end[:, sl]
        b_d = b_h * e_end[:, sl]
        upd = _bdot_tn(jnp.concatenate([v_h, sa], axis=0), jnp.concatenate([k_d, -b_d], axis=0))
        s_ref[h] = s0 * d_end[:, sl] + upd
        mu = jnp.mean(y, axis=-1, keepdims=True)
        yc = y - mu
        var = jnp.mean(yc * yc, axis=-1, keepdims=True)
        yn = yc * lax.rsqrt(var + GN_EPS) * gng_ref[:, sl] + gnb_ref[:, sl]
        bonus = jnp.sum(rkk[:, sl], axis=-1, keepdims=True) * v_h
        o_ref[:, sl] = ((yn + bonus) * g_ref[:, sl]).astype(o_ref.dtype)


def _rwkv_recur(r, lw, k, v, a, g, batch, seq_len, k_k, k_a, r_k, gn_g, gn_b):
    n_tok, d = r.shape
    C = RWKV_CHUNK
    nc = seq_len // C
    tile = pl.BlockSpec((C, d), lambda b, c: (b * nc + c, 0))
    vec = pl.BlockSpec((1, d), lambda b, c: (0, 0))
    return pl.pallas_call(
        _rwkv_recur_kernel,
        grid=(batch, nc),
        in_specs=[tile] * 6 + [vec] * 5,
        out_specs=tile,
        out_shape=jax.ShapeDtypeStruct((n_tok, d), BF16),
        scratch_shapes=[pltpu.VMEM((N_HEADS, HEAD_DIM, HEAD_DIM), F32)],
        compiler_params=pltpu.CompilerParams(
            dimension_semantics=("parallel", "arbitrary"),
            vmem_limit_bytes=_vmem_limit(2 * 7 * C * d * 4 + 32 * C * d * 4)),
        name="rwkv_recur",
    )(r, lw, k, v, a, g, k_k, k_a, r_k, gn_g, gn_b)


def _rank_among(vals, i):
    cnt = 0
    for j, vj in enumerate(vals):
        if j == i:
            continue
        before = (vj >= vals[i]) if j < i else (vj > vals[i])
        cnt = cnt + before.astype(jnp.int32)
    return cnt


def _pick(ranks, vals, want):
    out = vals[0]
    for rk, vl in zip(ranks[1:], vals[1:]):
        out = jnp.where(rk == want, vl, out)
    return out


def _epilogue_kernel(act_ref, wo_ref, res_ref, lng_ref, lnb_ref, rwt_ref, rb_ref,
                     h_ref, eidx_ref, rank_ref, gcol_ref, cnt_ref, base_ref):
    i = pl.program_id(0)

    @pl.when(i == 0)
    def _():
        base_ref[...] = jnp.zeros_like(base_ref)

    mixed = jnp.dot(act_ref[...], wo_ref[...], preferred_element_type=F32)
    h = _layer_norm_rows(ALPHA * res_ref[...] + mixed, lng_ref[...], lnb_ref[...])
    h_ref[...] = h
    tt = h.shape[0]

    hh, hm, hl = _split3(h)
    wh, wm, wl = _split3(rwt_ref[...])
    nt = lambda a, b: lax.dot_general(a, b, (((1,), (1,)), ((), ())), preferred_element_type=F32)
    logits = (nt(wh, hh) + (nt(wh, hm) + nt(wm, hh)) + (nt(wh, hl) + nt(wm, hm) + nt(wl, hh)))
    s = jax.nn.sigmoid(logits)
    s_sel = s + rb_ref[...]
    rows = [s_sel[e:e + 1, :] for e in range(N_EXPERTS)]
    grp_score, grp_i0, grp_i1 = [], [], []
    for gi in range(N_GROUPS):
        vals = rows[gi * EXPERTS_PER_GROUP:(gi + 1) * EXPERTS_PER_GROUP]
        ranks = [_rank_among(vals, q) for q in range(EXPERTS_PER_GROUP)]
        idx = [jnp.full_like(ranks[0], q) for q in range(EXPERTS_PER_GROUP)]
        grp_score.append(_pick(ranks, vals, 0) + _pick(ranks, vals, 1))
        grp_i0.append(_pick(ranks, idx, 0))
        grp_i1.append(_pick(ranks, idx, 1))
    g_ranks = [_rank_among(grp_score, q) for q in range(N_GROUPS)]
    gidx = [jnp.full_like(g_ranks[0], q) for q in range(N_GROUPS)]
    g_star = _pick(g_ranks, gidx, 0)
    e0 = g_star * EXPERTS_PER_GROUP + _pick(g_ranks, grp_i0, 0)
    e1 = g_star * EXPERTS_PER_GROUP + _pick(g_ranks, grp_i1, 0)

    e_iota = lax.broadcasted_iota(jnp.int32, (N_EXPERTS, tt), 0)
    hit0 = e_iota == e0
    hit1 = e_iota == e1
    gate0 = jnp.sum(jnp.where(hit0, s, 0.0), axis=0, keepdims=True)
    gate1 = jnp.sum(jnp.where(hit1, s, 0.0), axis=0, keepdims=True)
    denom = gate0 + gate1
    gate0 = gate0 / denom
    gate1 = gate1 / denom

    member = jnp.where(hit0 | hit1, 1.0, 0.0)
    ui = lax.broadcasted_iota(jnp.int32, (tt, tt), 0)
    uj = lax.broadcasted_iota(jnp.int32, (tt, tt), 1)
    before = (ui < uj).astype(BF16)
    base = base_ref[:, 0:1]
    prefix = jnp.dot(member.astype(BF16), before, preferred_element_type=F32) + base
    rank0 = jnp.sum(jnp.where(hit0, prefix, 0.0), axis=0, keepdims=True)
    rank1 = jnp.sum(jnp.where(hit1, prefix, 0.0), axis=0, keepdims=True)
    new_base = base + jnp.sum(member, axis=1, keepdims=True)
    base_ref[...] = jnp.broadcast_to(new_base, base_ref.shape)
    cnt_ref[...] = jnp.broadcast_to(new_base, cnt_ref.shape).astype(jnp.int32)

    eidx_ref[...] = jnp.concatenate([e0, e1], axis=0)
    rank_ref[...] = jnp.concatenate([rank0, rank1], axis=0).astype(jnp.int32)

    gpad = jnp.concatenate([gate0, gate1, jnp.zeros((V7X_LANES - 2, tt), F32)], axis=0)
    ident = (ui == uj).astype(BF16)
    gcol_ref[...] = sum(nt(ident, part) for part in _split3(gpad))


def _mixer_epilogue(act, wo, res, ln_g, ln_b, router_wt, router_bias):
    n_tok, d = res.shape
    k_in = act.shape[1]
    tt = PROJ_TILE
    n_tiles = n_tok // tt
    full = lambda arr: pl.BlockSpec(arr.shape, lambda i: (0,) * arr.ndim)
    return pl.pallas_call(
        _epilogue_kernel,
        grid=(n_tiles,),
        in_specs=[pl.BlockSpec((tt, k_in), lambda i: (i, 0)), full(wo), pl.BlockSpec((tt, d), lambda i: (i, 0)),
                  full(ln_g), full(ln_b), full(router_wt), full(router_bias)],
        out_specs=[pl.BlockSpec((tt, d), lambda i: (i, 0)),
                   pl.BlockSpec((2, tt), lambda i: (0, i)),
                   pl.BlockSpec((2, tt), lambda i: (0, i)),
                   pl.BlockSpec((tt, V7X_LANES), lambda i: (i, 0)),
                   pl.BlockSpec((N_EXPERTS, V7X_LANES), lambda i: (0, 0))],
        out_shape=[jax.ShapeDtypeStruct((n_tok, d), F32),
                   jax.ShapeDtypeStruct((2, n_tok), jnp.int32),
                   jax.ShapeDtypeStruct((2, n_tok), jnp.int32),
                   jax.ShapeDtypeStruct((n_tok, V7X_LANES), F32),
                   jax.ShapeDtypeStruct((N_EXPERTS, V7X_LANES), jnp.int32)],
        scratch_shapes=[pltpu.VMEM((N_EXPERTS, V7X_LANES), F32)],
        compiler_params=pltpu.CompilerParams(
            dimension_semantics=("arbitrary",),
            vmem_limit_bytes=_vmem_limit(2 * int(wo.size) * 2 + 2 * tt * (k_in * 2 + 2 * d * 4) + 16 * tt * d * 4)),
        name="mixer_epilogue",
    )(act, wo, res, ln_g, ln_b, router_wt, router_bias)


def _dispatch_kernel(dest_ref, h_ref, xs_in_ref, xs_ref, sem):
    del xs_in_ref
    tt = h_ref.shape[0]

    def row_copy(t, slot):
        d = dest_ref[0, 0, slot * tt + t]
        return pltpu.make_async_copy(h_ref.at[pl.ds(t, 1)], xs_ref.at[pl.ds(d, 1)], sem)

    def issue(t, carry):
        row_copy(t, 0).start()
        row_copy(t, 1).start()
        return carry

    lax.fori_loop(0, tt, issue, 0)

    def drain(t, carry):
        row_copy(t, 0).wait()
        row_copy(t, 1).wait()
        return carry

    lax.fori_loop(0, tt, drain, 0)


def _moe_dispatch(h, dest_tiles, n_rows):
    n_tok, d = h.shape
    tt = MOE_TOK_TILE
    n_tiles = n_tok // tt
    zeros = jnp.zeros((n_rows, d), h.dtype)
    return pl.pallas_call(
        _dispatch_kernel,
        grid=(n_tiles,),
        in_specs=[pl.BlockSpec((1, 1, 2 * tt), lambda i: (i, 0, 0), memory_space=pltpu.SMEM),
                  pl.BlockSpec((tt, d), lambda i: (i, 0)),
                  pl.BlockSpec(memory_space=pl.ANY)],
        out_specs=pl.BlockSpec(memory_space=pl.ANY),
        out_shape=jax.ShapeDtypeStruct((n_rows, d), h.dtype),
        scratch_shapes=[pltpu.SemaphoreType.DMA(())],
        input_output_aliases={2: 0},
        compiler_params=pltpu.CompilerParams(dimension_semantics=("arbitrary",)),
        name="moe_dispatch",
    )(dest_tiles, h, zeros)


def _ffn_kernel(bexp_ref, nblk_ref, x_ref, wg_ref, wu_ref, wd_ref, y_ref):
    del bexp_ref
    j = pl.program_id(0)

    @pl.when(j < nblk_ref[0])
    def _():
        x = x_ref[...].astype(BF16)
        gate = jnp.dot(x, wg_ref[0], preferred_element_type=F32)
        up = jnp.dot(x, wu_ref[0], preferred_element_type=F32)
        hid = (gate * jax.nn.sigmoid(gate)) * up
        y_ref[...] = jnp.dot(hid.astype(BF16), wd_ref[0], preferred_element_type=F32)

    @pl.when(j >= nblk_ref[0])
    def _():
        y_ref[...] = jnp.zeros_like(y_ref)


def _moe_ffn(xs, block_exp, n_used, wg, wu, wd):
    n_rows, d = xs.shape
    blk = MOE_BLOCK
    n_blocks = n_rows // blk
    de = wg.shape[2]
    grid_spec = pltpu.PrefetchScalarGridSpec(
        num_scalar_prefetch=2,
        grid=(n_blocks,),
        in_specs=[pl.BlockSpec((blk, d), lambda j, be, nb: (j, 0)),
                  pl.BlockSpec((1, d, de), lambda j, be, nb: (be[j], 0, 0)),
                  pl.BlockSpec((1, d, de), lambda j, be, nb: (be[j], 0, 0)),
                  pl.BlockSpec((1, de, d), lambda j, be, nb: (be[j], 0, 0))],
        out_specs=pl.BlockSpec((blk, d), lambda j, be, nb: (j, 0)),
    )
    return pl.pallas_call(
        _ffn_kernel,
        grid_spec=grid_spec,
        out_shape=jax.ShapeDtypeStruct((n_rows, d), F32),
        compiler_params=pltpu.CompilerParams(
            dimension_semantics=("arbitrary",),
            vmem_limit_bytes=_vmem_limit(2 * 3 * d * de * 2 + 4 * blk * d * 4 + 4 * blk * de * 4)),
        name="moe_ffn",
    )(block_exp, n_used, xs, wg, wu, wd)


def _combine_kernel(dest_ref, ys_ref, gcol_ref, res_ref, lng_ref, lnb_ref, o_ref, buf_ref, sem):
    tt = res_ref.shape[0]

    def row_copy(t, slot):
        d = dest_ref[0, 0, slot * tt + t]
        return pltpu.make_async_copy(ys_ref.at[pl.ds(d, 1)], buf_ref.at[slot, pl.ds(t, 1)], sem)

    def issue(t, carry):
        row_copy(t, 0).start()
        row_copy(t, 1).start()
        return carry

    lax.fori_loop(0, tt, issue, 0)

    def drain(t, carry):
        row_copy(t, 0).wait()
        row_copy(t, 1).wait()
        return carry

    lax.fori_loop(0, tt, drain, 0)

    gcol = gcol_ref[...]
    ffn = buf_ref[0] * gcol[:, 0:1] + buf_ref[1] * gcol[:, 1:2]
    o_ref[...] = _layer_norm_rows(ALPHA * res_ref[...] + ffn, lng_ref[...], lnb_ref[...])


def _moe_combine(ys, dest_tiles, gcol, res, ln_g, ln_b):
    n_tok, d = res.shape
    tt = MOE_TOK_TILE
    n_tiles = n_tok // tt
    full = lambda arr: pl.BlockSpec(arr.shape, lambda i: (0,) * arr.ndim)
    return pl.pallas_call(
        _combine_kernel,
        grid=(n_tiles,),
        in_specs=[pl.BlockSpec((1, 1, 2 * tt), lambda i: (i, 0, 0), memory_space=pltpu.SMEM),
                  pl.BlockSpec(memory_space=pl.ANY),
                  pl.BlockSpec((tt, V7X_LANES), lambda i: (i, 0)),
                  pl.BlockSpec((tt, d), lambda i: (i, 0)),
                  full(ln_g), full(ln_b)],
        out_specs=pl.BlockSpec((tt, d), lambda i: (i, 0)),
        out_shape=jax.ShapeDtypeStruct((n_tok, d), F32),
        scratch_shapes=[pltpu.VMEM((2, tt, d), F32), pltpu.SemaphoreType.DMA(())],
        compiler_params=pltpu.CompilerParams(
            dimension_semantics=("arbitrary",),
            vmem_limit_bytes=_vmem_limit(2 * tt * d * 4 + 2 * 3 * tt * d * 4)),
        name="moe_combine",
    )(dest_tiles, ys, gcol, res, ln_g, ln_b)


def _moe_layer(h, eidx, rank, counts, gcol, wg, wu, wd, ln_g, ln_b):
    n_tok, d = h.shape
    blk = MOE_BLOCK
    n_rows = (2 * n_tok + N_EXPERTS * (blk - 1) + blk - 1) // blk * blk
    n_blocks = n_rows // blk
    counts = counts[:, 0]
    padded = (counts + blk - 1) // blk * blk
    pend = jnp.cumsum(padded)
    pstart = pend - padded
    onehot = (eidx[:, :, None] == jnp.arange(N_EXPERTS, dtype=jnp.int32)).astype(jnp.int32)
    dest = rank + jnp.sum(onehot * pstart, axis=-1)
    tt = MOE_TOK_TILE
    dest_tiles = dest.reshape(2, n_tok // tt, tt).transpose(1, 0, 2).reshape(n_tok // tt, 1, 2 * tt)
    blk_start = jnp.arange(n_blocks, dtype=jnp.int32) * blk
    block_exp = jnp.minimum(jnp.sum((blk_start[:, None] >= pend[None, :]).astype(jnp.int32), axis=1),
                            N_EXPERTS - 1).astype(jnp.int32)
    n_used = (pend[-1:] // blk).astype(jnp.int32)
    xs = _moe_dispatch(h, dest_tiles, n_rows)
    ys = _moe_ffn(xs, block_exp, n_used, wg, wu, wd)
    return _moe_combine(ys, dest_tiles, gcol, h, ln_g, ln_b)


def _fox_proj_kernel(x_ref, wq_ref, wk_ref, wv_ref, wf_ref, bf_ref, q_ref, k_ref, v_ref, c_ref, carry_ref):
    t = pl.program_id(1)

    @pl.when(t == 0)
    def _():
        carry_ref[...] = jnp.zeros_like(carry_ref)

    x = x_ref[...]
    xb = x.astype(BF16)
    q_ref[...] = jnp.dot(xb, wq_ref[...], preferred_element_type=F32).astype(q_ref.dtype)
    k_ref[...] = jnp.dot(xb, wk_ref[...], preferred_element_type=F32).astype(k_ref.dtype)
    v_ref[...] = jnp.dot(xb, wv_ref[...], preferred_element_type=F32).astype(v_ref.dtype)
    xh, xm, xl = _split3(x)
    wh, wm, wl = _split3(wf_ref[...])
    mm = lambda a, b: jnp.dot(a, b, preferred_element_type=F32)
    logit = (mm(xh, wh) + (mm(xh, wm) + mm(xm, wh)) + (mm(xh, wl) + mm(xm, wm) + mm(xl, wh))) + bf_ref[...]
    log_f = jax.nn.log_sigmoid(logit)
    tt = x.shape[0]
    ti = lax.broadcasted_iota(jnp.int32, (tt, tt), 0)
    tj = lax.broadcasted_iota(jnp.int32, (tt, tt), 1)
    tril = (tj <= ti).astype(BF16)
    c = sum(mm(tril, part) for part in _split3(log_f)) + carry_ref[0:1, :]
    c_ref[...] = c
    carry_ref[...] = jnp.broadcast_to(c[tt - 1:tt, :], carry_ref.shape)


def _fox_proj(x2, batch, seq_len, wq, wk, wv, wf, b_f):
    n_tok, d = x2.shape
    tt = PROJ_TILE
    nt = seq_len // tt
    tile = pl.BlockSpec((tt, d), lambda b, t: (b * nt + t, 0))
    full = lambda arr: pl.BlockSpec(arr.shape, lambda b, t: (0,) * arr.ndim)
    return pl.pallas_call(
        _fox_proj_kernel,
        grid=(batch, nt),
        in_specs=[tile, full(wq), full(wk), full(wv), full(wf), full(b_f)],
        out_specs=[tile, tile, tile, pl.BlockSpec((tt, N_HEADS), lambda b, t: (b * nt + t, 0))],
        out_shape=[jax.ShapeDtypeStruct((n_tok, d), BF16)] * 3 + [jax.ShapeDtypeStruct((n_tok, N_HEADS), F32)],
        scratch_shapes=[pltpu.VMEM((8, N_HEADS), F32)],
        compiler_params=pltpu.CompilerParams(
            dimension_semantics=("parallel", "arbitrary"),
            vmem_limit_bytes=_vmem_limit(2 * 3 * d * d * 2 + 2 * tt * d * (4 + 3 * 2) + 8 * tt * d * 4)),
        name="fox_proj",
    )(x2, wq, wk, wv, wf, b_f)


def _fox_attn_kernel(q_ref, k_ref, v_ref, cq_ref, ck_ref, o_ref):
    hp = pl.program_id(1)
    qi = pl.program_id(2)
    tq = q_ref.shape[0]
    tk = tq
    N = HEAD_DIM
    q = q_ref[...]
    lane = lax.broadcasted_iota(jnp.int32, (tq, 2 * N), 1)
    first = lane < N
    zero = jnp.zeros_like(q)
    q_heads = (jnp.where(first, q, zero), jnp.where(first, zero, q))
    cq_all = cq_ref[...]
    head_lane = lax.broadcasted_iota(jnp.int32, cq_all.shape, 1)
    cq = [jnp.sum(jnp.where(head_lane == 2 * hp + u, cq_all, 0.0), axis=-1, keepdims=True) for u in range(2)]
    row = lax.broadcasted_iota(jnp.int32, (tq, tk), 0)
    col = lax.broadcasted_iota(jnp.int32, (tq, tk), 1)

    def chunk(j, carry):
        start = pl.multiple_of(j * tk, tk)
        k_j = k_ref[pl.ds(start, tk), :]
        v_j = v_ref[pl.ds(start, tk), :]
        allowed = (col + j * tk) <= (row + qi * tq)
        out = []
        for u in range(2):
            m, l, acc = carry[u]
            s = lax.dot_general(q_heads[u], k_j, (((1,), (1,)), ((), ())), preferred_element_type=F32)
            s = s + cq[u] - ck_ref[0, pl.ds(2 * hp + u, 1), pl.ds(start, tk)]
            s = jnp.where(allowed, s, -jnp.inf)
            m_new = jnp.maximum(m, jnp.max(s, axis=-1, keepdims=True))
            p = jnp.exp(s - m_new)
            alpha = jnp.exp(m - m_new)
            l = alpha * l + jnp.sum(p, axis=-1, keepdims=True)
            acc = alpha * acc + jnp.dot(p.astype(BF16), v_j, preferred_element_type=F32)
            out.append((m_new, l, acc))
        return tuple(out)

    init = tuple((jnp.full((tq, 1), -jnp.inf, F32), jnp.zeros((tq, 1), F32), jnp.zeros((tq, 2 * N), F32))
                 for _ in range(2))
    (m0, l0, acc0), (m1, l1, acc1) = lax.fori_loop(0, qi + 1, chunk, init)
    o = jnp.where(first, acc0 / l0, acc1 / l1)
    o_ref[...] = o.astype(o_ref.dtype)


def _fox_attn(q, k, v, c, c_t, batch, seq_len):
    n_tok, d = q.shape
    tq = ATTN_TILE
    nq = seq_len // tq
    pair = 2 * HEAD_DIM
    return pl.pallas_call(
        _fox_attn_kernel,
        grid=(batch, d // pair, nq),
        in_specs=[pl.BlockSpec((tq, pair), lambda b, hp, qi: (b * nq + qi, hp)),
                  pl.BlockSpec((seq_len, pair), lambda b, hp, qi: (b, hp)),
                  pl.BlockSpec((seq_len, pair), lambda b, hp, qi: (b, hp)),
                  pl.BlockSpec((tq, N_HEADS), lambda b, hp, qi: (b * nq + qi, 0)),
                  pl.BlockSpec((1, N_HEADS, seq_len), lambda b, hp, qi: (b, 0, 0))],
        out_specs=pl.BlockSpec((tq, pair), lambda b, hp, qi: (b * nq + qi, hp)),
        out_shape=jax.ShapeDtypeStruct((n_tok, d), BF16),
        compiler_params=pltpu.CompilerParams(
            dimension_semantics=("parallel", "parallel", "arbitrary"),
            vmem_limit_bytes=_vmem_limit(4 * seq_len * pair * 2 + 16 * tq * tq * 4)),
        name="fox_attn",
    )(q, k, v, c, c_t)


def kernel(x, rw_mix, rw_wr, rw_wk, rw_wv, rw_wo, rw_w0, rw_w1, rw_w2, rw_a0, rw_a1, rw_a2, rw_g1, rw_g2,
           rw_kk, rw_ka, rw_rk, rw_gn_g, rw_gn_b, fx_w_in, fx_b_f, fx_wo, router_w, router_bias,
           moe_w_gate, moe_w_up, moe_w_down, ln_g, ln_b):
    batch, seq_len, d = x.shape
    n_tok = batch * seq_len
    bf = lambda w: w.astype(BF16)
    row = lambda w: w.reshape(1, -1)
    router_wt = router_w.T
    router_b = router_bias.reshape(N_EXPERTS, 1)
    h = x.reshape(n_tok, d)

    for i in range(DEPTH):
        j = i // 2
        if i % 2 == 0:
            r, lw, k, v, a, g = _rwkv_proj(
                h, seq_len, rw_mix[j], bf(rw_wr[j]), bf(rw_wk[j]), bf(rw_wv[j]), bf(rw_w1[j]), bf(rw_w2[j]),
                bf(rw_a1[j]), bf(rw_a2[j]), bf(rw_g1[j]), bf(rw_g2[j]), row(rw_w0[j]), row(rw_a0[j]))
            act = _rwkv_recur(r, lw, k, v, a, g, batch, seq_len, row(rw_kk[j]), row(rw_ka[j]), row(rw_rk[j]),
                              row(rw_gn_g[j]), row(rw_gn_b[j]))
            wo = bf(rw_wo[j])
        else:
            w_in = fx_w_in[j]
            scale = HEAD_DIM ** -0.5
            q, k, v, c = _fox_proj(h, batch, seq_len, bf(w_in[:, :d] * scale), bf(w_in[:, d:2 * d]),
                                   bf(w_in[:, 2 * d:3 * d]), w_in[:, 3 * d:], row(fx_b_f[j]))
            c_t = c.reshape(batch, seq_len, N_HEADS).transpose(0, 2, 1)
            act = _fox_attn(q, k, v, c, c_t, batch, seq_len)
            wo = bf(fx_wo[j])
        h, eidx, rank, gcol, counts = _mixer_epilogue(act, wo, h, row(ln_g[i, 0]), row(ln_b[i, 0]),
                                                      router_wt, router_b)
        h = _moe_layer(h, eidx, rank, counts, gcol, bf(moe_w_gate[i]), bf(moe_w_up[i]), bf(moe_w_down[i]),
                       row(ln_g[i, 1]), row(ln_b[i, 1]))
    return h.reshape(batch, seq_len, d)
```

```python
import functools
import math

import jax
import jax.numpy as jnp
from jax import lax
from jax.experimental import pallas as pl
from jax.experimental.pallas import tpu as pltpu

D_MODEL = 1024
HEAD_DIM = 64
N_HEADS = D_MODEL // HEAD_DIM
N_EXPERTS = 16
N_GROUPS = 4
EXPERTS_PER_GROUP = N_EXPERTS // N_GROUPS
D_EXPERT = 512
GN_EPS = 64e-5
LN_EPS = 1e-5
DEPTH = 2
ALPHA = (2 * DEPTH) ** 0.25

V7X_LANES = 128
V7X_VMEM_BYTES = 64 * 2 ** 20

V7X_MXU_DIM = 256

RWKV_CHUNK = 64
RWKV_HEADS_PER_TILE = V7X_MXU_DIM // HEAD_DIM
PROJ_TILE = 256
ATTN_TILE = 256
MOE_BLOCK = 256
MOE_TOK_TILE = 256

F32 = jnp.float32
BF16 = jnp.bfloat16


def _vmem_limit(n_bytes):
    return int(min(n_bytes + 16 * 2 ** 20, V7X_VMEM_BYTES - 8 * 2 ** 20))


def _bdot(a, b):
    return jnp.dot(a.astype(BF16), b.astype(BF16), preferred_element_type=F32)


def _split3(x):
    hi = x.astype(BF16)
    r1 = x - hi.astype(F32)
    mid = r1.astype(BF16)
    lo = (r1 - mid.astype(F32)).astype(BF16)
    return hi, mid, lo


def _layer_norm_rows(x, g, b):
    mu = jnp.mean(x, axis=-1, keepdims=True)
    xc = x - mu
    var = jnp.mean(xc * xc, axis=-1, keepdims=True)
    return xc * lax.rsqrt(var + LN_EPS) * g + b


def _rwkv_proj_kernel(x_ref, xp_ref, mix_ref, wr_ref, wk_ref, wv_ref, w1_ref, w2_ref, a1_ref, a2_ref,
                      g1_ref, g2_ref, w0_ref, a0_ref,
                      r_ref, lw_ref, k_ref, v_ref, a_ref, g_ref, *, tiles_per_seq):
    i = pl.program_id(0)
    x = x_ref[...]
    tt = x.shape[0]
    first = (i % tiles_per_seq) == 0
    prev_row = jnp.where(first, 0.0, xp_ref[7:8, :])
    row = lax.broadcasted_iota(jnp.int32, (tt, 1), 0)
    xprev = jnp.where(row == 0, prev_row, pltpu.roll(x, 1, axis=0))
    xx = xprev - x
    mix = mix_ref[...]
    xr, xw, xk, xv, xa, xg = (x + xx * mix[j:j + 1, :] for j in range(6))
    r_ref[...] = _bdot(xr, wr_ref[...])
    k_ref[...] = _bdot(xk, wk_ref[...])
    v_ref[...] = _bdot(xv, wv_ref[...])
    z = w0_ref[...] + _bdot(jnp.tanh(_bdot(xw, w1_ref[...])), w2_ref[...])
    w = -jax.nn.softplus(-z) - 0.5
    lw_ref[...] = -jnp.exp(w)
    a_ref[...] = jax.nn.sigmoid(a0_ref[...] + _bdot(_bdot(xa, a1_ref[...]), a2_ref[...]))
    g_ref[...] = _bdot(jax.nn.sigmoid(_bdot(xg, g1_ref[...])), g2_ref[...])


def _rwkv_proj(x2, seq_len, mix, wr, wk, wv, w1, w2, a1, a2, g1, g2, w0, a0):
    n_tok, d = x2.shape
    tt = PROJ_TILE
    n_tiles = n_tok // tt
    tiles_per_seq = seq_len // tt
    tile = pl.BlockSpec((tt, d), lambda i: (i, 0))
    prev = pl.BlockSpec((8, d), lambda i: (jnp.maximum(i * (tt // 8) - 1, 0), 0))
    full = lambda arr: pl.BlockSpec(arr.shape, lambda i: (0,) * arr.ndim)
    weights = (mix, wr, wk, wv, w1, w2, a1, a2, g1, g2, w0, a0)
    out_sds = jax.ShapeDtypeStruct((n_tok, d), F32)
    w_bytes = sum(int(w.size) * w.dtype.itemsize for w in weights)
    return pl.pallas_call(
        functools.partial(_rwkv_proj_kernel, tiles_per_seq=tiles_per_seq),
        grid=(n_tiles,),
        in_specs=[tile, prev] + [full(w) for w in weights],
        out_specs=[tile] * 6,
        out_shape=[out_sds] * 6,
        compiler_params=pltpu.CompilerParams(
            dimension_semantics=("parallel",),
            vmem_limit_bytes=_vmem_limit(2 * w_bytes + 2 * 8 * tt * d * 4)),
        name="rwkv_proj",
    )(x2, x2, *weights)


def _rwkv_recur_kernel(r_ref, lw_ref, k_ref, v_ref, a_ref, g_ref, kk_ref, ka_ref, rk_ref, gng_ref, gnb_ref,
                       o_ref, s_ref):
    c = pl.program_id(1)
    C = r_ref.shape[0]
    N = HEAD_DIM

    @pl.when(c == 0)
    def _():
        s_ref[...] = jnp.zeros_like(s_ref)

    lw = lw_ref[...]
    ti = lax.broadcasted_iota(jnp.int32, (C, C), 0)
    tj = lax.broadcasted_iota(jnp.int32, (C, C), 1)
    tril = (tj <= ti).astype(BF16)
    cum = sum(jnp.dot(tril, part, preferred_element_type=F32) for part in _split3(lw))
    rho = cum[C // 2 - 1:C // 2, :]
    last = cum[C - 1:C, :]
    e_q = jnp.exp(cum - rho)
    e_qx = jnp.exp(cum - lw - rho)
    e_k = jnp.exp(rho - cum)
    e_end = jnp.exp(last - cum)
    e_rho = jnp.exp(rho)
    d_end = jnp.exp(last)

    r = r_ref[...]
    k = k_ref[...]
    v = v_ref[...]
    a = a_ref[...]
    kk_raw = k * kk_ref[...]
    k_mod = k * (1.0 + (a - 1.0) * ka_ref[...])
    rkk = r * k_mod * rk_ref[...]

    G = RWKV_HEADS_PER_TILE
    R = G * C
    GW = G * N
    er = lax.broadcasted_iota(jnp.int32, (R, GW), 0)
    ec = lax.broadcasted_iota(jnp.int32, (R, GW), 1)
    blk = (er // C) == (ec // N)
    strict = (ec % C) < (er % C)
    incl = (ec % C) <= (er % C)
    eye = (er == ec).astype(F32)
    ident = (er == ec).astype(BF16)
    ones_blk = blk.astype(BF16)

    def expand(x):
        return jnp.where(blk, jnp.concatenate([x] * G, axis=0), 0.0).astype(BF16)

    def head_sums(xs):
        parts = _split3(jnp.concatenate(xs, axis=0))
        tot = jnp.dot(jnp.concatenate(parts, axis=0), ones_blk, preferred_element_type=F32)
        n = len(xs) * C
        tot = tot[:n] + tot[n:2 * n] + tot[2 * n:]
        return [tot[u * C:(u + 1) * C] for u in range(len(xs))]

    mm = lambda p, q: jnp.dot(p, q, preferred_element_type=F32)
    nt = lambda p, q: lax.dot_general(p, q, (((1,), (1,)), ((), ())), preferred_element_type=F32)

    n_grp = N_HEADS // G
    groups = [slice(gi * GW, (gi + 1) * GW) for gi in range(n_grp)]
    pre = head_sums([kk_raw[:, sl] * kk_raw[:, sl] for sl in groups] + [rkk[:, sl] for sl in groups])
    kk_ss, rkk_sum = pre[:n_grp], pre[n_grp:]
    each = lambda fn, *lists: [fn(*args) for args in zip(*lists)]
    bf = lambda x: x.astype(BF16)
    kk_n = each(lambda sl, ss: kk_raw[:, sl] * lax.rsqrt(jnp.maximum(ss, 1e-24)), groups, kk_ss)
    b_n = each(lambda sl, kk_g: kk_g * a[:, sl], groups, kk_n)
    kk_q = each(lambda sl, kk_g: kk_g * e_qx[:, sl], groups, kk_n)
    r_q = each(lambda sl: r[:, sl] * e_q[:, sl], groups)
    q2 = each(lambda x, y: jnp.concatenate([expand(x), expand(y)], axis=0), kk_q, r_q)
    a_k = each(lambda sl, q: nt(q, expand(k_mod[:, sl] * e_k[:, sl])), groups, q2)
    a_b = each(lambda sl, q, b_g: nt(q, expand(b_g * e_k[:, sl])), groups, q2, b_n)
    a_kk = each(lambda x: jnp.where(strict, x[:R], 0.0), a_k)
    a_rk = each(lambda x: jnp.where(incl, x[R:], 0.0), a_k)
    a_kb = each(lambda x: jnp.where(strict, x[:R], 0.0), a_b)
    a_rb = each(lambda x: jnp.where(incl, x[R:], 0.0), a_b)
    t_inv = each(lambda x: eye - x, a_kb)
    p = each(lambda x: mm(bf(-x), bf(-x)), a_kb)
    for _ in range(int(math.log2(C)) - 2):
        both = each(lambda pg, tg: mm(bf(pg), jnp.concatenate([bf(pg), bf(tg)], axis=1)), p, t_inv)
        p = each(lambda x: x[:, :R], both)
        t_inv = each(lambda tg, x: tg + x[:, R:], t_inv, both)
    t_inv = each(lambda pg, tg: tg + mm(bf(pg), bf(tg)), p, t_inv)
    kd_t = each(lambda sl: bf(nt(ident, expand(k_mod[:, sl] * e_end[:, sl]))), groups)
    bd_t = each(lambda sl, b_g: bf(nt(ident, expand(b_g * e_end[:, sl]))), groups, b_n)

    def decay_column(sl):
        rows = jnp.concatenate([part.astype(F32) for part in _split3(d_end[:, sl])] + [jnp.zeros((5, GW), F32)], axis=0)
        d_t = nt(ident, bf(rows))
        return d_t[:, 0:1] + d_t[:, 1:2] + d_t[:, 2:3]

    d_col = each(decay_column, groups)
    v_e = each(lambda sl: expand(v[:, sl]), groups)
    av = each(lambda x, y, ve: mm(bf(jnp.concatenate([x, y], axis=0)), ve), a_kk, a_rk, v_e)
    q2_abs = each(lambda sl, x, y: jnp.concatenate([expand(x * e_rho[:, sl]), expand(y * e_rho[:, sl])], axis=0),
                  groups, kk_q, r_q)
    st = [s_ref[gi] for gi in range(n_grp)]
    qs = each(lambda q, s: mm(q, bf(s)), q2_abs, st)
    sa_e = each(lambda tg, q, x: bf(mm(bf(tg), bf(q[:R] + x[:R]))), t_inv, qs, av)
    y_e = each(lambda q, x, arb, sa: q[R:] + x[R:] - mm(bf(arb), sa), qs, av, a_rb, sa_e)
    ys = each(lambda x: sum(x[u * C:(u + 1) * C] for u in range(G)), y_e)
    upd = each(lambda kt, bt, ve, sa: mm(jnp.concatenate([kt, -bt], axis=1), jnp.concatenate([ve, sa], axis=0)),
               kd_t, bd_t, v_e, sa_e)
    for gi in range(n_grp):
        s_ref[gi] = st[gi] * d_col[gi] + upd[gi]

    inv_n = 1.0 / N
    ycs = [y - mu * inv_n for y, mu in zip(ys, head_sums(ys))]
    for sl, yc, sq, bsum in zip(groups, ycs, head_sums([yc * yc for yc in ycs]), rkk_sum):
        yn = yc * lax.rsqrt(sq * inv_n + GN_EPS) * gng_ref[:, sl] + gnb_ref[:, sl]
        o_ref[:, sl] = ((yn + bsum * v[:, sl]) * g_ref[:, sl]).astype(o_ref.dtype)


def _rwkv_recur(r, lw, k, v, a, g, batch, seq_len, k_k, k_a, r_k, gn_g, gn_b):
    n_tok, d = r.shape
    C = RWKV_CHUNK
    nc = seq_len // C
    tile = pl.BlockSpec((C, d), lambda b, c: (b * nc + c, 0))
    vec = pl.BlockSpec((1, d), lambda b, c: (0, 0))
    return pl.pallas_call(
        _rwkv_recur_kernel,
        grid=(batch, nc),
        in_specs=[tile] * 6 + [vec] * 5,
        out_specs=tile,
        out_shape=jax.ShapeDtypeStruct((n_tok, d), BF16),
        scratch_shapes=[pltpu.VMEM((N_HEADS // RWKV_HEADS_PER_TILE, V7X_MXU_DIM, V7X_MXU_DIM), F32)],
        compiler_params=pltpu.CompilerParams(
            dimension_semantics=("parallel", "arbitrary"),
            vmem_limit_bytes=_vmem_limit(2 * 7 * C * d * 4 + 32 * C * d * 4 + 64 * V7X_MXU_DIM ** 2 * 4)),
        name="rwkv_recur",
    )(r, lw, k, v, a, g, k_k, k_a, r_k, gn_g, gn_b)


def _rank_among(vals, i):
    cnt = 0
    for j, vj in enumerate(vals):
        if j == i:
            continue
        before = (vj >= vals[i]) if j < i else (vj > vals[i])
        cnt = cnt + before.astype(jnp.int32)
    return cnt


def _pick(ranks, vals, want):
    out = vals[0]
    for rk, vl in zip(ranks[1:], vals[1:]):
        out = jnp.where(rk == want, vl, out)
    return out


def _epilogue_kernel(act_ref, wo_ref, res_ref, lng_ref, lnb_ref, rwt_ref, rb_ref,
                     h_ref, eidx_ref, rank_ref, gcol_ref, cnt_ref, base_ref):
    i = pl.program_id(0)

    @pl.when(i == 0)
    def _():
        base_ref[...] = jnp.zeros_like(base_ref)

    mixed = jnp.dot(act_ref[...], wo_ref[...], preferred_element_type=F32)
    h = _layer_norm_rows(ALPHA * res_ref[...] + mixed, lng_ref[...], lnb_ref[...])
    h_ref[...] = h
    tt = h.shape[0]

    hh, hm, hl = _split3(h)
    wh, wm, wl = _split3(rwt_ref[...])
    nt = lambda a, b: lax.dot_general(a, b, (((1,), (1,)), ((), ())), preferred_element_type=F32)
    logits = (nt(wh, hh) + (nt(wh, hm) + nt(wm, hh)) + (nt(wh, hl) + nt(wm, hm) + nt(wl, hh)))
    s = jax.nn.sigmoid(logits)
    s_sel = s + rb_ref[...]
    rows = [s_sel[e:e + 1, :] for e in range(N_EXPERTS)]
    grp_score, grp_i0, grp_i1 = [], [], []
    for gi in range(N_GROUPS):
        vals = rows[gi * EXPERTS_PER_GROUP:(gi + 1) * EXPERTS_PER_GROUP]
        ranks = [_rank_among(vals, q) for q in range(EXPERTS_PER_GROUP)]
        idx = [jnp.full_like(ranks[0], q) for q in range(EXPERTS_PER_GROUP)]
        grp_score.append(_pick(ranks, vals, 0) + _pick(ranks, vals, 1))
        grp_i0.append(_pick(ranks, idx, 0))
        grp_i1.append(_pick(ranks, idx, 1))
    g_ranks = [_rank_among(grp_score, q) for q in range(N_GROUPS)]
    gidx = [jnp.full_like(g_ranks[0], q) for q in range(N_GROUPS)]
    g_star = _pick(g_ranks, gidx, 0)
    e0 = g_star * EXPERTS_PER_GROUP + _pick(g_ranks, grp_i0, 0)
    e1 = g_star * EXPERTS_PER_GROUP + _pick(g_ranks, grp_i1, 0)

    e_iota = lax.broadcasted_iota(jnp.int32, (N_EXPERTS, tt), 0)
    hit0 = e_iota == e0
    hit1 = e_iota == e1
    gate0 = jnp.sum(jnp.where(hit0, s, 0.0), axis=0, keepdims=True)
    gate1 = jnp.sum(jnp.where(hit1, s, 0.0), axis=0, keepdims=True)
    denom = gate0 + gate1
    gate0 = gate0 / denom
    gate1 = gate1 / denom

    member = jnp.where(hit0 | hit1, 1.0, 0.0)
    ui = lax.broadcasted_iota(jnp.int32, (tt, tt), 0)
    uj = lax.broadcasted_iota(jnp.int32, (tt, tt), 1)
    before = (ui < uj).astype(BF16)
    base = base_ref[:, 0:1]
    prefix = jnp.dot(member.astype(BF16), before, preferred_element_type=F32) + base
    rank0 = jnp.sum(jnp.where(hit0, prefix, 0.0), axis=0, keepdims=True)
    rank1 = jnp.sum(jnp.where(hit1, prefix, 0.0), axis=0, keepdims=True)
    new_base = base + jnp.sum(member, axis=1, keepdims=True)
    base_ref[...] = jnp.broadcast_to(new_base, base_ref.shape)
    cnt_ref[...] = jnp.broadcast_to(new_base, cnt_ref.shape).astype(jnp.int32)

    eidx_ref[...] = jnp.concatenate([e0, e1], axis=0)
    rank_ref[...] = jnp.concatenate([rank0, rank1], axis=0).astype(jnp.int32)

    gpad = jnp.concatenate([gate0, gate1, jnp.zeros((V7X_LANES - 2, tt), F32)], axis=0)
    ident = (ui == uj).astype(BF16)
    gcol_ref[...] = sum(nt(ident, part) for part in _split3(gpad))


def _mixer_epilogue(act, wo, res, ln_g, ln_b, router_wt, router_bias):
    n_tok, d = res.shape
    k_in = act.shape[1]
    tt = PROJ_TILE
    n_tiles = n_tok // tt
    full = lambda arr: pl.BlockSpec(arr.shape, lambda i: (0,) * arr.ndim)
    return pl.pallas_call(
        _epilogue_kernel,
        grid=(n_tiles,),
        in_specs=[pl.BlockSpec((tt, k_in), lambda i: (i, 0)), full(wo), pl.BlockSpec((tt, d), lambda i: (i, 0)),
                  full(ln_g), full(ln_b), full(router_wt), full(router_bias)],
        out_specs=[pl.BlockSpec((tt, d), lambda i: (i, 0)),
                   pl.BlockSpec((2, tt), lambda i: (0, i)),
                   pl.BlockSpec((2, tt), lambda i: (0, i)),
                   pl.BlockSpec((tt, V7X_LANES), lambda i: (i, 0)),
                   pl.BlockSpec((N_EXPERTS, V7X_LANES), lambda i: (0, 0))],
        out_shape=[jax.ShapeDtypeStruct((n_tok, d), F32),
                   jax.ShapeDtypeStruct((2, n_tok), jnp.int32),
                   jax.ShapeDtypeStruct((2, n_tok), jnp.int32),
                   jax.ShapeDtypeStruct((n_tok, V7X_LANES), F32),
                   jax.ShapeDtypeStruct((N_EXPERTS, V7X_LANES), jnp.int32)],
        scratch_shapes=[pltpu.VMEM((N_EXPERTS, V7X_LANES), F32)],
        compiler_params=pltpu.CompilerParams(
            dimension_semantics=("arbitrary",),
            vmem_limit_bytes=_vmem_limit(2 * int(wo.size) * 2 + 2 * tt * (k_in * 2 + 2 * d * 4) + 16 * tt * d * 4)),
        name="mixer_epilogue",
    )(act, wo, res, ln_g, ln_b, router_wt, router_bias)


def _dispatch_kernel(dest_ref, h_ref, xs_in_ref, xs_ref, sem):
    del xs_in_ref
    tt = h_ref.shape[0]

    def row_copy(t, slot):
        d = dest_ref[0, 0, slot * tt + t]
        return pltpu.make_async_copy(h_ref.at[pl.ds(t, 1)], xs_ref.at[pl.ds(d, 1)], sem)

    def issue(t, carry):
        row_copy(t, 0).start()
        row_copy(t, 1).start()
        return carry

    lax.fori_loop(0, tt, issue, 0)

    def drain(t, carry):
        row_copy(t, 0).wait()
        row_copy(t, 1).wait()
        return carry

    lax.fori_loop(0, tt, drain, 0)


def _moe_dispatch(h, dest_tiles, n_rows):
    n_tok, d = h.shape
    tt = MOE_TOK_TILE
    n_tiles = n_tok // tt
    zeros = jnp.zeros((n_rows, d), h.dtype)
    return pl.pallas_call(
        _dispatch_kernel,
        grid=(n_tiles,),
        in_specs=[pl.BlockSpec((1, 1, 2 * tt), lambda i: (i, 0, 0), memory_space=pltpu.SMEM),
                  pl.BlockSpec((tt, d), lambda i: (i, 0)),
                  pl.BlockSpec(memory_space=pl.ANY)],
        out_specs=pl.BlockSpec(memory_space=pl.ANY),
        out_shape=jax.ShapeDtypeStruct((n_rows, d), h.dtype),
        scratch_shapes=[pltpu.SemaphoreType.DMA(())],
        input_output_aliases={2: 0},
        compiler_params=pltpu.CompilerParams(dimension_semantics=("arbitrary",)),
        name="moe_dispatch",
    )(dest_tiles, h, zeros)


def _ffn_kernel(bexp_ref, nblk_ref, x_ref, wg_ref, wu_ref, wd_ref, y_ref):
    del bexp_ref
    j = pl.program_id(0)

    @pl.when(j < nblk_ref[0])
    def _():
        x = x_ref[...].astype(BF16)
        gate = jnp.dot(x, wg_ref[0], preferred_element_type=F32)
        up = jnp.dot(x, wu_ref[0], preferred_element_type=F32)
        hid = (gate * jax.nn.sigmoid(gate)) * up
        y_ref[...] = jnp.dot(hid.astype(BF16), wd_ref[0], preferred_element_type=F32)

    @pl.when(j >= nblk_ref[0])
    def _():
        y_ref[...] = jnp.zeros_like(y_ref)


def _moe_ffn(xs, block_exp, n_used, wg, wu, wd):
    n_rows, d = xs.shape
    blk = MOE_BLOCK
    n_blocks = n_rows // blk
    de = wg.shape[2]
    grid_spec = pltpu.PrefetchScalarGridSpec(
        num_scalar_prefetch=2,
        grid=(n_blocks,),
        in_specs=[pl.BlockSpec((blk, d), lambda j, be, nb: (j, 0)),
                  pl.BlockSpec((1, d, de), lambda j, be, nb: (be[j], 0, 0)),
                  pl.BlockSpec((1, d, de), lambda j, be, nb: (be[j], 0, 0)),
                  pl.BlockSpec((1, de, d), lambda j, be, nb: (be[j], 0, 0))],
        out_specs=pl.BlockSpec((blk, d), lambda j, be, nb: (j, 0)),
    )
    return pl.pallas_call(
        _ffn_kernel,
        grid_spec=grid_spec,
        out_shape=jax.ShapeDtypeStruct((n_rows, d), F32),
        compiler_params=pltpu.CompilerParams(
            dimension_semantics=("arbitrary",),
            vmem_limit_bytes=_vmem_limit(2 * 3 * d * de * 2 + 4 * blk * d * 4 + 4 * blk * de * 4)),
        name="moe_ffn",
    )(block_exp, n_used, xs, wg, wu, wd)


def _combine_kernel(dest_ref, ys_ref, gcol_ref, res_ref, lng_ref, lnb_ref, o_ref, buf_ref, sem):
    tt = res_ref.shape[0]

    def row_copy(t, slot):
        d = dest_ref[0, 0, slot * tt + t]
        return pltpu.make_async_copy(ys_ref.at[pl.ds(d, 1)], buf_ref.at[slot, pl.ds(t, 1)], sem)

    def issue(t, carry):
        row_copy(t, 0).start()
        row_copy(t, 1).start()
        return carry

    lax.fori_loop(0, tt, issue, 0)

    def drain(t, carry):
        row_copy(t, 0).wait()
        row_copy(t, 1).wait()
        return carry

    lax.fori_loop(0, tt, drain, 0)

    gcol = gcol_ref[...]
    ffn = buf_ref[0] * gcol[:, 0:1] + buf_ref[1] * gcol[:, 1:2]
    o_ref[...] = _layer_norm_rows(ALPHA * res_ref[...] + ffn, lng_ref[...], lnb_ref[...])


def _moe_combine(ys, dest_tiles, gcol, res, ln_g, ln_b):
    n_tok, d = res.shape
    tt = MOE_TOK_TILE
    n_tiles = n_tok // tt
    full = lambda arr: pl.BlockSpec(arr.shape, lambda i: (0,) * arr.ndim)
    return pl.pallas_call(
        _combine_kernel,
        grid=(n_tiles,),
        in_specs=[pl.BlockSpec((1, 1, 2 * tt), lambda i: (i, 0, 0), memory_space=pltpu.SMEM),
                  pl.BlockSpec(memory_space=pl.ANY),
                  pl.BlockSpec((tt, V7X_LANES), lambda i: (i, 0)),
                  pl.BlockSpec((tt, d), lambda i: (i, 0)),
                  full(ln_g), full(ln_b)],
        out_specs=pl.BlockSpec((tt, d), lambda i: (i, 0)),
        out_shape=jax.ShapeDtypeStruct((n_tok, d), F32),
        scratch_shapes=[pltpu.VMEM((2, tt, d), F32), pltpu.SemaphoreType.DMA(())],
        compiler_params=pltpu.CompilerParams(
            dimension_semantics=("arbitrary",),
            vmem_limit_bytes=_vmem_limit(2 * tt * d * 4 + 2 * 3 * tt * d * 4)),
        name="moe_combine",
    )(dest_tiles, ys, gcol, res, ln_g, ln_b)


def _moe_layer(h, eidx, rank, counts, gcol, wg, wu, wd, ln_g, ln_b):
    n_tok, d = h.shape
    blk = MOE_BLOCK
    n_rows = (2 * n_tok + N_EXPERTS * (blk - 1) + blk - 1) // blk * blk
    n_blocks = n_rows // blk
    counts = counts[:, 0]
    padded = (counts + blk - 1) // blk * blk
    pend = jnp.cumsum(padded)
    pstart = pend - padded
    onehot = (eidx[:, :, None] == jnp.arange(N_EXPERTS, dtype=jnp.int32)).astype(jnp.int32)
    dest = rank + jnp.sum(onehot * pstart, axis=-1)
    tt = MOE_TOK_TILE
    dest_tiles = dest.reshape(2, n_tok // tt, tt).transpose(1, 0, 2).reshape(n_tok // tt, 1, 2 * tt)
    blk_start = jnp.arange(n_blocks, dtype=jnp.int32) * blk
    block_exp = jnp.minimum(jnp.sum((blk_start[:, None] >= pend[None, :]).astype(jnp.int32), axis=1),
                            N_EXPERTS - 1).astype(jnp.int32)
    n_used = (pend[-1:] // blk).astype(jnp.int32)
    xs = _moe_dispatch(h, dest_tiles, n_rows)
    ys = _moe_ffn(xs, block_exp, n_used, wg, wu, wd)
    return _moe_combine(ys, dest_tiles, gcol, h, ln_g, ln_b)


def _fox_proj_kernel(x_ref, wq_ref, wk_ref, wv_ref, wf_ref, bf_ref, q_ref, k_ref, v_ref, c_ref, carry_ref):
    t = pl.program_id(1)

    @pl.when(t == 0)
    def _():
        carry_ref[...] = jnp.zeros_like(carry_ref)

    x = x_ref[...]
    xb = x.astype(BF16)
    q_ref[...] = jnp.dot(xb, wq_ref[...], preferred_element_type=F32).astype(q_ref.dtype)
    k_ref[...] = jnp.dot(xb, wk_ref[...], preferred_element_type=F32).astype(k_ref.dtype)
    v_ref[...] = jnp.dot(xb, wv_ref[...], preferred_element_type=F32).astype(v_ref.dtype)
    xh, xm, xl = _split3(x)
    wh, wm, wl = _split3(wf_ref[...])
    mm = lambda a, b: jnp.dot(a, b, preferred_element_type=F32)
    logit = (mm(xh, wh) + (mm(xh, wm) + mm(xm, wh)) + (mm(xh, wl) + mm(xm, wm) + mm(xl, wh))) + bf_ref[...]
    log_f = jax.nn.log_sigmoid(logit)
    tt = x.shape[0]
    ti = lax.broadcasted_iota(jnp.int32, (tt, tt), 0)
    tj = lax.broadcasted_iota(jnp.int32, (tt, tt), 1)
    tril = (tj <= ti).astype(BF16)
    c = sum(mm(tril, part) for part in _split3(log_f)) + carry_ref[0:1, :]
    c_ref[...] = c
    carry_ref[...] = jnp.broadcast_to(c[tt - 1:tt, :], carry_ref.shape)


def _fox_proj(x2, batch, seq_len, wq, wk, wv, wf, b_f):
    n_tok, d = x2.shape
    tt = PROJ_TILE
    nt = seq_len // tt
    tile = pl.BlockSpec((tt, d), lambda b, t: (b * nt + t, 0))
    full = lambda arr: pl.BlockSpec(arr.shape, lambda b, t: (0,) * arr.ndim)
    return pl.pallas_call(
        _fox_proj_kernel,
        grid=(batch, nt),
        in_specs=[tile, full(wq), full(wk), full(wv), full(wf), full(b_f)],
        out_specs=[tile, tile, tile, pl.BlockSpec((tt, N_HEADS), lambda b, t: (b * nt + t, 0))],
        out_shape=[jax.ShapeDtypeStruct((n_tok, d), BF16)] * 3 + [jax.ShapeDtypeStruct((n_tok, N_HEADS), F32)],
        scratch_shapes=[pltpu.VMEM((8, N_HEADS), F32)],
        compiler_params=pltpu.CompilerParams(
            dimension_semantics=("parallel", "arbitrary"),
            vmem_limit_bytes=_vmem_limit(2 * 3 * d * d * 2 + 2 * tt * d * (4 + 3 * 2) + 8 * tt * d * 4)),
        name="fox_proj",
    )(x2, wq, wk, wv, wf, b_f)


def _fox_attn_kernel(q_ref, k_ref, v_ref, cq_ref, ck_ref, o_ref):
    hp = pl.program_id(1)
    qi = pl.program_id(2)
    tq = q_ref.shape[0]
    tk = tq
    N = HEAD_DIM
    q = q_ref[...]
    lane = lax.broadcasted_iota(jnp.int32, (tq, 2 * N), 1)
    first = lane < N
    zero = jnp.zeros_like(q)
    q_heads = (jnp.where(first, q, zero), jnp.where(first, zero, q))
    cq_all = cq_ref[...]
    head_lane = lax.broadcasted_iota(jnp.int32, cq_all.shape, 1)
    cq = [jnp.sum(jnp.where(head_lane == 2 * hp + u, cq_all, 0.0), axis=-1, keepdims=True) for u in range(2)]
    row = lax.broadcasted_iota(jnp.int32, (tq, tk), 0)
    col = lax.broadcasted_iota(jnp.int32, (tq, tk), 1)

    def chunk(j, carry):
        start = pl.multiple_of(j * tk, tk)
        k_j = k_ref[pl.ds(start, tk), :]
        v_j = v_ref[pl.ds(start, tk), :]
        allowed = (col + j * tk) <= (row + qi * tq)
        out = []
        for u in range(2):
            m, l, acc = carry[u]
            s = lax.dot_general(q_heads[u], k_j, (((1,), (1,)), ((), ())), preferred_element_type=F32)
            s = s + cq[u] - ck_ref[0, pl.ds(2 * hp + u, 1), pl.ds(start, tk)]
            s = jnp.where(allowed, s, -jnp.inf)
            m_new = jnp.maximum(m, jnp.max(s, axis=-1, keepdims=True))
            p = jnp.exp(s - m_new)
            alpha = jnp.exp(m - m_new)
            l = alpha * l + jnp.sum(p, axis=-1, keepdims=True)
            acc = alpha * acc + jnp.dot(p.astype(BF16), v_j, preferred_element_type=F32)
            out.append((m_new, l, acc))
        return tuple(out)

    init = tuple((jnp.full((tq, 1), -jnp.inf, F32), jnp.zeros((tq, 1), F32), jnp.zeros((tq, 2 * N), F32))
                 for _ in range(2))
    (m0, l0, acc0), (m1, l1, acc1) = lax.fori_loop(0, qi + 1, chunk, init)
    o = jnp.where(first, acc0 / l0, acc1 / l1)
    o_ref[...] = o.astype(o_ref.dtype)


def _fox_attn(q, k, v, c, c_t, batch, seq_len):
    n_tok, d = q.shape
    tq = ATTN_TILE
    nq = seq_len // tq
    pair = 2 * HEAD_DIM
    return pl.pallas_call(
        _fox_attn_kernel,
        grid=(batch, d // pair, nq),
        in_specs=[pl.BlockSpec((tq, pair), lambda b, hp, qi: (b * nq + qi, hp)),
                  pl.BlockSpec((seq_len, pair), lambda b, hp, qi: (b, hp)),
                  pl.BlockSpec((seq_len, pair), lambda b, hp, qi: (b, hp)),
                  pl.BlockSpec((tq, N_HEADS), lambda b, hp, qi: (b * nq + qi, 0)),
                  pl.BlockSpec((1, N_HEADS, seq_len), lambda b, hp, qi: (b, 0, 0))],
        out_specs=pl.BlockSpec((tq, pair), lambda b, hp, qi: (b * nq + qi, hp)),
        out_shape=jax.ShapeDtypeStruct((n_tok, d), BF16),
        compiler_params=pltpu.CompilerParams(
            dimension_semantics=("parallel", "parallel", "arbitrary"),
            vmem_limit_bytes=_vmem_limit(4 * seq_len * pair * 2 + 16 * tq * tq * 4)),
        name="fox_attn",
    )(q, k, v, c, c_t)


def kernel(x, rw_mix, rw_wr, rw_wk, rw_wv, rw_wo, rw_w0, rw_w1, rw_w2, rw_a0, rw_a1, rw_a2, rw_g1, rw_g2,
           rw_kk, rw_ka, rw_rk, rw_gn_g, rw_gn_b, fx_w_in, fx_b_f, fx_wo, router_w, router_bias,
           moe_w_gate, moe_w_up, moe_w_down, ln_g, ln_b):
    batch, seq_len, d = x.shape
    n_tok = batch * seq_len
    bf = lambda w: w.astype(BF16)
    row = lambda w: w.reshape(1, -1)
    router_wt = router_w.T
    router_b = router_bias.reshape(N_EXPERTS, 1)
    h = x.reshape(n_tok, d)

    for i in range(DEPTH):
        j = i // 2
        if i % 2 == 0:
            r, lw, k, v, a, g = _rwkv_proj(
                h, seq_len, rw_mix[j], bf(rw_wr[j]), bf(rw_wk[j]), bf(rw_wv[j]), bf(rw_w1[j]), bf(rw_w2[j]),
                bf(rw_a1[j]), bf(rw_a2[j]), bf(rw_g1[j]), bf(rw_g2[j]), row(rw_w0[j]), row(rw_a0[j]))
            act = _rwkv_recur(r, lw, k, v, a, g, batch, seq_len, row(rw_kk[j]), row(rw_ka[j]), row(rw_rk[j]),
                              row(rw_gn_g[j]), row(rw_gn_b[j]))
            wo = bf(rw_wo[j])
        else:
            w_in = fx_w_in[j]
            scale = HEAD_DIM ** -0.5
            q, k, v, c = _fox_proj(h, batch, seq_len, bf(w_in[:, :d] * scale), bf(w_in[:, d:2 * d]),
                                   bf(w_in[:, 2 * d:3 * d]), w_in[:, 3 * d:], row(fx_b_f[j]))
            c_t = c.reshape(batch, seq_len, N_HEADS).transpose(0, 2, 1)
            act = _fox_attn(q, k, v, c, c_t, batch, seq_len)
            wo = bf(fx_wo[j])
        h, eidx, rank, gcol, counts = _mixer_epilogue(act, wo, h, row(ln_g[i, 0]), row(ln_b[i, 0]),
                                                      router_wt, router_b)
        h = _moe_layer(h, eidx, rank, counts, gcol, bf(moe_w_gate[i]), bf(moe_w_up[i]), bf(moe_w_down[i]),
                       row(ln_g[i, 1]), row(ln_b[i, 1]))
    return h.reshape(batch, seq_len, d)
```

```python
import functools
import math

import jax
import jax.numpy as jnp
from jax import lax
from jax.experimental import pallas as pl
from jax.experimental.pallas import tpu as pltpu

D_MODEL = 1024
HEAD_DIM = 64
N_HEADS = D_MODEL // HEAD_DIM
N_EXPERTS = 16
N_GROUPS = 4
EXPERTS_PER_GROUP = N_EXPERTS // N_GROUPS
D_EXPERT = 512
GN_EPS = 64e-5
LN_EPS = 1e-5
DEPTH = 2
ALPHA = (2 * DEPTH) ** 0.25
LOG2E = math.log2(math.e)

V7X_LANES = 128
V7X_VMEM_BYTES = 64 * 2 ** 20

V7X_MXU_DIM = 256

RWKV_CHUNK = 64
RWKV_HEADS_PER_TILE = V7X_MXU_DIM // HEAD_DIM
PROJ_TILE = 256
EPI_TILE = 1024
EPI_SUBTILE = 256
ATTN_TILE = 256
MOE_BLOCK = 256
DISPATCH_TILE = 1024
COMBINE_TILE = 256
DMA_ISSUE_UNROLL = 8

F32 = jnp.float32
BF16 = jnp.bfloat16


def _vmem_limit(n_bytes):
    return int(min(n_bytes + 16 * 2 ** 20, V7X_VMEM_BYTES - 8 * 2 ** 20))


def _bdot(a, b):
    return jnp.dot(a.astype(BF16), b.astype(BF16), preferred_element_type=F32)


def _split3(x):
    hi = x.astype(BF16)
    r1 = x - hi.astype(F32)
    mid = r1.astype(BF16)
    lo = (r1 - mid.astype(F32)).astype(BF16)
    return hi, mid, lo


def _layer_norm_rows(x, g, b):
    mu = jnp.mean(x, axis=-1, keepdims=True)
    xc = x - mu
    var = jnp.mean(xc * xc, axis=-1, keepdims=True)
    return xc * lax.rsqrt(var + LN_EPS) * g + b


def _rwkv_proj_kernel(x_ref, xp_ref, mix_ref, wr_ref, wk_ref, wv_ref, w1_ref, w2_ref, a1_ref, a2_ref,
                      g1_ref, g2_ref, w0_ref, a0_ref,
                      r_ref, lw_ref, k_ref, v_ref, a_ref, g_ref, *, tiles_per_seq):
    i = pl.program_id(0)
    x = x_ref[...]
    tt = x.shape[0]
    first = (i % tiles_per_seq) == 0
    prev_row = jnp.where(first, 0.0, xp_ref[7:8, :])
    row = lax.broadcasted_iota(jnp.int32, (tt, 1), 0)
    xprev = jnp.where(row == 0, prev_row, pltpu.roll(x, 1, axis=0))
    xx = xprev - x
    mix = mix_ref[...]
    xr, xw, xk, xv, xa, xg = (x + xx * mix[j:j + 1, :] for j in range(6))
    r_ref[...] = _bdot(xr, wr_ref[...])
    k_ref[...] = _bdot(xk, wk_ref[...])
    v_ref[...] = _bdot(xv, wv_ref[...])
    z = w0_ref[...] + _bdot(jnp.tanh(_bdot(xw, w1_ref[...])), w2_ref[...])
    w = -jax.nn.softplus(-z) - 0.5
    lw_ref[...] = -jnp.exp(w)
    a_ref[...] = jax.nn.sigmoid(a0_ref[...] + _bdot(_bdot(xa, a1_ref[...]), a2_ref[...]))
    g_ref[...] = _bdot(jax.nn.sigmoid(_bdot(xg, g1_ref[...])), g2_ref[...])


def _rwkv_proj(x2, seq_len, mix, wr, wk, wv, w1, w2, a1, a2, g1, g2, w0, a0):
    n_tok, d = x2.shape
    tt = PROJ_TILE
    n_tiles = n_tok // tt
    tiles_per_seq = seq_len // tt
    tile = pl.BlockSpec((tt, d), lambda i: (i, 0))
    prev = pl.BlockSpec((8, d), lambda i: (jnp.maximum(i * (tt // 8) - 1, 0), 0))
    full = lambda arr: pl.BlockSpec(arr.shape, lambda i: (0,) * arr.ndim)
    weights = (mix, wr, wk, wv, w1, w2, a1, a2, g1, g2, w0, a0)
    out_sds = jax.ShapeDtypeStruct((n_tok, d), F32)
    w_bytes = sum(int(w.size) * w.dtype.itemsize for w in weights)
    return pl.pallas_call(
        functools.partial(_rwkv_proj_kernel, tiles_per_seq=tiles_per_seq),
        grid=(n_tiles,),
        in_specs=[tile, prev] + [full(w) for w in weights],
        out_specs=[tile] * 6,
        out_shape=[out_sds] * 6,
        compiler_params=pltpu.CompilerParams(
            dimension_semantics=("parallel",),
            vmem_limit_bytes=_vmem_limit(2 * w_bytes + 2 * 8 * tt * d * 4)),
        name="rwkv_proj",
    )(x2, x2, *weights)


def _rwkv_recur_kernel(r_ref, lw_ref, k_ref, v_ref, a_ref, g_ref, kk_ref, ka_ref, rk_ref, gng_ref, gnb_ref,
                       o_ref, s_ref):
    c = pl.program_id(1)
    C = r_ref.shape[0]
    N = HEAD_DIM

    @pl.when(c == 0)
    def _():
        s_ref[...] = jnp.zeros_like(s_ref)

    lw = lw_ref[...]
    ti = lax.broadcasted_iota(jnp.int32, (C, C), 0)
    tj = lax.broadcasted_iota(jnp.int32, (C, C), 1)
    tril = (tj <= ti).astype(BF16)
    cum = sum(jnp.dot(tril, part, preferred_element_type=F32) for part in _split3(lw))
    rho = cum[C // 2 - 1:C // 2, :]
    last = cum[C - 1:C, :]
    e_q = jnp.exp(cum - rho)
    e_qx = jnp.exp(cum - lw - rho)
    e_k = jnp.exp(rho - cum)
    e_end = jnp.exp(last - cum)
    e_rho = jnp.exp(rho)
    d_end = jnp.exp(last)

    r = r_ref[...]
    k = k_ref[...]
    v = v_ref[...]
    a = a_ref[...]
    kk_raw = k * kk_ref[...]
    k_mod = k * (1.0 + (a - 1.0) * ka_ref[...])
    rkk = r * k_mod * rk_ref[...]

    G = RWKV_HEADS_PER_TILE
    R = G * C
    GW = G * N
    er = lax.broadcasted_iota(jnp.int32, (R, GW), 0)
    ec = lax.broadcasted_iota(jnp.int32, (R, GW), 1)
    blk = (er // C) == (ec // N)
    strict = (ec % C) < (er % C)
    incl = (ec % C) <= (er % C)
    eye = (er == ec).astype(F32)
    ident = (er == ec).astype(BF16)
    ones_blk = blk.astype(BF16)

    def expand(x):
        return jnp.where(blk, jnp.concatenate([x] * G, axis=0), 0.0).astype(BF16)

    def head_sums(xs):
        parts = _split3(jnp.concatenate(xs, axis=0))
        tot = jnp.dot(jnp.concatenate(parts, axis=0), ones_blk, preferred_element_type=F32)
        n = len(xs) * C
        tot = tot[:n] + tot[n:2 * n] + tot[2 * n:]
        return [tot[u * C:(u + 1) * C] for u in range(len(xs))]

    mm = lambda p, q: jnp.dot(p, q, preferred_element_type=F32)
    nt = lambda p, q: lax.dot_general(p, q, (((1,), (1,)), ((), ())), preferred_element_type=F32)

    n_grp = N_HEADS // G
    groups = [slice(gi * GW, (gi + 1) * GW) for gi in range(n_grp)]
    pre = head_sums([kk_raw[:, sl] * kk_raw[:, sl] for sl in groups] + [rkk[:, sl] for sl in groups])
    kk_ss, rkk_sum = pre[:n_grp], pre[n_grp:]
    each = lambda fn, *lists: [fn(*args) for args in zip(*lists)]
    bf = lambda x: x.astype(BF16)
    kk_n = each(lambda sl, ss: kk_raw[:, sl] * lax.rsqrt(jnp.maximum(ss, 1e-24)), groups, kk_ss)
    b_n = each(lambda sl, kk_g: kk_g * a[:, sl], groups, kk_n)
    kk_q = each(lambda sl, kk_g: kk_g * e_qx[:, sl], groups, kk_n)
    r_q = each(lambda sl: r[:, sl] * e_q[:, sl], groups)
    q2 = each(lambda x, y: jnp.concatenate([expand(x), expand(y)], axis=0), kk_q, r_q)
    a_k = each(lambda sl, q: nt(q, expand(k_mod[:, sl] * e_k[:, sl])), groups, q2)
    a_b = each(lambda sl, q, b_g: nt(q, expand(b_g * e_k[:, sl])), groups, q2, b_n)
    a_kk = each(lambda x: jnp.where(strict, x[:R], 0.0), a_k)
    a_rk = each(lambda x: jnp.where(incl, x[R:], 0.0), a_k)
    a_kb = each(lambda x: jnp.where(strict, x[:R], 0.0), a_b)
    a_rb = each(lambda x: jnp.where(incl, x[R:], 0.0), a_b)
    t_inv = each(lambda x: eye - x, a_kb)
    p = each(lambda x: mm(bf(-x), bf(-x)), a_kb)
    for _ in range(int(math.log2(C)) - 2):
        both = each(lambda pg, tg: mm(bf(pg), jnp.concatenate([bf(pg), bf(tg)], axis=1)), p, t_inv)
        p = each(lambda x: x[:, :R], both)
        t_inv = each(lambda tg, x: tg + x[:, R:], t_inv, both)
    t_inv = each(lambda pg, tg: tg + mm(bf(pg), bf(tg)), p, t_inv)
    kd_t = each(lambda sl: bf(nt(ident, expand(k_mod[:, sl] * e_end[:, sl]))), groups)
    bd_t = each(lambda sl, b_g: bf(nt(ident, expand(b_g * e_end[:, sl]))), groups, b_n)

    def decay_column(sl):
        rows = jnp.concatenate([part.astype(F32) for part in _split3(d_end[:, sl])] + [jnp.zeros((5, GW), F32)], axis=0)
        d_t = nt(ident, bf(rows))
        return d_t[:, 0:1] + d_t[:, 1:2] + d_t[:, 2:3]

    d_col = each(decay_column, groups)
    v_e = each(lambda sl: expand(v[:, sl]), groups)
    av = each(lambda x, y, ve: mm(bf(jnp.concatenate([x, y], axis=0)), ve), a_kk, a_rk, v_e)
    q2_abs = each(lambda sl, x, y: jnp.concatenate([expand(x * e_rho[:, sl]), expand(y * e_rho[:, sl])], axis=0),
                  groups, kk_q, r_q)
    st = [s_ref[gi] for gi in range(n_grp)]
    qs = each(lambda q, s: mm(q, bf(s)), q2_abs, st)
    sa_e = each(lambda tg, q, x: bf(mm(bf(tg), bf(q[:R] + x[:R]))), t_inv, qs, av)
    y_e = each(lambda q, x, arb, sa: q[R:] + x[R:] - mm(bf(arb), sa), qs, av, a_rb, sa_e)
    ys = each(lambda x: sum(x[u * C:(u + 1) * C] for u in range(G)), y_e)
    upd = each(lambda kt, bt, ve, sa: mm(jnp.concatenate([kt, -bt], axis=1), jnp.concatenate([ve, sa], axis=0)),
               kd_t, bd_t, v_e, sa_e)
    for gi in range(n_grp):
        s_ref[gi] = st[gi] * d_col[gi] + upd[gi]

    inv_n = 1.0 / N
    ycs = [y - mu * inv_n for y, mu in zip(ys, head_sums(ys))]
    for sl, yc, sq, bsum in zip(groups, ycs, head_sums([yc * yc for yc in ycs]), rkk_sum):
        yn = yc * lax.rsqrt(sq * inv_n + GN_EPS) * gng_ref[:, sl] + gnb_ref[:, sl]
        o_ref[:, sl] = ((yn + bsum * v[:, sl]) * g_ref[:, sl]).astype(o_ref.dtype)


def _rwkv_recur(r, lw, k, v, a, g, batch, seq_len, k_k, k_a, r_k, gn_g, gn_b):
    n_tok, d = r.shape
    C = RWKV_CHUNK
    nc = seq_len // C
    tile = pl.BlockSpec((C, d), lambda b, c: (b * nc + c, 0))
    vec = pl.BlockSpec((1, d), lambda b, c: (0, 0))
    return pl.pallas_call(
        _rwkv_recur_kernel,
        grid=(batch, nc),
        in_specs=[tile] * 6 + [vec] * 5,
        out_specs=tile,
        out_shape=jax.ShapeDtypeStruct((n_tok, d), BF16),
        scratch_shapes=[pltpu.VMEM((N_HEADS // RWKV_HEADS_PER_TILE, V7X_MXU_DIM, V7X_MXU_DIM), F32)],
        compiler_params=pltpu.CompilerParams(
            dimension_semantics=("parallel", "arbitrary"),
            vmem_limit_bytes=_vmem_limit(2 * 7 * C * d * 4 + 32 * C * d * 4 + 64 * V7X_MXU_DIM ** 2 * 4)),
        name="rwkv_recur",
    )(r, lw, k, v, a, g, k_k, k_a, r_k, gn_g, gn_b)


def _rank_among(vals, i):
    cnt = 0
    for j, vj in enumerate(vals):
        if j == i:
            continue
        before = (vj >= vals[i]) if j < i else (vj > vals[i])
        cnt = cnt + before.astype(jnp.int32)
    return cnt


def _pick(ranks, vals, want):
    out = vals[0]
    for rk, vl in zip(ranks[1:], vals[1:]):
        out = jnp.where(rk == want, vl, out)
    return out


def _epilogue_kernel(act_ref, wo_ref, res_ref, lng_ref, lnb_ref, rwt_ref, rb_ref,
                     h_ref, eidx_ref, rank_ref, gcol_ref, cnt_ref, base_ref):
    i = pl.program_id(0)

    @pl.when(i == 0)
    def _():
        base_ref[...] = jnp.zeros_like(base_ref)

    tt = EPI_SUBTILE
    subs = [slice(u * tt, (u + 1) * tt) for u in range(act_ref.shape[0] // tt)]
    each = lambda fn, *lists: [fn(*args) for args in zip(*lists)]
    nt = lambda a, b: lax.dot_general(a, b, (((1,), (1,)), ((), ())), preferred_element_type=F32)

    mixed = each(lambda sl: jnp.dot(act_ref[sl, :], wo_ref[...], preferred_element_type=F32), subs)
    h = each(lambda sl, mx: _layer_norm_rows(ALPHA * res_ref[sl, :] + mx, lng_ref[...], lnb_ref[...]), subs, mixed)
    for sl, hs in zip(subs, h):
        h_ref[sl, :] = hs

    wh, wm, wl = _split3(rwt_ref[...])

    def router_logits(hs):
        hh, hm, hl = _split3(hs)
        return nt(wh, hh) + (nt(wh, hm) + nt(wm, hh)) + (nt(wh, hl) + nt(wm, hm) + nt(wl, hh))

    s = each(lambda hs: jax.nn.sigmoid(router_logits(hs)), h)

    def select(sg):
        s_sel = sg + rb_ref[...]
        rows = [s_sel[e:e + 1, :] for e in range(N_EXPERTS)]
        grp_score, grp_i0, grp_i1 = [], [], []
        for gi in range(N_GROUPS):
            vals = rows[gi * EXPERTS_PER_GROUP:(gi + 1) * EXPERTS_PER_GROUP]
            ranks = [_rank_among(vals, q) for q in range(EXPERTS_PER_GROUP)]
            idx = [jnp.full_like(ranks[0], q) for q in range(EXPERTS_PER_GROUP)]
            grp_score.append(_pick(ranks, vals, 0) + _pick(ranks, vals, 1))
            grp_i0.append(_pick(ranks, idx, 0))
            grp_i1.append(_pick(ranks, idx, 1))
        g_ranks = [_rank_among(grp_score, q) for q in range(N_GROUPS)]
        gidx = [jnp.full_like(g_ranks[0], q) for q in range(N_GROUPS)]
        g_star = _pick(g_ranks, gidx, 0)
        e0 = g_star * EXPERTS_PER_GROUP + _pick(g_ranks, grp_i0, 0)
        e1 = g_star * EXPERTS_PER_GROUP + _pick(g_ranks, grp_i1, 0)
        return e0, e1

    picked = each(select, s)
    e_iota = lax.broadcasted_iota(jnp.int32, (N_EXPERTS, tt), 0)
    hit0 = each(lambda pk: e_iota == pk[0], picked)
    hit1 = each(lambda pk: e_iota == pk[1], picked)

    def gates(sg, h0, h1):
        gate0 = jnp.sum(jnp.where(h0, sg, 0.0), axis=0, keepdims=True)
        gate1 = jnp.sum(jnp.where(h1, sg, 0.0), axis=0, keepdims=True)
        denom = gate0 + gate1
        return gate0 / denom, gate1 / denom

    gate = each(gates, s, hit0, hit1)

    member = each(lambda h0, h1: jnp.where(h0 | h1, 1.0, 0.0), hit0, hit1)
    ui = lax.broadcasted_iota(jnp.int32, (tt, tt), 0)
    uj = lax.broadcasted_iota(jnp.int32, (tt, tt), 1)
    before = (ui < uj).astype(BF16)
    prefix = each(lambda mb: jnp.dot(mb.astype(BF16), before, preferred_element_type=F32), member)
    base = base_ref[:, 0:1]
    for u, sl in enumerate(subs):
        pre = prefix[u] + base
        rank0 = jnp.sum(jnp.where(hit0[u], pre, 0.0), axis=0, keepdims=True)
        rank1 = jnp.sum(jnp.where(hit1[u], pre, 0.0), axis=0, keepdims=True)
        eidx_ref[:, sl] = jnp.concatenate(picked[u], axis=0)
        rank_ref[:, sl] = jnp.concatenate([rank0, rank1], axis=0).astype(jnp.int32)
        base = base + jnp.sum(member[u], axis=1, keepdims=True)
    base_ref[...] = jnp.broadcast_to(base, base_ref.shape)
    cnt_ref[...] = jnp.broadcast_to(base, cnt_ref.shape).astype(jnp.int32)

    ident = (ui == uj).astype(BF16)

    def gate_columns(gt):
        gpad = jnp.concatenate([gt[0], gt[1], jnp.zeros((V7X_LANES - 2, tt), F32)], axis=0)
        return sum(nt(ident, part) for part in _split3(gpad))

    for sl, gc in zip(subs, each(gate_columns, gate)):
        gcol_ref[sl, :] = gc


def _mixer_epilogue(act, wo, res, ln_g, ln_b, router_wt, router_bias):
    n_tok, d = res.shape
    k_in = act.shape[1]
    tt = EPI_TILE
    n_tiles = n_tok // tt
    full = lambda arr: pl.BlockSpec(arr.shape, lambda i: (0,) * arr.ndim)
    return pl.pallas_call(
        _epilogue_kernel,
        grid=(n_tiles,),
        in_specs=[pl.BlockSpec((tt, k_in), lambda i: (i, 0)), full(wo), pl.BlockSpec((tt, d), lambda i: (i, 0)),
                  full(ln_g), full(ln_b), full(router_wt), full(router_bias)],
        out_specs=[pl.BlockSpec((tt, d), lambda i: (i, 0)),
                   pl.BlockSpec((2, tt), lambda i: (0, i)),
                   pl.BlockSpec((2, tt), lambda i: (0, i)),
                   pl.BlockSpec((tt, V7X_LANES), lambda i: (i, 0)),
                   pl.BlockSpec((N_EXPERTS, V7X_LANES), lambda i: (0, 0))],
        out_shape=[jax.ShapeDtypeStruct((n_tok, d), F32),
                   jax.ShapeDtypeStruct((2, n_tok), jnp.int32),
                   jax.ShapeDtypeStruct((2, n_tok), jnp.int32),
                   jax.ShapeDtypeStruct((n_tok, V7X_LANES), F32),
                   jax.ShapeDtypeStruct((N_EXPERTS, V7X_LANES), jnp.int32)],
        scratch_shapes=[pltpu.VMEM((N_EXPERTS, V7X_LANES), F32)],
        compiler_params=pltpu.CompilerParams(
            dimension_semantics=("arbitrary",),
            vmem_limit_bytes=_vmem_limit(2 * int(wo.size) * 2 + 2 * tt * (k_in * 2 + 2 * d * 4) + 16 * tt * d * 4)),
        name="mixer_epilogue",
    )(act, wo, res, ln_g, ln_b, router_wt, router_bias)


def _dispatch_kernel(dest_ref, h_ref, xs_in_ref, xs_ref, sem):
    del xs_in_ref
    tt = h_ref.shape[0]

    def row_copy(t, slot):
        d = dest_ref[0, 0, slot * tt + t]
        return pltpu.make_async_copy(h_ref.at[pl.ds(t, 1)], xs_ref.at[pl.ds(d, 1)], sem)

    def issue(t0, carry):
        for u in range(DMA_ISSUE_UNROLL):
            t = t0 * DMA_ISSUE_UNROLL + u
            row_copy(t, 0).start()
            row_copy(t, 1).start()
        return carry

    lax.fori_loop(0, tt // DMA_ISSUE_UNROLL, issue, 0)
    for _ in range(2):
        pltpu.make_async_copy(h_ref, xs_ref.at[pl.ds(0, tt)], sem).wait()


def _moe_dispatch(h, dest_tiles, n_rows):
    n_tok, d = h.shape
    tt = dest_tiles.shape[2] // 2
    n_tiles = n_tok // tt
    zeros = jnp.zeros((n_rows, d), h.dtype)
    return pl.pallas_call(
        _dispatch_kernel,
        grid=(n_tiles,),
        in_specs=[pl.BlockSpec((1, 1, 2 * tt), lambda i: (i, 0, 0), memory_space=pltpu.SMEM),
                  pl.BlockSpec((tt, d), lambda i: (i, 0)),
                  pl.BlockSpec(memory_space=pl.ANY)],
        out_specs=pl.BlockSpec(memory_space=pl.ANY),
        out_shape=jax.ShapeDtypeStruct((n_rows, d), h.dtype),
        scratch_shapes=[pltpu.SemaphoreType.DMA(())],
        input_output_aliases={2: 0},
        compiler_params=pltpu.CompilerParams(dimension_semantics=("arbitrary",),
                                             vmem_limit_bytes=_vmem_limit(2 * tt * d * 4)),
        name="moe_dispatch",
    )(dest_tiles, h, zeros)


def _ffn_kernel(bexp_ref, nblk_ref, x_ref, wg_ref, wu_ref, wd_ref, y_ref):
    del bexp_ref
    j = pl.program_id(0)

    @pl.when(j < nblk_ref[0])
    def _():
        x = x_ref[...].astype(BF16)
        gate = jnp.dot(x, wg_ref[0], preferred_element_type=F32)
        up = jnp.dot(x, wu_ref[0], preferred_element_type=F32)
        hid = (gate * jax.nn.sigmoid(gate)) * up
        y_ref[...] = jnp.dot(hid.astype(BF16), wd_ref[0], preferred_element_type=F32)

    @pl.when(j >= nblk_ref[0])
    def _():
        y_ref[...] = jnp.zeros_like(y_ref)


def _moe_ffn(xs, block_exp, n_used, wg, wu, wd):
    n_rows, d = xs.shape
    blk = MOE_BLOCK
    n_blocks = n_rows // blk
    de = wg.shape[2]
    grid_spec = pltpu.PrefetchScalarGridSpec(
        num_scalar_prefetch=2,
        grid=(n_blocks,),
        in_specs=[pl.BlockSpec((blk, d), lambda j, be, nb: (j, 0)),
                  pl.BlockSpec((1, d, de), lambda j, be, nb: (be[j], 0, 0)),
                  pl.BlockSpec((1, d, de), lambda j, be, nb: (be[j], 0, 0)),
                  pl.BlockSpec((1, de, d), lambda j, be, nb: (be[j], 0, 0))],
        out_specs=pl.BlockSpec((blk, d), lambda j, be, nb: (j, 0)),
    )
    return pl.pallas_call(
        _ffn_kernel,
        grid_spec=grid_spec,
        out_shape=jax.ShapeDtypeStruct((n_rows, d), F32),
        compiler_params=pltpu.CompilerParams(
            dimension_semantics=("arbitrary",),
            vmem_limit_bytes=_vmem_limit(2 * 3 * d * de * 2 + 4 * blk * d * 4 + 4 * blk * de * 4)),
        name="moe_ffn",
    )(block_exp, n_used, xs, wg, wu, wd)


def _combine_kernel(dest_ref, dest_next_ref, ys_ref, gcol_ref, res_ref, lng_ref, lnb_ref, o_ref, buf_ref, sem):
    i = pl.program_id(0)
    n = pl.num_programs(0)
    tt = res_ref.shape[0]
    cur = i % 2

    def gather(idx_ref, buf):
        def row_copy(t, slot):
            d = idx_ref[0, 0, slot * tt + t]
            return pltpu.make_async_copy(ys_ref.at[pl.ds(d, 1)], buf_ref.at[buf, slot, pl.ds(t, 1)], sem.at[buf])

        def issue(t0, carry):
            for u in range(DMA_ISSUE_UNROLL):
                t = t0 * DMA_ISSUE_UNROLL + u
                row_copy(t, 0).start()
                row_copy(t, 1).start()
            return carry

        lax.fori_loop(0, tt // DMA_ISSUE_UNROLL, issue, 0)

    @pl.when(i == 0)
    def _():
        gather(dest_ref, cur)

    @pl.when(i + 1 < n)
    def _():
        gather(dest_next_ref, 1 - cur)

    for slot in range(2):
        pltpu.make_async_copy(ys_ref.at[pl.ds(0, tt)], buf_ref.at[cur, slot], sem.at[cur]).wait()

    gcol = gcol_ref[...]
    ffn = buf_ref[cur, 0] * gcol[:, 0:1] + buf_ref[cur, 1] * gcol[:, 1:2]
    o_ref[...] = _layer_norm_rows(ALPHA * res_ref[...] + ffn, lng_ref[...], lnb_ref[...])


def _moe_combine(ys, dest_tiles, gcol, res, ln_g, ln_b):
    n_tok, d = res.shape
    tt = dest_tiles.shape[2] // 2
    n_tiles = n_tok // tt
    full = lambda arr: pl.BlockSpec(arr.shape, lambda i: (0,) * arr.ndim)
    return pl.pallas_call(
        _combine_kernel,
        grid=(n_tiles,),
        in_specs=[pl.BlockSpec((1, 1, 2 * tt), lambda i: (i, 0, 0), memory_space=pltpu.SMEM),
                  pl.BlockSpec((1, 1, 2 * tt), lambda i: (jnp.minimum(i + 1, n_tiles - 1), 0, 0),
                               memory_space=pltpu.SMEM),
                  pl.BlockSpec(memory_space=pl.ANY),
                  pl.BlockSpec((tt, V7X_LANES), lambda i: (i, 0)),
                  pl.BlockSpec((tt, d), lambda i: (i, 0)),
                  full(ln_g), full(ln_b)],
        out_specs=pl.BlockSpec((tt, d), lambda i: (i, 0)),
        out_shape=jax.ShapeDtypeStruct((n_tok, d), F32),
        scratch_shapes=[pltpu.VMEM((2, 2, tt, d), F32), pltpu.SemaphoreType.DMA((2,))],
        compiler_params=pltpu.CompilerParams(
            dimension_semantics=("arbitrary",),
            vmem_limit_bytes=_vmem_limit(4 * tt * d * 4 + 2 * 3 * tt * d * 4)),
        name="moe_combine",
    )(dest_tiles, dest_tiles, ys, gcol, res, ln_g, ln_b)


def _moe_layer(h, eidx, rank, counts, gcol, wg, wu, wd, ln_g, ln_b):
    n_tok, d = h.shape
    blk = MOE_BLOCK
    n_rows = (2 * n_tok + N_EXPERTS * (blk - 1) + blk - 1) // blk * blk
    n_blocks = n_rows // blk
    counts = counts[:, 0]
    padded = (counts + blk - 1) // blk * blk
    pend = jnp.cumsum(padded)
    pstart = pend - padded
    onehot = (eidx[:, :, None] == jnp.arange(N_EXPERTS, dtype=jnp.int32)).astype(jnp.int32)
    dest = rank + jnp.sum(onehot * pstart, axis=-1)
    tiles = lambda tt: dest.reshape(2, n_tok // tt, tt).transpose(1, 0, 2).reshape(n_tok // tt, 1, 2 * tt)
    blk_start = jnp.arange(n_blocks, dtype=jnp.int32) * blk
    block_exp = jnp.minimum(jnp.sum((blk_start[:, None] >= pend[None, :]).astype(jnp.int32), axis=1),
                            N_EXPERTS - 1).astype(jnp.int32)
    n_used = (pend[-1:] // blk).astype(jnp.int32)
    xs = _moe_dispatch(h, tiles(DISPATCH_TILE), n_rows)
    ys = _moe_ffn(xs, block_exp, n_used, wg, wu, wd)
    return _moe_combine(ys, tiles(COMBINE_TILE), gcol, h, ln_g, ln_b)


def _fox_proj_kernel(x_ref, wq_ref, wk_ref, wv_ref, wf_ref, bf_ref, q_ref, k_ref, v_ref, c_ref, carry_ref):
    t = pl.program_id(1)

    @pl.when(t == 0)
    def _():
        carry_ref[...] = jnp.zeros_like(carry_ref)

    x = x_ref[...]
    xb = x.astype(BF16)
    q_ref[...] = (jnp.dot(xb, wq_ref[...], preferred_element_type=F32) * LOG2E).astype(q_ref.dtype)
    k_ref[...] = jnp.dot(xb, wk_ref[...], preferred_element_type=F32).astype(k_ref.dtype)
    v_ref[...] = jnp.dot(xb, wv_ref[...], preferred_element_type=F32).astype(v_ref.dtype)
    xh, xm, xl = _split3(x)
    wh, wm, wl = _split3(wf_ref[...])
    mm = lambda a, b: jnp.dot(a, b, preferred_element_type=F32)
    logit = (mm(xh, wh) + (mm(xh, wm) + mm(xm, wh)) + (mm(xh, wl) + mm(xm, wm) + mm(xl, wh))) + bf_ref[...]
    log_f = jax.nn.log_sigmoid(logit)
    tt = x.shape[0]
    ti = lax.broadcasted_iota(jnp.int32, (tt, tt), 0)
    tj = lax.broadcasted_iota(jnp.int32, (tt, tt), 1)
    tril = (tj <= ti).astype(BF16)
    c = sum(mm(tril, part) for part in _split3(log_f)) + carry_ref[0:1, :]
    c_ref[...] = c
    carry_ref[...] = jnp.broadcast_to(c[tt - 1:tt, :], carry_ref.shape)


def _fox_proj(x2, batch, seq_len, wq, wk, wv, wf, b_f):
    n_tok, d = x2.shape
    tt = PROJ_TILE
    nt = seq_len // tt
    tile = pl.BlockSpec((tt, d), lambda b, t: (b * nt + t, 0))
    full = lambda arr: pl.BlockSpec(arr.shape, lambda b, t: (0,) * arr.ndim)
    return pl.pallas_call(
        _fox_proj_kernel,
        grid=(batch, nt),
        in_specs=[tile, full(wq), full(wk), full(wv), full(wf), full(b_f)],
        out_specs=[tile, tile, tile, pl.BlockSpec((tt, N_HEADS), lambda b, t: (b * nt + t, 0))],
        out_shape=[jax.ShapeDtypeStruct((n_tok, d), BF16)] * 3 + [jax.ShapeDtypeStruct((n_tok, N_HEADS), F32)],
        scratch_shapes=[pltpu.VMEM((8, N_HEADS), F32)],
        compiler_params=pltpu.CompilerParams(
            dimension_semantics=("parallel", "arbitrary"),
            vmem_limit_bytes=_vmem_limit(2 * 3 * d * d * 2 + 2 * tt * d * (4 + 3 * 2) + 8 * tt * d * 4)),
        name="fox_proj",
    )(x2, wq, wk, wv, wf, b_f)


def _fox_attn_kernel(q_ref, k_ref, v_ref, c_ref, ct_ref, o_ref):
    hp = pl.program_id(1)
    seq_len = q_ref.shape[0]
    tq = ATTN_TILE
    tk = tq
    nq = seq_len // tq
    N = HEAD_DIM
    nt = lambda a, b: lax.dot_general(a, b, (((1,), (1,)), ((), ())), preferred_element_type=F32)
    mm = lambda a, b: jnp.dot(a, b, preferred_element_type=F32)

    lane = lax.broadcasted_iota(jnp.int32, (tq, 2 * N), 1)
    vrow = lax.broadcasted_iota(jnp.int32, (2 * N, tk), 0)
    orow = lax.broadcasted_iota(jnp.int32, (2 * N, tq), 0)
    kpos = lax.broadcasted_iota(jnp.int32, (tk, tq), 0)
    qpos = lax.broadcasted_iota(jnp.int32, (tk, tq), 1)
    causal = kpos <= qpos
    ident_v = (lax.broadcasted_iota(jnp.int32, (2 * N, 2 * N), 0)
               == lax.broadcasted_iota(jnp.int32, (2 * N, 2 * N), 1)).astype(BF16)
    ident_q = (kpos == qpos).astype(BF16)

    c_all = c_ref[...]
    head_lane = lax.broadcasted_iota(jnp.int32, c_all.shape, 1)
    ck_col = [jnp.sum(jnp.where(head_lane == 2 * hp + u, c_all, 0.0), axis=-1, keepdims=True) * LOG2E
              for u in range(2)]
    cq_row = [ct_ref[0, pl.ds(2 * hp + u, 1), :] * LOG2E for u in range(2)]

    m = [[None, None] for _ in range(nq)]
    l = [[None, None] for _ in range(nq)]
    acc = [None] * nq

    for j in range(nq):
        ks = slice(j * tk, (j + 1) * tk)
        k_j = k_ref[ks, :]
        v_t = nt(ident_v, v_ref[ks, :]).astype(BF16)
        v_heads = [jnp.where(vrow < N, v_t, jnp.zeros_like(v_t)), jnp.where(vrow < N, jnp.zeros_like(v_t), v_t)]
        ck_b = [jnp.broadcast_to(ck_col[u][ks, :], (tk, tq)) for u in range(2)]
        for q0 in range(j, nq, 2):
            chains = [(qi, u) for qi in range(q0, min(q0 + 2, nq)) for u in range(2)]
            qs_ = {qi: q_ref[qi * tq:(qi + 1) * tq, :] for qi, _ in chains}
            zero = jnp.zeros((tq, 2 * N), BF16)
            q_heads = {(qi, u): jnp.where((lane < N) == (u == 0), qs_[qi], zero) for qi, u in chains}
            t = {ch: nt(k_j, q_heads[ch]) - ck_b[ch[1]] for ch in chains}
            for ch in chains:
                if ch[0] == j:
                    t[ch] = jnp.where(causal, t[ch], -jnp.inf)
            cq = {ch: cq_row[ch[1]][:, ch[0] * tq:(ch[0] + 1) * tq] for ch in chains}
            rmax = {ch: jnp.max(t[ch], axis=0, keepdims=True) + cq[ch] for ch in chains}
            if j == 0:
                m_new = rmax
            else:
                m_new = {ch: jnp.maximum(m[ch[0]][ch[1]], rmax[ch]) for ch in chains}
            p = {ch: jnp.exp2(t[ch] + (cq[ch] - m_new[ch])) for ch in chains}
            psum = {ch: jnp.sum(p[ch], axis=0, keepdims=True) for ch in chains}
            for qi in sorted({qi for qi, _ in chains}):
                pv = mm(v_heads[0], p[(qi, 0)].astype(BF16)) + mm(v_heads[1], p[(qi, 1)].astype(BF16))
                if j == 0:
                    acc[qi] = pv
                    for u in range(2):
                        l[qi][u] = psum[(qi, u)]
                else:
                    alpha = [jnp.exp2(m[qi][u] - m_new[(qi, u)]) for u in range(2)]
                    acc[qi] = acc[qi] * jnp.where(orow < N, alpha[0], alpha[1]) + pv
                    for u in range(2):
                        l[qi][u] = alpha[u] * l[qi][u] + psum[(qi, u)]
                for u in range(2):
                    m[qi][u] = m_new[(qi, u)]
        o_t = (acc[j] / jnp.where(orow < N, l[j][0], l[j][1])).astype(BF16)
        o_ref[j * tq:(j + 1) * tq, :] = nt(ident_q, o_t).astype(o_ref.dtype)


def _fox_attn(q, k, v, c, c_t, batch, seq_len):
    n_tok, d = q.shape
    pair = 2 * HEAD_DIM
    seq = pl.BlockSpec((seq_len, pair), lambda b, hp: (b, hp))
    return pl.pallas_call(
        _fox_attn_kernel,
        grid=(batch, d // pair),
        in_specs=[seq, seq, seq,
                  pl.BlockSpec((seq_len, N_HEADS), lambda b, hp: (b, 0)),
                  pl.BlockSpec((1, N_HEADS, seq_len), lambda b, hp: (b, 0, 0))],
        out_specs=seq,
        out_shape=jax.ShapeDtypeStruct((n_tok, d), BF16),
        compiler_params=pltpu.CompilerParams(
            dimension_semantics=("parallel", "arbitrary"),
            vmem_limit_bytes=_vmem_limit(2 * 4 * seq_len * pair * 2 + 2 * seq_len * V7X_LANES * 4
                                         + 64 * ATTN_TILE * ATTN_TILE * 4)),
        name="fox_attn",
    )(q, k, v, c, c_t)


def kernel(x, rw_mix, rw_wr, rw_wk, rw_wv, rw_wo, rw_w0, rw_w1, rw_w2, rw_a0, rw_a1, rw_a2, rw_g1, rw_g2,
           rw_kk, rw_ka, rw_rk, rw_gn_g, rw_gn_b, fx_w_in, fx_b_f, fx_wo, router_w, router_bias,
           moe_w_gate, moe_w_up, moe_w_down, ln_g, ln_b):
    batch, seq_len, d = x.shape
    n_tok = batch * seq_len
    bf = lambda w: w.astype(BF16)
    row = lambda w: w.reshape(1, -1)
    router_wt = router_w.T
    router_b = router_bias.reshape(N_EXPERTS, 1)
    h = x.reshape(n_tok, d)

    for i in range(DEPTH):
        j = i // 2
        if i % 2 == 0:
            r, lw, k, v, a, g = _rwkv_proj(
                h, seq_len, rw_mix[j], bf(rw_wr[j]), bf(rw_wk[j]), bf(rw_wv[j]), bf(rw_w1[j]), bf(rw_w2[j]),
                bf(rw_a1[j]), bf(rw_a2[j]), bf(rw_g1[j]), bf(rw_g2[j]), row(rw_w0[j]), row(rw_a0[j]))
            act = _rwkv_recur(r, lw, k, v, a, g, batch, seq_len, row(rw_kk[j]), row(rw_ka[j]), row(rw_rk[j]),
                              row(rw_gn_g[j]), row(rw_gn_b[j]))
            wo = bf(rw_wo[j])
        else:
            w_in = fx_w_in[j]
            scale = HEAD_DIM ** -0.5
            q, k, v, c = _fox_proj(h, batch, seq_len, bf(w_in[:, :d] * scale), bf(w_in[:, d:2 * d]),
                                   bf(w_in[:, 2 * d:3 * d]), w_in[:, 3 * d:], row(fx_b_f[j]))
            c_t = c.reshape(batch, seq_len, N_HEADS).transpose(0, 2, 1)
            act = _fox_attn(q, k, v, c, c_t, batch, seq_len)
            wo = bf(fx_wo[j])
        h, eidx, rank, gcol, counts = _mixer_epilogue(act, wo, h, row(ln_g[i, 0]), row(ln_b[i, 0]),
                                                      router_wt, router_b)
        h = _moe_layer(h, eidx, rank, counts, gcol, bf(moe_w_gate[i]), bf(moe_w_up[i]), bf(moe_w_down[i]),
                       row(ln_g[i, 1]), row(ln_b[i, 1]))
    return h.reshape(batch, seq_len, d)
```

```python
import functools
import math

import jax
import jax.numpy as jnp
from jax import lax
from jax.experimental import pallas as pl
from jax.experimental.pallas import tpu as pltpu

D_MODEL = 1024
HEAD_DIM = 64
N_HEADS = D_MODEL // HEAD_DIM
N_EXPERTS = 16
N_GROUPS = 4
EXPERTS_PER_GROUP = N_EXPERTS // N_GROUPS
D_EXPERT = 512
GN_EPS = 64e-5
LN_EPS = 1e-5
DEPTH = 2
ALPHA = (2 * DEPTH) ** 0.25
LOG2E = math.log2(math.e)

V7X_LANES = 128
V7X_VMEM_BYTES = 64 * 2 ** 20

V7X_MXU_DIM = 256

RWKV_CHUNK = 64
RWKV_HEADS_PER_TILE = V7X_MXU_DIM // HEAD_DIM
PROJ_TILE = 256
EPI_TILE = 1024
EPI_SUBTILE = 256
ATTN_TILE = 256
MOE_BLOCK = 256
DISPATCH_TILE = 1024
COMBINE_TILE = 256
DMA_ISSUE_UNROLL = 8

F32 = jnp.float32
BF16 = jnp.bfloat16


def _vmem_limit(n_bytes):
    return int(min(n_bytes + 16 * 2 ** 20, V7X_VMEM_BYTES - 8 * 2 ** 20))


def _bdot(a, b):
    return jnp.dot(a.astype(BF16), b.astype(BF16), preferred_element_type=F32)


def _split3(x):
    hi = x.astype(BF16)
    r1 = x - hi.astype(F32)
    mid = r1.astype(BF16)
    lo = (r1 - mid.astype(F32)).astype(BF16)
    return hi, mid, lo


def _layer_norm_rows(x, g, b):
    mu = jnp.mean(x, axis=-1, keepdims=True)
    xc = x - mu
    var = jnp.mean(xc * xc, axis=-1, keepdims=True)
    return xc * lax.rsqrt(var + LN_EPS) * g + b


def _rwkv_proj_kernel(x_ref, xp_ref, mix_ref, wr_ref, wk_ref, wv_ref, w1_ref, w2_ref, a1_ref, a2_ref,
                      g1_ref, g2_ref, w0_ref, a0_ref,
                      r_ref, lw_ref, k_ref, v_ref, a_ref, g_ref, *, tiles_per_seq):
    i = pl.program_id(0)
    x = x_ref[...]
    tt = x.shape[0]
    first = (i % tiles_per_seq) == 0
    prev_row = jnp.where(first, 0.0, xp_ref[7:8, :])
    row = lax.broadcasted_iota(jnp.int32, (tt, 1), 0)
    xprev = jnp.where(row == 0, prev_row, pltpu.roll(x, 1, axis=0))
    xx = xprev - x
    mix = mix_ref[...]
    xr, xw, xk, xv, xa, xg = (x + xx * mix[j:j + 1, :] for j in range(6))
    w_mid = _bdot(xw, w1_ref[...])
    a_mid = _bdot(xa, a1_ref[...])
    g_mid = _bdot(xg, g1_ref[...])
    r_ref[...] = _bdot(xr, wr_ref[...])
    k_ref[...] = _bdot(xk, wk_ref[...])
    v_ref[...] = _bdot(xv, wv_ref[...])
    z = w0_ref[...] + _bdot(jnp.tanh(w_mid), w2_ref[...])
    a_ref[...] = jax.nn.sigmoid(a0_ref[...] + _bdot(a_mid, a2_ref[...]))
    g_ref[...] = _bdot(jax.nn.sigmoid(g_mid), g2_ref[...])
    w = -jax.nn.softplus(-z) - 0.5
    lw_ref[...] = -jnp.exp(w)


def _rwkv_proj(x2, seq_len, mix, wr, wk, wv, w1, w2, a1, a2, g1, g2, w0, a0):
    n_tok, d = x2.shape
    tt = PROJ_TILE
    n_tiles = n_tok // tt
    tiles_per_seq = seq_len // tt
    tile = pl.BlockSpec((tt, d), lambda i: (i, 0))
    prev = pl.BlockSpec((8, d), lambda i: (jnp.maximum(i * (tt // 8) - 1, 0), 0))
    full = lambda arr: pl.BlockSpec(arr.shape, lambda i: (0,) * arr.ndim)
    weights = (mix, wr, wk, wv, w1, w2, a1, a2, g1, g2, w0, a0)
    out_sds = jax.ShapeDtypeStruct((n_tok, d), F32)
    w_bytes = sum(int(w.size) * w.dtype.itemsize for w in weights)
    return pl.pallas_call(
        functools.partial(_rwkv_proj_kernel, tiles_per_seq=tiles_per_seq),
        grid=(n_tiles,),
        in_specs=[tile, prev] + [full(w) for w in weights],
        out_specs=[tile] * 6,
        out_shape=[out_sds] * 6,
        compiler_params=pltpu.CompilerParams(
            dimension_semantics=("parallel",),
            vmem_limit_bytes=_vmem_limit(2 * w_bytes + 2 * 8 * tt * d * 4)),
        name="rwkv_proj",
    )(x2, x2, *weights)


def _rwkv_recur_kernel(r_ref, lw_ref, k_ref, v_ref, a_ref, g_ref, kk_ref, ka_ref, rk_ref, gng_ref, gnb_ref,
                       o_ref, s_ref):
    c = pl.program_id(1)
    C = r_ref.shape[0]
    N = HEAD_DIM

    @pl.when(c == 0)
    def _():
        s_ref[...] = jnp.zeros_like(s_ref)

    lw = lw_ref[...]
    ti = lax.broadcasted_iota(jnp.int32, (C, C), 0)
    tj = lax.broadcasted_iota(jnp.int32, (C, C), 1)
    tril = (tj <= ti).astype(BF16)
    cum = sum(jnp.dot(tril, part, preferred_element_type=F32) for part in _split3(lw))
    rho = cum[C // 2 - 1:C // 2, :]
    last = cum[C - 1:C, :]
    e_q = jnp.exp(cum - rho)
    e_qx = jnp.exp(cum - lw - rho)
    e_k = jnp.exp(rho - cum)
    e_end = jnp.exp(last - cum)
    e_rho = jnp.exp(rho)
    d_end = jnp.exp(last)

    r = r_ref[...]
    k = k_ref[...]
    v = v_ref[...]
    a = a_ref[...]
    kk_raw = k * kk_ref[...]
    k_mod = k * (1.0 + (a - 1.0) * ka_ref[...])
    rkk = r * k_mod * rk_ref[...]

    G = RWKV_HEADS_PER_TILE
    R = G * C
    GW = G * N
    er = lax.broadcasted_iota(jnp.int32, (R, GW), 0)
    ec = lax.broadcasted_iota(jnp.int32, (R, GW), 1)
    blk = (er // C) == (ec // N)
    strict = (ec % C) < (er % C)
    incl = (ec % C) <= (er % C)
    eye = (er == ec).astype(F32)
    ident = (er == ec).astype(BF16)
    ones_blk = blk.astype(BF16)

    def expand(x):
        return jnp.where(blk, jnp.concatenate([x] * G, axis=0), 0.0).astype(BF16)

    def head_sums(xs):
        parts = _split3(jnp.concatenate(xs, axis=0))
        tot = jnp.dot(jnp.concatenate(parts, axis=0), ones_blk, preferred_element_type=F32)
        n = len(xs) * C
        tot = tot[:n] + tot[n:2 * n] + tot[2 * n:]
        return [tot[u * C:(u + 1) * C] for u in range(len(xs))]

    mm = lambda p, q: jnp.dot(p, q, preferred_element_type=F32)
    nt = lambda p, q: lax.dot_general(p, q, (((1,), (1,)), ((), ())), preferred_element_type=F32)

    n_grp = N_HEADS // G
    groups = [slice(gi * GW, (gi + 1) * GW) for gi in range(n_grp)]
    pre = head_sums([kk_raw[:, sl] * kk_raw[:, sl] for sl in groups] + [rkk[:, sl] for sl in groups])
    kk_ss, rkk_sum = pre[:n_grp], pre[n_grp:]
    each = lambda fn, *lists: [fn(*args) for args in zip(*lists)]
    bf = lambda x: x.astype(BF16)
    kk_n = each(lambda sl, ss: kk_raw[:, sl] * lax.rsqrt(jnp.maximum(ss, 1e-24)), groups, kk_ss)
    b_n = each(lambda sl, kk_g: kk_g * a[:, sl], groups, kk_n)
    kk_q = each(lambda sl, kk_g: kk_g * e_qx[:, sl], groups, kk_n)
    r_q = each(lambda sl: r[:, sl] * e_q[:, sl], groups)
    q2 = each(lambda x, y: jnp.concatenate([expand(x), expand(y)], axis=0), kk_q, r_q)
    a_k = each(lambda sl, q: nt(q, expand(k_mod[:, sl] * e_k[:, sl])), groups, q2)
    a_b = each(lambda sl, q, b_g: nt(q, expand(b_g * e_k[:, sl])), groups, q2, b_n)
    a_kk = each(lambda x: jnp.where(strict, x[:R], 0.0), a_k)
    a_rk = each(lambda x: jnp.where(incl, x[R:], 0.0), a_k)
    a_kb = each(lambda x: jnp.where(strict, x[:R], 0.0), a_b)
    a_rb = each(lambda x: jnp.where(incl, x[R:], 0.0), a_b)
    t_inv = each(lambda x: eye - x, a_kb)
    p = each(lambda x: mm(bf(-x), bf(-x)), a_kb)
    for _ in range(int(math.log2(C)) - 2):
        both = each(lambda pg, tg: mm(bf(pg), jnp.concatenate([bf(pg), bf(tg)], axis=1)), p, t_inv)
        p = each(lambda x: x[:, :R], both)
        t_inv = each(lambda tg, x: tg + x[:, R:], t_inv, both)
    t_inv = each(lambda pg, tg: tg + mm(bf(pg), bf(tg)), p, t_inv)
    kd_t = each(lambda sl: bf(nt(ident, expand(k_mod[:, sl] * e_end[:, sl]))), groups)
    bd_t = each(lambda sl, b_g: bf(nt(ident, expand(b_g * e_end[:, sl]))), groups, b_n)

    def decay_column(sl):
        rows = jnp.concatenate([part.astype(F32) for part in _split3(d_end[:, sl])] + [jnp.zeros((5, GW), F32)], axis=0)
        d_t = nt(ident, bf(rows))
        return d_t[:, 0:1] + d_t[:, 1:2] + d_t[:, 2:3]

    d_col = each(decay_column, groups)
    v_e = each(lambda sl: expand(v[:, sl]), groups)
    av = each(lambda x, y, ve: mm(bf(jnp.concatenate([x, y], axis=0)), ve), a_kk, a_rk, v_e)
    q2_abs = each(lambda sl, x, y: jnp.concatenate([expand(x * e_rho[:, sl]), expand(y * e_rho[:, sl])], axis=0),
                  groups, kk_q, r_q)
    st = [s_ref[gi] for gi in range(n_grp)]
    qs = each(lambda q, s: mm(q, bf(s)), q2_abs, st)
    sa_e = each(lambda tg, q, x: bf(mm(bf(tg), bf(q[:R] + x[:R]))), t_inv, qs, av)
    y_e = each(lambda q, x, arb, sa: q[R:] + x[R:] - mm(bf(arb), sa), qs, av, a_rb, sa_e)
    ys = each(lambda x: sum(x[u * C:(u + 1) * C] for u in range(G)), y_e)
    upd = each(lambda kt, bt, ve, sa: mm(jnp.concatenate([kt, -bt], axis=1), jnp.concatenate([ve, sa], axis=0)),
               kd_t, bd_t, v_e, sa_e)
    for gi in range(n_grp):
        s_ref[gi] = st[gi] * d_col[gi] + upd[gi]

    inv_n = 1.0 / N
    ycs = [y - mu * inv_n for y, mu in zip(ys, head_sums(ys))]
    for sl, yc, sq, bsum in zip(groups, ycs, head_sums([yc * yc for yc in ycs]), rkk_sum):
        yn = yc * lax.rsqrt(sq * inv_n + GN_EPS) * gng_ref[:, sl] + gnb_ref[:, sl]
        o_ref[:, sl] = ((yn + bsum * v[:, sl]) * g_ref[:, sl]).astype(o_ref.dtype)


def _rwkv_recur(r, lw, k, v, a, g, batch, seq_len, k_k, k_a, r_k, gn_g, gn_b):
    n_tok, d = r.shape
    C = RWKV_CHUNK
    nc = seq_len // C
    tile = pl.BlockSpec((C, d), lambda b, c: (b * nc + c, 0))
    vec = pl.BlockSpec((1, d), lambda b, c: (0, 0))
    return pl.pallas_call(
        _rwkv_recur_kernel,
        grid=(batch, nc),
        in_specs=[tile] * 6 + [vec] * 5,
        out_specs=tile,
        out_shape=jax.ShapeDtypeStruct((n_tok, d), BF16),
        scratch_shapes=[pltpu.VMEM((N_HEADS // RWKV_HEADS_PER_TILE, V7X_MXU_DIM, V7X_MXU_DIM), F32)],
        compiler_params=pltpu.CompilerParams(
            dimension_semantics=("parallel", "arbitrary"),
            vmem_limit_bytes=_vmem_limit(2 * 7 * C * d * 4 + 32 * C * d * 4 + 64 * V7X_MXU_DIM ** 2 * 4)),
        name="rwkv_recur",
    )(r, lw, k, v, a, g, k_k, k_a, r_k, gn_g, gn_b)


def _rank_among(vals, i):
    cnt = 0
    for j, vj in enumerate(vals):
        if j == i:
            continue
        before = (vj >= vals[i]) if j < i else (vj > vals[i])
        cnt = cnt + before.astype(jnp.int32)
    return cnt


def _pick(ranks, vals, want):
    out = vals[0]
    for rk, vl in zip(ranks[1:], vals[1:]):
        out = jnp.where(rk == want, vl, out)
    return out


def _epilogue_kernel(act_ref, wo_ref, res_ref, lng_ref, lnb_ref, rwt_ref, rb_ref,
                     h_ref, eidx_ref, rank_ref, gcol_ref, cnt_ref, base_ref):
    i = pl.program_id(0)

    @pl.when(i == 0)
    def _():
        base_ref[...] = jnp.zeros_like(base_ref)

    tt = EPI_SUBTILE
    subs = [slice(u * tt, (u + 1) * tt) for u in range(act_ref.shape[0] // tt)]
    each = lambda fn, *lists: [fn(*args) for args in zip(*lists)]
    nt = lambda a, b: lax.dot_general(a, b, (((1,), (1,)), ((), ())), preferred_element_type=F32)

    mixed = each(lambda sl: jnp.dot(act_ref[sl, :], wo_ref[...], preferred_element_type=F32), subs)
    h = each(lambda sl, mx: _layer_norm_rows(ALPHA * res_ref[sl, :] + mx, lng_ref[...], lnb_ref[...]), subs, mixed)
    for sl, hs in zip(subs, h):
        h_ref[sl, :] = hs

    wh, wm, wl = _split3(rwt_ref[...])

    def router_logits(hs):
        hh, hm, hl = _split3(hs)
        return nt(wh, hh) + (nt(wh, hm) + nt(wm, hh)) + (nt(wh, hl) + nt(wm, hm) + nt(wl, hh))

    s = each(lambda hs: jax.nn.sigmoid(router_logits(hs)), h)

    def select(sg):
        s_sel = sg + rb_ref[...]
        rows = [s_sel[e:e + 1, :] for e in range(N_EXPERTS)]
        grp_score, grp_i0, grp_i1 = [], [], []
        for gi in range(N_GROUPS):
            vals = rows[gi * EXPERTS_PER_GROUP:(gi + 1) * EXPERTS_PER_GROUP]
            ranks = [_rank_among(vals, q) for q in range(EXPERTS_PER_GROUP)]
            idx = [jnp.full_like(ranks[0], q) for q in range(EXPERTS_PER_GROUP)]
            grp_score.append(_pick(ranks, vals, 0) + _pick(ranks, vals, 1))
            grp_i0.append(_pick(ranks, idx, 0))
            grp_i1.append(_pick(ranks, idx, 1))
        g_ranks = [_rank_among(grp_score, q) for q in range(N_GROUPS)]
        gidx = [jnp.full_like(g_ranks[0], q) for q in range(N_GROUPS)]
        g_star = _pick(g_ranks, gidx, 0)
        e0 = g_star * EXPERTS_PER_GROUP + _pick(g_ranks, grp_i0, 0)
        e1 = g_star * EXPERTS_PER_GROUP + _pick(g_ranks, grp_i1, 0)
        return e0, e1

    picked = each(select, s)
    e_iota = lax.broadcasted_iota(jnp.int32, (N_EXPERTS, tt), 0)
    hit0 = each(lambda pk: e_iota == pk[0], picked)
    hit1 = each(lambda pk: e_iota == pk[1], picked)

    def gates(sg, h0, h1):
        gate0 = jnp.sum(jnp.where(h0, sg, 0.0), axis=0, keepdims=True)
        gate1 = jnp.sum(jnp.where(h1, sg, 0.0), axis=0, keepdims=True)
        denom = gate0 + gate1
        return gate0 / denom, gate1 / denom

    gate = each(gates, s, hit0, hit1)

    member = each(lambda h0, h1: jnp.where(h0 | h1, 1.0, 0.0), hit0, hit1)
    ui = lax.broadcasted_iota(jnp.int32, (tt, tt), 0)
    uj = lax.broadcasted_iota(jnp.int32, (tt, tt), 1)
    before = (ui < uj).astype(BF16)
    prefix = each(lambda mb: jnp.dot(mb.astype(BF16), before, preferred_element_type=F32), member)
    base = base_ref[:, 0:1]
    for u, sl in enumerate(subs):
        pre = prefix[u] + base
        rank0 = jnp.sum(jnp.where(hit0[u], pre, 0.0), axis=0, keepdims=True)
        rank1 = jnp.sum(jnp.where(hit1[u], pre, 0.0), axis=0, keepdims=True)
        eidx_ref[:, sl] = jnp.concatenate(picked[u], axis=0)
        rank_ref[:, sl] = jnp.concatenate([rank0, rank1], axis=0).astype(jnp.int32)
        base = base + jnp.sum(member[u], axis=1, keepdims=True)
    base_ref[...] = jnp.broadcast_to(base, base_ref.shape)
    cnt_ref[...] = jnp.broadcast_to(base, cnt_ref.shape).astype(jnp.int32)

    ident = (ui == uj).astype(BF16)

    def gate_columns(gt):
        gpad = jnp.concatenate([gt[0], gt[1], jnp.zeros((V7X_LANES - 2, tt), F32)], axis=0)
        return sum(nt(ident, part) for part in _split3(gpad))

    for sl, gc in zip(subs, each(gate_columns, gate)):
        gcol_ref[sl, :] = gc


def _mixer_epilogue(act, wo, res, ln_g, ln_b, router_wt, router_bias):
    n_tok, d = res.shape
    k_in = act.shape[1]
    tt = EPI_TILE
    n_tiles = n_tok // tt
    full = lambda arr: pl.BlockSpec(arr.shape, lambda i: (0,) * arr.ndim)
    return pl.pallas_call(
        _epilogue_kernel,
        grid=(n_tiles,),
        in_specs=[pl.BlockSpec((tt, k_in), lambda i: (i, 0)), full(wo), pl.BlockSpec((tt, d), lambda i: (i, 0)),
                  full(ln_g), full(ln_b), full(router_wt), full(router_bias)],
        out_specs=[pl.BlockSpec((tt, d), lambda i: (i, 0)),
                   pl.BlockSpec((2, tt), lambda i: (0, i)),
                   pl.BlockSpec((2, tt), lambda i: (0, i)),
                   pl.BlockSpec((tt, V7X_LANES), lambda i: (i, 0)),
                   pl.BlockSpec((N_EXPERTS, V7X_LANES), lambda i: (0, 0))],
        out_shape=[jax.ShapeDtypeStruct((n_tok, d), F32),
                   jax.ShapeDtypeStruct((2, n_tok), jnp.int32),
                   jax.ShapeDtypeStruct((2, n_tok), jnp.int32),
                   jax.ShapeDtypeStruct((n_tok, V7X_LANES), F32),
                   jax.ShapeDtypeStruct((N_EXPERTS, V7X_LANES), jnp.int32)],
        scratch_shapes=[pltpu.VMEM((N_EXPERTS, V7X_LANES), F32)],
        compiler_params=pltpu.CompilerParams(
            dimension_semantics=("arbitrary",),
            vmem_limit_bytes=_vmem_limit(2 * int(wo.size) * 2 + 2 * tt * (k_in * 2 + 2 * d * 4) + 16 * tt * d * 4)),
        name="mixer_epilogue",
    )(act, wo, res, ln_g, ln_b, router_wt, router_bias)


def _dispatch_kernel(dest_ref, h_ref, xs_in_ref, xs_ref, sem):
    del xs_in_ref
    tt = h_ref.shape[0]

    def row_copy(t, slot):
        d = dest_ref[0, 0, slot * tt + t]
        return pltpu.make_async_copy(h_ref.at[pl.ds(t, 1)], xs_ref.at[pl.ds(d, 1)], sem)

    def issue(t0, carry):
        for u in range(DMA_ISSUE_UNROLL):
            t = t0 * DMA_ISSUE_UNROLL + u
            row_copy(t, 0).start(priority=0)
            row_copy(t, 1).start(priority=1)
        return carry

    lax.fori_loop(0, tt // DMA_ISSUE_UNROLL, issue, 0)
    for _ in range(2):
        pltpu.make_async_copy(h_ref, xs_ref.at[pl.ds(0, tt)], sem).wait()


def _moe_dispatch(h, dest_tiles, n_rows):
    n_tok, d = h.shape
    tt = dest_tiles.shape[2] // 2
    n_tiles = n_tok // tt
    zeros = jnp.zeros((n_rows, d), h.dtype)
    return pl.pallas_call(
        _dispatch_kernel,
        grid=(n_tiles,),
        in_specs=[pl.BlockSpec((1, 1, 2 * tt), lambda i: (i, 0, 0), memory_space=pltpu.SMEM),
                  pl.BlockSpec((tt, d), lambda i: (i, 0)),
                  pl.BlockSpec(memory_space=pl.ANY)],
        out_specs=pl.BlockSpec(memory_space=pl.ANY),
        out_shape=jax.ShapeDtypeStruct((n_rows, d), h.dtype),
        scratch_shapes=[pltpu.SemaphoreType.DMA(())],
        input_output_aliases={2: 0},
        compiler_params=pltpu.CompilerParams(dimension_semantics=("arbitrary",),
                                             vmem_limit_bytes=_vmem_limit(2 * tt * d * 4)),
        name="moe_dispatch",
    )(dest_tiles, h, zeros)


def _ffn_kernel(bexp_ref, nblk_ref, x_ref, wg_ref, wu_ref, wd_ref, y_ref):
    del bexp_ref
    j = pl.program_id(0)

    @pl.when(j < nblk_ref[0])
    def _():
        x = x_ref[...].astype(BF16)
        gate = jnp.dot(x, wg_ref[0], preferred_element_type=F32)
        up = jnp.dot(x, wu_ref[0], preferred_element_type=F32)
        hid = (gate * jax.nn.sigmoid(gate)) * up
        y_ref[...] = jnp.dot(hid.astype(BF16), wd_ref[0], preferred_element_type=F32)

    @pl.when(j >= nblk_ref[0])
    def _():
        y_ref[...] = jnp.zeros_like(y_ref)


def _moe_ffn(xs, block_exp, n_used, wg, wu, wd):
    n_rows, d = xs.shape
    blk = MOE_BLOCK
    n_blocks = n_rows // blk
    de = wg.shape[2]
    grid_spec = pltpu.PrefetchScalarGridSpec(
        num_scalar_prefetch=2,
        grid=(n_blocks,),
        in_specs=[pl.BlockSpec((blk, d), lambda j, be, nb: (j, 0)),
                  pl.BlockSpec((1, d, de), lambda j, be, nb: (be[j], 0, 0)),
                  pl.BlockSpec((1, d, de), lambda j, be, nb: (be[j], 0, 0)),
                  pl.BlockSpec((1, de, d), lambda j, be, nb: (be[j], 0, 0))],
        out_specs=pl.BlockSpec((blk, d), lambda j, be, nb: (j, 0)),
    )
    return pl.pallas_call(
        _ffn_kernel,
        grid_spec=grid_spec,
        out_shape=jax.ShapeDtypeStruct((n_rows, d), F32),
        compiler_params=pltpu.CompilerParams(
            dimension_semantics=("arbitrary",),
            vmem_limit_bytes=_vmem_limit(2 * 3 * d * de * 2 + 4 * blk * d * 4 + 4 * blk * de * 4)),
        name="moe_ffn",
    )(block_exp, n_used, xs, wg, wu, wd)


def _combine_kernel(dest_ref, dest_next_ref, ys_ref, gcol_ref, res_ref, lng_ref, lnb_ref, o_ref, buf_ref, sem):
    i = pl.program_id(0)
    n = pl.num_programs(0)
    tt = res_ref.shape[0]
    cur = i % 2

    def gather(idx_ref, buf):
        def row_copy(t, slot):
            d = idx_ref[0, 0, slot * tt + t]
            return pltpu.make_async_copy(ys_ref.at[pl.ds(d, 1)], buf_ref.at[buf, slot, pl.ds(t, 1)], sem.at[buf])

        def issue(t0, carry):
            for u in range(DMA_ISSUE_UNROLL):
                t = t0 * DMA_ISSUE_UNROLL + u
                row_copy(t, 0).start(priority=0)
                row_copy(t, 1).start(priority=1)
            return carry

        lax.fori_loop(0, tt // DMA_ISSUE_UNROLL, issue, 0)

    @pl.when(i == 0)
    def _():
        gather(dest_ref, cur)

    @pl.when(i + 1 < n)
    def _():
        gather(dest_next_ref, 1 - cur)

    for slot in range(2):
        pltpu.make_async_copy(ys_ref.at[pl.ds(0, tt)], buf_ref.at[cur, slot], sem.at[cur]).wait()

    gcol = gcol_ref[...]
    ffn = buf_ref[cur, 0] * gcol[:, 0:1] + buf_ref[cur, 1] * gcol[:, 1:2]
    o_ref[...] = _layer_norm_rows(ALPHA * res_ref[...] + ffn, lng_ref[...], lnb_ref[...])


def _moe_combine(ys, dest_tiles, gcol, res, ln_g, ln_b):
    n_tok, d = res.shape
    tt = dest_tiles.shape[2] // 2
    n_tiles = n_tok // tt
    full = lambda arr: pl.BlockSpec(arr.shape, lambda i: (0,) * arr.ndim)
    return pl.pallas_call(
        _combine_kernel,
        grid=(n_tiles,),
        in_specs=[pl.BlockSpec((1, 1, 2 * tt), lambda i: (i, 0, 0), memory_space=pltpu.SMEM),
                  pl.BlockSpec((1, 1, 2 * tt), lambda i: (jnp.minimum(i + 1, n_tiles - 1), 0, 0),
                               memory_space=pltpu.SMEM),
                  pl.BlockSpec(memory_space=pl.ANY),
                  pl.BlockSpec((tt, V7X_LANES), lambda i: (i, 0)),
                  pl.BlockSpec((tt, d), lambda i: (i, 0)),
                  full(ln_g), full(ln_b)],
        out_specs=pl.BlockSpec((tt, d), lambda i: (i, 0)),
        out_shape=jax.ShapeDtypeStruct((n_tok, d), F32),
        scratch_shapes=[pltpu.VMEM((2, 2, tt, d), F32), pltpu.SemaphoreType.DMA((2,))],
        compiler_params=pltpu.CompilerParams(
            dimension_semantics=("arbitrary",),
            vmem_limit_bytes=_vmem_limit(4 * tt * d * 4 + 2 * 3 * tt * d * 4)),
        name="moe_combine",
    )(dest_tiles, dest_tiles, ys, gcol, res, ln_g, ln_b)


def _moe_layer(h, eidx, rank, counts, gcol, wg, wu, wd, ln_g, ln_b):
    n_tok, d = h.shape
    blk = MOE_BLOCK
    n_rows = (2 * n_tok + N_EXPERTS * (blk - 1) + blk - 1) // blk * blk
    n_blocks = n_rows // blk
    counts = counts[:, 0]
    padded = (counts + blk - 1) // blk * blk
    pend = jnp.cumsum(padded)
    pstart = pend - padded
    onehot = (eidx[:, :, None] == jnp.arange(N_EXPERTS, dtype=jnp.int32)).astype(jnp.int32)
    dest = rank + jnp.sum(onehot * pstart, axis=-1)
    tiles = lambda tt: dest.reshape(2, n_tok // tt, tt).transpose(1, 0, 2).reshape(n_tok // tt, 1, 2 * tt)
    blk_start = jnp.arange(n_blocks, dtype=jnp.int32) * blk
    block_exp = jnp.minimum(jnp.sum((blk_start[:, None] >= pend[None, :]).astype(jnp.int32), axis=1),
                            N_EXPERTS - 1).astype(jnp.int32)
    n_used = (pend[-1:] // blk).astype(jnp.int32)
    xs = _moe_dispatch(h, tiles(DISPATCH_TILE), n_rows)
    ys = _moe_ffn(xs, block_exp, n_used, wg, wu, wd)
    return _moe_combine(ys, tiles(COMBINE_TILE), gcol, h, ln_g, ln_b)


def _fox_proj_kernel(x_ref, wq_ref, wk_ref, wv_ref, wf_ref, bf_ref, q_ref, k_ref, v_ref, c_ref, carry_ref):
    t = pl.program_id(1)

    @pl.when(t == 0)
    def _():
        carry_ref[...] = jnp.zeros_like(carry_ref)

    x = x_ref[...]
    xh, xm, xl = _split3(x)
    wh, wm, wl = _split3(wf_ref[...])
    mm = lambda a, b: jnp.dot(a, b, preferred_element_type=F32)
    logit = (mm(xh, wh) + (mm(xh, wm) + mm(xm, wh)) + (mm(xh, wl) + mm(xm, wm) + mm(xl, wh))) + bf_ref[...]
    q_ref[...] = (mm(xh, wq_ref[...]) * LOG2E).astype(q_ref.dtype)
    k_ref[...] = mm(xh, wk_ref[...]).astype(k_ref.dtype)
    v_ref[...] = mm(xh, wv_ref[...]).astype(v_ref.dtype)
    log_f = jax.nn.log_sigmoid(logit)
    tt = x.shape[0]
    ti = lax.broadcasted_iota(jnp.int32, (tt, tt), 0)
    tj = lax.broadcasted_iota(jnp.int32, (tt, tt), 1)
    tril = (tj <= ti).astype(BF16)
    c = sum(mm(tril, part) for part in _split3(log_f)) + carry_ref[0:1, :]
    c_ref[...] = c
    carry_ref[...] = jnp.broadcast_to(c[tt - 1:tt, :], carry_ref.shape)


def _fox_proj(x2, batch, seq_len, wq, wk, wv, wf, b_f):
    n_tok, d = x2.shape
    tt = PROJ_TILE
    nt = seq_len // tt
    tile = pl.BlockSpec((tt, d), lambda b, t: (b * nt + t, 0))
    full = lambda arr: pl.BlockSpec(arr.shape, lambda b, t: (0,) * arr.ndim)
    return pl.pallas_call(
        _fox_proj_kernel,
        grid=(batch, nt),
        in_specs=[tile, full(wq), full(wk), full(wv), full(wf), full(b_f)],
        out_specs=[tile, tile, tile, pl.BlockSpec((tt, N_HEADS), lambda b, t: (b * nt + t, 0))],
        out_shape=[jax.ShapeDtypeStruct((n_tok, d), BF16)] * 3 + [jax.ShapeDtypeStruct((n_tok, N_HEADS), F32)],
        scratch_shapes=[pltpu.VMEM((8, N_HEADS), F32)],
        compiler_params=pltpu.CompilerParams(
            dimension_semantics=("parallel", "arbitrary"),
            vmem_limit_bytes=_vmem_limit(2 * 3 * d * d * 2 + 2 * tt * d * (4 + 3 * 2) + 8 * tt * d * 4)),
        name="fox_proj",
    )(x2, wq, wk, wv, wf, b_f)


def _fox_attn_kernel(q_ref, k_ref, v_ref, ct_ref, o_ref):
    hp = pl.program_id(1)
    seq_len = q_ref.shape[0]
    tq = ATTN_TILE
    tk = tq
    nq = seq_len // tq
    N = HEAD_DIM
    nt = lambda a, b: lax.dot_general(a, b, (((1,), (1,)), ((), ())), preferred_element_type=F32)
    mm = lambda a, b: jnp.dot(a, b, preferred_element_type=F32)

    lane = lax.broadcasted_iota(jnp.int32, (tq, 2 * N), 1)
    vrow = lax.broadcasted_iota(jnp.int32, (2 * N, tk), 0)
    orow = lax.broadcasted_iota(jnp.int32, (2 * N, tq), 0)
    kpos = lax.broadcasted_iota(jnp.int32, (tk, tq), 0)
    qpos = lax.broadcasted_iota(jnp.int32, (tk, tq), 1)
    causal = kpos <= qpos
    ident_v = (lax.broadcasted_iota(jnp.int32, (2 * N, 2 * N), 0)
               == lax.broadcasted_iota(jnp.int32, (2 * N, 2 * N), 1)).astype(BF16)
    ident_q = (kpos == qpos).astype(BF16)

    cq_row = [ct_ref[0, pl.ds(2 * hp + u, 1), :] * LOG2E for u in range(2)]

    def key_columns(ks):
        rows = jnp.concatenate([part.astype(F32) for u in range(2) for part in _split3(cq_row[u][:, ks])]
                               + [jnp.zeros((2, tk), F32)], axis=0)
        cols = nt(ident_q, rows.astype(BF16))
        return [cols[:, 3 * u:3 * u + 1] + cols[:, 3 * u + 1:3 * u + 2] + cols[:, 3 * u + 2:3 * u + 3]
                for u in range(2)]

    ck_cols = [key_columns(slice(j * tk, (j + 1) * tk)) for j in range(nq)]
    m = [[None, None] for _ in range(nq)]
    l = [[None, None] for _ in range(nq)]
    acc = [None] * nq

    zero_q = jnp.zeros((tq, 2 * N), BF16)
    keys, values = {}, {}

    def scores(j, qi):
        if j not in keys:
            keys[j] = (k_ref[j * tk:(j + 1) * tk, :], [jnp.broadcast_to(col, (tk, tq)) for col in ck_cols[j]])
        k_j, ck_b = keys[j]
        q = q_ref[qi * tq:(qi + 1) * tq, :]
        t = [nt(k_j, jnp.where((lane < N) == (u == 0), q, zero_q)) - ck_b[u] for u in range(2)]
        return [jnp.where(causal, tu, -jnp.inf) for tu in t] if qi == j else t

    def absorb(j, qi, t):
        if j not in values:
            v_t = nt(ident_v, v_ref[j * tk:(j + 1) * tk, :]).astype(BF16)
            values[j] = [jnp.where(vrow < N, v_t, jnp.zeros_like(v_t)), jnp.where(vrow < N, jnp.zeros_like(v_t), v_t)]
        v_heads = values[j]
        cq = [cq_row[u][:, qi * tq:(qi + 1) * tq] for u in range(2)]
        rmax = [jnp.max(t[u], axis=0, keepdims=True) + cq[u] for u in range(2)]
        m_new = rmax if j == 0 else [jnp.maximum(m[qi][u], rmax[u]) for u in range(2)]
        p = [jnp.exp2(t[u] + (cq[u] - m_new[u])) for u in range(2)]
        psum = [jnp.sum(p[u], axis=0, keepdims=True) for u in range(2)]
        pv = mm(v_heads[0], p[0].astype(BF16)) + mm(v_heads[1], p[1].astype(BF16))
        if j == 0:
            acc[qi] = pv
            l[qi] = psum
        else:
            alpha = [jnp.exp2(m[qi][u] - m_new[u]) for u in range(2)]
            acc[qi] = acc[qi] * jnp.where(orow < N, alpha[0], alpha[1]) + pv
            l[qi] = [alpha[u] * l[qi][u] + psum[u] for u in range(2)]
        m[qi] = m_new
        if j == qi:
            o_t = (acc[qi] / jnp.where(orow < N, l[qi][0], l[qi][1])).astype(BF16)
            o_ref[qi * tq:(qi + 1) * tq, :] = nt(ident_q, o_t).astype(o_ref.dtype)

    pairs = [(j, qi) for j in range(nq) for qi in range(j, nq)]
    t_next = scores(*pairs[0])
    for n, (j, qi) in enumerate(pairs):
        t_cur = t_next
        if n + 1 < len(pairs):
            t_next = scores(*pairs[n + 1])
        absorb(j, qi, t_cur)


def _fox_attn(q, k, v, c_t, batch, seq_len):
    n_tok, d = q.shape
    pair = 2 * HEAD_DIM
    seq = pl.BlockSpec((seq_len, pair), lambda b, hp: (b, hp))
    return pl.pallas_call(
        _fox_attn_kernel,
        grid=(batch, d // pair),
        in_specs=[seq, seq, seq, pl.BlockSpec((1, N_HEADS, seq_len), lambda b, hp: (b, 0, 0))],
        out_specs=seq,
        out_shape=jax.ShapeDtypeStruct((n_tok, d), BF16),
        compiler_params=pltpu.CompilerParams(
            dimension_semantics=("parallel", "arbitrary"),
            vmem_limit_bytes=_vmem_limit(2 * 4 * seq_len * pair * 2 + 64 * ATTN_TILE * ATTN_TILE * 4)),
        name="fox_attn",
    )(q, k, v, c_t)


def kernel(x, rw_mix, rw_wr, rw_wk, rw_wv, rw_wo, rw_w0, rw_w1, rw_w2, rw_a0, rw_a1, rw_a2, rw_g1, rw_g2,
           rw_kk, rw_ka, rw_rk, rw_gn_g, rw_gn_b, fx_w_in, fx_b_f, fx_wo, router_w, router_bias,
           moe_w_gate, moe_w_up, moe_w_down, ln_g, ln_b):
    batch, seq_len, d = x.shape
    n_tok = batch * seq_len
    bf = lambda w: w.astype(BF16)
    row = lambda w: w.reshape(1, -1)
    router_wt = router_w.T
    router_b = router_bias.reshape(N_EXPERTS, 1)
    h = x.reshape(n_tok, d)

    for i in range(DEPTH):
        j = i // 2
        if i % 2 == 0:
            r, lw, k, v, a, g = _rwkv_proj(
                h, seq_len, rw_mix[j], bf(rw_wr[j]), bf(rw_wk[j]), bf(rw_wv[j]), bf(rw_w1[j]), bf(rw_w2[j]),
                bf(rw_a1[j]), bf(rw_a2[j]), bf(rw_g1[j]), bf(rw_g2[j]), row(rw_w0[j]), row(rw_a0[j]))
            act = _rwkv_recur(r, lw, k, v, a, g, batch, seq_len, row(rw_kk[j]), row(rw_ka[j]), row(rw_rk[j]),
                              row(rw_gn_g[j]), row(rw_gn_b[j]))
            wo = bf(rw_wo[j])
        else:
            w_in = fx_w_in[j]
            scale = HEAD_DIM ** -0.5
            q, k, v, c = _fox_proj(h, batch, seq_len, bf(w_in[:, :d] * scale), bf(w_in[:, d:2 * d]),
                                   bf(w_in[:, 2 * d:3 * d]), w_in[:, 3 * d:], row(fx_b_f[j]))
            c_t = c.reshape(batch, seq_len, N_HEADS).transpose(0, 2, 1)
            act = _fox_attn(q, k, v, c_t, batch, seq_len)
            wo = bf(fx_wo[j])
        h, eidx, rank, gcol, counts = _mixer_epilogue(act, wo, h, row(ln_g[i, 0]), row(ln_b[i, 0]),
                                                      router_wt, router_b)
        h = _moe_layer(h, eidx, rank, counts, gcol, bf(moe_w_gate[i]), bf(moe_w_up[i]), bf(moe_w_down[i]),
                       row(ln_g[i, 1]), row(ln_b[i, 1]))
    return h.reshape(batch, seq_len, d)
```

```python
import functools
import math

import jax
import jax.numpy as jnp
from jax import lax
from jax.experimental import pallas as pl
from jax.experimental.pallas import tpu as pltpu

D_MODEL = 1024
HEAD_DIM = 64
N_HEADS = D_MODEL // HEAD_DIM
N_EXPERTS = 16
N_GROUPS = 4
EXPERTS_PER_GROUP = N_EXPERTS // N_GROUPS
D_EXPERT = 512
GN_EPS = 64e-5
LN_EPS = 1e-5
DEPTH = 2
ALPHA = (2 * DEPTH) ** 0.25
LOG2E = math.log2(math.e)

V7X_LANES = 128
V7X_VMEM_BYTES = 64 * 2 ** 20

V7X_MXU_DIM = 256

RWKV_CHUNK = 64
RWKV_HEADS_PER_TILE = V7X_MXU_DIM // HEAD_DIM
PROJ_TILE = 512
PROJ_SUBTILE = 256
EPI_TILE = 1024
EPI_SUBTILE = 256
ATTN_TILE = 256
MOE_BLOCK = 256
DISPATCH_TILE = 1024
COMBINE_TILE = 256
DMA_ISSUE_UNROLL = 8

F32 = jnp.float32
BF16 = jnp.bfloat16


def _vmem_limit(n_bytes):
    return int(min(n_bytes + 16 * 2 ** 20, V7X_VMEM_BYTES - 8 * 2 ** 20))


def _split3(x):
    hi = x.astype(BF16)
    r1 = x - hi.astype(F32)
    mid = r1.astype(BF16)
    lo = (r1 - mid.astype(F32)).astype(BF16)
    return hi, mid, lo


def _layer_norm_rows(x, g, b):
    mu = jnp.mean(x, axis=-1, keepdims=True)
    xc = x - mu
    var = jnp.mean(xc * xc, axis=-1, keepdims=True)
    return xc * lax.rsqrt(var + LN_EPS) * g + b


def _rwkv_proj_kernel(x_ref, xp_ref, mix_ref, wr_ref, wk_ref, wv_ref, w1_ref, w2_ref, a1_ref, a2_ref,
                      g1_ref, g2_ref, w0_ref, a0_ref,
                      r_ref, lw_ref, k_ref, v_ref, a_ref, g_ref, *, tiles_per_seq):
    i = pl.program_id(0)
    x = x_ref[...]
    tt = x.shape[0]
    first = (i % tiles_per_seq) == 0
    prev_row = jnp.where(first, 0.0, xp_ref[7:8, :])
    row = lax.broadcasted_iota(jnp.int32, (tt, 1), 0)
    xprev = jnp.where(row == 0, prev_row, pltpu.roll(x, 1, axis=0))
    xx = xprev - x
    mix = mix_ref[...]
    subs = [slice(u * PROJ_SUBTILE, (u + 1) * PROJ_SUBTILE) for u in range(tt // PROJ_SUBTILE)]
    each = lambda fn, *lists: [fn(*args) for args in zip(*lists)]
    mixed = lambda j: each(lambda sl: (x[sl] + xx[sl] * mix[j:j + 1, :]).astype(BF16), subs)
    xr, xw, xk, xv, xa, xg = (mixed(j) for j in range(6))
    mm = lambda a, w_ref: jnp.dot(a, w_ref[...], preferred_element_type=F32)
    w_mid = each(lambda a: mm(a, w1_ref), xw)
    a_mid = each(lambda a: mm(a, a1_ref), xa)
    g_mid = each(lambda a: mm(a, g1_ref), xg)
    for sl, a in zip(subs, xr):
        r_ref[sl, :] = mm(a, wr_ref).astype(r_ref.dtype)
    for sl, a in zip(subs, xk):
        k_ref[sl, :] = mm(a, wk_ref).astype(k_ref.dtype)
    for sl, a in zip(subs, xv):
        v_ref[sl, :] = mm(a, wv_ref).astype(v_ref.dtype)
    z = each(lambda t: w0_ref[...] + mm(jnp.tanh(t).astype(BF16), w2_ref), w_mid)
    for sl, t in zip(subs, a_mid):
        a_ref[sl, :] = jax.nn.sigmoid(a0_ref[...] + mm(t.astype(BF16), a2_ref)).astype(a_ref.dtype)
    for sl, t in zip(subs, g_mid):
        g_ref[sl, :] = mm(jax.nn.sigmoid(t).astype(BF16), g2_ref).astype(g_ref.dtype)
    for sl, zs in zip(subs, z):
        lw_ref[sl, :] = -jnp.exp(-jax.nn.softplus(-zs) - 0.5)


def _rwkv_proj(x2, seq_len, mix, wr, wk, wv, w1, w2, a1, a2, g1, g2, w0, a0):
    n_tok, d = x2.shape
    tt = PROJ_TILE
    n_tiles = n_tok // tt
    tiles_per_seq = seq_len // tt
    tile = pl.BlockSpec((tt, d), lambda i: (i, 0))
    prev = pl.BlockSpec((8, d), lambda i: (jnp.maximum(i * (tt // 8) - 1, 0), 0))
    full = lambda arr: pl.BlockSpec(arr.shape, lambda i: (0,) * arr.ndim)
    weights = (mix, wr, wk, wv, w1, w2, a1, a2, g1, g2, w0, a0)
    out_dtypes = (BF16, F32, BF16, BF16, BF16, BF16)
    w_bytes = sum(int(w.size) * w.dtype.itemsize for w in weights)
    return pl.pallas_call(
        functools.partial(_rwkv_proj_kernel, tiles_per_seq=tiles_per_seq),
        grid=(n_tiles,),
        in_specs=[tile, prev] + [full(w) for w in weights],
        out_specs=[tile] * 6,
        out_shape=[jax.ShapeDtypeStruct((n_tok, d), dt) for dt in out_dtypes],
        compiler_params=pltpu.CompilerParams(
            dimension_semantics=("parallel",),
            vmem_limit_bytes=_vmem_limit(2 * w_bytes + 2 * tt * d * (4 + 4 + 5 * 2) + 8 * tt * d * 4)),
        name="rwkv_proj",
    )(x2, x2, *weights)


def _rwkv_recur_kernel(r_ref, lw_ref, k_ref, v_ref, a_ref, g_ref, kk_ref, ka_ref, rk_ref, gng_ref, gnb_ref,
                       o_ref, s_ref):
    c = pl.program_id(1)
    C = r_ref.shape[0]
    N = HEAD_DIM

    @pl.when(c == 0)
    def _():
        s_ref[...] = jnp.zeros_like(s_ref)

    lw = lw_ref[...]
    ti = lax.broadcasted_iota(jnp.int32, (C, C), 0)
    tj = lax.broadcasted_iota(jnp.int32, (C, C), 1)
    tril = (tj <= ti).astype(BF16)
    cum = sum(jnp.dot(tril, part, preferred_element_type=F32) for part in _split3(lw))
    rho = cum[C // 2 - 1:C // 2, :]
    last = cum[C - 1:C, :]
    e_q = jnp.exp(cum - rho)
    e_qx = jnp.exp(cum - lw - rho)
    e_k = jnp.exp(rho - cum)
    e_end = jnp.exp(last - cum)
    e_rho = jnp.exp(rho)
    d_end = jnp.exp(last)

    r = r_ref[...].astype(F32)
    k = k_ref[...].astype(F32)
    v = v_ref[...].astype(F32)
    a = a_ref[...].astype(F32)
    kk_raw = k * kk_ref[...]
    k_mod = k * (1.0 + (a - 1.0) * ka_ref[...])
    rkk = r * k_mod * rk_ref[...]

    G = RWKV_HEADS_PER_TILE
    R = G * C
    GW = G * N
    er = lax.broadcasted_iota(jnp.int32, (R, GW), 0)
    ec = lax.broadcasted_iota(jnp.int32, (R, GW), 1)
    blk = (er // C) == (ec // N)
    strict = (ec % C) < (er % C)
    incl = (ec % C) <= (er % C)
    eye = (er == ec).astype(F32)
    ident = (er == ec).astype(BF16)
    ones_blk = blk.astype(BF16)

    def expand(x):
        return jnp.where(blk, jnp.concatenate([x] * G, axis=0), 0.0).astype(BF16)

    def head_sums(xs):
        parts = _split3(jnp.concatenate(xs, axis=0))
        tot = jnp.dot(jnp.concatenate(parts, axis=0), ones_blk, preferred_element_type=F32)
        n = len(xs) * C
        tot = tot[:n] + tot[n:2 * n] + tot[2 * n:]
        return [tot[u * C:(u + 1) * C] for u in range(len(xs))]

    mm = lambda p, q: jnp.dot(p, q, preferred_element_type=F32)
    nt = lambda p, q: lax.dot_general(p, q, (((1,), (1,)), ((), ())), preferred_element_type=F32)

    n_grp = N_HEADS // G
    groups = [slice(gi * GW, (gi + 1) * GW) for gi in range(n_grp)]
    pre = head_sums([kk_raw[:, sl] * kk_raw[:, sl] for sl in groups] + [rkk[:, sl] for sl in groups])
    kk_ss, rkk_sum = pre[:n_grp], pre[n_grp:]
    each = lambda fn, *lists: [fn(*args) for args in zip(*lists)]
    bf = lambda x: x.astype(BF16)
    kk_n = each(lambda sl, ss: kk_raw[:, sl] * lax.rsqrt(jnp.maximum(ss, 1e-24)), groups, kk_ss)
    b_n = each(lambda sl, kk_g: kk_g * a[:, sl], groups, kk_n)
    kk_q = each(lambda sl, kk_g: kk_g * e_qx[:, sl], groups, kk_n)
    r_q = each(lambda sl: r[:, sl] * e_q[:, sl], groups)
    q2 = each(lambda x, y: jnp.concatenate([expand(x), expand(y)], axis=0), kk_q, r_q)
    a_k = each(lambda sl, q: nt(q, expand(k_mod[:, sl] * e_k[:, sl])), groups, q2)
    a_b = each(lambda sl, q, b_g: nt(q, expand(b_g * e_k[:, sl])), groups, q2, b_n)
    a_kk = each(lambda x: jnp.where(strict, x[:R], 0.0), a_k)
    a_rk = each(lambda x: jnp.where(incl, x[R:], 0.0), a_k)
    a_kb = each(lambda x: jnp.where(strict, x[:R], 0.0), a_b)
    a_rb = each(lambda x: jnp.where(incl, x[R:], 0.0), a_b)
    t_inv = each(lambda x: eye - x, a_kb)
    p = each(lambda x: mm(bf(-x), bf(-x)), a_kb)
    for _ in range(int(math.log2(C)) - 2):
        both = each(lambda pg, tg: mm(bf(pg), jnp.concatenate([bf(pg), bf(tg)], axis=1)), p, t_inv)
        p = each(lambda x: x[:, :R], both)
        t_inv = each(lambda tg, x: tg + x[:, R:], t_inv, both)
    t_inv = each(lambda pg, tg: tg + mm(bf(pg), bf(tg)), p, t_inv)
    kd_t = each(lambda sl: bf(nt(ident, expand(k_mod[:, sl] * e_end[:, sl]))), groups)
    bd_t = each(lambda sl, b_g: bf(nt(ident, expand(b_g * e_end[:, sl]))), groups, b_n)

    def decay_column(sl):
        rows = jnp.concatenate([part.astype(F32) for part in _split3(d_end[:, sl])] + [jnp.zeros((5, GW), F32)], axis=0)
        d_t = nt(ident, bf(rows))
        return d_t[:, 0:1] + d_t[:, 1:2] + d_t[:, 2:3]

    d_col = each(decay_column, groups)
    v_e = each(lambda sl: expand(v[:, sl]), groups)
    av = each(lambda x, y, ve: mm(bf(jnp.concatenate([x, y], axis=0)), ve), a_kk, a_rk, v_e)
    q2_abs = each(lambda sl, x, y: jnp.concatenate([expand(x * e_rho[:, sl]), expand(y * e_rho[:, sl])], axis=0),
                  groups, kk_q, r_q)
    st = [s_ref[gi] for gi in range(n_grp)]
    qs = each(lambda q, s: mm(q, bf(s)), q2_abs, st)
    sa_e = each(lambda tg, q, x: bf(mm(bf(tg), bf(q[:R] + x[:R]))), t_inv, qs, av)
    y_e = each(lambda q, x, arb, sa: q[R:] + x[R:] - mm(bf(arb), sa), qs, av, a_rb, sa_e)
    ys = each(lambda x: sum(x[u * C:(u + 1) * C] for u in range(G)), y_e)
    upd = each(lambda kt, bt, ve, sa: mm(jnp.concatenate([kt, -bt], axis=1), jnp.concatenate([ve, sa], axis=0)),
               kd_t, bd_t, v_e, sa_e)
    for gi in range(n_grp):
        s_ref[gi] = st[gi] * d_col[gi] + upd[gi]

    inv_n = 1.0 / N
    ycs = [y - mu * inv_n for y, mu in zip(ys, head_sums(ys))]
    for sl, yc, sq, bsum in zip(groups, ycs, head_sums([yc * yc for yc in ycs]), rkk_sum):
        yn = yc * lax.rsqrt(sq * inv_n + GN_EPS) * gng_ref[:, sl] + gnb_ref[:, sl]
        o_ref[:, sl] = ((yn + bsum * v[:, sl]) * g_ref[:, sl]).astype(o_ref.dtype)


def _rwkv_recur(r, lw, k, v, a, g, batch, seq_len, k_k, k_a, r_k, gn_g, gn_b):
    n_tok, d = r.shape
    C = RWKV_CHUNK
    nc = seq_len // C
    tile = pl.BlockSpec((C, d), lambda b, c: (b * nc + c, 0))
    vec = pl.BlockSpec((1, d), lambda b, c: (0, 0))
    return pl.pallas_call(
        _rwkv_recur_kernel,
        grid=(batch, nc),
        in_specs=[tile] * 6 + [vec] * 5,
        out_specs=tile,
        out_shape=jax.ShapeDtypeStruct((n_tok, d), BF16),
        scratch_shapes=[pltpu.VMEM((N_HEADS // RWKV_HEADS_PER_TILE, V7X_MXU_DIM, V7X_MXU_DIM), F32)],
        compiler_params=pltpu.CompilerParams(
            dimension_semantics=("parallel", "arbitrary"),
            vmem_limit_bytes=_vmem_limit(2 * 7 * C * d * 4 + 32 * C * d * 4 + 64 * V7X_MXU_DIM ** 2 * 4)),
        name="rwkv_recur",
    )(r, lw, k, v, a, g, k_k, k_a, r_k, gn_g, gn_b)


def _rank_among(vals, i):
    cnt = 0
    for j, vj in enumerate(vals):
        if j == i:
            continue
        before = (vj >= vals[i]) if j < i else (vj > vals[i])
        cnt = cnt + before.astype(jnp.int32)
    return cnt


def _pick(ranks, vals, want):
    out = vals[0]
    for rk, vl in zip(ranks[1:], vals[1:]):
        out = jnp.where(rk == want, vl, out)
    return out


def _epilogue_kernel(act_ref, wo_ref, res_ref, lng_ref, lnb_ref, rwt_ref, rb_ref,
                     h_ref, eidx_ref, rank_ref, gcol_ref, cnt_ref, base_ref):
    i = pl.program_id(0)

    @pl.when(i == 0)
    def _():
        base_ref[...] = jnp.zeros_like(base_ref)

    tt = EPI_SUBTILE
    subs = [slice(u * tt, (u + 1) * tt) for u in range(act_ref.shape[0] // tt)]
    each = lambda fn, *lists: [fn(*args) for args in zip(*lists)]
    nt = lambda a, b: lax.dot_general(a, b, (((1,), (1,)), ((), ())), preferred_element_type=F32)

    mixed = each(lambda sl: jnp.dot(act_ref[sl, :], wo_ref[...], preferred_element_type=F32), subs)
    h = each(lambda sl, mx: _layer_norm_rows(ALPHA * res_ref[sl, :] + mx, lng_ref[...], lnb_ref[...]), subs, mixed)
    for sl, hs in zip(subs, h):
        h_ref[sl, :] = hs

    wh, wm, wl = _split3(rwt_ref[...])

    def router_logits(hs):
        hh, hm, hl = _split3(hs)
        return nt(wh, hh) + (nt(wh, hm) + nt(wm, hh)) + (nt(wh, hl) + nt(wm, hm) + nt(wl, hh))

    s = each(lambda hs: jax.nn.sigmoid(router_logits(hs)), h)

    def select(sg):
        s_sel = sg + rb_ref[...]
        rows = [s_sel[e:e + 1, :] for e in range(N_EXPERTS)]
        grp_score, grp_i0, grp_i1 = [], [], []
        for gi in range(N_GROUPS):
            vals = rows[gi * EXPERTS_PER_GROUP:(gi + 1) * EXPERTS_PER_GROUP]
            ranks = [_rank_among(vals, q) for q in range(EXPERTS_PER_GROUP)]
            idx = [jnp.full_like(ranks[0], q) for q in range(EXPERTS_PER_GROUP)]
            grp_score.append(_pick(ranks, vals, 0) + _pick(ranks, vals, 1))
            grp_i0.append(_pick(ranks, idx, 0))
            grp_i1.append(_pick(ranks, idx, 1))
        g_ranks = [_rank_among(grp_score, q) for q in range(N_GROUPS)]
        gidx = [jnp.full_like(g_ranks[0], q) for q in range(N_GROUPS)]
        g_star = _pick(g_ranks, gidx, 0)
        e0 = g_star * EXPERTS_PER_GROUP + _pick(g_ranks, grp_i0, 0)
        e1 = g_star * EXPERTS_PER_GROUP + _pick(g_ranks, grp_i1, 0)
        return e0, e1

    picked = each(select, s)
    e_iota = lax.broadcasted_iota(jnp.int32, (N_EXPERTS, tt), 0)
    hit0 = each(lambda pk: e_iota == pk[0], picked)
    hit1 = each(lambda pk: e_iota == pk[1], picked)

    def gates(sg, h0, h1):
        gate0 = jnp.sum(jnp.where(h0, sg, 0.0), axis=0, keepdims=True)
        gate1 = jnp.sum(jnp.where(h1, sg, 0.0), axis=0, keepdims=True)
        denom = gate0 + gate1
        return gate0 / denom, gate1 / denom

    gate = each(gates, s, hit0, hit1)

    member = each(lambda h0, h1: jnp.where(h0 | h1, 1.0, 0.0), hit0, hit1)
    ui = lax.broadcasted_iota(jnp.int32, (tt, tt), 0)
    uj = lax.broadcasted_iota(jnp.int32, (tt, tt), 1)
    before = (ui < uj).astype(BF16)
    prefix = each(lambda mb: jnp.dot(mb.astype(BF16), before, preferred_element_type=F32), member)
    base = base_ref[:, 0:1]
    for u, sl in enumerate(subs):
        pre = prefix[u] + base
        rank0 = jnp.sum(jnp.where(hit0[u], pre, 0.0), axis=0, keepdims=True)
        rank1 = jnp.sum(jnp.where(hit1[u], pre, 0.0), axis=0, keepdims=True)
        eidx_ref[:, sl] = jnp.concatenate(picked[u], axis=0)
        rank_ref[:, sl] = jnp.concatenate([rank0, rank1], axis=0).astype(jnp.int32)
        base = base + jnp.sum(member[u], axis=1, keepdims=True)
    base_ref[...] = jnp.broadcast_to(base, base_ref.shape)
    cnt_ref[...] = jnp.broadcast_to(base, cnt_ref.shape).astype(jnp.int32)

    ident = (ui == uj).astype(BF16)

    def gate_columns(gt):
        gpad = jnp.concatenate([gt[0], gt[1], jnp.zeros((V7X_LANES - 2, tt), F32)], axis=0)
        return sum(nt(ident, part) for part in _split3(gpad))

    for sl, gc in zip(subs, each(gate_columns, gate)):
        gcol_ref[sl, :] = gc


def _mixer_epilogue(act, wo, res, ln_g, ln_b, router_wt, router_bias):
    n_tok, d = res.shape
    k_in = act.shape[1]
    tt = EPI_TILE
    n_tiles = n_tok // tt
    full = lambda arr: pl.BlockSpec(arr.shape, lambda i: (0,) * arr.ndim)
    return pl.pallas_call(
        _epilogue_kernel,
        grid=(n_tiles,),
        in_specs=[pl.BlockSpec((tt, k_in), lambda i: (i, 0)), full(wo), pl.BlockSpec((tt, d), lambda i: (i, 0)),
                  full(ln_g), full(ln_b), full(router_wt), full(router_bias)],
        out_specs=[pl.BlockSpec((tt, d), lambda i: (i, 0)),
                   pl.BlockSpec((2, tt), lambda i: (0, i)),
                   pl.BlockSpec((2, tt), lambda i: (0, i)),
                   pl.BlockSpec((tt, V7X_LANES), lambda i: (i, 0)),
                   pl.BlockSpec((N_EXPERTS, V7X_LANES), lambda i: (0, 0))],
        out_shape=[jax.ShapeDtypeStruct((n_tok, d), F32),
                   jax.ShapeDtypeStruct((2, n_tok), jnp.int32),
                   jax.ShapeDtypeStruct((2, n_tok), jnp.int32),
                   jax.ShapeDtypeStruct((n_tok, V7X_LANES), F32),
                   jax.ShapeDtypeStruct((N_EXPERTS, V7X_LANES), jnp.int32)],
        scratch_shapes=[pltpu.VMEM((N_EXPERTS, V7X_LANES), F32)],
        compiler_params=pltpu.CompilerParams(
            dimension_semantics=("arbitrary",),
            vmem_limit_bytes=_vmem_limit(2 * int(wo.size) * 2 + 2 * tt * (k_in * 2 + 2 * d * 4) + 16 * tt * d * 4)),
        name="mixer_epilogue",
    )(act, wo, res, ln_g, ln_b, router_wt, router_bias)


def _dispatch_kernel(dest_ref, h_ref, xs_in_ref, xs_ref, sem):
    del xs_in_ref
    tt = h_ref.shape[0]

    def row_copy(t, slot):
        d = dest_ref[0, 0, slot * tt + t]
        return pltpu.make_async_copy(h_ref.at[pl.ds(t, 1)], xs_ref.at[pl.ds(d, 1)], sem)

    def issue(t0, carry):
        for u in range(DMA_ISSUE_UNROLL):
            t = t0 * DMA_ISSUE_UNROLL + u
            row_copy(t, 0).start(priority=0)
            row_copy(t, 1).start(priority=1)
        return carry

    lax.fori_loop(0, tt // DMA_ISSUE_UNROLL, issue, 0)
    for _ in range(2):
        pltpu.make_async_copy(h_ref, xs_ref.at[pl.ds(0, tt)], sem).wait()


def _moe_dispatch(h, dest_tiles, n_rows):
    n_tok, d = h.shape
    tt = dest_tiles.shape[2] // 2
    n_tiles = n_tok // tt
    zeros = jnp.zeros((n_rows, d), h.dtype)
    return pl.pallas_call(
        _dispatch_kernel,
        grid=(n_tiles,),
        in_specs=[pl.BlockSpec((1, 1, 2 * tt), lambda i: (i, 0, 0), memory_space=pltpu.SMEM),
                  pl.BlockSpec((tt, d), lambda i: (i, 0)),
                  pl.BlockSpec(memory_space=pl.ANY)],
        out_specs=pl.BlockSpec(memory_space=pl.ANY),
        out_shape=jax.ShapeDtypeStruct((n_rows, d), h.dtype),
        scratch_shapes=[pltpu.SemaphoreType.DMA(())],
        input_output_aliases={2: 0},
        compiler_params=pltpu.CompilerParams(dimension_semantics=("arbitrary",),
                                             vmem_limit_bytes=_vmem_limit(2 * tt * d * 4)),
        name="moe_dispatch",
    )(dest_tiles, h, zeros)


def _ffn_kernel(bexp_ref, nblk_ref, x_ref, wg_ref, wu_ref, wd_ref, y_ref, wg_bf, wu_bf, wd_bf):
    j = pl.program_id(0)
    active = j < nblk_ref[0]
    new_expert = (j == 0) | (bexp_ref[j] != bexp_ref[jnp.maximum(j - 1, 0)])

    @pl.when(active & new_expert)
    def _():
        wg_bf[...] = wg_ref[0, 0].astype(BF16)
        wu_bf[...] = wu_ref[0, 0].astype(BF16)
        wd_bf[...] = wd_ref[0, 0].astype(BF16)

    @pl.when(active)
    def _():
        x = x_ref[...].astype(BF16)
        gate = jnp.dot(x, wg_bf[...], preferred_element_type=F32)
        up = jnp.dot(x, wu_bf[...], preferred_element_type=F32)
        hid = (gate * jax.nn.sigmoid(gate)) * up
        y_ref[...] = jnp.dot(hid.astype(BF16), wd_bf[...], preferred_element_type=F32)

    @pl.when(jnp.logical_not(active))
    def _():
        y_ref[...] = jnp.zeros_like(y_ref)


def _moe_ffn(xs, block_exp, n_used, layer, wg, wu, wd):
    n_rows, d = xs.shape
    blk = MOE_BLOCK
    n_blocks = n_rows // blk
    de = wg.shape[3]
    expert = lambda j, be, nb: (layer, be[j], 0, 0)
    grid_spec = pltpu.PrefetchScalarGridSpec(
        num_scalar_prefetch=2,
        grid=(n_blocks,),
        in_specs=[pl.BlockSpec((blk, d), lambda j, be, nb: (j, 0)),
                  pl.BlockSpec((1, 1, d, de), expert),
                  pl.BlockSpec((1, 1, d, de), expert),
                  pl.BlockSpec((1, 1, de, d), expert)],
        out_specs=pl.BlockSpec((blk, d), lambda j, be, nb: (j, 0)),
        scratch_shapes=[pltpu.VMEM((d, de), BF16), pltpu.VMEM((d, de), BF16), pltpu.VMEM((de, d), BF16)],
    )
    return pl.pallas_call(
        _ffn_kernel,
        grid_spec=grid_spec,
        out_shape=jax.ShapeDtypeStruct((n_rows, d), F32),
        compiler_params=pltpu.CompilerParams(
            dimension_semantics=("arbitrary",),
            vmem_limit_bytes=_vmem_limit(3 * d * de * (2 * 4 + 2) + 4 * blk * d * 4 + 4 * blk * de * 4)),
        name="moe_ffn",
    )(block_exp, n_used, xs, wg, wu, wd)


def _combine_kernel(dest_ref, dest_next_ref, ys_ref, gcol_ref, res_ref, lng_ref, lnb_ref, o_ref, buf_ref, sem):
    i = pl.program_id(0)
    n = pl.num_programs(0)
    tt = res_ref.shape[0]
    cur = i % 2

    def gather(idx_ref, buf):
        def row_copy(t, slot):
            d = idx_ref[0, 0, slot * tt + t]
            return pltpu.make_async_copy(ys_ref.at[pl.ds(d, 1)], buf_ref.at[buf, slot, pl.ds(t, 1)], sem.at[buf])

        def issue(t0, carry):
            for u in range(DMA_ISSUE_UNROLL):
                t = t0 * DMA_ISSUE_UNROLL + u
                row_copy(t, 0).start(priority=0)
                row_copy(t, 1).start(priority=1)
            return carry

        lax.fori_loop(0, tt // DMA_ISSUE_UNROLL, issue, 0)

    @pl.when(i == 0)
    def _():
        gather(dest_ref, cur)

    @pl.when(i + 1 < n)
    def _():
        gather(dest_next_ref, 1 - cur)

    for slot in range(2):
        pltpu.make_async_copy(ys_ref.at[pl.ds(0, tt)], buf_ref.at[cur, slot], sem.at[cur]).wait()

    gcol = gcol_ref[...]
    ffn = buf_ref[cur, 0] * gcol[:, 0:1] + buf_ref[cur, 1] * gcol[:, 1:2]
    o_ref[...] = _layer_norm_rows(ALPHA * res_ref[...] + ffn, lng_ref[...], lnb_ref[...])


def _moe_combine(ys, dest_tiles, gcol, res, ln_g, ln_b):
    n_tok, d = res.shape
    tt = dest_tiles.shape[2] // 2
    n_tiles = n_tok // tt
    full = lambda arr: pl.BlockSpec(arr.shape, lambda i: (0,) * arr.ndim)
    return pl.pallas_call(
        _combine_kernel,
        grid=(n_tiles,),
        in_specs=[pl.BlockSpec((1, 1, 2 * tt), lambda i: (i, 0, 0), memory_space=pltpu.SMEM),
                  pl.BlockSpec((1, 1, 2 * tt), lambda i: (jnp.minimum(i + 1, n_tiles - 1), 0, 0),
                               memory_space=pltpu.SMEM),
                  pl.BlockSpec(memory_space=pl.ANY),
                  pl.BlockSpec((tt, V7X_LANES), lambda i: (i, 0)),
                  pl.BlockSpec((tt, d), lambda i: (i, 0)),
                  full(ln_g), full(ln_b)],
        out_specs=pl.BlockSpec((tt, d), lambda i: (i, 0)),
        out_shape=jax.ShapeDtypeStruct((n_tok, d), F32),
        scratch_shapes=[pltpu.VMEM((2, 2, tt, d), F32), pltpu.SemaphoreType.DMA((2,))],
        compiler_params=pltpu.CompilerParams(
            dimension_semantics=("arbitrary",),
            vmem_limit_bytes=_vmem_limit(4 * tt * d * 4 + 2 * 3 * tt * d * 4)),
        name="moe_combine",
    )(dest_tiles, dest_tiles, ys, gcol, res, ln_g, ln_b)


def _moe_layer(h, eidx, rank, counts, gcol, layer, wg, wu, wd, ln_g, ln_b):
    n_tok, d = h.shape
    blk = MOE_BLOCK
    n_rows = (2 * n_tok + N_EXPERTS * (blk - 1) + blk - 1) // blk * blk
    n_blocks = n_rows // blk
    counts = counts[:, 0]
    padded = (counts + blk - 1) // blk * blk
    pend = jnp.cumsum(padded)
    pstart = pend - padded
    onehot = (eidx[:, :, None] == jnp.arange(N_EXPERTS, dtype=jnp.int32)).astype(jnp.int32)
    dest = rank + jnp.sum(onehot * pstart, axis=-1)
    tiles = lambda tt: dest.reshape(2, n_tok // tt, tt).transpose(1, 0, 2).reshape(n_tok // tt, 1, 2 * tt)
    blk_start = jnp.arange(n_blocks, dtype=jnp.int32) * blk
    block_exp = jnp.minimum(jnp.sum((blk_start[:, None] >= pend[None, :]).astype(jnp.int32), axis=1),
                            N_EXPERTS - 1).astype(jnp.int32)
    n_used = (pend[-1:] // blk).astype(jnp.int32)
    blk_idx = jnp.arange(n_blocks, dtype=jnp.int32)
    block_exp = jnp.where(blk_idx < n_used[0], block_exp, block_exp[jnp.maximum(n_used[0] - 1, 0)])
    xs = _moe_dispatch(h, tiles(DISPATCH_TILE), n_rows)
    ys = _moe_ffn(xs, block_exp, n_used, layer, wg, wu, wd)
    return _moe_combine(ys, tiles(COMBINE_TILE), gcol, h, ln_g, ln_b)


def _fox_proj_kernel(x_ref, wq_ref, wk_ref, wv_ref, wf_ref, bf_ref, q_ref, k_ref, v_ref, c_ref, carry_ref):
    t = pl.program_id(1)

    @pl.when(t == 0)
    def _():
        carry_ref[...] = jnp.zeros_like(carry_ref)

    tt = PROJ_SUBTILE
    subs = [slice(u * tt, (u + 1) * tt) for u in range(x_ref.shape[0] // tt)]
    each = lambda fn, *lists: [fn(*args) for args in zip(*lists)]
    parts = each(lambda sl: _split3(x_ref[sl, :]), subs)
    wh, wm, wl = _split3(wf_ref[...])
    mm = lambda a, b: jnp.dot(a, b, preferred_element_type=F32)
    logit = each(lambda p: (mm(p[0], wh) + (mm(p[0], wm) + mm(p[1], wh))
                            + (mm(p[0], wl) + mm(p[1], wm) + mm(p[2], wh))) + bf_ref[...], parts)
    for sl, p in zip(subs, parts):
        q_ref[sl, :] = (mm(p[0], wq_ref[...]) * LOG2E).astype(q_ref.dtype)
    for sl, p in zip(subs, parts):
        k_ref[sl, :] = mm(p[0], wk_ref[...]).astype(k_ref.dtype)
    for sl, p in zip(subs, parts):
        v_ref[sl, :] = mm(p[0], wv_ref[...]).astype(v_ref.dtype)
    ti = lax.broadcasted_iota(jnp.int32, (tt, tt), 0)
    tj = lax.broadcasted_iota(jnp.int32, (tt, tt), 1)
    tril = (tj <= ti).astype(BF16)
    local = each(lambda lg: sum(mm(tril, part) for part in _split3(jax.nn.log_sigmoid(lg))), logit)
    carry = carry_ref[0:1, :]
    for sl, cs in zip(subs, local):
        c = cs + carry
        c_ref[sl, :] = c
        carry = c[tt - 1:tt, :]
    carry_ref[...] = jnp.broadcast_to(carry, carry_ref.shape)


def _fox_proj(x2, batch, seq_len, wq, wk, wv, wf, b_f):
    n_tok, d = x2.shape
    tt = PROJ_TILE
    nt = seq_len // tt
    tile = pl.BlockSpec((tt, d), lambda b, t: (b * nt + t, 0))
    full = lambda arr: pl.BlockSpec(arr.shape, lambda b, t: (0,) * arr.ndim)
    return pl.pallas_call(
        _fox_proj_kernel,
        grid=(batch, nt),
        in_specs=[tile, full(wq), full(wk), full(wv), full(wf), full(b_f)],
        out_specs=[tile, tile, tile, pl.BlockSpec((tt, N_HEADS), lambda b, t: (b * nt + t, 0))],
        out_shape=[jax.ShapeDtypeStruct((n_tok, d), BF16)] * 3 + [jax.ShapeDtypeStruct((n_tok, N_HEADS), F32)],
        scratch_shapes=[pltpu.VMEM((8, N_HEADS), F32)],
        compiler_params=pltpu.CompilerParams(
            dimension_semantics=("parallel", "arbitrary"),
            vmem_limit_bytes=_vmem_limit(2 * 3 * d * d * 2 + 2 * tt * d * (4 + 3 * 2) + 8 * tt * d * 4)),
        name="fox_proj",
    )(x2, wq, wk, wv, wf, b_f)


def _fox_attn_kernel(q_ref, k_ref, v_ref, ct_ref, o_ref):
    hp = pl.program_id(1)
    seq_len = q_ref.shape[0]
    tq = ATTN_TILE
    tk = tq
    nq = seq_len // tq
    N = HEAD_DIM
    nt = lambda a, b: lax.dot_general(a, b, (((1,), (1,)), ((), ())), preferred_element_type=F32)
    mm = lambda a, b: jnp.dot(a, b, preferred_element_type=F32)

    lane = lax.broadcasted_iota(jnp.int32, (tq, 2 * N), 1)
    vrow = lax.broadcasted_iota(jnp.int32, (2 * N, tk), 0)
    orow = lax.broadcasted_iota(jnp.int32, (2 * N, tq), 0)
    kpos = lax.broadcasted_iota(jnp.int32, (tk, tq), 0)
    qpos = lax.broadcasted_iota(jnp.int32, (tk, tq), 1)
    causal = kpos <= qpos
    ident_v = (lax.broadcasted_iota(jnp.int32, (2 * N, 2 * N), 0)
               == lax.broadcasted_iota(jnp.int32, (2 * N, 2 * N), 1)).astype(BF16)
    ident_q = (kpos == qpos).astype(BF16)

    cq_row = [ct_ref[0, pl.ds(2 * hp + u, 1), :] * LOG2E for u in range(2)]

    def key_columns(ks):
        rows = jnp.concatenate([part.astype(F32) for u in range(2) for part in _split3(cq_row[u][:, ks])]
                               + [jnp.zeros((2, tk), F32)], axis=0)
        cols = nt(ident_q, rows.astype(BF16))
        return [cols[:, 3 * u:3 * u + 1] + cols[:, 3 * u + 1:3 * u + 2] + cols[:, 3 * u + 2:3 * u + 3]
                for u in range(2)]

    ck_cols = [key_columns(slice(j * tk, (j + 1) * tk)) for j in range(nq)]
    m = [[None, None] for _ in range(nq)]
    l = [[None, None] for _ in range(nq)]
    acc = [None] * nq

    zero_q = jnp.zeros((tq, 2 * N), BF16)
    keys, values = {}, {}

    def scores(j, qi):
        if j not in keys:
            keys[j] = (k_ref[j * tk:(j + 1) * tk, :], [jnp.broadcast_to(col, (tk, tq)) for col in ck_cols[j]])
        k_j, ck_b = keys[j]
        q = q_ref[qi * tq:(qi + 1) * tq, :]
        t = [nt(k_j, jnp.where((lane < N) == (u == 0), q, zero_q)) - ck_b[u] for u in range(2)]
        return [jnp.where(causal, tu, -jnp.inf) for tu in t] if qi == j else t

    def absorb(j, qi, t):
        if j not in values:
            v_t = nt(ident_v, v_ref[j * tk:(j + 1) * tk, :]).astype(BF16)
            values[j] = [jnp.where(vrow < N, v_t, jnp.zeros_like(v_t)), jnp.where(vrow < N, jnp.zeros_like(v_t), v_t)]
        v_heads = values[j]
        cq = [cq_row[u][:, qi * tq:(qi + 1) * tq] for u in range(2)]
        rmax = [jnp.max(t[u], axis=0, keepdims=True) + cq[u] for u in range(2)]
        m_new = rmax if j == 0 else [jnp.maximum(m[qi][u], rmax[u]) for u in range(2)]
        p = [jnp.exp2(t[u] + (cq[u] - m_new[u])) for u in range(2)]
        psum = [jnp.sum(p[u], axis=0, keepdims=True) for u in range(2)]
        pv = mm(v_heads[0], p[0].astype(BF16)) + mm(v_heads[1], p[1].astype(BF16))
        if j == 0:
            acc[qi] = pv
            l[qi] = psum
        else:
            alpha = [jnp.exp2(m[qi][u] - m_new[u]) for u in range(2)]
            acc[qi] = acc[qi] * jnp.where(orow < N, alpha[0], alpha[1]) + pv
            l[qi] = [alpha[u] * l[qi][u] + psum[u] for u in range(2)]
        m[qi] = m_new
        if j == qi:
            o_t = (acc[qi] / jnp.where(orow < N, l[qi][0], l[qi][1])).astype(BF16)
            o_ref[qi * tq:(qi + 1) * tq, :] = nt(ident_q, o_t).astype(o_ref.dtype)

    pairs = [(j, qi) for j in range(nq) for qi in range(j, nq)]
    t_next = scores(*pairs[0])
    for n, (j, qi) in enumerate(pairs):
        t_cur = t_next
        if n + 1 < len(pairs):
            t_next = scores(*pairs[n + 1])
        absorb(j, qi, t_cur)


def _fox_attn(q, k, v, c_t, batch, seq_len):
    n_tok, d = q.shape
    pair = 2 * HEAD_DIM
    seq = pl.BlockSpec((seq_len, pair), lambda b, hp: (b, hp))
    return pl.pallas_call(
        _fox_attn_kernel,
        grid=(batch, d // pair),
        in_specs=[seq, seq, seq, pl.BlockSpec((1, N_HEADS, seq_len), lambda b, hp: (b, 0, 0))],
        out_specs=seq,
        out_shape=jax.ShapeDtypeStruct((n_tok, d), BF16),
        compiler_params=pltpu.CompilerParams(
            dimension_semantics=("parallel", "arbitrary"),
            vmem_limit_bytes=_vmem_limit(2 * 4 * seq_len * pair * 2 + 64 * ATTN_TILE * ATTN_TILE * 4)),
        name="fox_attn",
    )(q, k, v, c_t)


def kernel(x, rw_mix, rw_wr, rw_wk, rw_wv, rw_wo, rw_w0, rw_w1, rw_w2, rw_a0, rw_a1, rw_a2, rw_g1, rw_g2,
           rw_kk, rw_ka, rw_rk, rw_gn_g, rw_gn_b, fx_w_in, fx_b_f, fx_wo, router_w, router_bias,
           moe_w_gate, moe_w_up, moe_w_down, ln_g, ln_b):
    batch, seq_len, d = x.shape
    n_tok = batch * seq_len
    bf = lambda w: w.astype(BF16)
    row = lambda w: w.reshape(1, -1)
    router_wt = router_w.T
    router_b = router_bias.reshape(N_EXPERTS, 1)
    h = x.reshape(n_tok, d)

    for i in range(DEPTH):
        j = i // 2
        if i % 2 == 0:
            r, lw, k, v, a, g = _rwkv_proj(
                h, seq_len, rw_mix[j], bf(rw_wr[j]), bf(rw_wk[j]), bf(rw_wv[j]), bf(rw_w1[j]), bf(rw_w2[j]),
                bf(rw_a1[j]), bf(rw_a2[j]), bf(rw_g1[j]), bf(rw_g2[j]), row(rw_w0[j]), row(rw_a0[j]))
            act = _rwkv_recur(r, lw, k, v, a, g, batch, seq_len, row(rw_kk[j]), row(rw_ka[j]), row(rw_rk[j]),
                              row(rw_gn_g[j]), row(rw_gn_b[j]))
            wo = bf(rw_wo[j])
        else:
            w_in = fx_w_in[j]
            scale = HEAD_DIM ** -0.5
            q, k, v, c = _fox_proj(h, batch, seq_len, bf(w_in[:, :d] * scale), bf(w_in[:, d:2 * d]),
                                   bf(w_in[:, 2 * d:3 * d]), w_in[:, 3 * d:], row(fx_b_f[j]))
            c_t = c.reshape(batch, seq_len, N_HEADS).transpose(0, 2, 1)
            act = _fox_attn(q, k, v, c_t, batch, seq_len)
            wo = bf(fx_wo[j])
        h, eidx, rank, gcol, counts = _mixer_epilogue(act, wo, h, row(ln_g[i, 0]), row(ln_b[i, 0]),
                                                      router_wt, router_b)
        h = _moe_layer(h, eidx, rank, counts, gcol, i, moe_w_gate, moe_w_up, moe_w_down,
                       row(ln_g[i, 1]), row(ln_b[i, 1]))
    return h.reshape(batch, seq_len, d)
```

```python
import functools
import math

import jax
import jax.numpy as jnp
from jax import lax
from jax.experimental import pallas as pl
from jax.experimental.pallas import tpu as pltpu

D_MODEL = 1024
HEAD_DIM = 64
N_HEADS = D_MODEL // HEAD_DIM
N_EXPERTS = 16
N_GROUPS = 4
EXPERTS_PER_GROUP = N_EXPERTS // N_GROUPS
D_EXPERT = 512
GN_EPS = 64e-5
LN_EPS = 1e-5
DEPTH = 2
ALPHA = (2 * DEPTH) ** 0.25
LOG2E = math.log2(math.e)

V7X_LANES = 128
V7X_VMEM_BYTES = 64 * 2 ** 20

V7X_MXU_DIM = 256

RWKV_CHUNK = 64
RWKV_HEADS_PER_TILE = V7X_MXU_DIM // HEAD_DIM
PROJ_TILE = 512
PROJ_SUBTILE = 256
EPI_TILE = 1024
EPI_SUBTILE = 256
ATTN_TILE = 256
MOE_BLOCK = 256
DISPATCH_TILE = 1024
COMBINE_TILE = 256
DMA_ISSUE_UNROLL = 8

F32 = jnp.float32
BF16 = jnp.bfloat16


def _vmem_limit(n_bytes):
    return int(min(n_bytes + 16 * 2 ** 20, V7X_VMEM_BYTES - 8 * 2 ** 20))


def _split3(x):
    hi = x.astype(BF16)
    r1 = x - hi.astype(F32)
    mid = r1.astype(BF16)
    lo = (r1 - mid.astype(F32)).astype(BF16)
    return hi, mid, lo


def _pack_halves(x):
    half = x.shape[1] // 2
    bits = lambda v: lax.bitcast_convert_type(v.astype(BF16).astype(F32), jnp.uint32)
    return (bits(x[:, :half]) >> 16) | (bits(x[:, half:]) & jnp.uint32(0xFFFF0000))


def _unpack_halves(w):
    lo = lax.bitcast_convert_type(w << 16, F32)
    hi = lax.bitcast_convert_type(w & jnp.uint32(0xFFFF0000), F32)
    return lo, hi


def _layer_norm_rows(x, g, b):
    mu = jnp.mean(x, axis=-1, keepdims=True)
    xc = x - mu
    var = jnp.mean(xc * xc, axis=-1, keepdims=True)
    return xc * lax.rsqrt(var + LN_EPS) * g + b


def _rwkv_proj_kernel(x_ref, xp_ref, mix_ref, wr_ref, wk_ref, wv_ref, w1_ref, w2_ref, a1_ref, a2_ref,
                      g1_ref, g2_ref, w0_ref, a0_ref,
                      r_ref, lw_ref, k_ref, v_ref, a_ref, g_ref, *, tiles_per_seq):
    i = pl.program_id(0)
    x = x_ref[...]
    tt = x.shape[0]
    first = (i % tiles_per_seq) == 0
    prev_row = jnp.where(first, 0.0, xp_ref[7:8, :])
    row = lax.broadcasted_iota(jnp.int32, (tt, 1), 0)
    xprev = jnp.where(row == 0, prev_row, pltpu.roll(x, 1, axis=0))
    xx = xprev - x
    mix = mix_ref[...]
    subs = [slice(u * PROJ_SUBTILE, (u + 1) * PROJ_SUBTILE) for u in range(tt // PROJ_SUBTILE)]
    each = lambda fn, *lists: [fn(*args) for args in zip(*lists)]
    mixed = lambda j: each(lambda sl: (x[sl] + xx[sl] * mix[j:j + 1, :]).astype(BF16), subs)
    xr, xw, xk, xv, xa, xg = (mixed(j) for j in range(6))
    mm = lambda a, w_ref: jnp.dot(a, w_ref[...], preferred_element_type=F32)
    w_mid = each(lambda a: mm(a, w1_ref), xw)
    a_mid = each(lambda a: mm(a, a1_ref), xa)
    g_mid = each(lambda a: mm(a, g1_ref), xg)
    for sl, a in zip(subs, xr):
        r_ref[sl, :] = mm(a, wr_ref).astype(r_ref.dtype)
    for sl, a in zip(subs, xk):
        k_ref[sl, :] = mm(a, wk_ref).astype(k_ref.dtype)
    for sl, a in zip(subs, xv):
        v_ref[sl, :] = mm(a, wv_ref).astype(v_ref.dtype)
    z = each(lambda t: w0_ref[...] + mm(jnp.tanh(t).astype(BF16), w2_ref), w_mid)
    for sl, t in zip(subs, a_mid):
        a_ref[sl, :] = jax.nn.sigmoid(a0_ref[...] + mm(t.astype(BF16), a2_ref)).astype(a_ref.dtype)
    for sl, t in zip(subs, g_mid):
        g_ref[sl, :] = mm(jax.nn.sigmoid(t).astype(BF16), g2_ref).astype(g_ref.dtype)
    for sl, zs in zip(subs, z):
        lw_ref[sl, :] = -jnp.exp(-jax.nn.softplus(-zs) - 0.5)


def _rwkv_proj(x2, seq_len, mix, wr, wk, wv, w1, w2, a1, a2, g1, g2, w0, a0):
    n_tok, d = x2.shape
    tt = PROJ_TILE
    n_tiles = n_tok // tt
    tiles_per_seq = seq_len // tt
    tile = pl.BlockSpec((tt, d), lambda i: (i, 0))
    prev = pl.BlockSpec((8, d), lambda i: (jnp.maximum(i * (tt // 8) - 1, 0), 0))
    full = lambda arr: pl.BlockSpec(arr.shape, lambda i: (0,) * arr.ndim)
    weights = (mix, wr, wk, wv, w1, w2, a1, a2, g1, g2, w0, a0)
    out_dtypes = (BF16, F32, BF16, BF16, BF16, BF16)
    w_bytes = sum(int(w.size) * w.dtype.itemsize for w in weights)
    return pl.pallas_call(
        functools.partial(_rwkv_proj_kernel, tiles_per_seq=tiles_per_seq),
        grid=(n_tiles,),
        in_specs=[tile, prev] + [full(w) for w in weights],
        out_specs=[tile] * 6,
        out_shape=[jax.ShapeDtypeStruct((n_tok, d), dt) for dt in out_dtypes],
        compiler_params=pltpu.CompilerParams(
            dimension_semantics=("parallel",),
            vmem_limit_bytes=_vmem_limit(2 * w_bytes + 2 * tt * d * (4 + 4 + 5 * 2) + 8 * tt * d * 4)),
        name="rwkv_proj",
    )(x2, x2, *weights)


def _rwkv_recur_kernel(r_ref, lw_ref, k_ref, v_ref, a_ref, g_ref, kk_ref, ka_ref, rk_ref, gng_ref, gnb_ref,
                       o_ref, s_ref):
    c = pl.program_id(1)
    C = r_ref.shape[0]
    N = HEAD_DIM

    @pl.when(c == 0)
    def _():
        s_ref[...] = jnp.zeros_like(s_ref)

    lw = lw_ref[...]
    ti = lax.broadcasted_iota(jnp.int32, (C, C), 0)
    tj = lax.broadcasted_iota(jnp.int32, (C, C), 1)
    tril = (tj <= ti).astype(BF16)
    cum = sum(jnp.dot(tril, part, preferred_element_type=F32) for part in _split3(lw))
    rho = cum[C // 2 - 1:C // 2, :]
    last = cum[C - 1:C, :]
    e_q = jnp.exp(cum - rho)
    e_qx = jnp.exp(cum - lw - rho)
    e_k = jnp.exp(rho - cum)
    e_end = jnp.exp(last - cum)
    e_rho = jnp.exp(rho)
    d_end = jnp.exp(last)

    r = r_ref[...].astype(F32)
    k = k_ref[...].astype(F32)
    v = v_ref[...].astype(F32)
    a = a_ref[...].astype(F32)
    kk_raw = k * kk_ref[...]
    k_mod = k * (1.0 + (a - 1.0) * ka_ref[...])
    rkk = r * k_mod * rk_ref[...]

    G = RWKV_HEADS_PER_TILE
    R = G * C
    GW = G * N
    er = lax.broadcasted_iota(jnp.int32, (R, GW), 0)
    ec = lax.broadcasted_iota(jnp.int32, (R, GW), 1)
    blk = (er // C) == (ec // N)
    strict = (ec % C) < (er % C)
    incl = (ec % C) <= (er % C)
    eye = (er == ec).astype(F32)
    ones_blk = blk.astype(BF16)

    def expand(x):
        return jnp.where(blk, jnp.concatenate([x] * G, axis=0), 0.0).astype(BF16)

    def head_sums(xs):
        parts = _split3(jnp.concatenate(xs, axis=0))
        tot = jnp.dot(jnp.concatenate(parts, axis=0), ones_blk, preferred_element_type=F32)
        n = len(xs) * C
        tot = tot[:n] + tot[n:2 * n] + tot[2 * n:]
        return [tot[u * C:(u + 1) * C] for u in range(len(xs))]

    mm = lambda p, q: jnp.dot(p, q, preferred_element_type=F32)
    nt = lambda p, q: lax.dot_general(p, q, (((1,), (1,)), ((), ())), preferred_element_type=F32)

    n_grp = N_HEADS // G
    groups = [slice(gi * GW, (gi + 1) * GW) for gi in range(n_grp)]
    pre = head_sums([kk_raw[:, sl] * kk_raw[:, sl] for sl in groups] + [rkk[:, sl] for sl in groups])
    kk_ss, rkk_sum = pre[:n_grp], pre[n_grp:]
    each = lambda fn, *lists: [fn(*args) for args in zip(*lists)]
    bf = lambda x: x.astype(BF16)
    kk_n = each(lambda sl, ss: kk_raw[:, sl] * lax.rsqrt(jnp.maximum(ss, 1e-24)), groups, kk_ss)
    b_n = each(lambda sl, kk_g: kk_g * a[:, sl], groups, kk_n)
    kk_q = each(lambda sl, kk_g: kk_g * e_qx[:, sl], groups, kk_n)
    r_q = each(lambda sl: r[:, sl] * e_q[:, sl], groups)
    q2 = each(lambda x, y: jnp.concatenate([expand(x), expand(y)], axis=0), kk_q, r_q)
    a_k = each(lambda sl, q: nt(q, expand(k_mod[:, sl] * e_k[:, sl])), groups, q2)
    a_b = each(lambda sl, q, b_g: nt(q, expand(b_g * e_k[:, sl])), groups, q2, b_n)
    a_kk = each(lambda x: jnp.where(strict, x[:R], 0.0), a_k)
    a_rk = each(lambda x: jnp.where(incl, x[R:], 0.0), a_k)
    a_kb = each(lambda x: jnp.where(strict, x[:R], 0.0), a_b)
    a_rb = each(lambda x: jnp.where(incl, x[R:], 0.0), a_b)
    t_inv = each(lambda x: eye - x, a_kb)
    p = each(lambda x: mm(bf(-x), bf(-x)), a_kb)
    for _ in range(int(math.log2(C)) - 2):
        both = each(lambda pg, tg: mm(bf(pg), jnp.concatenate([bf(pg), bf(tg)], axis=1)), p, t_inv)
        p = each(lambda x: x[:, :R], both)
        t_inv = each(lambda tg, x: tg + x[:, R:], t_inv, both)
    t_inv = each(lambda pg, tg: tg + mm(bf(pg), bf(tg)), p, t_inv)
    expand_t = lambda x: bf(jnp.where(blk, jnp.concatenate([x] * G, axis=0), 0.0).T)
    kd_t = each(lambda sl: expand_t(k_mod[:, sl] * e_end[:, sl]), groups)
    bd_t = each(lambda sl, b_g: expand_t(b_g * e_end[:, sl]), groups, b_n)
    d_col = each(lambda sl: jnp.broadcast_to(d_end[:, sl], (GW, GW)).T, groups)
    v_e = each(lambda sl: expand(v[:, sl]), groups)
    av = each(lambda x, y, ve: mm(bf(jnp.concatenate([x, y], axis=0)), ve), a_kk, a_rk, v_e)
    q2_abs = each(lambda sl, x, y: jnp.concatenate([expand(x * e_rho[:, sl]), expand(y * e_rho[:, sl])], axis=0),
                  groups, kk_q, r_q)
    st = [s_ref[gi] for gi in range(n_grp)]
    qs = each(lambda q, s: mm(q, bf(s)), q2_abs, st)
    sa_e = each(lambda tg, q, x: bf(mm(bf(tg), bf(q[:R] + x[:R]))), t_inv, qs, av)
    y_e = each(lambda q, x, arb, sa: q[R:] + x[R:] - mm(bf(arb), sa), qs, av, a_rb, sa_e)
    ys = each(lambda x: sum(x[u * C:(u + 1) * C] for u in range(G)), y_e)
    upd = each(lambda kt, bt, ve, sa: mm(jnp.concatenate([kt, -bt], axis=1), jnp.concatenate([ve, sa], axis=0)),
               kd_t, bd_t, v_e, sa_e)
    for gi in range(n_grp):
        s_ref[gi] = st[gi] * d_col[gi] + upd[gi]

    inv_n = 1.0 / N
    ycs = [y - mu * inv_n for y, mu in zip(ys, head_sums(ys))]
    for sl, yc, sq, bsum in zip(groups, ycs, head_sums([yc * yc for yc in ycs]), rkk_sum):
        yn = yc * lax.rsqrt(sq * inv_n + GN_EPS) * gng_ref[:, sl] + gnb_ref[:, sl]
        o_ref[:, sl] = ((yn + bsum * v[:, sl]) * g_ref[:, sl]).astype(o_ref.dtype)


def _rwkv_recur(r, lw, k, v, a, g, batch, seq_len, k_k, k_a, r_k, gn_g, gn_b):
    n_tok, d = r.shape
    C = RWKV_CHUNK
    nc = seq_len // C
    tile = pl.BlockSpec((C, d), lambda b, c: (b * nc + c, 0))
    vec = pl.BlockSpec((1, d), lambda b, c: (0, 0))
    return pl.pallas_call(
        _rwkv_recur_kernel,
        grid=(batch, nc),
        in_specs=[tile] * 6 + [vec] * 5,
        out_specs=tile,
        out_shape=jax.ShapeDtypeStruct((n_tok, d), BF16),
        scratch_shapes=[pltpu.VMEM((N_HEADS // RWKV_HEADS_PER_TILE, V7X_MXU_DIM, V7X_MXU_DIM), F32)],
        compiler_params=pltpu.CompilerParams(
            dimension_semantics=("parallel", "arbitrary"),
            vmem_limit_bytes=_vmem_limit(2 * 7 * C * d * 4 + 32 * C * d * 4 + 64 * V7X_MXU_DIM ** 2 * 4)),
        name="rwkv_recur",
    )(r, lw, k, v, a, g, k_k, k_a, r_k, gn_g, gn_b)


def _rank_among(vals, i):
    cnt = 0
    for j, vj in enumerate(vals):
        if j == i:
            continue
        before = (vj >= vals[i]) if j < i else (vj > vals[i])
        cnt = cnt + before.astype(jnp.int32)
    return cnt


def _pick(ranks, vals, want):
    out = vals[0]
    for rk, vl in zip(ranks[1:], vals[1:]):
        out = jnp.where(rk == want, vl, out)
    return out


def _epilogue_kernel(act_ref, wo_ref, res_ref, lng_ref, lnb_ref, rwt_ref, rb_ref,
                     h_ref, hp_ref, eidx_ref, rank_ref, gcol_ref, cnt_ref, base_ref):
    i = pl.program_id(0)

    @pl.when(i == 0)
    def _():
        base_ref[...] = jnp.zeros_like(base_ref)

    tt = EPI_SUBTILE
    subs = [slice(u * tt, (u + 1) * tt) for u in range(act_ref.shape[0] // tt)]
    each = lambda fn, *lists: [fn(*args) for args in zip(*lists)]
    nt = lambda a, b: lax.dot_general(a, b, (((1,), (1,)), ((), ())), preferred_element_type=F32)

    mixed = each(lambda sl: jnp.dot(act_ref[sl, :], wo_ref[...], preferred_element_type=F32), subs)
    h = each(lambda sl, mx: _layer_norm_rows(ALPHA * res_ref[sl, :] + mx, lng_ref[...], lnb_ref[...]), subs, mixed)
    for sl, hs in zip(subs, h):
        h_ref[sl, :] = hs
        hp_ref[sl, :] = _pack_halves(hs)

    wh, wm, wl = _split3(rwt_ref[...])

    def router_logits(hs):
        hh, hm, hl = _split3(hs)
        return nt(wh, hh) + (nt(wh, hm) + nt(wm, hh)) + (nt(wh, hl) + nt(wm, hm) + nt(wl, hh))

    s = each(lambda hs: jax.nn.sigmoid(router_logits(hs)), h)

    def select(sg):
        s_sel = sg + rb_ref[...]
        rows = [s_sel[e:e + 1, :] for e in range(N_EXPERTS)]
        grp_score, grp_i0, grp_i1 = [], [], []
        for gi in range(N_GROUPS):
            vals = rows[gi * EXPERTS_PER_GROUP:(gi + 1) * EXPERTS_PER_GROUP]
            ranks = [_rank_among(vals, q) for q in range(EXPERTS_PER_GROUP)]
            idx = [jnp.full_like(ranks[0], q) for q in range(EXPERTS_PER_GROUP)]
            grp_score.append(_pick(ranks, vals, 0) + _pick(ranks, vals, 1))
            grp_i0.append(_pick(ranks, idx, 0))
            grp_i1.append(_pick(ranks, idx, 1))
        g_ranks = [_rank_among(grp_score, q) for q in range(N_GROUPS)]
        gidx = [jnp.full_like(g_ranks[0], q) for q in range(N_GROUPS)]
        g_star = _pick(g_ranks, gidx, 0)
        e0 = g_star * EXPERTS_PER_GROUP + _pick(g_ranks, grp_i0, 0)
        e1 = g_star * EXPERTS_PER_GROUP + _pick(g_ranks, grp_i1, 0)
        return e0, e1

    picked = each(select, s)
    e_iota = lax.broadcasted_iota(jnp.int32, (N_EXPERTS, tt), 0)
    hit0 = each(lambda pk: e_iota == pk[0], picked)
    hit1 = each(lambda pk: e_iota == pk[1], picked)

    def gates(sg, h0, h1):
        gate0 = jnp.sum(jnp.where(h0, sg, 0.0), axis=0, keepdims=True)
        gate1 = jnp.sum(jnp.where(h1, sg, 0.0), axis=0, keepdims=True)
        denom = gate0 + gate1
        return gate0 / denom, gate1 / denom

    gate = each(gates, s, hit0, hit1)

    member = each(lambda h0, h1: jnp.where(h0 | h1, 1.0, 0.0), hit0, hit1)
    ui = lax.broadcasted_iota(jnp.int32, (tt, tt), 0)
    uj = lax.broadcasted_iota(jnp.int32, (tt, tt), 1)
    before = (ui < uj).astype(BF16)
    prefix = each(lambda mb: jnp.dot(mb.astype(BF16), before, preferred_element_type=F32), member)
    base = base_ref[:, 0:1]
    for u, sl in enumerate(subs):
        pre = prefix[u] + base
        rank0 = jnp.sum(jnp.where(hit0[u], pre, 0.0), axis=0, keepdims=True)
        rank1 = jnp.sum(jnp.where(hit1[u], pre, 0.0), axis=0, keepdims=True)
        eidx_ref[:, sl] = jnp.concatenate(picked[u], axis=0)
        rank_ref[:, sl] = jnp.concatenate([rank0, rank1], axis=0).astype(jnp.int32)
        base = base + jnp.sum(member[u], axis=1, keepdims=True)
    base_ref[...] = jnp.broadcast_to(base, base_ref.shape)
    cnt_ref[...] = jnp.broadcast_to(base, cnt_ref.shape).astype(jnp.int32)

    ident = (ui == uj).astype(BF16)

    def gate_columns(gt):
        gpad = jnp.concatenate([gt[0], gt[1], jnp.zeros((V7X_LANES - 2, tt), F32)], axis=0)
        return sum(nt(ident, part) for part in _split3(gpad))

    for sl, gc in zip(subs, each(gate_columns, gate)):
        gcol_ref[sl, :] = gc


def _mixer_epilogue(act, wo, res, ln_g, ln_b, router_wt, router_bias):
    n_tok, d = res.shape
    k_in = act.shape[1]
    tt = EPI_TILE
    n_tiles = n_tok // tt
    full = lambda arr: pl.BlockSpec(arr.shape, lambda i: (0,) * arr.ndim)
    return pl.pallas_call(
        _epilogue_kernel,
        grid=(n_tiles,),
        in_specs=[pl.BlockSpec((tt, k_in), lambda i: (i, 0)), full(wo), pl.BlockSpec((tt, d), lambda i: (i, 0)),
                  full(ln_g), full(ln_b), full(router_wt), full(router_bias)],
        out_specs=[pl.BlockSpec((tt, d), lambda i: (i, 0)),
                   pl.BlockSpec((tt, d // 2), lambda i: (i, 0)),
                   pl.BlockSpec((2, tt), lambda i: (0, i)),
                   pl.BlockSpec((2, tt), lambda i: (0, i)),
                   pl.BlockSpec((tt, V7X_LANES), lambda i: (i, 0)),
                   pl.BlockSpec((N_EXPERTS, V7X_LANES), lambda i: (0, 0))],
        out_shape=[jax.ShapeDtypeStruct((n_tok, d), F32),
                   jax.ShapeDtypeStruct((n_tok, d // 2), jnp.uint32),
                   jax.ShapeDtypeStruct((2, n_tok), jnp.int32),
                   jax.ShapeDtypeStruct((2, n_tok), jnp.int32),
                   jax.ShapeDtypeStruct((n_tok, V7X_LANES), F32),
                   jax.ShapeDtypeStruct((N_EXPERTS, V7X_LANES), jnp.int32)],
        scratch_shapes=[pltpu.VMEM((N_EXPERTS, V7X_LANES), F32)],
        compiler_params=pltpu.CompilerParams(
            dimension_semantics=("arbitrary",),
            vmem_limit_bytes=_vmem_limit(2 * int(wo.size) * 2 + 2 * tt * (k_in * 2 + 2 * d * 4) + 16 * tt * d * 4)),
        name="mixer_epilogue",
    )(act, wo, res, ln_g, ln_b, router_wt, router_bias)


def _dispatch_kernel(dest_ref, h_ref, xs_in_ref, xs_ref, sem):
    del xs_in_ref
    tt = h_ref.shape[0]

    def row_copy(t, slot):
        d = dest_ref[0, 0, slot * tt + t]
        return pltpu.make_async_copy(h_ref.at[pl.ds(t, 1)], xs_ref.at[pl.ds(d, 1)], sem)

    def issue(t0, carry):
        for u in range(DMA_ISSUE_UNROLL):
            t = t0 * DMA_ISSUE_UNROLL + u
            row_copy(t, 0).start(priority=0)
            row_copy(t, 1).start(priority=1)
        return carry

    lax.fori_loop(0, tt // DMA_ISSUE_UNROLL, issue, 0)
    for _ in range(2):
        pltpu.make_async_copy(h_ref, xs_ref.at[pl.ds(0, tt)], sem).wait()


def _moe_dispatch(h, dest_tiles, n_rows):
    n_tok, d = h.shape
    tt = dest_tiles.shape[2] // 2
    n_tiles = n_tok // tt
    zeros = jnp.zeros((n_rows, d), h.dtype)
    return pl.pallas_call(
        _dispatch_kernel,
        grid=(n_tiles,),
        in_specs=[pl.BlockSpec((1, 1, 2 * tt), lambda i: (i, 0, 0), memory_space=pltpu.SMEM),
                  pl.BlockSpec((tt, d), lambda i: (i, 0)),
                  pl.BlockSpec(memory_space=pl.ANY)],
        out_specs=pl.BlockSpec(memory_space=pl.ANY),
        out_shape=jax.ShapeDtypeStruct((n_rows, d), h.dtype),
        scratch_shapes=[pltpu.SemaphoreType.DMA(())],
        input_output_aliases={2: 0},
        compiler_params=pltpu.CompilerParams(dimension_semantics=("arbitrary",),
                                             vmem_limit_bytes=_vmem_limit(2 * tt * d * 4)),
        name="moe_dispatch",
    )(dest_tiles, h, zeros)


def _ffn_kernel(bexp_ref, nblk_ref, x_ref, wg_ref, wu_ref, wd_ref, y_ref, wg_bf, wu_bf, wd_bf):
    j = pl.program_id(0)
    active = j < nblk_ref[0]
    new_expert = (j == 0) | (bexp_ref[j] != bexp_ref[jnp.maximum(j - 1, 0)])

    @pl.when(active & new_expert)
    def _():
        wg_bf[...] = wg_ref[0, 0].astype(BF16)
        wu_bf[...] = wu_ref[0, 0].astype(BF16)
        wd_bf[...] = wd_ref[0, 0].astype(BF16)

    @pl.when(active)
    def _():
        half = x_ref.shape[1]
        x_lo, x_hi = (v.astype(BF16) for v in _unpack_halves(x_ref[...]))
        mm = lambda a, b: jnp.dot(a, b, preferred_element_type=F32)
        gate = mm(x_lo, wg_bf[:half, :]) + mm(x_hi, wg_bf[half:, :])
        up = mm(x_lo, wu_bf[:half, :]) + mm(x_hi, wu_bf[half:, :])
        hid = (gate * jax.nn.sigmoid(gate)) * up
        y_ref[...] = _pack_halves(mm(hid.astype(BF16), wd_bf[...]))

    @pl.when(jnp.logical_not(active))
    def _():
        y_ref[...] = jnp.zeros_like(y_ref)


def _moe_ffn(xs, block_exp, n_used, layer, wg, wu, wd):
    n_rows, half = xs.shape
    d = 2 * half
    blk = MOE_BLOCK
    n_blocks = n_rows // blk
    de = wg.shape[3]
    expert = lambda j, be, nb: (layer, be[j], 0, 0)
    grid_spec = pltpu.PrefetchScalarGridSpec(
        num_scalar_prefetch=2,
        grid=(n_blocks,),
        in_specs=[pl.BlockSpec((blk, half), lambda j, be, nb: (j, 0)),
                  pl.BlockSpec((1, 1, d, de), expert),
                  pl.BlockSpec((1, 1, d, de), expert),
                  pl.BlockSpec((1, 1, de, d), expert)],
        out_specs=pl.BlockSpec((blk, half), lambda j, be, nb: (j, 0)),
        scratch_shapes=[pltpu.VMEM((d, de), BF16), pltpu.VMEM((d, de), BF16), pltpu.VMEM((de, d), BF16)],
    )
    return pl.pallas_call(
        _ffn_kernel,
        grid_spec=grid_spec,
        out_shape=jax.ShapeDtypeStruct((n_rows, half), jnp.uint32),
        compiler_params=pltpu.CompilerParams(
            dimension_semantics=("arbitrary",),
            vmem_limit_bytes=_vmem_limit(3 * d * de * (2 * 4 + 2) + 4 * blk * d * 4 + 4 * blk * de * 4)),
        name="moe_ffn",
    )(block_exp, n_used, xs, wg, wu, wd)


def _combine_kernel(dest_ref, dest_next_ref, ys_ref, gcol_ref, res_ref, lng_ref, lnb_ref, o_ref, buf_ref, sem):
    i = pl.program_id(0)
    n = pl.num_programs(0)
    tt = res_ref.shape[0]
    cur = i % 2

    def gather(idx_ref, buf):
        def row_copy(t, slot):
            d = idx_ref[0, 0, slot * tt + t]
            return pltpu.make_async_copy(ys_ref.at[pl.ds(d, 1)], buf_ref.at[buf, slot, pl.ds(t, 1)], sem.at[buf])

        def issue(t0, carry):
            for u in range(DMA_ISSUE_UNROLL):
                t = t0 * DMA_ISSUE_UNROLL + u
                row_copy(t, 0).start(priority=0)
                row_copy(t, 1).start(priority=1)
            return carry

        lax.fori_loop(0, tt // DMA_ISSUE_UNROLL, issue, 0)

    @pl.when(i == 0)
    def _():
        gather(dest_ref, cur)

    @pl.when(i + 1 < n)
    def _():
        gather(dest_next_ref, 1 - cur)

    for slot in range(2):
        pltpu.make_async_copy(ys_ref.at[pl.ds(0, tt)], buf_ref.at[cur, slot], sem.at[cur]).wait()

    gcol = gcol_ref[...]
    lo0, hi0 = _unpack_halves(buf_ref[cur, 0])
    lo1, hi1 = _unpack_halves(buf_ref[cur, 1])
    g0, g1 = gcol[:, 0:1], gcol[:, 1:2]
    ffn = jnp.concatenate([lo0 * g0 + lo1 * g1, hi0 * g0 + hi1 * g1], axis=1)
    o_ref[...] = _layer_norm_rows(ALPHA * res_ref[...] + ffn, lng_ref[...], lnb_ref[...])


def _moe_combine(ys, dest_tiles, gcol, res, ln_g, ln_b):
    n_tok, d = res.shape
    tt = dest_tiles.shape[2] // 2
    n_tiles = n_tok // tt
    full = lambda arr: pl.BlockSpec(arr.shape, lambda i: (0,) * arr.ndim)
    return pl.pallas_call(
        _combine_kernel,
        grid=(n_tiles,),
        in_specs=[pl.BlockSpec((1, 1, 2 * tt), lambda i: (i, 0, 0), memory_space=pltpu.SMEM),
                  pl.BlockSpec((1, 1, 2 * tt), lambda i: (jnp.minimum(i + 1, n_tiles - 1), 0, 0),
                               memory_space=pltpu.SMEM),
                  pl.BlockSpec(memory_space=pl.ANY),
                  pl.BlockSpec((tt, V7X_LANES), lambda i: (i, 0)),
                  pl.BlockSpec((tt, d), lambda i: (i, 0)),
                  full(ln_g), full(ln_b)],
        out_specs=pl.BlockSpec((tt, d), lambda i: (i, 0)),
        out_shape=jax.ShapeDtypeStruct((n_tok, d), F32),
        scratch_shapes=[pltpu.VMEM((2, 2, tt, ys.shape[1]), ys.dtype), pltpu.SemaphoreType.DMA((2,))],
        compiler_params=pltpu.CompilerParams(
            dimension_semantics=("arbitrary",),
            vmem_limit_bytes=_vmem_limit(4 * tt * d * 2 + 2 * 3 * tt * d * 4)),
        name="moe_combine",
    )(dest_tiles, dest_tiles, ys, gcol, res, ln_g, ln_b)


def _moe_layer(h, h_packed, eidx, rank, counts, gcol, layer, wg, wu, wd, ln_g, ln_b):
    n_tok, d = h.shape
    blk = MOE_BLOCK
    n_rows = (2 * n_tok + N_EXPERTS * (blk - 1) + blk - 1) // blk * blk
    n_blocks = n_rows // blk
    counts = counts[:, 0]
    padded = (counts + blk - 1) // blk * blk
    pend = jnp.cumsum(padded)
    pstart = pend - padded
    onehot = (eidx[:, :, None] == jnp.arange(N_EXPERTS, dtype=jnp.int32)).astype(jnp.int32)
    dest = rank + jnp.sum(onehot * pstart, axis=-1)
    tiles = lambda tt: dest.reshape(2, n_tok // tt, tt).transpose(1, 0, 2).reshape(n_tok // tt, 1, 2 * tt)
    blk_start = jnp.arange(n_blocks, dtype=jnp.int32) * blk
    block_exp = jnp.minimum(jnp.sum((blk_start[:, None] >= pend[None, :]).astype(jnp.int32), axis=1),
                            N_EXPERTS - 1).astype(jnp.int32)
    n_used = (pend[-1:] // blk).astype(jnp.int32)
    blk_idx = jnp.arange(n_blocks, dtype=jnp.int32)
    block_exp = jnp.where(blk_idx < n_used[0], block_exp, block_exp[jnp.maximum(n_used[0] - 1, 0)])
    xs = _moe_dispatch(h_packed, tiles(DISPATCH_TILE), n_rows)
    ys = _moe_ffn(xs, block_exp, n_used, layer, wg, wu, wd)
    return _moe_combine(ys, tiles(COMBINE_TILE), gcol, h, ln_g, ln_b)


def _fox_proj_kernel(x_ref, wq_ref, wk_ref, wv_ref, wf_ref, bf_ref, q_ref, k_ref, v_ref, c_ref, carry_ref):
    t = pl.program_id(1)

    @pl.when(t == 0)
    def _():
        carry_ref[...] = jnp.zeros_like(carry_ref)

    tt = PROJ_SUBTILE
    subs = [slice(u * tt, (u + 1) * tt) for u in range(x_ref.shape[0] // tt)]
    each = lambda fn, *lists: [fn(*args) for args in zip(*lists)]
    parts = each(lambda sl: _split3(x_ref[sl, :]), subs)
    wh, wm, wl = _split3(wf_ref[...])
    mm = lambda a, b: jnp.dot(a, b, preferred_element_type=F32)
    logit = each(lambda p: (mm(p[0], wh) + (mm(p[0], wm) + mm(p[1], wh))
                            + (mm(p[0], wl) + mm(p[1], wm) + mm(p[2], wh))) + bf_ref[...], parts)
    for sl, p in zip(subs, parts):
        q_ref[sl, :] = (mm(p[0], wq_ref[...]) * LOG2E).astype(q_ref.dtype)
    for sl, p in zip(subs, parts):
        k_ref[sl, :] = mm(p[0], wk_ref[...]).astype(k_ref.dtype)
    for sl, p in zip(subs, parts):
        v_ref[sl, :] = mm(p[0], wv_ref[...]).astype(v_ref.dtype)
    ti = lax.broadcasted_iota(jnp.int32, (tt, tt), 0)
    tj = lax.broadcasted_iota(jnp.int32, (tt, tt), 1)
    tril = (tj <= ti).astype(BF16)
    local = each(lambda lg: sum(mm(tril, part) for part in _split3(jax.nn.log_sigmoid(lg))), logit)
    carry = carry_ref[0:1, :]
    for sl, cs in zip(subs, local):
        c = cs + carry
        c_ref[sl, :] = c
        carry = c[tt - 1:tt, :]
    carry_ref[...] = jnp.broadcast_to(carry, carry_ref.shape)


def _fox_proj(x2, batch, seq_len, wq, wk, wv, wf, b_f):
    n_tok, d = x2.shape
    tt = PROJ_TILE
    nt = seq_len // tt
    tile = pl.BlockSpec((tt, d), lambda b, t: (b * nt + t, 0))
    full = lambda arr: pl.BlockSpec(arr.shape, lambda b, t: (0,) * arr.ndim)
    return pl.pallas_call(
        _fox_proj_kernel,
        grid=(batch, nt),
        in_specs=[tile, full(wq), full(wk), full(wv), full(wf), full(b_f)],
        out_specs=[tile, tile, tile, pl.BlockSpec((tt, N_HEADS), lambda b, t: (b * nt + t, 0))],
        out_shape=[jax.ShapeDtypeStruct((n_tok, d), BF16)] * 3 + [jax.ShapeDtypeStruct((n_tok, N_HEADS), F32)],
        scratch_shapes=[pltpu.VMEM((8, N_HEADS), F32)],
        compiler_params=pltpu.CompilerParams(
            dimension_semantics=("parallel", "arbitrary"),
            vmem_limit_bytes=_vmem_limit(2 * 3 * d * d * 2 + 2 * tt * d * (4 + 3 * 2) + 8 * tt * d * 4)),
        name="fox_proj",
    )(x2, wq, wk, wv, wf, b_f)


def _fox_attn_kernel(q_ref, k_ref, v_ref, ct_ref, o_ref):
    hp = pl.program_id(1)
    seq_len = q_ref.shape[0]
    tq = ATTN_TILE
    tk = tq
    nq = seq_len // tq
    N = HEAD_DIM
    nt = lambda a, b: lax.dot_general(a, b, (((1,), (1,)), ((), ())), preferred_element_type=F32)
    mm = lambda a, b: jnp.dot(a, b, preferred_element_type=F32)

    lane = lax.broadcasted_iota(jnp.int32, (tq, 2 * N), 1)
    vrow = lax.broadcasted_iota(jnp.int32, (2 * N, tk), 0)
    orow = lax.broadcasted_iota(jnp.int32, (2 * N, tq), 0)
    kpos = lax.broadcasted_iota(jnp.int32, (tk, tq), 0)
    qpos = lax.broadcasted_iota(jnp.int32, (tk, tq), 1)
    causal = kpos <= qpos
    ident_v = (lax.broadcasted_iota(jnp.int32, (2 * N, 2 * N), 0)
               == lax.broadcasted_iota(jnp.int32, (2 * N, 2 * N), 1)).astype(BF16)

    cq_row = [ct_ref[0, pl.ds(2 * hp + u, 1), :] * LOG2E for u in range(2)]

    def key_gates(j):
        return [jnp.broadcast_to(cq_row[u][:, j * tk:(j + 1) * tk], (tq, tk)).T for u in range(2)]

    m = [[None, None] for _ in range(nq)]
    l = [[None, None] for _ in range(nq)]
    acc = [None] * nq

    zero_q = jnp.zeros((tq, 2 * N), BF16)
    keys, values = {}, {}

    def scores(j, qi):
        if j not in keys:
            keys[j] = (k_ref[j * tk:(j + 1) * tk, :], key_gates(j))
        k_j, ck_b = keys[j]
        q = q_ref[qi * tq:(qi + 1) * tq, :]
        t = [nt(k_j, jnp.where((lane < N) == (u == 0), q, zero_q)) - ck_b[u] for u in range(2)]
        return [jnp.where(causal, tu, -jnp.inf) for tu in t] if qi == j else t

    def absorb(j, qi, t):
        if j not in values:
            v_t = nt(ident_v, v_ref[j * tk:(j + 1) * tk, :]).astype(BF16)
            values[j] = [jnp.where(vrow < N, v_t, jnp.zeros_like(v_t)), jnp.where(vrow < N, jnp.zeros_like(v_t), v_t)]
        v_heads = values[j]
        cq = [cq_row[u][:, qi * tq:(qi + 1) * tq] for u in range(2)]
        rmax = [jnp.max(t[u], axis=0, keepdims=True) + cq[u] for u in range(2)]
        m_new = rmax if j == 0 else [jnp.maximum(m[qi][u], rmax[u]) for u in range(2)]
        p = [jnp.exp2(t[u] + (cq[u] - m_new[u])) for u in range(2)]
        psum = [jnp.sum(p[u], axis=0, keepdims=True) for u in range(2)]
        pv = mm(v_heads[0], p[0].astype(BF16)) + mm(v_heads[1], p[1].astype(BF16))
        if j == 0:
            acc[qi] = pv
            l[qi] = psum
        else:
            alpha = [jnp.exp2(m[qi][u] - m_new[u]) for u in range(2)]
            acc[qi] = acc[qi] * jnp.where(orow < N, alpha[0], alpha[1]) + pv
            l[qi] = [alpha[u] * l[qi][u] + psum[u] for u in range(2)]
        m[qi] = m_new
        if j == qi:
            o_t = acc[qi] / jnp.where(orow < N, l[qi][0], l[qi][1])
            o_ref[qi * tq:(qi + 1) * tq, :] = o_t.T.astype(o_ref.dtype)

    pairs = [(j, qi) for j in range(nq) for qi in range(j, nq)]
    t_next = scores(*pairs[0])
    for n, (j, qi) in enumerate(pairs):
        t_cur = t_next
        if n + 1 < len(pairs):
            t_next = scores(*pairs[n + 1])
        absorb(j, qi, t_cur)


def _fox_attn(q, k, v, c_t, batch, seq_len):
    n_tok, d = q.shape
    pair = 2 * HEAD_DIM
    seq = pl.BlockSpec((seq_len, pair), lambda b, hp: (b, hp))
    return pl.pallas_call(
        _fox_attn_kernel,
        grid=(batch, d // pair),
        in_specs=[seq, seq, seq, pl.BlockSpec((1, N_HEADS, seq_len), lambda b, hp: (b, 0, 0))],
        out_specs=seq,
        out_shape=jax.ShapeDtypeStruct((n_tok, d), BF16),
        compiler_params=pltpu.CompilerParams(
            dimension_semantics=("parallel", "arbitrary"),
            vmem_limit_bytes=_vmem_limit(2 * 4 * seq_len * pair * 2 + 64 * ATTN_TILE * ATTN_TILE * 4)),
        name="fox_attn",
    )(q, k, v, c_t)


def kernel(x, rw_mix, rw_wr, rw_wk, rw_wv, rw_wo, rw_w0, rw_w1, rw_w2, rw_a0, rw_a1, rw_a2, rw_g1, rw_g2,
           rw_kk, rw_ka, rw_rk, rw_gn_g, rw_gn_b, fx_w_in, fx_b_f, fx_wo, router_w, router_bias,
           moe_w_gate, moe_w_up, moe_w_down, ln_g, ln_b):
    batch, seq_len, d = x.shape
    n_tok = batch * seq_len
    bf = lambda w: w.astype(BF16)
    row = lambda w: w.reshape(1, -1)
    router_wt = router_w.T
    router_b = router_bias.reshape(N_EXPERTS, 1)
    h = x.reshape(n_tok, d)

    for i in range(DEPTH):
        j = i // 2
        if i % 2 == 0:
            r, lw, k, v, a, g = _rwkv_proj(
                h, seq_len, rw_mix[j], bf(rw_wr[j]), bf(rw_wk[j]), bf(rw_wv[j]), bf(rw_w1[j]), bf(rw_w2[j]),
                bf(rw_a1[j]), bf(rw_a2[j]), bf(rw_g1[j]), bf(rw_g2[j]), row(rw_w0[j]), row(rw_a0[j]))
            act = _rwkv_recur(r, lw, k, v, a, g, batch, seq_len, row(rw_kk[j]), row(rw_ka[j]), row(rw_rk[j]),
                              row(rw_gn_g[j]), row(rw_gn_b[j]))
            wo = bf(rw_wo[j])
        else:
            w_in = fx_w_in[j]
            scale = HEAD_DIM ** -0.5
            q, k, v, c = _fox_proj(h, batch, seq_len, bf(w_in[:, :d] * scale), bf(w_in[:, d:2 * d]),
                                   bf(w_in[:, 2 * d:3 * d]), w_in[:, 3 * d:], row(fx_b_f[j]))
            c_t = c.reshape(batch, seq_len, N_HEADS).transpose(0, 2, 1)
            act = _fox_attn(q, k, v, c_t, batch, seq_len)
            wo = bf(fx_wo[j])
        h, h_packed, eidx, rank, gcol, counts = _mixer_epilogue(act, wo, h, row(ln_g[i, 0]), row(ln_b[i, 0]),
                                                                router_wt, router_b)
        h = _moe_layer(h, h_packed, eidx, rank, counts, gcol, i, moe_w_gate, moe_w_up, moe_w_down,
                       row(ln_g[i, 1]), row(ln_b[i, 1]))
    return h.reshape(batch, seq_len, d)
```

```python
import functools
import math

import jax
import jax.numpy as jnp
from jax import lax
from jax.experimental import pallas as pl
from jax.experimental.pallas import tpu as pltpu

D_MODEL = 1024
HEAD_DIM = 64
N_HEADS = D_MODEL // HEAD_DIM
N_EXPERTS = 16
N_GROUPS = 4
EXPERTS_PER_GROUP = N_EXPERTS // N_GROUPS
D_EXPERT = 512
GN_EPS = 64e-5
LN_EPS = 1e-5
DEPTH = 2
ALPHA = (2 * DEPTH) ** 0.25
LOG2E = math.log2(math.e)

V7X_LANES = 128
V7X_VMEM_BYTES = 64 * 2 ** 20

V7X_MXU_DIM = 256

RWKV_CHUNK = 64
RWKV_CHUNKS_PER_STEP = 2
RWKV_HEADS_PER_TILE = V7X_MXU_DIM // HEAD_DIM
PROJ_TILE = 512
PROJ_SUBTILE = 256
EPI_TILE = 1024
EPI_SUBTILE = 256
ATTN_TILE = 256
MOE_BLOCK = 512
MOE_SUBTILE = 256
DISPATCH_TILE = 1024
COMBINE_TILE = 256
DMA_ISSUE_UNROLL = 8

F32 = jnp.float32
BF16 = jnp.bfloat16


def _vmem_limit(n_bytes):
    return int(min(n_bytes + 16 * 2 ** 20, V7X_VMEM_BYTES - 8 * 2 ** 20))


def _split3(x):
    hi = x.astype(BF16)
    r1 = x - hi.astype(F32)
    mid = r1.astype(BF16)
    lo = (r1 - mid.astype(F32)).astype(BF16)
    return hi, mid, lo


def _pack_halves(x):
    half = x.shape[1] // 2
    bits = lambda v: lax.bitcast_convert_type(v.astype(BF16).astype(F32), jnp.uint32)
    return (bits(x[:, :half]) >> 16) | (bits(x[:, half:]) & jnp.uint32(0xFFFF0000))


def _unpack_halves(w):
    lo = lax.bitcast_convert_type(w << 16, F32)
    hi = lax.bitcast_convert_type(w & jnp.uint32(0xFFFF0000), F32)
    return lo, hi


def _layer_norm_rows(x, g, b):
    mu = jnp.mean(x, axis=-1, keepdims=True)
    xc = x - mu
    var = jnp.mean(xc * xc, axis=-1, keepdims=True)
    return xc * lax.rsqrt(var + LN_EPS) * g + b


def _rwkv_proj_kernel(x_ref, xp_ref, mix_ref, wr_ref, wk_ref, wv_ref, w1_ref, w2_ref, a1_ref, a2_ref,
                      g1_ref, g2_ref, w0_ref, a0_ref,
                      r_ref, lw_ref, k_ref, v_ref, a_ref, g_ref, *, tiles_per_seq):
    i = pl.program_id(0)
    x = x_ref[...]
    tt = x.shape[0]
    first = (i % tiles_per_seq) == 0
    prev_row = jnp.where(first, 0.0, xp_ref[7:8, :])
    row = lax.broadcasted_iota(jnp.int32, (tt, 1), 0)
    xprev = jnp.where(row == 0, prev_row, pltpu.roll(x, 1, axis=0))
    xx = xprev - x
    mix = mix_ref[...]
    subs = [slice(u * PROJ_SUBTILE, (u + 1) * PROJ_SUBTILE) for u in range(tt // PROJ_SUBTILE)]
    each = lambda fn, *lists: [fn(*args) for args in zip(*lists)]
    mixed = lambda j: each(lambda sl: (x[sl] + xx[sl] * mix[j:j + 1, :]).astype(BF16), subs)
    xr, xw, xk, xv, xa, xg = (mixed(j) for j in range(6))
    mm = lambda a, w_ref: jnp.dot(a, w_ref[...], preferred_element_type=F32)
    w_mid = each(lambda a: mm(a, w1_ref), xw)
    a_mid = each(lambda a: mm(a, a1_ref), xa)
    g_mid = each(lambda a: mm(a, g1_ref), xg)
    for sl, a in zip(subs, xr):
        r_ref[sl, :] = mm(a, wr_ref).astype(r_ref.dtype)
    for sl, a in zip(subs, xk):
        k_ref[sl, :] = mm(a, wk_ref).astype(k_ref.dtype)
    for sl, a in zip(subs, xv):
        v_ref[sl, :] = mm(a, wv_ref).astype(v_ref.dtype)
    z = each(lambda t: w0_ref[...] + mm(jnp.tanh(t).astype(BF16), w2_ref), w_mid)
    for sl, t in zip(subs, a_mid):
        a_ref[sl, :] = jax.nn.sigmoid(a0_ref[...] + mm(t.astype(BF16), a2_ref)).astype(a_ref.dtype)
    for sl, t in zip(subs, g_mid):
        g_ref[sl, :] = mm(jax.nn.sigmoid(t).astype(BF16), g2_ref).astype(g_ref.dtype)
    for sl, zs in zip(subs, z):
        lw_ref[sl, :] = -jnp.exp(-jax.nn.softplus(-zs) - 0.5)


def _rwkv_proj(x2, seq_len, mix, wr, wk, wv, w1, w2, a1, a2, g1, g2, w0, a0):
    n_tok, d = x2.shape
    tt = PROJ_TILE
    n_tiles = n_tok // tt
    tiles_per_seq = seq_len // tt
    tile = pl.BlockSpec((tt, d), lambda i: (i, 0))
    prev = pl.BlockSpec((8, d), lambda i: (jnp.maximum(i * (tt // 8) - 1, 0), 0))
    full = lambda arr: pl.BlockSpec(arr.shape, lambda i: (0,) * arr.ndim)
    weights = (mix, wr, wk, wv, w1, w2, a1, a2, g1, g2, w0, a0)
    out_dtypes = (BF16, F32, BF16, BF16, BF16, BF16)
    w_bytes = sum(int(w.size) * w.dtype.itemsize for w in weights)
    return pl.pallas_call(
        functools.partial(_rwkv_proj_kernel, tiles_per_seq=tiles_per_seq),
        grid=(n_tiles,),
        in_specs=[tile, prev] + [full(w) for w in weights],
        out_specs=[tile] * 6,
        out_shape=[jax.ShapeDtypeStruct((n_tok, d), dt) for dt in out_dtypes],
        compiler_params=pltpu.CompilerParams(
            dimension_semantics=("parallel",),
            vmem_limit_bytes=_vmem_limit(2 * w_bytes + 2 * tt * d * (4 + 4 + 5 * 2) + 8 * tt * d * 4)),
        name="rwkv_proj",
    )(x2, x2, *weights)


def _rwkv_recur_kernel(r_ref, lw_ref, k_ref, v_ref, a_ref, g_ref, kk_ref, ka_ref, rk_ref, gng_ref, gnb_ref,
                       o_ref, s_ref):
    c = pl.program_id(1)
    C = RWKV_CHUNK
    N = HEAD_DIM

    @pl.when(c == 0)
    def _():
        s_ref[...] = jnp.zeros_like(s_ref)

    nt = lambda p, q: lax.dot_general(p, q, (((1,), (1,)), ((), ())), preferred_element_type=F32)
    mm = lambda p, q: jnp.dot(p, q, preferred_element_type=F32)
    each = lambda fn, *lists: [fn(*args) for args in zip(*lists)]
    bf = lambda x: x.astype(BF16)

    ti = lax.broadcasted_iota(jnp.int32, (C, C), 0)
    tj = lax.broadcasted_iota(jnp.int32, (C, C), 1)
    tril = (tj <= ti).astype(BF16)

    def chunk_terms(rows):
        lw = lw_ref[rows, :]
        cum = sum(mm(tril, part) for part in _split3(lw))
        rho = cum[C // 2 - 1:C // 2, :]
        last = cum[C - 1:C, :]
        r = r_ref[rows, :].astype(F32)
        k = k_ref[rows, :].astype(F32)
        a = a_ref[rows, :].astype(F32)
        k_mod = k * (1.0 + (a - 1.0) * ka_ref[...])
        return dict(e_q=jnp.exp(cum - rho), e_qx=jnp.exp(cum - lw - rho), e_k=jnp.exp(rho - cum),
                    e_end=jnp.exp(last - cum), e_rho=jnp.exp(rho), d_end=jnp.exp(last),
                    r=r, a=a, v=v_ref[rows, :].astype(F32), kk_raw=k * kk_ref[...], k_mod=k_mod,
                    rkk=r * k_mod * rk_ref[...])

    n_ch = r_ref.shape[0] // C
    terms = [chunk_terms(slice(ci * C, (ci + 1) * C)) for ci in range(n_ch)]

    G = RWKV_HEADS_PER_TILE
    R = G * C
    GW = G * N
    er = lax.broadcasted_iota(jnp.int32, (R, GW), 0)
    ec = lax.broadcasted_iota(jnp.int32, (R, GW), 1)
    blk = (er // C) == (ec // N)
    strict = (ec % C) < (er % C)
    incl = (ec % C) <= (er % C)
    eye = (er == ec).astype(F32)
    ones_blk = blk.astype(BF16)

    def expand(x):
        return jnp.where(blk, jnp.concatenate([x] * G, axis=0), 0.0).astype(BF16)

    expand_t = lambda x: bf(jnp.where(blk, jnp.concatenate([x] * G, axis=0), 0.0).T)

    def head_sums(xs):
        parts = _split3(jnp.concatenate(xs, axis=0))
        tot = mm(jnp.concatenate(parts, axis=0), ones_blk)
        n = len(xs) * C
        tot = tot[:n] + tot[n:2 * n] + tot[2 * n:]
        return [tot[u * C:(u + 1) * C] for u in range(len(xs))]

    n_grp = N_HEADS // G
    groups = [slice(gi * GW, (gi + 1) * GW) for gi in range(n_grp)]
    units = [(tm, sl) for tm in terms for sl in groups]
    u_tm = [tm for tm, _ in units]
    u_sl = [sl for _, sl in units]
    pre = head_sums([tm["kk_raw"][:, sl] * tm["kk_raw"][:, sl] for tm, sl in units]
                    + [tm["rkk"][:, sl] for tm, sl in units])
    kk_ss, rkk_sum = pre[:len(units)], pre[len(units):]
    kk_n = each(lambda tm, sl, ss: tm["kk_raw"][:, sl] * lax.rsqrt(jnp.maximum(ss, 1e-24)), u_tm, u_sl, kk_ss)
    b_n = each(lambda tm, sl, kk_g: kk_g * tm["a"][:, sl], u_tm, u_sl, kk_n)
    kk_q = each(lambda tm, sl, kk_g: kk_g * tm["e_qx"][:, sl], u_tm, u_sl, kk_n)
    r_q = each(lambda tm, sl: tm["r"][:, sl] * tm["e_q"][:, sl], u_tm, u_sl)
    q2 = each(lambda x, y: jnp.concatenate([expand(x), expand(y)], axis=0), kk_q, r_q)
    a_k = each(lambda tm, sl, q: nt(q, expand(tm["k_mod"][:, sl] * tm["e_k"][:, sl])), u_tm, u_sl, q2)
    a_b = each(lambda tm, sl, q, b_g: nt(q, expand(b_g * tm["e_k"][:, sl])), u_tm, u_sl, q2, b_n)
    a_kk = each(lambda x: jnp.where(strict, x[:R], 0.0), a_k)
    a_rk = each(lambda x: jnp.where(incl, x[R:], 0.0), a_k)
    a_kb = each(lambda x: jnp.where(strict, x[:R], 0.0), a_b)
    a_rb = each(lambda x: jnp.where(incl, x[R:], 0.0), a_b)
    t_inv = each(lambda x: eye - x, a_kb)
    p = each(lambda x: mm(bf(-x), bf(-x)), a_kb)
    for _ in range(int(math.log2(C)) - 2):
        both = each(lambda pg, tg: mm(bf(pg), jnp.concatenate([bf(pg), bf(tg)], axis=1)), p, t_inv)
        p = each(lambda x: x[:, :R], both)
        t_inv = each(lambda tg, x: tg + x[:, R:], t_inv, both)
    t_inv = each(lambda pg, tg: tg + mm(bf(pg), bf(tg)), p, t_inv)
    kd_t = each(lambda tm, sl: expand_t(tm["k_mod"][:, sl] * tm["e_end"][:, sl]), u_tm, u_sl)
    bd_t = each(lambda tm, sl, b_g: expand_t(b_g * tm["e_end"][:, sl]), u_tm, u_sl, b_n)
    d_col = each(lambda tm, sl: jnp.broadcast_to(tm["d_end"][:, sl], (GW, GW)).T, u_tm, u_sl)
    v_e = each(lambda tm, sl: expand(tm["v"][:, sl]), u_tm, u_sl)
    av = each(lambda x, y, ve: mm(bf(jnp.concatenate([x, y], axis=0)), ve), a_kk, a_rk, v_e)
    q2_abs = each(lambda tm, sl, x, y: jnp.concatenate([expand(x * tm["e_rho"][:, sl]), expand(y * tm["e_rho"][:, sl])],
                                                       axis=0), u_tm, u_sl, kk_q, r_q)

    st = [s_ref[gi] for gi in range(n_grp)]
    ys = []
    for ci in range(n_ch):
        span = slice(ci * n_grp, (ci + 1) * n_grp)
        qs = each(lambda q, s: mm(q, bf(s)), q2_abs[span], st)
        sa_e = each(lambda tg, q, x: bf(mm(bf(tg), bf(q[:R] + x[:R]))), t_inv[span], qs, av[span])
        y_e = each(lambda q, x, arb, sa: q[R:] + x[R:] - mm(bf(arb), sa), qs, av[span], a_rb[span], sa_e)
        ys += each(lambda x: sum(x[u * C:(u + 1) * C] for u in range(G)), y_e)
        upd = each(lambda kt, bt, ve, sa: mm(jnp.concatenate([kt, -bt], axis=1), jnp.concatenate([ve, sa], axis=0)),
                   kd_t[span], bd_t[span], v_e[span], sa_e)
        st = each(lambda s, dc, up: s * dc + up, st, d_col[span], upd)
    for gi in range(n_grp):
        s_ref[gi] = st[gi]

    inv_n = 1.0 / N
    ycs = [y - mu * inv_n for y, mu in zip(ys, head_sums(ys))]
    sqs = head_sums([yc * yc for yc in ycs])
    for ui, (tm, sl) in enumerate(units):
        rows = slice((ui // n_grp) * C, (ui // n_grp + 1) * C)
        yn = ycs[ui] * lax.rsqrt(sqs[ui] * inv_n + GN_EPS) * gng_ref[:, sl] + gnb_ref[:, sl]
        o_ref[rows, sl] = ((yn + rkk_sum[ui] * tm["v"][:, sl]) * g_ref[rows, sl]).astype(o_ref.dtype)


def _rwkv_recur(r, lw, k, v, a, g, batch, seq_len, k_k, k_a, r_k, gn_g, gn_b):
    n_tok, d = r.shape
    C = RWKV_CHUNK * RWKV_CHUNKS_PER_STEP
    nc = seq_len // C
    tile = pl.BlockSpec((C, d), lambda b, c: (b * nc + c, 0))
    vec = pl.BlockSpec((1, d), lambda b, c: (0, 0))
    return pl.pallas_call(
        _rwkv_recur_kernel,
        grid=(batch, nc),
        in_specs=[tile] * 6 + [vec] * 5,
        out_specs=tile,
        out_shape=jax.ShapeDtypeStruct((n_tok, d), BF16),
        scratch_shapes=[pltpu.VMEM((N_HEADS // RWKV_HEADS_PER_TILE, V7X_MXU_DIM, V7X_MXU_DIM), F32)],
        compiler_params=pltpu.CompilerParams(
            dimension_semantics=("parallel", "arbitrary"),
            vmem_limit_bytes=_vmem_limit(2 * 7 * C * d * 4 + 32 * C * d * 4 + 64 * V7X_MXU_DIM ** 2 * 4)),
        name="rwkv_recur",
    )(r, lw, k, v, a, g, k_k, k_a, r_k, gn_g, gn_b)


def _rank_among(vals, i):
    cnt = 0
    for j, vj in enumerate(vals):
        if j == i:
            continue
        before = (vj >= vals[i]) if j < i else (vj > vals[i])
        cnt = cnt + before.astype(jnp.int32)
    return cnt


def _pick(ranks, vals, want):
    out = vals[0]
    for rk, vl in zip(ranks[1:], vals[1:]):
        out = jnp.where(rk == want, vl, out)
    return out


def _epilogue_kernel(act_ref, wo_ref, res_ref, lng_ref, lnb_ref, rwt_ref, rb_ref,
                     h_ref, hp_ref, eidx_ref, rank_ref, gcol_ref, cnt_ref, base_ref):
    i = pl.program_id(0)

    @pl.when(i == 0)
    def _():
        base_ref[...] = jnp.zeros_like(base_ref)

    tt = EPI_SUBTILE
    subs = [slice(u * tt, (u + 1) * tt) for u in range(act_ref.shape[0] // tt)]
    each = lambda fn, *lists: [fn(*args) for args in zip(*lists)]
    nt = lambda a, b: lax.dot_general(a, b, (((1,), (1,)), ((), ())), preferred_element_type=F32)

    mixed = each(lambda sl: jnp.dot(act_ref[sl, :], wo_ref[...], preferred_element_type=F32), subs)
    h = each(lambda sl, mx: _layer_norm_rows(ALPHA * res_ref[sl, :] + mx, lng_ref[...], lnb_ref[...]), subs, mixed)
    for sl, hs in zip(subs, h):
        h_ref[sl, :] = hs
        hp_ref[sl, :] = _pack_halves(hs)

    wh, wm, wl = _split3(rwt_ref[...])

    def router_logits(hs):
        hh, hm, hl = _split3(hs)
        return nt(wh, hh) + (nt(wh, hm) + nt(wm, hh)) + (nt(wh, hl) + nt(wm, hm) + nt(wl, hh))

    s = each(lambda hs: jax.nn.sigmoid(router_logits(hs)), h)

    def select(sg):
        s_sel = sg + rb_ref[...]
        rows = [s_sel[e:e + 1, :] for e in range(N_EXPERTS)]
        grp_score, grp_i0, grp_i1 = [], [], []
        for gi in range(N_GROUPS):
            vals = rows[gi * EXPERTS_PER_GROUP:(gi + 1) * EXPERTS_PER_GROUP]
            ranks = [_rank_among(vals, q) for q in range(EXPERTS_PER_GROUP)]
            idx = [jnp.full_like(ranks[0], q) for q in range(EXPERTS_PER_GROUP)]
            grp_score.append(_pick(ranks, vals, 0) + _pick(ranks, vals, 1))
            grp_i0.append(_pick(ranks, idx, 0))
            grp_i1.append(_pick(ranks, idx, 1))
        g_ranks = [_rank_among(grp_score, q) for q in range(N_GROUPS)]
        gidx = [jnp.full_like(g_ranks[0], q) for q in range(N_GROUPS)]
        g_star = _pick(g_ranks, gidx, 0)
        e0 = g_star * EXPERTS_PER_GROUP + _pick(g_ranks, grp_i0, 0)
        e1 = g_star * EXPERTS_PER_GROUP + _pick(g_ranks, grp_i1, 0)
        return e0, e1

    picked = each(select, s)
    e_iota = lax.broadcasted_iota(jnp.int32, (N_EXPERTS, tt), 0)
    hit0 = each(lambda pk: e_iota == pk[0], picked)
    hit1 = each(lambda pk: e_iota == pk[1], picked)

    def gates(sg, h0, h1):
        gate0 = jnp.sum(jnp.where(h0, sg, 0.0), axis=0, keepdims=True)
        gate1 = jnp.sum(jnp.where(h1, sg, 0.0), axis=0, keepdims=True)
        denom = gate0 + gate1
        return gate0 / denom, gate1 / denom

    gate = each(gates, s, hit0, hit1)

    member = each(lambda h0, h1: jnp.where(h0 | h1, 1.0, 0.0), hit0, hit1)
    ui = lax.broadcasted_iota(jnp.int32, (tt, tt), 0)
    uj = lax.broadcasted_iota(jnp.int32, (tt, tt), 1)
    before = (ui < uj).astype(BF16)
    prefix = each(lambda mb: jnp.dot(mb.astype(BF16), before, preferred_element_type=F32), member)
    base = base_ref[:, 0:1]
    for u, sl in enumerate(subs):
        pre = prefix[u] + base
        rank0 = jnp.sum(jnp.where(hit0[u], pre, 0.0), axis=0, keepdims=True)
        rank1 = jnp.sum(jnp.where(hit1[u], pre, 0.0), axis=0, keepdims=True)
        eidx_ref[:, sl] = jnp.concatenate(picked[u], axis=0)
        rank_ref[:, sl] = jnp.concatenate([rank0, rank1], axis=0).astype(jnp.int32)
        base = base + jnp.sum(member[u], axis=1, keepdims=True)
    base_ref[...] = jnp.broadcast_to(base, base_ref.shape)
    cnt_ref[...] = jnp.broadcast_to(base, cnt_ref.shape).astype(jnp.int32)

    ident = (ui == uj).astype(BF16)

    def gate_columns(gt):
        gpad = jnp.concatenate([gt[0], gt[1], jnp.zeros((V7X_LANES - 2, tt), F32)], axis=0)
        return sum(nt(ident, part) for part in _split3(gpad))

    for sl, gc in zip(subs, each(gate_columns, gate)):
        gcol_ref[sl, :] = gc


def _mixer_epilogue(act, wo, res, ln_g, ln_b, router_wt, router_bias):
    n_tok, d = res.shape
    k_in = act.shape[1]
    tt = EPI_TILE
    n_tiles = n_tok // tt
    full = lambda arr: pl.BlockSpec(arr.shape, lambda i: (0,) * arr.ndim)
    return pl.pallas_call(
        _epilogue_kernel,
        grid=(n_tiles,),
        in_specs=[pl.BlockSpec((tt, k_in), lambda i: (i, 0)), full(wo), pl.BlockSpec((tt, d), lambda i: (i, 0)),
                  full(ln_g), full(ln_b), full(router_wt), full(router_bias)],
        out_specs=[pl.BlockSpec((tt, d), lambda i: (i, 0)),
                   pl.BlockSpec((tt, d // 2), lambda i: (i, 0)),
                   pl.BlockSpec((2, tt), lambda i: (0, i)),
                   pl.BlockSpec((2, tt), lambda i: (0, i)),
                   pl.BlockSpec((tt, V7X_LANES), lambda i: (i, 0)),
                   pl.BlockSpec((N_EXPERTS, V7X_LANES), lambda i: (0, 0))],
        out_shape=[jax.ShapeDtypeStruct((n_tok, d), F32),
                   jax.ShapeDtypeStruct((n_tok, d // 2), jnp.uint32),
                   jax.ShapeDtypeStruct((2, n_tok), jnp.int32),
                   jax.ShapeDtypeStruct((2, n_tok), jnp.int32),
                   jax.ShapeDtypeStruct((n_tok, V7X_LANES), F32),
                   jax.ShapeDtypeStruct((N_EXPERTS, V7X_LANES), jnp.int32)],
        scratch_shapes=[pltpu.VMEM((N_EXPERTS, V7X_LANES), F32)],
        compiler_params=pltpu.CompilerParams(
            dimension_semantics=("arbitrary",),
            vmem_limit_bytes=_vmem_limit(2 * int(wo.size) * 2 + 2 * tt * (k_in * 2 + 2 * d * 4) + 16 * tt * d * 4)),
        name="mixer_epilogue",
    )(act, wo, res, ln_g, ln_b, router_wt, router_bias)


def _dispatch_kernel(dest_ref, h_ref, xs_in_ref, xs_ref, sem):
    del xs_in_ref
    tt = h_ref.shape[0]

    def row_copy(t, slot):
        d = dest_ref[0, 0, slot * tt + t]
        return pltpu.make_async_copy(h_ref.at[pl.ds(t, 1)], xs_ref.at[pl.ds(d, 1)], sem)

    def issue(t0, carry):
        for u in range(DMA_ISSUE_UNROLL):
            t = t0 * DMA_ISSUE_UNROLL + u
            row_copy(t, 0).start(priority=0)
            row_copy(t, 1).start(priority=1)
        return carry

    lax.fori_loop(0, tt // DMA_ISSUE_UNROLL, issue, 0)
    for _ in range(2):
        pltpu.make_async_copy(h_ref, xs_ref.at[pl.ds(0, tt)], sem).wait()


def _moe_dispatch(h, dest_tiles, n_rows):
    n_tok, d = h.shape
    tt = dest_tiles.shape[2] // 2
    n_tiles = n_tok // tt
    zeros = jnp.zeros((n_rows, d), h.dtype)
    return pl.pallas_call(
        _dispatch_kernel,
        grid=(n_tiles,),
        in_specs=[pl.BlockSpec((1, 1, 2 * tt), lambda i: (i, 0, 0), memory_space=pltpu.SMEM),
                  pl.BlockSpec((tt, d), lambda i: (i, 0)),
                  pl.BlockSpec(memory_space=pl.ANY)],
        out_specs=pl.BlockSpec(memory_space=pl.ANY),
        out_shape=jax.ShapeDtypeStruct((n_rows, d), h.dtype),
        scratch_shapes=[pltpu.SemaphoreType.DMA(())],
        input_output_aliases={2: 0},
        compiler_params=pltpu.CompilerParams(dimension_semantics=("arbitrary",),
                                             vmem_limit_bytes=_vmem_limit(2 * tt * d * 4)),
        name="moe_dispatch",
    )(dest_tiles, h, zeros)


def _ffn_kernel(bexp_ref, nblk_ref, x_ref, wg_ref, wu_ref, wd_ref, y_ref, wg_bf, wu_bf, wd_bf):
    j = pl.program_id(0)
    active = j < nblk_ref[0]
    new_expert = (j == 0) | (bexp_ref[j] != bexp_ref[jnp.maximum(j - 1, 0)])

    @pl.when(active & new_expert)
    def _():
        wg_bf[...] = wg_ref[0, 0].astype(BF16)
        wu_bf[...] = wu_ref[0, 0].astype(BF16)
        wd_bf[...] = wd_ref[0, 0].astype(BF16)

    @pl.when(active)
    def _():
        half = x_ref.shape[1]
        mm = lambda a, b: jnp.dot(a, b, preferred_element_type=F32)
        each = lambda fn, *lists: [fn(*args) for args in zip(*lists)]
        subs = [slice(u * MOE_SUBTILE, (u + 1) * MOE_SUBTILE) for u in range(x_ref.shape[0] // MOE_SUBTILE)]
        xs = each(lambda sl: [v.astype(BF16) for v in _unpack_halves(x_ref[sl, :])], subs)
        gate = each(lambda x: mm(x[0], wg_bf[:half, :]) + mm(x[1], wg_bf[half:, :]), xs)
        up = each(lambda x: mm(x[0], wu_bf[:half, :]) + mm(x[1], wu_bf[half:, :]), xs)
        hid = each(lambda g, u: ((g * jax.nn.sigmoid(g)) * u).astype(BF16), gate, up)
        for sl, y in zip(subs, each(lambda h: mm(h, wd_bf[...]), hid)):
            y_ref[sl, :] = _pack_halves(y)

    @pl.when(jnp.logical_not(active))
    def _():
        y_ref[...] = jnp.zeros_like(y_ref)


def _moe_ffn(xs, block_exp, n_used, layer, wg, wu, wd):
    n_rows, half = xs.shape
    d = 2 * half
    blk = MOE_BLOCK
    n_blocks = n_rows // blk
    de = wg.shape[3]
    expert = lambda j, be, nb: (layer, be[j], 0, 0)
    grid_spec = pltpu.PrefetchScalarGridSpec(
        num_scalar_prefetch=2,
        grid=(n_blocks,),
        in_specs=[pl.BlockSpec((blk, half), lambda j, be, nb: (j, 0)),
                  pl.BlockSpec((1, 1, d, de), expert),
                  pl.BlockSpec((1, 1, d, de), expert),
                  pl.BlockSpec((1, 1, de, d), expert)],
        out_specs=pl.BlockSpec((blk, half), lambda j, be, nb: (j, 0)),
        scratch_shapes=[pltpu.VMEM((d, de), BF16), pltpu.VMEM((d, de), BF16), pltpu.VMEM((de, d), BF16)],
    )
    return pl.pallas_call(
        _ffn_kernel,
        grid_spec=grid_spec,
        out_shape=jax.ShapeDtypeStruct((n_rows, half), jnp.uint32),
        compiler_params=pltpu.CompilerParams(
            dimension_semantics=("arbitrary",),
            vmem_limit_bytes=_vmem_limit(3 * d * de * (2 * 4 + 2) + 4 * blk * d * 4 + 4 * blk * de * 4)),
        name="moe_ffn",
    )(block_exp, n_used, xs, wg, wu, wd)


def _combine_kernel(dest_ref, dest_next_ref, ys_ref, gcol_ref, res_ref, lng_ref, lnb_ref, o_ref, buf_ref, sem):
    i = pl.program_id(0)
    n = pl.num_programs(0)
    tt = res_ref.shape[0]
    cur = i % 2

    def gather(idx_ref, buf):
        def row_copy(t, slot):
            d = idx_ref[0, 0, slot * tt + t]
            return pltpu.make_async_copy(ys_ref.at[pl.ds(d, 1)], buf_ref.at[buf, slot, pl.ds(t, 1)], sem.at[buf])

        def issue(t0, carry):
            for u in range(DMA_ISSUE_UNROLL):
                t = t0 * DMA_ISSUE_UNROLL + u
                row_copy(t, 0).start(priority=0)
                row_copy(t, 1).start(priority=1)
            return carry

        lax.fori_loop(0, tt // DMA_ISSUE_UNROLL, issue, 0)

    @pl.when(i == 0)
    def _():
        gather(dest_ref, cur)

    @pl.when(i + 1 < n)
    def _():
        gather(dest_next_ref, 1 - cur)

    for slot in range(2):
        pltpu.make_async_copy(ys_ref.at[pl.ds(0, tt)], buf_ref.at[cur, slot], sem.at[cur]).wait()

    gcol = gcol_ref[...]
    lo0, hi0 = _unpack_halves(buf_ref[cur, 0])
    lo1, hi1 = _unpack_halves(buf_ref[cur, 1])
    g0, g1 = gcol[:, 0:1], gcol[:, 1:2]
    ffn = jnp.concatenate([lo0 * g0 + lo1 * g1, hi0 * g0 + hi1 * g1], axis=1)
    o_ref[...] = _layer_norm_rows(ALPHA * res_ref[...] + ffn, lng_ref[...], lnb_ref[...])


def _moe_combine(ys, dest_tiles, gcol, res, ln_g, ln_b):
    n_tok, d = res.shape
    tt = dest_tiles.shape[2] // 2
    n_tiles = n_tok // tt
    full = lambda arr: pl.BlockSpec(arr.shape, lambda i: (0,) * arr.ndim)
    return pl.pallas_call(
        _combine_kernel,
        grid=(n_tiles,),
        in_specs=[pl.BlockSpec((1, 1, 2 * tt), lambda i: (i, 0, 0), memory_space=pltpu.SMEM),
                  pl.BlockSpec((1, 1, 2 * tt), lambda i: (jnp.minimum(i + 1, n_tiles - 1), 0, 0),
                               memory_space=pltpu.SMEM),
                  pl.BlockSpec(memory_space=pl.ANY),
                  pl.BlockSpec((tt, V7X_LANES), lambda i: (i, 0)),
                  pl.BlockSpec((tt, d), lambda i: (i, 0)),
                  full(ln_g), full(ln_b)],
        out_specs=pl.BlockSpec((tt, d), lambda i: (i, 0)),
        out_shape=jax.ShapeDtypeStruct((n_tok, d), F32),
        scratch_shapes=[pltpu.VMEM((2, 2, tt, ys.shape[1]), ys.dtype), pltpu.SemaphoreType.DMA((2,))],
        compiler_params=pltpu.CompilerParams(
            dimension_semantics=("arbitrary",),
            vmem_limit_bytes=_vmem_limit(4 * tt * d * 2 + 2 * 3 * tt * d * 4)),
        name="moe_combine",
    )(dest_tiles, dest_tiles, ys, gcol, res, ln_g, ln_b)


def _moe_layer(h, h_packed, eidx, rank, counts, gcol, layer, wg, wu, wd, ln_g, ln_b):
    n_tok, d = h.shape
    blk = MOE_BLOCK
    n_rows = (2 * n_tok + N_EXPERTS * (blk - 1) + blk - 1) // blk * blk
    n_blocks = n_rows // blk
    counts = counts[:, 0]
    padded = (counts + blk - 1) // blk * blk
    pend = jnp.cumsum(padded)
    pstart = pend - padded
    onehot = (eidx[:, :, None] == jnp.arange(N_EXPERTS, dtype=jnp.int32)).astype(jnp.int32)
    dest = rank + jnp.sum(onehot * pstart, axis=-1)
    tiles = lambda tt: dest.reshape(2, n_tok // tt, tt).transpose(1, 0, 2).reshape(n_tok // tt, 1, 2 * tt)
    blk_start = jnp.arange(n_blocks, dtype=jnp.int32) * blk
    block_exp = jnp.minimum(jnp.sum((blk_start[:, None] >= pend[None, :]).astype(jnp.int32), axis=1),
                            N_EXPERTS - 1).astype(jnp.int32)
    n_used = (pend[-1:] // blk).astype(jnp.int32)
    blk_idx = jnp.arange(n_blocks, dtype=jnp.int32)
    block_exp = jnp.where(blk_idx < n_used[0], block_exp, block_exp[jnp.maximum(n_used[0] - 1, 0)])
    xs = _moe_dispatch(h_packed, tiles(DISPATCH_TILE), n_rows)
    ys = _moe_ffn(xs, block_exp, n_used, layer, wg, wu, wd)
    return _moe_combine(ys, tiles(COMBINE_TILE), gcol, h, ln_g, ln_b)


def _fox_proj_kernel(x_ref, wq_ref, wk_ref, wv_ref, wf_ref, bf_ref, q_ref, k_ref, v_ref, c_ref, carry_ref):
    t = pl.program_id(1)

    @pl.when(t == 0)
    def _():
        carry_ref[...] = jnp.zeros_like(carry_ref)

    tt = PROJ_SUBTILE
    subs = [slice(u * tt, (u + 1) * tt) for u in range(x_ref.shape[0] // tt)]
    each = lambda fn, *lists: [fn(*args) for args in zip(*lists)]
    parts = each(lambda sl: _split3(x_ref[sl, :]), subs)
    wh, wm, wl = _split3(wf_ref[...])
    mm = lambda a, b: jnp.dot(a, b, preferred_element_type=F32)
    logit = each(lambda p: (mm(p[0], wh) + (mm(p[0], wm) + mm(p[1], wh))
                            + (mm(p[0], wl) + mm(p[1], wm) + mm(p[2], wh))) + bf_ref[...], parts)
    for sl, p in zip(subs, parts):
        q_ref[sl, :] = (mm(p[0], wq_ref[...]) * LOG2E).astype(q_ref.dtype)
    for sl, p in zip(subs, parts):
        k_ref[sl, :] = mm(p[0], wk_ref[...]).astype(k_ref.dtype)
    for sl, p in zip(subs, parts):
        v_ref[sl, :] = mm(p[0], wv_ref[...]).astype(v_ref.dtype)
    ti = lax.broadcasted_iota(jnp.int32, (tt, tt), 0)
    tj = lax.broadcasted_iota(jnp.int32, (tt, tt), 1)
    tril = (tj <= ti).astype(BF16)
    local = each(lambda lg: sum(mm(tril, part) for part in _split3(jax.nn.log_sigmoid(lg))), logit)
    carry = carry_ref[0:1, :]
    for sl, cs in zip(subs, local):
        c = cs + carry
        c_ref[sl, :] = c
        carry = c[tt - 1:tt, :]
    carry_ref[...] = jnp.broadcast_to(carry, carry_ref.shape)


def _fox_proj(x2, batch, seq_len, wq, wk, wv, wf, b_f):
    n_tok, d = x2.shape
    tt = PROJ_TILE
    nt = seq_len // tt
    tile = pl.BlockSpec((tt, d), lambda b, t: (b * nt + t, 0))
    full = lambda arr: pl.BlockSpec(arr.shape, lambda b, t: (0,) * arr.ndim)
    return pl.pallas_call(
        _fox_proj_kernel,
        grid=(batch, nt),
        in_specs=[tile, full(wq), full(wk), full(wv), full(wf), full(b_f)],
        out_specs=[tile, tile, tile, pl.BlockSpec((tt, N_HEADS), lambda b, t: (b * nt + t, 0))],
        out_shape=[jax.ShapeDtypeStruct((n_tok, d), BF16)] * 3 + [jax.ShapeDtypeStruct((n_tok, N_HEADS), F32)],
        scratch_shapes=[pltpu.VMEM((8, N_HEADS), F32)],
        compiler_params=pltpu.CompilerParams(
            dimension_semantics=("parallel", "arbitrary"),
            vmem_limit_bytes=_vmem_limit(2 * 3 * d * d * 2 + 2 * tt * d * (4 + 3 * 2) + 8 * tt * d * 4)),
        name="fox_proj",
    )(x2, wq, wk, wv, wf, b_f)


def _fox_attn_kernel(q_ref, k_ref, v_ref, ct_ref, o_ref):
    hp = pl.program_id(1)
    seq_len = q_ref.shape[0]
    tq = ATTN_TILE
    tk = tq
    nq = seq_len // tq
    N = HEAD_DIM
    nt = lambda a, b: lax.dot_general(a, b, (((1,), (1,)), ((), ())), preferred_element_type=F32)
    mm = lambda a, b: jnp.dot(a, b, preferred_element_type=F32)

    lane = lax.broadcasted_iota(jnp.int32, (tq, 2 * N), 1)
    vrow = lax.broadcasted_iota(jnp.int32, (2 * N, tk), 0)
    orow = lax.broadcasted_iota(jnp.int32, (2 * N, tq), 0)
    kpos = lax.broadcasted_iota(jnp.int32, (tk, tq), 0)
    qpos = lax.broadcasted_iota(jnp.int32, (tk, tq), 1)
    causal = kpos <= qpos
    ident_v = (lax.broadcasted_iota(jnp.int32, (2 * N, 2 * N), 0)
               == lax.broadcasted_iota(jnp.int32, (2 * N, 2 * N), 1)).astype(BF16)

    cq_row = [ct_ref[0, pl.ds(2 * hp + u, 1), :] * LOG2E for u in range(2)]

    def key_gates(j):
        return [jnp.broadcast_to(cq_row[u][:, j * tk:(j + 1) * tk], (tq, tk)).T for u in range(2)]

    m = [[None, None] for _ in range(nq)]
    l = [[None, None] for _ in range(nq)]
    acc = [None] * nq

    zero_q = jnp.zeros((tq, 2 * N), BF16)
    keys, values = {}, {}

    def scores(j, qi):
        if j not in keys:
            keys[j] = (k_ref[j * tk:(j + 1) * tk, :], key_gates(j))
        k_j, ck_b = keys[j]
        q = q_ref[qi * tq:(qi + 1) * tq, :]
        t = [nt(k_j, jnp.where((lane < N) == (u == 0), q, zero_q)) - ck_b[u] for u in range(2)]
        return [jnp.where(causal, tu, -jnp.inf) for tu in t] if qi == j else t

    def absorb(j, qi, t):
        if j not in values:
            v_t = nt(ident_v, v_ref[j * tk:(j + 1) * tk, :]).astype(BF16)
            values[j] = [jnp.where(vrow < N, v_t, jnp.zeros_like(v_t)), jnp.where(vrow < N, jnp.zeros_like(v_t), v_t)]
        v_heads = values[j]
        cq = [cq_row[u][:, qi * tq:(qi + 1) * tq] for u in range(2)]
        rmax = [jnp.max(t[u], axis=0, keepdims=True) + cq[u] for u in range(2)]
        m_new = rmax if j == 0 else [jnp.maximum(m[qi][u], rmax[u]) for u in range(2)]
        p = [jnp.exp2(t[u] + (cq[u] - m_new[u])) for u in range(2)]
        psum = [jnp.sum(p[u], axis=0, keepdims=True) for u in range(2)]
        pv = mm(v_heads[0], p[0].astype(BF16)) + mm(v_heads[1], p[1].astype(BF16))
        if j == 0:
            acc[qi] = pv
            l[qi] = psum
        else:
            alpha = [jnp.exp2(m[qi][u] - m_new[u]) for u in range(2)]
            acc[qi] = acc[qi] * jnp.where(orow < N, alpha[0], alpha[1]) + pv
            l[qi] = [alpha[u] * l[qi][u] + psum[u] for u in range(2)]
        m[qi] = m_new
        if j == qi:
            o_t = acc[qi] / jnp.where(orow < N, l[qi][0], l[qi][1])
            o_ref[qi * tq:(qi + 1) * tq, :] = o_t.T.astype(o_ref.dtype)

    pairs = [(j, qi) for j in range(nq) for qi in range(j, nq)]
    t_next = scores(*pairs[0])
    for n, (j, qi) in enumerate(pairs):
        t_cur = t_next
        if n + 1 < len(pairs):
            t_next = scores(*pairs[n + 1])
        absorb(j, qi, t_cur)


def _fox_attn(q, k, v, c_t, batch, seq_len):
    n_tok, d = q.shape
    pair = 2 * HEAD_DIM
    seq = pl.BlockSpec((seq_len, pair), lambda b, hp: (b, hp))
    return pl.pallas_call(
        _fox_attn_kernel,
        grid=(batch, d // pair),
        in_specs=[seq, seq, seq, pl.BlockSpec((1, N_HEADS, seq_len), lambda b, hp: (b, 0, 0))],
        out_specs=seq,
        out_shape=jax.ShapeDtypeStruct((n_tok, d), BF16),
        compiler_params=pltpu.CompilerParams(
            dimension_semantics=("parallel", "arbitrary"),
            vmem_limit_bytes=_vmem_limit(2 * 4 * seq_len * pair * 2 + 64 * ATTN_TILE * ATTN_TILE * 4)),
        name="fox_attn",
    )(q, k, v, c_t)


def kernel(x, rw_mix, rw_wr, rw_wk, rw_wv, rw_wo, rw_w0, rw_w1, rw_w2, rw_a0, rw_a1, rw_a2, rw_g1, rw_g2,
           rw_kk, rw_ka, rw_rk, rw_gn_g, rw_gn_b, fx_w_in, fx_b_f, fx_wo, router_w, router_bias,
           moe_w_gate, moe_w_up, moe_w_down, ln_g, ln_b):
    batch, seq_len, d = x.shape
    n_tok = batch * seq_len
    bf = lambda w: w.astype(BF16)
    row = lambda w: w.reshape(1, -1)
    router_wt = router_w.T
    router_b = router_bias.reshape(N_EXPERTS, 1)
    h = x.reshape(n_tok, d)

    for i in range(DEPTH):
        j = i // 2
        if i % 2 == 0:
            r, lw, k, v, a, g = _rwkv_proj(
                h, seq_len, rw_mix[j], bf(rw_wr[j]), bf(rw_wk[j]), bf(rw_wv[j]), bf(rw_w1[j]), bf(rw_w2[j]),
                bf(rw_a1[j]), bf(rw_a2[j]), bf(rw_g1[j]), bf(rw_g2[j]), row(rw_w0[j]), row(rw_a0[j]))
            act = _rwkv_recur(r, lw, k, v, a, g, batch, seq_len, row(rw_kk[j]), row(rw_ka[j]), row(rw_rk[j]),
                              row(rw_gn_g[j]), row(rw_gn_b[j]))
            wo = bf(rw_wo[j])
        else:
            w_in = fx_w_in[j]
            scale = HEAD_DIM ** -0.5
            q, k, v, c = _fox_proj(h, batch, seq_len, bf(w_in[:, :d] * scale), bf(w_in[:, d:2 * d]),
                                   bf(w_in[:, 2 * d:3 * d]), w_in[:, 3 * d:], row(fx_b_f[j]))
            c_t = c.reshape(batch, seq_len, N_HEADS).transpose(0, 2, 1)
            act = _fox_attn(q, k, v, c_t, batch, seq_len)
            wo = bf(fx_wo[j])
        h, h_packed, eidx, rank, gcol, counts = _mixer_epilogue(act, wo, h, row(ln_g[i, 0]), row(ln_b[i, 0]),
                                                                router_wt, router_b)
        h = _moe_layer(h, h_packed, eidx, rank, counts, gcol, i, moe_w_gate, moe_w_up, moe_w_down,
                       row(ln_g[i, 1]), row(ln_b[i, 1]))
    return h.reshape(batch, seq_len, d)
```

```python
import functools
import math

import jax
import jax.numpy as jnp
from jax import lax
from jax.experimental import pallas as pl
from jax.experimental.pallas import tpu as pltpu

D_MODEL = 1024
HEAD_DIM = 64
N_HEADS = D_MODEL // HEAD_DIM
N_EXPERTS = 16
N_GROUPS = 4
EXPERTS_PER_GROUP = N_EXPERTS // N_GROUPS
D_EXPERT = 512
N_PAIRS = 6
N_CLASSES = N_GROUPS * N_PAIRS
CLASS_ROWS = 32
_PAIR_SLOT_A = (0, 2, 2, 3, 3, 3)
_PAIR_SLOT_B = (1, 1, 0, 0, 1, 2)
GN_EPS = 64e-5
LN_EPS = 1e-5
DEPTH = 2
ALPHA = (2 * DEPTH) ** 0.25
LOG2E = math.log2(math.e)

V7X_LANES = 128
V7X_VMEM_BYTES = 64 * 2 ** 20

V7X_MXU_DIM = 256

RWKV_CHUNK = 64
RWKV_CHUNKS_PER_STEP = 2
RWKV_HEADS_PER_TILE = V7X_MXU_DIM // HEAD_DIM
PROJ_TILE = 512
PROJ_SUBTILE = 256
EPI_TILE = 1024
EPI_SUBTILE = 256
ATTN_TILE = 256
MOE_BLOCK = 256
DISPATCH_TILE = 1024
COMBINE_TILE = 256
DMA_ISSUE_UNROLL = 8

F32 = jnp.float32
BF16 = jnp.bfloat16


def _vmem_limit(n_bytes):
    return int(min(n_bytes + 16 * 2 ** 20, V7X_VMEM_BYTES - 8 * 2 ** 20))


def _split3(x):
    hi = x.astype(BF16)
    r1 = x - hi.astype(F32)
    mid = r1.astype(BF16)
    lo = (r1 - mid.astype(F32)).astype(BF16)
    return hi, mid, lo


def _pack_halves(x):
    half = x.shape[1] // 2
    bits = lambda v: lax.bitcast_convert_type(v.astype(BF16).astype(F32), jnp.uint32)
    return (bits(x[:, :half]) >> 16) | (bits(x[:, half:]) & jnp.uint32(0xFFFF0000))


def _unpack_halves(w):
    lo = lax.bitcast_convert_type(w << 16, F32)
    hi = lax.bitcast_convert_type(w & jnp.uint32(0xFFFF0000), F32)
    return lo, hi


def _layer_norm_rows(x, g, b):
    mu = jnp.mean(x, axis=-1, keepdims=True)
    xc = x - mu
    var = jnp.mean(xc * xc, axis=-1, keepdims=True)
    return xc * lax.rsqrt(var + LN_EPS) * g + b


def _rwkv_proj_kernel(x_ref, xp_ref, mix_ref, wr_ref, wk_ref, wv_ref, w1_ref, w2_ref, a1_ref, a2_ref,
                      g1_ref, g2_ref, w0_ref, a0_ref,
                      r_ref, lw_ref, k_ref, v_ref, a_ref, g_ref, *, tiles_per_seq):
    i = pl.program_id(0)
    x = x_ref[...]
    tt = x.shape[0]
    first = (i % tiles_per_seq) == 0
    prev_row = jnp.where(first, 0.0, xp_ref[7:8, :])
    row = lax.broadcasted_iota(jnp.int32, (tt, 1), 0)
    xprev = jnp.where(row == 0, prev_row, pltpu.roll(x, 1, axis=0))
    xx = xprev - x
    mix = mix_ref[...]
    subs = [slice(u * PROJ_SUBTILE, (u + 1) * PROJ_SUBTILE) for u in range(tt // PROJ_SUBTILE)]
    each = lambda fn, *lists: [fn(*args) for args in zip(*lists)]
    mixed = lambda j: each(lambda sl: (x[sl] + xx[sl] * mix[j:j + 1, :]).astype(BF16), subs)
    xr, xw, xk, xv, xa, xg = (mixed(j) for j in range(6))
    mm = lambda a, w_ref: jnp.dot(a, w_ref[...], preferred_element_type=F32)
    w_mid = each(lambda a: mm(a, w1_ref), xw)
    a_mid = each(lambda a: mm(a, a1_ref), xa)
    g_mid = each(lambda a: mm(a, g1_ref), xg)
    for sl, a in zip(subs, xr):
        r_ref[sl, :] = mm(a, wr_ref).astype(r_ref.dtype)
    for sl, a in zip(subs, xk):
        k_ref[sl, :] = mm(a, wk_ref).astype(k_ref.dtype)
    for sl, a in zip(subs, xv):
        v_ref[sl, :] = mm(a, wv_ref).astype(v_ref.dtype)
    z = each(lambda t: w0_ref[...] + mm(jnp.tanh(t).astype(BF16), w2_ref), w_mid)
    for sl, t in zip(subs, a_mid):
        a_ref[sl, :] = jax.nn.sigmoid(a0_ref[...] + mm(t.astype(BF16), a2_ref)).astype(a_ref.dtype)
    for sl, t in zip(subs, g_mid):
        g_ref[sl, :] = mm(jax.nn.sigmoid(t).astype(BF16), g2_ref).astype(g_ref.dtype)
    for sl, zs in zip(subs, z):
        lw_ref[sl, :] = -jnp.exp(-jax.nn.softplus(-zs) - 0.5)


def _rwkv_proj(x2, seq_len, mix, wr, wk, wv, w1, w2, a1, a2, g1, g2, w0, a0):
    n_tok, d = x2.shape
    tt = PROJ_TILE
    n_tiles = n_tok // tt
    tiles_per_seq = seq_len // tt
    tile = pl.BlockSpec((tt, d), lambda i: (i, 0))
    prev = pl.BlockSpec((8, d), lambda i: (jnp.maximum(i * (tt // 8) - 1, 0), 0))
    full = lambda arr: pl.BlockSpec(arr.shape, lambda i: (0,) * arr.ndim)
    weights = (mix, wr, wk, wv, w1, w2, a1, a2, g1, g2, w0, a0)
    out_dtypes = (BF16, F32, BF16, BF16, BF16, BF16)
    w_bytes = sum(int(w.size) * w.dtype.itemsize for w in weights)
    return pl.pallas_call(
        functools.partial(_rwkv_proj_kernel, tiles_per_seq=tiles_per_seq),
        grid=(n_tiles,),
        in_specs=[tile, prev] + [full(w) for w in weights],
        out_specs=[tile] * 6,
        out_shape=[jax.ShapeDtypeStruct((n_tok, d), dt) for dt in out_dtypes],
        compiler_params=pltpu.CompilerParams(
            dimension_semantics=("parallel",),
            vmem_limit_bytes=_vmem_limit(2 * w_bytes + 2 * tt * d * (4 + 4 + 5 * 2) + 8 * tt * d * 4)),
        name="rwkv_proj",
    )(x2, x2, *weights)


def _rwkv_recur_kernel(r_ref, lw_ref, k_ref, v_ref, a_ref, g_ref, kk_ref, ka_ref, rk_ref, gng_ref, gnb_ref,
                       o_ref, s_ref):
    c = pl.program_id(1)
    C = RWKV_CHUNK
    N = HEAD_DIM

    @pl.when(c == 0)
    def _():
        s_ref[...] = jnp.zeros_like(s_ref)

    nt = lambda p, q: lax.dot_general(p, q, (((1,), (1,)), ((), ())), preferred_element_type=F32)
    mm = lambda p, q: jnp.dot(p, q, preferred_element_type=F32)
    each = lambda fn, *lists: [fn(*args) for args in zip(*lists)]
    bf = lambda x: x.astype(BF16)

    ti = lax.broadcasted_iota(jnp.int32, (C, C), 0)
    tj = lax.broadcasted_iota(jnp.int32, (C, C), 1)
    tril = (tj <= ti).astype(BF16)

    def chunk_terms(rows):
        lw = lw_ref[rows, :]
        cum = sum(mm(tril, part) for part in _split3(lw))
        rho = cum[C // 2 - 1:C // 2, :]
        last = cum[C - 1:C, :]
        r = r_ref[rows, :].astype(F32)
        k = k_ref[rows, :].astype(F32)
        a = a_ref[rows, :].astype(F32)
        k_mod = k * (1.0 + (a - 1.0) * ka_ref[...])
        return dict(e_q=jnp.exp(cum - rho), e_qx=jnp.exp(cum - lw - rho), e_k=jnp.exp(rho - cum),
                    e_end=jnp.exp(last - cum), e_rho=jnp.exp(rho), d_end=jnp.exp(last),
                    r=r, a=a, v=v_ref[rows, :].astype(F32), kk_raw=k * kk_ref[...], k_mod=k_mod,
                    rkk=r * k_mod * rk_ref[...])

    n_ch = r_ref.shape[0] // C
    terms = [chunk_terms(slice(ci * C, (ci + 1) * C)) for ci in range(n_ch)]

    G = RWKV_HEADS_PER_TILE
    R = G * C
    GW = G * N
    er = lax.broadcasted_iota(jnp.int32, (R, GW), 0)
    ec = lax.broadcasted_iota(jnp.int32, (R, GW), 1)
    blk = (er // C) == (ec // N)
    strict = (ec % C) < (er % C)
    incl = (ec % C) <= (er % C)
    eye = (er == ec).astype(F32)
    ones_blk = blk.astype(BF16)

    def expand(x):
        return jnp.where(blk, jnp.concatenate([x] * G, axis=0), 0.0).astype(BF16)

    expand_t = lambda x: bf(jnp.where(blk, jnp.concatenate([x] * G, axis=0), 0.0).T)

    def head_sums(xs):
        parts = _split3(jnp.concatenate(xs, axis=0))
        tot = mm(jnp.concatenate(parts, axis=0), ones_blk)
        n = len(xs) * C
        tot = tot[:n] + tot[n:2 * n] + tot[2 * n:]
        return [tot[u * C:(u + 1) * C] for u in range(len(xs))]

    n_grp = N_HEADS // G
    groups = [slice(gi * GW, (gi + 1) * GW) for gi in range(n_grp)]
    units = [(tm, sl) for tm in terms for sl in groups]
    u_tm = [tm for tm, _ in units]
    u_sl = [sl for _, sl in units]
    pre = head_sums([tm["kk_raw"][:, sl] * tm["kk_raw"][:, sl] for tm, sl in units]
                    + [tm["rkk"][:, sl] for tm, sl in units])
    kk_ss, rkk_sum = pre[:len(units)], pre[len(units):]
    kk_n = each(lambda tm, sl, ss: tm["kk_raw"][:, sl] * lax.rsqrt(jnp.maximum(ss, 1e-24)), u_tm, u_sl, kk_ss)
    b_n = each(lambda tm, sl, kk_g: kk_g * tm["a"][:, sl], u_tm, u_sl, kk_n)
    kk_q = each(lambda tm, sl, kk_g: kk_g * tm["e_qx"][:, sl], u_tm, u_sl, kk_n)
    r_q = each(lambda tm, sl: tm["r"][:, sl] * tm["e_q"][:, sl], u_tm, u_sl)
    q2 = each(lambda x, y: jnp.concatenate([expand(x), expand(y)], axis=0), kk_q, r_q)
    a_k = each(lambda tm, sl, q: nt(q, expand(tm["k_mod"][:, sl] * tm["e_k"][:, sl])), u_tm, u_sl, q2)
    a_b = each(lambda tm, sl, q, b_g: nt(q, expand(b_g * tm["e_k"][:, sl])), u_tm, u_sl, q2, b_n)
    a_kk = each(lambda x: jnp.where(strict, x[:R], 0.0), a_k)
    a_rk = each(lambda x: jnp.where(incl, x[R:], 0.0), a_k)
    a_kb = each(lambda x: jnp.where(strict, x[:R], 0.0), a_b)
    a_rb = each(lambda x: jnp.where(incl, x[R:], 0.0), a_b)
    t_inv = each(lambda x: eye - x, a_kb)
    p = each(lambda x: mm(bf(-x), bf(-x)), a_kb)
    for _ in range(int(math.log2(C)) - 2):
        both = each(lambda pg, tg: mm(bf(pg), jnp.concatenate([bf(pg), bf(tg)], axis=1)), p, t_inv)
        p = each(lambda x: x[:, :R], both)
        t_inv = each(lambda tg, x: tg + x[:, R:], t_inv, both)
    t_inv = each(lambda pg, tg: tg + mm(bf(pg), bf(tg)), p, t_inv)
    kd_t = each(lambda tm, sl: expand_t(tm["k_mod"][:, sl] * tm["e_end"][:, sl]), u_tm, u_sl)
    bd_t = each(lambda tm, sl, b_g: expand_t(b_g * tm["e_end"][:, sl]), u_tm, u_sl, b_n)
    d_col = each(lambda tm, sl: jnp.broadcast_to(tm["d_end"][:, sl], (GW, GW)).T, u_tm, u_sl)
    v_e = each(lambda tm, sl: expand(tm["v"][:, sl]), u_tm, u_sl)
    av = each(lambda x, y, ve: mm(bf(jnp.concatenate([x, y], axis=0)), ve), a_kk, a_rk, v_e)
    q2_abs = each(lambda tm, sl, x, y: jnp.concatenate([expand(x * tm["e_rho"][:, sl]), expand(y * tm["e_rho"][:, sl])],
                                                       axis=0), u_tm, u_sl, kk_q, r_q)

    st = [s_ref[gi] for gi in range(n_grp)]
    ys = []
    for ci in range(n_ch):
        span = slice(ci * n_grp, (ci + 1) * n_grp)
        qs = each(lambda q, s: mm(q, bf(s)), q2_abs[span], st)
        sa_e = each(lambda tg, q, x: bf(mm(bf(tg), bf(q[:R] + x[:R]))), t_inv[span], qs, av[span])
        y_e = each(lambda q, x, arb, sa: q[R:] + x[R:] - mm(bf(arb), sa), qs, av[span], a_rb[span], sa_e)
        ys += each(lambda x: sum(x[u * C:(u + 1) * C] for u in range(G)), y_e)
        upd = each(lambda kt, bt, ve, sa: mm(jnp.concatenate([kt, -bt], axis=1), jnp.concatenate([ve, sa], axis=0)),
                   kd_t[span], bd_t[span], v_e[span], sa_e)
        st = each(lambda s, dc, up: s * dc + up, st, d_col[span], upd)
    for gi in range(n_grp):
        s_ref[gi] = st[gi]

    inv_n = 1.0 / N
    ycs = [y - mu * inv_n for y, mu in zip(ys, head_sums(ys))]
    sqs = head_sums([yc * yc for yc in ycs])
    for ui, (tm, sl) in enumerate(units):
        rows = slice((ui // n_grp) * C, (ui // n_grp + 1) * C)
        yn = ycs[ui] * lax.rsqrt(sqs[ui] * inv_n + GN_EPS) * gng_ref[:, sl] + gnb_ref[:, sl]
        o_ref[rows, sl] = ((yn + rkk_sum[ui] * tm["v"][:, sl]) * g_ref[rows, sl]).astype(o_ref.dtype)


def _rwkv_recur(r, lw, k, v, a, g, batch, seq_len, k_k, k_a, r_k, gn_g, gn_b):
    n_tok, d = r.shape
    C = RWKV_CHUNK * RWKV_CHUNKS_PER_STEP
    nc = seq_len // C
    tile = pl.BlockSpec((C, d), lambda b, c: (b * nc + c, 0))
    vec = pl.BlockSpec((1, d), lambda b, c: (0, 0))
    return pl.pallas_call(
        _rwkv_recur_kernel,
        grid=(batch, nc),
        in_specs=[tile] * 6 + [vec] * 5,
        out_specs=tile,
        out_shape=jax.ShapeDtypeStruct((n_tok, d), BF16),
        scratch_shapes=[pltpu.VMEM((N_HEADS // RWKV_HEADS_PER_TILE, V7X_MXU_DIM, V7X_MXU_DIM), F32)],
        compiler_params=pltpu.CompilerParams(
            dimension_semantics=("parallel", "arbitrary"),
            vmem_limit_bytes=_vmem_limit(2 * 7 * C * d * 4 + 32 * C * d * 4 + 64 * V7X_MXU_DIM ** 2 * 4)),
        name="rwkv_recur",
    )(r, lw, k, v, a, g, k_k, k_a, r_k, gn_g, gn_b)


def _rank_among(vals, i):
    cnt = 0
    for j, vj in enumerate(vals):
        if j == i:
            continue
        before = (vj >= vals[i]) if j < i else (vj > vals[i])
        cnt = cnt + before.astype(jnp.int32)
    return cnt


def _pick(ranks, vals, want):
    out = vals[0]
    for rk, vl in zip(ranks[1:], vals[1:]):
        out = jnp.where(rk == want, vl, out)
    return out


def _epilogue_kernel(act_ref, wo_ref, res_ref, lng_ref, lnb_ref, rwt_ref, rb_ref,
                     h_ref, hp_ref, route_ref, gcol_ref, cnt_ref, base_ref):
    i = pl.program_id(0)

    @pl.when(i == 0)
    def _():
        base_ref[...] = jnp.zeros_like(base_ref)

    tt = EPI_SUBTILE
    subs = [slice(u * tt, (u + 1) * tt) for u in range(act_ref.shape[0] // tt)]
    each = lambda fn, *lists: [fn(*args) for args in zip(*lists)]
    nt = lambda a, b: lax.dot_general(a, b, (((1,), (1,)), ((), ())), preferred_element_type=F32)

    wh, wm, wl = _split3(rwt_ref[...])

    def router_logits(hs):
        hh, hm, hl = _split3(hs)
        return nt(wh, hh) + (nt(wh, hm) + nt(wm, hh)) + (nt(wh, hl) + nt(wm, hm) + nt(wl, hh))

    def project(sl):
        return jnp.dot(act_ref[sl, :], wo_ref[...], preferred_element_type=F32)

    def normalise(sl, mx):
        hs = _layer_norm_rows(ALPHA * res_ref[sl, :] + mx, lng_ref[...], lnb_ref[...])
        h_ref[sl, :] = hs
        hp_ref[sl, :] = _pack_halves(hs)
        return jax.nn.sigmoid(router_logits(hs))

    s = []
    mixed = project(subs[0])
    for u in range(len(subs)):
        nxt = project(subs[u + 1]) if u + 1 < len(subs) else None
        s.append(normalise(subs[u], mixed))
        mixed = nxt

    def select(sg):
        s_sel = sg + rb_ref[...]
        rows = [s_sel[e:e + 1, :] for e in range(N_EXPERTS)]
        grp_score, grp_i0, grp_i1 = [], [], []
        for gi in range(N_GROUPS):
            vals = rows[gi * EXPERTS_PER_GROUP:(gi + 1) * EXPERTS_PER_GROUP]
            ranks = [_rank_among(vals, q) for q in range(EXPERTS_PER_GROUP)]
            idx = [jnp.full_like(ranks[0], q) for q in range(EXPERTS_PER_GROUP)]
            grp_score.append(_pick(ranks, vals, 0) + _pick(ranks, vals, 1))
            grp_i0.append(_pick(ranks, idx, 0))
            grp_i1.append(_pick(ranks, idx, 1))
        g_ranks = [_rank_among(grp_score, q) for q in range(N_GROUPS)]
        gidx = [jnp.full_like(g_ranks[0], q) for q in range(N_GROUPS)]
        g_star = _pick(g_ranks, gidx, 0)
        e0 = g_star * EXPERTS_PER_GROUP + _pick(g_ranks, grp_i0, 0)
        e1 = g_star * EXPERTS_PER_GROUP + _pick(g_ranks, grp_i1, 0)
        return e0, e1

    picked = each(select, s)
    e_iota = lax.broadcasted_iota(jnp.int32, (N_EXPERTS, tt), 0)

    def gates(sg, pk):
        gate0 = jnp.sum(jnp.where(e_iota == pk[0], sg, 0.0), axis=0, keepdims=True)
        gate1 = jnp.sum(jnp.where(e_iota == pk[1], sg, 0.0), axis=0, keepdims=True)
        denom = gate0 + gate1
        return gate0 / denom, gate1 / denom

    gate = each(gates, s, picked)

    def classify(pk):
        e0, e1 = pk
        grp = lax.shift_right_logical(e0, 2)
        l0 = e0 - grp * EXPERTS_PER_GROUP
        l1 = e1 - grp * EXPERTS_PER_GROUP
        lo, hi = jnp.minimum(l0, l1), jnp.maximum(l0, l1)
        pair = jnp.where(hi == 1, 0, jnp.where(hi == 2, jnp.where(lo == 1, 1, 2),
                                               jnp.where(lo == 0, 3, jnp.where(lo == 1, 4, 5))))
        slot_a = jnp.where(pair == 0, 0, jnp.where(pair <= 2, 2, 3))
        return grp * N_PAIRS + pair, l0 != slot_a

    classes = each(classify, picked)
    c_iota = lax.broadcasted_iota(jnp.int32, (cnt_ref.shape[0], tt), 0)
    hit = each(lambda cl: c_iota == cl[0], classes)
    member = each(lambda hc: jnp.where(hc, 1.0, 0.0), hit)
    ui = lax.broadcasted_iota(jnp.int32, (tt, tt), 0)
    uj = lax.broadcasted_iota(jnp.int32, (tt, tt), 1)
    before = (ui < uj).astype(BF16)
    prefix = each(lambda mb: jnp.dot(mb.astype(BF16), before, preferred_element_type=F32), member)
    base = base_ref[:, 0:1]
    for u, sl in enumerate(subs):
        rank = jnp.sum(jnp.where(hit[u], prefix[u] + base, 0.0), axis=0, keepdims=True)
        route_ref[:, sl] = jnp.concatenate([classes[u][0], rank.astype(jnp.int32)], axis=0)
        base = base + jnp.sum(member[u], axis=1, keepdims=True)
    base_ref[...] = jnp.broadcast_to(base, base_ref.shape)
    cnt_ref[...] = jnp.broadcast_to(base, cnt_ref.shape).astype(jnp.int32)

    for sl, gt, cl in zip(subs, gate, classes):
        gpad = jnp.concatenate([gt[0], gt[1], jnp.where(cl[1], 1.0, 0.0), jnp.zeros((V7X_LANES - 3, tt), F32)], axis=0)
        gcol_ref[sl, :] = gpad.T


def _mixer_epilogue(act, wo, res, ln_g, ln_b, router_wt, router_bias):
    n_tok, d = res.shape
    k_in = act.shape[1]
    tt = EPI_TILE
    n_tiles = n_tok // tt
    full = lambda arr: pl.BlockSpec(arr.shape, lambda i: (0,) * arr.ndim)
    return pl.pallas_call(
        _epilogue_kernel,
        grid=(n_tiles,),
        in_specs=[pl.BlockSpec((tt, k_in), lambda i: (i, 0)), full(wo), pl.BlockSpec((tt, d), lambda i: (i, 0)),
                  full(ln_g), full(ln_b), full(router_wt), full(router_bias)],
        out_specs=[pl.BlockSpec((tt, d), lambda i: (i, 0)),
                   pl.BlockSpec((tt, d // 2), lambda i: (i, 0)),
                   pl.BlockSpec((2, tt), lambda i: (0, i)),
                   pl.BlockSpec((tt, V7X_LANES), lambda i: (i, 0)),
                   pl.BlockSpec((CLASS_ROWS, V7X_LANES), lambda i: (0, 0))],
        out_shape=[jax.ShapeDtypeStruct((n_tok, d), F32),
                   jax.ShapeDtypeStruct((n_tok, d // 2), jnp.uint32),
                   jax.ShapeDtypeStruct((2, n_tok), jnp.int32),
                   jax.ShapeDtypeStruct((n_tok, V7X_LANES), F32),
                   jax.ShapeDtypeStruct((CLASS_ROWS, V7X_LANES), jnp.int32)],
        scratch_shapes=[pltpu.VMEM((CLASS_ROWS, V7X_LANES), F32)],
        compiler_params=pltpu.CompilerParams(
            dimension_semantics=("arbitrary",),
            vmem_limit_bytes=_vmem_limit(2 * int(wo.size) * 2 + 2 * tt * (k_in * 2 + 2 * d * 4) + 16 * tt * d * 4)),
        name="mixer_epilogue",
    )(act, wo, res, ln_g, ln_b, router_wt, router_bias)


def _dispatch_kernel(dest_ref, h_ref, xs_in_ref, xs_ref, sem):
    del xs_in_ref
    tt = h_ref.shape[0]

    def row_copy(t):
        return pltpu.make_async_copy(h_ref.at[pl.ds(t, 1)], xs_ref.at[pl.ds(dest_ref[0, 0, t], 1)], sem)

    def issue(t0, carry):
        for u in range(DMA_ISSUE_UNROLL):
            row_copy(t0 * DMA_ISSUE_UNROLL + u).start()
        return carry

    lax.fori_loop(0, tt // DMA_ISSUE_UNROLL, issue, 0)
    pltpu.make_async_copy(h_ref, xs_ref.at[pl.ds(0, tt)], sem).wait()


def _moe_dispatch(h, dest_tiles, n_rows):
    n_tok, d = h.shape
    tt = dest_tiles.shape[2]
    n_tiles = n_tok // tt
    zeros = jnp.zeros((n_rows, d), h.dtype)
    return pl.pallas_call(
        _dispatch_kernel,
        grid=(n_tiles,),
        in_specs=[pl.BlockSpec((1, 1, tt), lambda i: (i, 0, 0), memory_space=pltpu.SMEM),
                  pl.BlockSpec((tt, d), lambda i: (i, 0)),
                  pl.BlockSpec(memory_space=pl.ANY)],
        out_specs=pl.BlockSpec(memory_space=pl.ANY),
        out_shape=jax.ShapeDtypeStruct((n_rows, d), h.dtype),
        scratch_shapes=[pltpu.SemaphoreType.DMA(())],
        input_output_aliases={2: 0},
        compiler_params=pltpu.CompilerParams(dimension_semantics=("arbitrary",),
                                             vmem_limit_bytes=_vmem_limit(2 * tt * d * 4)),
        name="moe_dispatch",
    )(dest_tiles, h, zeros)


def _ffn_kernel(bea_ref, beb_ref, nblk_ref, x_ref, wga_ref, wua_ref, wda_ref, wgb_ref, wub_ref, wdb_ref,
                y_ref, wg_bf, wu_bf, wd_bf):
    j = pl.program_id(0)
    active = j < nblk_ref[0]
    prev = jnp.maximum(j - 1, 0)
    slots = ((bea_ref, wga_ref, wua_ref, wda_ref), (beb_ref, wgb_ref, wub_ref, wdb_ref))

    for s, (be_ref, wg_ref, wu_ref, wd_ref) in enumerate(slots):
        @pl.when(active & ((j == 0) | (be_ref[j] != be_ref[prev])))
        def _():
            wg_bf[s] = wg_ref[0, 0].astype(BF16)
            wu_bf[s] = wu_ref[0, 0].astype(BF16)
            wd_bf[s] = wd_ref[0, 0].astype(BF16)

    @pl.when(active)
    def _():
        half = x_ref.shape[1]
        mm = lambda a, b: jnp.dot(a, b, preferred_element_type=F32)
        x_lo, x_hi = (v.astype(BF16) for v in _unpack_halves(x_ref[...]))
        gate = [mm(x_lo, wg_bf[s, :half, :]) + mm(x_hi, wg_bf[s, half:, :]) for s in range(2)]
        up = [mm(x_lo, wu_bf[s, :half, :]) + mm(x_hi, wu_bf[s, half:, :]) for s in range(2)]
        hid = [((g * jax.nn.sigmoid(g)) * u).astype(BF16) for g, u in zip(gate, up)]
        for s in range(2):
            y_ref[:, s * half:(s + 1) * half] = _pack_halves(mm(hid[s], wd_bf[s]))

    @pl.when(jnp.logical_not(active))
    def _():
        y_ref[...] = jnp.zeros_like(y_ref)


def _moe_ffn(xs, block_ea, block_eb, n_used, layer, wg, wu, wd):
    n_rows, half = xs.shape
    d = 2 * half
    blk = MOE_BLOCK
    n_blocks = n_rows // blk
    de = wg.shape[3]
    expert_a = lambda j, ea, eb, nb: (layer, ea[j], 0, 0)
    expert_b = lambda j, ea, eb, nb: (layer, eb[j], 0, 0)
    rows = lambda j, ea, eb, nb: (j, 0)
    grid_spec = pltpu.PrefetchScalarGridSpec(
        num_scalar_prefetch=3,
        grid=(n_blocks,),
        in_specs=[pl.BlockSpec((blk, half), rows),
                  pl.BlockSpec((1, 1, d, de), expert_a), pl.BlockSpec((1, 1, d, de), expert_a),
                  pl.BlockSpec((1, 1, de, d), expert_a),
                  pl.BlockSpec((1, 1, d, de), expert_b), pl.BlockSpec((1, 1, d, de), expert_b),
                  pl.BlockSpec((1, 1, de, d), expert_b)],
        out_specs=pl.BlockSpec((blk, d), rows),
        scratch_shapes=[pltpu.VMEM((2, d, de), BF16), pltpu.VMEM((2, d, de), BF16), pltpu.VMEM((2, de, d), BF16)],
    )
    return pl.pallas_call(
        _ffn_kernel,
        grid_spec=grid_spec,
        out_shape=jax.ShapeDtypeStruct((n_rows, d), jnp.uint32),
        compiler_params=pltpu.CompilerParams(
            dimension_semantics=("arbitrary",),
            vmem_limit_bytes=_vmem_limit(2 * 3 * d * de * (2 * 4 + 2) + 6 * blk * d * 4 + 8 * blk * de * 4)),
        name="moe_ffn",
    )(block_ea, block_eb, n_used, xs, wg, wu, wd, wg, wu, wd)


def _combine_kernel(dest_ref, dest_next_ref, ys_ref, gcol_ref, res_ref, lng_ref, lnb_ref, o_ref, buf_ref, sem):
    i = pl.program_id(0)
    n = pl.num_programs(0)
    tt = res_ref.shape[0]
    cur = i % 2

    def gather(idx_ref, buf):
        def row_copy(t):
            return pltpu.make_async_copy(ys_ref.at[pl.ds(idx_ref[0, 0, t], 1)], buf_ref.at[buf, pl.ds(t, 1)],
                                         sem.at[buf])

        def issue(t0, carry):
            for u in range(DMA_ISSUE_UNROLL):
                row_copy(t0 * DMA_ISSUE_UNROLL + u).start()
            return carry

        lax.fori_loop(0, tt // DMA_ISSUE_UNROLL, issue, 0)

    @pl.when(i == 0)
    def _():
        gather(dest_ref, cur)

    @pl.when(i + 1 < n)
    def _():
        gather(dest_next_ref, 1 - cur)

    pltpu.make_async_copy(ys_ref.at[pl.ds(0, tt)], buf_ref.at[cur], sem.at[cur]).wait()

    gcol = gcol_ref[...]
    half = buf_ref.shape[2] // 2
    lo_a, hi_a = _unpack_halves(buf_ref[cur, :, :half])
    lo_b, hi_b = _unpack_halves(buf_ref[cur, :, half:])
    swap = gcol[:, 2:3] > 0.5
    g_a = jnp.where(swap, gcol[:, 1:2], gcol[:, 0:1])
    g_b = jnp.where(swap, gcol[:, 0:1], gcol[:, 1:2])
    ffn = jnp.concatenate([lo_a * g_a + lo_b * g_b, hi_a * g_a + hi_b * g_b], axis=1)
    o_ref[...] = _layer_norm_rows(ALPHA * res_ref[...] + ffn, lng_ref[...], lnb_ref[...])


def _moe_combine(ys, dest_tiles, gcol, res, ln_g, ln_b):
    n_tok, d = res.shape
    tt = dest_tiles.shape[2]
    n_tiles = n_tok // tt
    full = lambda arr: pl.BlockSpec(arr.shape, lambda i: (0,) * arr.ndim)
    return pl.pallas_call(
        _combine_kernel,
        grid=(n_tiles,),
        in_specs=[pl.BlockSpec((1, 1, tt), lambda i: (i, 0, 0), memory_space=pltpu.SMEM),
                  pl.BlockSpec((1, 1, tt), lambda i: (jnp.minimum(i + 1, n_tiles - 1), 0, 0),
                               memory_space=pltpu.SMEM),
                  pl.BlockSpec(memory_space=pl.ANY),
                  pl.BlockSpec((tt, V7X_LANES), lambda i: (i, 0)),
                  pl.BlockSpec((tt, d), lambda i: (i, 0)),
                  full(ln_g), full(ln_b)],
        out_specs=pl.BlockSpec((tt, d), lambda i: (i, 0)),
        out_shape=jax.ShapeDtypeStruct((n_tok, d), F32),
        scratch_shapes=[pltpu.VMEM((2, tt, ys.shape[1]), ys.dtype), pltpu.SemaphoreType.DMA((2,))],
        compiler_params=pltpu.CompilerParams(
            dimension_semantics=("arbitrary",),
            vmem_limit_bytes=_vmem_limit(4 * tt * d * 2 + 2 * 3 * tt * d * 4)),
        name="moe_combine",
    )(dest_tiles, dest_tiles, ys, gcol, res, ln_g, ln_b)


def _moe_layer(h, h_packed, route, counts, gcol, layer, wg, wu, wd, ln_g, ln_b):
    n_tok, d = h.shape
    blk = MOE_BLOCK
    n_rows = (n_tok + N_CLASSES * (blk - 1) + blk - 1) // blk * blk
    n_blocks = n_rows // blk
    cls, rank = route[0], route[1]
    counts = counts[:N_CLASSES, 0]
    padded = (counts + blk - 1) // blk * blk
    pend = jnp.cumsum(padded)
    pstart = pend - padded
    onehot = (cls[:, None] == jnp.arange(N_CLASSES, dtype=jnp.int32)).astype(jnp.int32)
    dest = rank + jnp.sum(onehot * pstart, axis=-1)
    tiles = lambda tt: dest.reshape(n_tok // tt, 1, tt)
    blk_start = jnp.arange(n_blocks, dtype=jnp.int32) * blk
    block_cls = jnp.minimum(jnp.sum((blk_start[:, None] >= pend[None, :]).astype(jnp.int32), axis=1), N_CLASSES - 1)
    n_used = (pend[-1:] // blk).astype(jnp.int32)
    blk_idx = jnp.arange(n_blocks, dtype=jnp.int32)
    block_cls = jnp.where(blk_idx < n_used[0], block_cls, block_cls[jnp.maximum(n_used[0] - 1, 0)])
    class_ids = jnp.arange(N_CLASSES, dtype=jnp.int32)
    first_expert = (class_ids // N_PAIRS) * EXPERTS_PER_GROUP
    expert_a = first_expert + jnp.array(_PAIR_SLOT_A, jnp.int32)[class_ids % N_PAIRS]
    expert_b = first_expert + jnp.array(_PAIR_SLOT_B, jnp.int32)[class_ids % N_PAIRS]
    xs = _moe_dispatch(h_packed, tiles(DISPATCH_TILE), n_rows)
    ys = _moe_ffn(xs, expert_a[block_cls].astype(jnp.int32), expert_b[block_cls].astype(jnp.int32), n_used,
                  layer, wg, wu, wd)
    return _moe_combine(ys, tiles(COMBINE_TILE), gcol, h, ln_g, ln_b)


def _fox_proj_kernel(x_ref, wq_ref, wk_ref, wv_ref, wf_ref, bf_ref, q_ref, k_ref, v_ref, c_ref, carry_ref):
    t = pl.program_id(1)

    @pl.when(t == 0)
    def _():
        carry_ref[...] = jnp.zeros_like(carry_ref)

    tt = PROJ_SUBTILE
    subs = [slice(u * tt, (u + 1) * tt) for u in range(x_ref.shape[0] // tt)]
    each = lambda fn, *lists: [fn(*args) for args in zip(*lists)]
    parts = each(lambda sl: _split3(x_ref[sl, :]), subs)
    wh, wm, wl = _split3(wf_ref[...])
    mm = lambda a, b: jnp.dot(a, b, preferred_element_type=F32)
    logit = each(lambda p: (mm(p[0], wh) + (mm(p[0], wm) + mm(p[1], wh))
                            + (mm(p[0], wl) + mm(p[1], wm) + mm(p[2], wh))) + bf_ref[...], parts)
    for sl, p in zip(subs, parts):
        q_ref[sl, :] = (mm(p[0], wq_ref[...]) * LOG2E).astype(q_ref.dtype)
    for sl, p in zip(subs, parts):
        k_ref[sl, :] = mm(p[0], wk_ref[...]).astype(k_ref.dtype)
    for sl, p in zip(subs, parts):
        v_ref[sl, :] = mm(p[0], wv_ref[...]).astype(v_ref.dtype)
    ti = lax.broadcasted_iota(jnp.int32, (tt, tt), 0)
    tj = lax.broadcasted_iota(jnp.int32, (tt, tt), 1)
    tril = (tj <= ti).astype(BF16)
    local = each(lambda lg: sum(mm(tril, part) for part in _split3(jax.nn.log_sigmoid(lg))), logit)
    carry = carry_ref[0:1, :]
    for sl, cs in zip(subs, local):
        c = cs + carry
        c_ref[sl, :] = c
        carry = c[tt - 1:tt, :]
    carry_ref[...] = jnp.broadcast_to(carry, carry_ref.shape)


def _fox_proj(x2, batch, seq_len, wq, wk, wv, wf, b_f):
    n_tok, d = x2.shape
    tt = PROJ_TILE
    nt = seq_len // tt
    tile = pl.BlockSpec((tt, d), lambda b, t: (b * nt + t, 0))
    full = lambda arr: pl.BlockSpec(arr.shape, lambda b, t: (0,) * arr.ndim)
    return pl.pallas_call(
        _fox_proj_kernel,
        grid=(batch, nt),
        in_specs=[tile, full(wq), full(wk), full(wv), full(wf), full(b_f)],
        out_specs=[tile, tile, tile, pl.BlockSpec((tt, N_HEADS), lambda b, t: (b * nt + t, 0))],
        out_shape=[jax.ShapeDtypeStruct((n_tok, d), BF16)] * 3 + [jax.ShapeDtypeStruct((n_tok, N_HEADS), F32)],
        scratch_shapes=[pltpu.VMEM((8, N_HEADS), F32)],
        compiler_params=pltpu.CompilerParams(
            dimension_semantics=("parallel", "arbitrary"),
            vmem_limit_bytes=_vmem_limit(2 * 3 * d * d * 2 + 2 * tt * d * (4 + 3 * 2) + 8 * tt * d * 4)),
        name="fox_proj",
    )(x2, wq, wk, wv, wf, b_f)


def _fox_attn_kernel(q_ref, k_ref, v_ref, ct_ref, o_ref):
    hp = pl.program_id(1)
    seq_len = q_ref.shape[0]
    tq = ATTN_TILE
    tk = tq
    nq = seq_len // tq
    N = HEAD_DIM
    nt = lambda a, b: lax.dot_general(a, b, (((1,), (1,)), ((), ())), preferred_element_type=F32)
    mm = lambda a, b: jnp.dot(a, b, preferred_element_type=F32)

    lane = lax.broadcasted_iota(jnp.int32, (tq, 2 * N), 1)
    vrow = lax.broadcasted_iota(jnp.int32, (2 * N, tk), 0)
    orow = lax.broadcasted_iota(jnp.int32, (2 * N, tq), 0)
    kpos = lax.broadcasted_iota(jnp.int32, (tk, tq), 0)
    qpos = lax.broadcasted_iota(jnp.int32, (tk, tq), 1)
    causal = kpos <= qpos
    ident_v = (lax.broadcasted_iota(jnp.int32, (2 * N, 2 * N), 0)
               == lax.broadcasted_iota(jnp.int32, (2 * N, 2 * N), 1)).astype(BF16)

    cq_row = [ct_ref[0, pl.ds(2 * hp + u, 1), :] * LOG2E for u in range(2)]

    def key_gates(j):
        return [jnp.broadcast_to(cq_row[u][:, j * tk:(j + 1) * tk], (tq, tk)).T for u in range(2)]

    m = [[None, None] for _ in range(nq)]
    l = [[None, None] for _ in range(nq)]
    acc = [None] * nq

    zero_q = jnp.zeros((tq, 2 * N), BF16)
    keys, values = {}, {}

    def scores(j, qi):
        if j not in keys:
            keys[j] = (k_ref[j * tk:(j + 1) * tk, :], key_gates(j))
        k_j, ck_b = keys[j]
        q = q_ref[qi * tq:(qi + 1) * tq, :]
        t = [nt(k_j, jnp.where((lane < N) == (u == 0), q, zero_q)) - ck_b[u] for u in range(2)]
        return [jnp.where(causal, tu, -jnp.inf) for tu in t] if qi == j else t

    def absorb(j, qi, t):
        if j not in values:
            v_t = nt(ident_v, v_ref[j * tk:(j + 1) * tk, :]).astype(BF16)
            values[j] = [jnp.where(vrow < N, v_t, jnp.zeros_like(v_t)), jnp.where(vrow < N, jnp.zeros_like(v_t), v_t)]
        v_heads = values[j]
        cq = [cq_row[u][:, qi * tq:(qi + 1) * tq] for u in range(2)]
        rmax = [jnp.max(t[u], axis=0, keepdims=True) + cq[u] for u in range(2)]
        m_new = rmax if j == 0 else [jnp.maximum(m[qi][u], rmax[u]) for u in range(2)]
        p = [jnp.exp2(t[u] + (cq[u] - m_new[u])) for u in range(2)]
        psum = [jnp.sum(p[u], axis=0, keepdims=True) for u in range(2)]
        pv = mm(v_heads[0], p[0].astype(BF16)) + mm(v_heads[1], p[1].astype(BF16))
        if j == 0:
            acc[qi] = pv
            l[qi] = psum
        else:
            alpha = [jnp.exp2(m[qi][u] - m_new[u]) for u in range(2)]
            acc[qi] = acc[qi] * jnp.where(orow < N, alpha[0], alpha[1]) + pv
            l[qi] = [alpha[u] * l[qi][u] + psum[u] for u in range(2)]
        m[qi] = m_new
        if j == qi:
            o_t = acc[qi] / jnp.where(orow < N, l[qi][0], l[qi][1])
            o_ref[qi * tq:(qi + 1) * tq, :] = o_t.T.astype(o_ref.dtype)

    pairs = [(j, qi) for j in range(nq) for qi in range(j, nq)]
    t_next = scores(*pairs[0])
    for n, (j, qi) in enumerate(pairs):
        t_cur = t_next
        if n + 1 < len(pairs):
            t_next = scores(*pairs[n + 1])
        absorb(j, qi, t_cur)


def _fox_attn(q, k, v, c_t, batch, seq_len):
    n_tok, d = q.shape
    pair = 2 * HEAD_DIM
    seq = pl.BlockSpec((seq_len, pair), lambda b, hp: (b, hp))
    return pl.pallas_call(
        _fox_attn_kernel,
        grid=(batch, d // pair),
        in_specs=[seq, seq, seq, pl.BlockSpec((1, N_HEADS, seq_len), lambda b, hp: (b, 0, 0))],
        out_specs=seq,
        out_shape=jax.ShapeDtypeStruct((n_tok, d), BF16),
        compiler_params=pltpu.CompilerParams(
            dimension_semantics=("parallel", "arbitrary"),
            vmem_limit_bytes=_vmem_limit(2 * 4 * seq_len * pair * 2 + 64 * ATTN_TILE * ATTN_TILE * 4)),
        name="fox_attn",
    )(q, k, v, c_t)


def kernel(x, rw_mix, rw_wr, rw_wk, rw_wv, rw_wo, rw_w0, rw_w1, rw_w2, rw_a0, rw_a1, rw_a2, rw_g1, rw_g2,
           rw_kk, rw_ka, rw_rk, rw_gn_g, rw_gn_b, fx_w_in, fx_b_f, fx_wo, router_w, router_bias,
           moe_w_gate, moe_w_up, moe_w_down, ln_g, ln_b):
    batch, seq_len, d = x.shape
    n_tok = batch * seq_len
    bf = lambda w: w.astype(BF16)
    row = lambda w: w.reshape(1, -1)
    router_wt = router_w.T
    router_b = router_bias.reshape(N_EXPERTS, 1)
    h = x.reshape(n_tok, d)

    for i in range(DEPTH):
        j = i // 2
        if i % 2 == 0:
            r, lw, k, v, a, g = _rwkv_proj(
                h, seq_len, rw_mix[j], bf(rw_wr[j]), bf(rw_wk[j]), bf(rw_wv[j]), bf(rw_w1[j]), bf(rw_w2[j]),
                bf(rw_a1[j]), bf(rw_a2[j]), bf(rw_g1[j]), bf(rw_g2[j]), row(rw_w0[j]), row(rw_a0[j]))
            act = _rwkv_recur(r, lw, k, v, a, g, batch, seq_len, row(rw_kk[j]), row(rw_ka[j]), row(rw_rk[j]),
                              row(rw_gn_g[j]), row(rw_gn_b[j]))
            wo = bf(rw_wo[j])
        else:
            w_in = fx_w_in[j]
            scale = HEAD_DIM ** -0.5
            q, k, v, c = _fox_proj(h, batch, seq_len, bf(w_in[:, :d] * scale), bf(w_in[:, d:2 * d]),
                                   bf(w_in[:, 2 * d:3 * d]), w_in[:, 3 * d:], row(fx_b_f[j]))
            c_t = c.reshape(batch, seq_len, N_HEADS).transpose(0, 2, 1)
            act = _fox_attn(q, k, v, c_t, batch, seq_len)
            wo = bf(fx_wo[j])
        h, h_packed, route, gcol, counts = _mixer_epilogue(act, wo, h, row(ln_g[i, 0]), row(ln_b[i, 0]),
                                                           router_wt, router_b)
        h = _moe_layer(h, h_packed, route, counts, gcol, i, moe_w_gate, moe_w_up, moe_w_down,
                       row(ln_g[i, 1]), row(ln_b[i, 1]))
    return h.reshape(batch, seq_len, d)
```

```python
import functools
import math

import jax
import jax.numpy as jnp
from jax import lax
from jax.experimental import pallas as pl
from jax.experimental.pallas import tpu as pltpu

D_MODEL = 1024
HEAD_DIM = 64
N_HEADS = D_MODEL // HEAD_DIM
N_EXPERTS = 16
N_GROUPS = 4
EXPERTS_PER_GROUP = N_EXPERTS // N_GROUPS
D_EXPERT = 512
N_PAIRS = 6
N_CLASSES = N_GROUPS * N_PAIRS
CLASS_ROWS = 32
_PAIR_SLOT_A = (0, 2, 2, 3, 3, 3)
_PAIR_SLOT_B = (1, 1, 0, 0, 1, 2)
GN_EPS = 64e-5
LN_EPS = 1e-5
DEPTH = 2
ALPHA = (2 * DEPTH) ** 0.25
LOG2E = math.log2(math.e)

V7X_LANES = 128
V7X_VMEM_BYTES = 64 * 2 ** 20

V7X_MXU_DIM = 256

RWKV_CHUNK = 64
RWKV_CHUNKS_PER_STEP = 4
RWKV_HEADS_PER_TILE = V7X_MXU_DIM // HEAD_DIM
PROJ_TILE = 512
PROJ_SUBTILE = 256
EPI_TILE = 1024
EPI_SUBTILE = 256
ATTN_TILE = 256
MOE_BLOCK = 256
DISPATCH_TILE = 2048
COMBINE_TILE = 512
DMA_ISSUE_UNROLL = 8

F32 = jnp.float32
BF16 = jnp.bfloat16


def _vmem_limit(n_bytes):
    return int(min(n_bytes + 16 * 2 ** 20, V7X_VMEM_BYTES - 8 * 2 ** 20))


def _split3(x):
    hi = x.astype(BF16)
    r1 = x - hi.astype(F32)
    mid = r1.astype(BF16)
    lo = (r1 - mid.astype(F32)).astype(BF16)
    return hi, mid, lo


def _pack_halves(x):
    half = x.shape[1] // 2
    bits = lambda v: lax.bitcast_convert_type(v.astype(BF16).astype(F32), jnp.uint32)
    return (bits(x[:, :half]) >> 16) | (bits(x[:, half:]) & jnp.uint32(0xFFFF0000))


def _unpack_halves(w):
    lo = lax.bitcast_convert_type(w << 16, F32)
    hi = lax.bitcast_convert_type(w & jnp.uint32(0xFFFF0000), F32)
    return lo, hi


def _layer_norm_rows(x, g, b):
    mu = jnp.mean(x, axis=-1, keepdims=True)
    xc = x - mu
    var = jnp.mean(xc * xc, axis=-1, keepdims=True)
    return xc * lax.rsqrt(var + LN_EPS) * g + b


def _rwkv_proj_kernel(x_ref, xp_ref, mix_ref, wr_ref, wk_ref, wv_ref, w1_ref, w2_ref, a1_ref, a2_ref,
                      g1_ref, g2_ref, w0_ref, a0_ref,
                      r_ref, lw_ref, k_ref, v_ref, a_ref, g_ref, *, tiles_per_seq):
    i = pl.program_id(0)
    x = x_ref[...]
    tt = x.shape[0]
    first = (i % tiles_per_seq) == 0
    prev_row = jnp.where(first, 0.0, xp_ref[7:8, :])
    row = lax.broadcasted_iota(jnp.int32, (tt, 1), 0)
    xprev = jnp.where(row == 0, prev_row, pltpu.roll(x, 1, axis=0))
    xx = xprev - x
    mix = mix_ref[...]
    subs = [slice(u * PROJ_SUBTILE, (u + 1) * PROJ_SUBTILE) for u in range(tt // PROJ_SUBTILE)]
    each = lambda fn, *lists: [fn(*args) for args in zip(*lists)]
    mixed = lambda j: each(lambda sl: (x[sl] + xx[sl] * mix[j:j + 1, :]).astype(BF16), subs)
    xr, xw, xk, xv, xa, xg = (mixed(j) for j in range(6))
    mm = lambda a, w_ref: jnp.dot(a, w_ref[...], preferred_element_type=F32)
    w_mid = each(lambda a: mm(a, w1_ref), xw)
    a_mid = each(lambda a: mm(a, a1_ref), xa)
    g_mid = each(lambda a: mm(a, g1_ref), xg)
    for sl, a in zip(subs, xr):
        r_ref[sl, :] = mm(a, wr_ref).astype(r_ref.dtype)
    for sl, a in zip(subs, xk):
        k_ref[sl, :] = mm(a, wk_ref).astype(k_ref.dtype)
    for sl, a in zip(subs, xv):
        v_ref[sl, :] = mm(a, wv_ref).astype(v_ref.dtype)
    z = each(lambda t: w0_ref[...] + mm(jnp.tanh(t).astype(BF16), w2_ref), w_mid)
    for sl, t in zip(subs, a_mid):
        a_ref[sl, :] = jax.nn.sigmoid(a0_ref[...] + mm(t.astype(BF16), a2_ref)).astype(a_ref.dtype)
    for sl, t in zip(subs, g_mid):
        g_ref[sl, :] = mm(jax.nn.sigmoid(t).astype(BF16), g2_ref).astype(g_ref.dtype)
    for sl, zs in zip(subs, z):
        lw_ref[sl, :] = -jnp.exp(-jax.nn.softplus(-zs) - 0.5)


def _rwkv_proj(x2, seq_len, mix, wr, wk, wv, w1, w2, a1, a2, g1, g2, w0, a0):
    n_tok, d = x2.shape
    tt = PROJ_TILE
    n_tiles = n_tok // tt
    tiles_per_seq = seq_len // tt
    tile = pl.BlockSpec((tt, d), lambda i: (i, 0))
    prev = pl.BlockSpec((8, d), lambda i: (jnp.maximum(i * (tt // 8) - 1, 0), 0))
    full = lambda arr: pl.BlockSpec(arr.shape, lambda i: (0,) * arr.ndim)
    weights = (mix, wr, wk, wv, w1, w2, a1, a2, g1, g2, w0, a0)
    out_dtypes = (BF16, F32, BF16, BF16, BF16, BF16)
    w_bytes = sum(int(w.size) * w.dtype.itemsize for w in weights)
    return pl.pallas_call(
        functools.partial(_rwkv_proj_kernel, tiles_per_seq=tiles_per_seq),
        grid=(n_tiles,),
        in_specs=[tile, prev] + [full(w) for w in weights],
        out_specs=[tile] * 6,
        out_shape=[jax.ShapeDtypeStruct((n_tok, d), dt) for dt in out_dtypes],
        compiler_params=pltpu.CompilerParams(
            dimension_semantics=("parallel",),
            vmem_limit_bytes=_vmem_limit(2 * w_bytes + 2 * tt * d * (4 + 4 + 5 * 2) + 8 * tt * d * 4)),
        name="rwkv_proj",
    )(x2, x2, *weights)


def _rwkv_recur_kernel(r_ref, lw_ref, k_ref, v_ref, a_ref, g_ref, kk_ref, ka_ref, rk_ref, gng_ref, gnb_ref,
                       o_ref, s_ref):
    c = pl.program_id(1)
    C = RWKV_CHUNK
    N = HEAD_DIM

    @pl.when(c == 0)
    def _():
        s_ref[...] = jnp.zeros_like(s_ref)

    nt = lambda p, q: lax.dot_general(p, q, (((1,), (1,)), ((), ())), preferred_element_type=F32)
    mm = lambda p, q: jnp.dot(p, q, preferred_element_type=F32)
    each = lambda fn, *lists: [fn(*args) for args in zip(*lists)]
    bf = lambda x: x.astype(BF16)

    ti = lax.broadcasted_iota(jnp.int32, (C, C), 0)
    tj = lax.broadcasted_iota(jnp.int32, (C, C), 1)
    tril = (tj <= ti).astype(BF16)

    def chunk_terms(rows):
        lw = lw_ref[rows, :]
        cum = sum(mm(tril, part) for part in _split3(lw))
        rho = cum[C // 2 - 1:C // 2, :]
        last = cum[C - 1:C, :]
        r = r_ref[rows, :].astype(F32)
        k = k_ref[rows, :].astype(F32)
        a = a_ref[rows, :].astype(F32)
        k_mod = k * (1.0 + (a - 1.0) * ka_ref[...])
        return dict(e_q=jnp.exp(cum - rho), e_qx=jnp.exp(cum - lw - rho), e_k=jnp.exp(rho - cum),
                    e_end=jnp.exp(last - cum), e_rho=jnp.exp(rho), d_end=jnp.exp(last),
                    r=r, a=a, v=v_ref[rows, :].astype(F32), kk_raw=k * kk_ref[...], k_mod=k_mod,
                    rkk=r * k_mod * rk_ref[...])

    n_ch = r_ref.shape[0] // C
    terms = [chunk_terms(slice(ci * C, (ci + 1) * C)) for ci in range(n_ch)]

    G = RWKV_HEADS_PER_TILE
    R = G * C
    GW = G * N
    er = lax.broadcasted_iota(jnp.int32, (R, GW), 0)
    ec = lax.broadcasted_iota(jnp.int32, (R, GW), 1)
    blk = (er // C) == (ec // N)
    strict = (ec % C) < (er % C)
    incl = (ec % C) <= (er % C)
    eye = (er == ec).astype(F32)
    ones_blk = blk.astype(BF16)

    def expand(x):
        return jnp.where(blk, jnp.concatenate([x] * G, axis=0), 0.0).astype(BF16)

    expand_t = lambda x: bf(jnp.where(blk, jnp.concatenate([x] * G, axis=0), 0.0).T)

    def head_sums(xs):
        parts = _split3(jnp.concatenate(xs, axis=0))[:2]
        tot = mm(jnp.concatenate(parts, axis=0), ones_blk)
        n = len(xs) * C
        tot = tot[:n] + tot[n:]
        return [tot[u * C:(u + 1) * C] for u in range(len(xs))]

    n_grp = N_HEADS // G
    groups = [slice(gi * GW, (gi + 1) * GW) for gi in range(n_grp)]
    units = [(tm, sl) for tm in terms for sl in groups]
    u_tm = [tm for tm, _ in units]
    u_sl = [sl for _, sl in units]
    pre = head_sums([tm["kk_raw"][:, sl] * tm["kk_raw"][:, sl] for tm, sl in units]
                    + [tm["rkk"][:, sl] for tm, sl in units])
    kk_ss, rkk_sum = pre[:len(units)], pre[len(units):]
    kk_n = each(lambda tm, sl, ss: tm["kk_raw"][:, sl] * lax.rsqrt(jnp.maximum(ss, 1e-24)), u_tm, u_sl, kk_ss)
    b_n = each(lambda tm, sl, kk_g: kk_g * tm["a"][:, sl], u_tm, u_sl, kk_n)
    kk_q = each(lambda tm, sl, kk_g: kk_g * tm["e_qx"][:, sl], u_tm, u_sl, kk_n)
    r_q = each(lambda tm, sl: tm["r"][:, sl] * tm["e_q"][:, sl], u_tm, u_sl)
    q2 = each(lambda x, y: jnp.concatenate([expand(x), expand(y)], axis=0), kk_q, r_q)
    a_k = each(lambda tm, sl, q: nt(q, expand(tm["k_mod"][:, sl] * tm["e_k"][:, sl])), u_tm, u_sl, q2)
    a_b = each(lambda tm, sl, q, b_g: nt(q, expand(b_g * tm["e_k"][:, sl])), u_tm, u_sl, q2, b_n)
    a_kk = each(lambda x: jnp.where(strict, x[:R], 0.0), a_k)
    a_rk = each(lambda x: jnp.where(incl, x[R:], 0.0), a_k)
    a_kb = each(lambda x: jnp.where(strict, x[:R], 0.0), a_b)
    a_rb = each(lambda x: jnp.where(incl, x[R:], 0.0), a_b)
    t_inv = each(lambda x: eye - x, a_kb)
    p = each(lambda x: mm(bf(-x), bf(-x)), a_kb)
    for _ in range(int(math.log2(C)) - 2):
        both = each(lambda pg, tg: mm(bf(pg), jnp.concatenate([bf(pg), bf(tg)], axis=1)), p, t_inv)
        p = each(lambda x: x[:, :R], both)
        t_inv = each(lambda tg, x: tg + x[:, R:], t_inv, both)
    t_inv = each(lambda pg, tg: tg + mm(bf(pg), bf(tg)), p, t_inv)
    kd_t = each(lambda tm, sl: expand_t(tm["k_mod"][:, sl] * tm["e_end"][:, sl]), u_tm, u_sl)
    bd_t = each(lambda tm, sl, b_g: expand_t(b_g * tm["e_end"][:, sl]), u_tm, u_sl, b_n)
    d_col = each(lambda tm, sl: jnp.broadcast_to(tm["d_end"][:, sl], (GW, GW)).T, u_tm, u_sl)
    v_e = each(lambda tm, sl: expand(tm["v"][:, sl]), u_tm, u_sl)
    av = each(lambda x, y, ve: mm(bf(jnp.concatenate([x, y], axis=0)), ve), a_kk, a_rk, v_e)
    q2_abs = each(lambda tm, sl, x, y: jnp.concatenate([expand(x * tm["e_rho"][:, sl]), expand(y * tm["e_rho"][:, sl])],
                                                       axis=0), u_tm, u_sl, kk_q, r_q)

    st = [s_ref[gi] for gi in range(n_grp)]
    ys = []
    for ci in range(n_ch):
        span = slice(ci * n_grp, (ci + 1) * n_grp)
        qs = each(lambda q, s: mm(q, bf(s)), q2_abs[span], st)
        sa_e = each(lambda tg, q, x: bf(mm(bf(tg), bf(q[:R] + x[:R]))), t_inv[span], qs, av[span])
        y_e = each(lambda q, x, arb, sa: q[R:] + x[R:] - mm(bf(arb), sa), qs, av[span], a_rb[span], sa_e)
        ys += each(lambda x: sum(x[u * C:(u + 1) * C] for u in range(G)), y_e)
        upd = each(lambda kt, bt, ve, sa: mm(jnp.concatenate([kt, -bt], axis=1), jnp.concatenate([ve, sa], axis=0)),
                   kd_t[span], bd_t[span], v_e[span], sa_e)
        st = each(lambda s, dc, up: s * dc + up, st, d_col[span], upd)
    for gi in range(n_grp):
        s_ref[gi] = st[gi]

    inv_n = 1.0 / N
    ycs = [y - mu * inv_n for y, mu in zip(ys, head_sums(ys))]
    sqs = head_sums([yc * yc for yc in ycs])
    for ui, (tm, sl) in enumerate(units):
        rows = slice((ui // n_grp) * C, (ui // n_grp + 1) * C)
        yn = ycs[ui] * lax.rsqrt(sqs[ui] * inv_n + GN_EPS) * gng_ref[:, sl] + gnb_ref[:, sl]
        o_ref[rows, sl] = ((yn + rkk_sum[ui] * tm["v"][:, sl]) * g_ref[rows, sl]).astype(o_ref.dtype)


def _rwkv_recur(r, lw, k, v, a, g, batch, seq_len, k_k, k_a, r_k, gn_g, gn_b):
    n_tok, d = r.shape
    C = RWKV_CHUNK * RWKV_CHUNKS_PER_STEP
    nc = seq_len // C
    tile = pl.BlockSpec((C, d), lambda b, c: (b * nc + c, 0))
    vec = pl.BlockSpec((1, d), lambda b, c: (0, 0))
    return pl.pallas_call(
        _rwkv_recur_kernel,
        grid=(batch, nc),
        in_specs=[tile] * 6 + [vec] * 5,
        out_specs=tile,
        out_shape=jax.ShapeDtypeStruct((n_tok, d), BF16),
        scratch_shapes=[pltpu.VMEM((N_HEADS // RWKV_HEADS_PER_TILE, V7X_MXU_DIM, V7X_MXU_DIM), F32)],
        compiler_params=pltpu.CompilerParams(
            dimension_semantics=("parallel", "arbitrary"),
            vmem_limit_bytes=_vmem_limit(2 * 7 * C * d * 4 + 32 * C * d * 4 + 64 * V7X_MXU_DIM ** 2 * 4)),
        name="rwkv_recur",
    )(r, lw, k, v, a, g, k_k, k_a, r_k, gn_g, gn_b)


def _rank_among(vals, i):
    cnt = 0
    for j, vj in enumerate(vals):
        if j == i:
            continue
        before = (vj >= vals[i]) if j < i else (vj > vals[i])
        cnt = cnt + before.astype(jnp.int32)
    return cnt


def _pick(ranks, vals, want):
    out = vals[0]
    for rk, vl in zip(ranks[1:], vals[1:]):
        out = jnp.where(rk == want, vl, out)
    return out


def _epilogue_kernel(act_ref, wo_ref, res_ref, lng_ref, lnb_ref, rwt_ref, rb_ref,
                     h_ref, hp_ref, route_ref, gcol_ref, cnt_ref, base_ref):
    i = pl.program_id(0)

    @pl.when(i == 0)
    def _():
        base_ref[...] = jnp.zeros_like(base_ref)

    tt = EPI_SUBTILE
    subs = [slice(u * tt, (u + 1) * tt) for u in range(act_ref.shape[0] // tt)]
    each = lambda fn, *lists: [fn(*args) for args in zip(*lists)]
    nt = lambda a, b: lax.dot_general(a, b, (((1,), (1,)), ((), ())), preferred_element_type=F32)

    wh, wm, wl = _split3(rwt_ref[...])

    def router_logits(hs):
        hh, hm, hl = _split3(hs)
        return nt(wh, hh) + (nt(wh, hm) + nt(wm, hh)) + (nt(wh, hl) + nt(wm, hm) + nt(wl, hh))

    def project(sl):
        return jnp.dot(act_ref[sl, :], wo_ref[...], preferred_element_type=F32)

    def normalise(sl, mx):
        hs = _layer_norm_rows(ALPHA * res_ref[sl, :] + mx, lng_ref[...], lnb_ref[...])
        h_ref[sl, :] = hs
        hp_ref[sl, :] = _pack_halves(hs)
        return jax.nn.sigmoid(router_logits(hs))

    s = []
    mixed = project(subs[0])
    for u in range(len(subs)):
        nxt = project(subs[u + 1]) if u + 1 < len(subs) else None
        s.append(normalise(subs[u], mixed))
        mixed = nxt

    def select(sg):
        s_sel = sg + rb_ref[...]
        rows = [s_sel[e:e + 1, :] for e in range(N_EXPERTS)]
        grp_score, grp_i0, grp_i1 = [], [], []
        for gi in range(N_GROUPS):
            vals = rows[gi * EXPERTS_PER_GROUP:(gi + 1) * EXPERTS_PER_GROUP]
            ranks = [_rank_among(vals, q) for q in range(EXPERTS_PER_GROUP)]
            idx = [jnp.full_like(ranks[0], q) for q in range(EXPERTS_PER_GROUP)]
            grp_score.append(_pick(ranks, vals, 0) + _pick(ranks, vals, 1))
            grp_i0.append(_pick(ranks, idx, 0))
            grp_i1.append(_pick(ranks, idx, 1))
        g_ranks = [_rank_among(grp_score, q) for q in range(N_GROUPS)]
        gidx = [jnp.full_like(g_ranks[0], q) for q in range(N_GROUPS)]
        g_star = _pick(g_ranks, gidx, 0)
        e0 = g_star * EXPERTS_PER_GROUP + _pick(g_ranks, grp_i0, 0)
        e1 = g_star * EXPERTS_PER_GROUP + _pick(g_ranks, grp_i1, 0)
        return e0, e1

    picked = each(select, s)
    e_iota = lax.broadcasted_iota(jnp.int32, (N_EXPERTS, tt), 0)

    def gates(sg, pk):
        gate0 = jnp.sum(jnp.where(e_iota == pk[0], sg, 0.0), axis=0, keepdims=True)
        gate1 = jnp.sum(jnp.where(e_iota == pk[1], sg, 0.0), axis=0, keepdims=True)
        denom = gate0 + gate1
        return gate0 / denom, gate1 / denom

    gate = each(gates, s, picked)

    def classify(pk):
        e0, e1 = pk
        grp = lax.shift_right_logical(e0, 2)
        l0 = e0 - grp * EXPERTS_PER_GROUP
        l1 = e1 - grp * EXPERTS_PER_GROUP
        lo, hi = jnp.minimum(l0, l1), jnp.maximum(l0, l1)
        pair = jnp.where(hi == 1, 0, jnp.where(hi == 2, jnp.where(lo == 1, 1, 2),
                                               jnp.where(lo == 0, 3, jnp.where(lo == 1, 4, 5))))
        slot_a = jnp.where(pair == 0, 0, jnp.where(pair <= 2, 2, 3))
        return grp * N_PAIRS + pair, l0 != slot_a

    classes = each(classify, picked)
    c_iota = lax.broadcasted_iota(jnp.int32, (cnt_ref.shape[0], tt), 0)
    hit = each(lambda cl: c_iota == cl[0], classes)
    member = each(lambda hc: jnp.where(hc, 1.0, 0.0), hit)
    ui = lax.broadcasted_iota(jnp.int32, (tt, tt), 0)
    uj = lax.broadcasted_iota(jnp.int32, (tt, tt), 1)
    before = (ui < uj).astype(BF16)
    prefix = each(lambda mb: jnp.dot(mb.astype(BF16), before, preferred_element_type=F32), member)
    base = base_ref[:, 0:1]
    for u, sl in enumerate(subs):
        rank = jnp.sum(jnp.where(hit[u], prefix[u] + base, 0.0), axis=0, keepdims=True)
        route_ref[:, sl] = jnp.concatenate([classes[u][0], rank.astype(jnp.int32)], axis=0)
        base = base + jnp.sum(member[u], axis=1, keepdims=True)
    base_ref[...] = jnp.broadcast_to(base, base_ref.shape)
    cnt_ref[...] = jnp.broadcast_to(base, cnt_ref.shape).astype(jnp.int32)

    for sl, gt, cl in zip(subs, gate, classes):
        gpad = jnp.concatenate([gt[0], gt[1], jnp.where(cl[1], 1.0, 0.0), jnp.zeros((V7X_LANES - 3, tt), F32)], axis=0)
        gcol_ref[sl, :] = gpad.T


def _mixer_epilogue(act, wo, res, ln_g, ln_b, router_wt, router_bias):
    n_tok, d = res.shape
    k_in = act.shape[1]
    tt = EPI_TILE
    n_tiles = n_tok // tt
    full = lambda arr: pl.BlockSpec(arr.shape, lambda i: (0,) * arr.ndim)
    return pl.pallas_call(
        _epilogue_kernel,
        grid=(n_tiles,),
        in_specs=[pl.BlockSpec((tt, k_in), lambda i: (i, 0)), full(wo), pl.BlockSpec((tt, d), lambda i: (i, 0)),
                  full(ln_g), full(ln_b), full(router_wt), full(router_bias)],
        out_specs=[pl.BlockSpec((tt, d), lambda i: (i, 0)),
                   pl.BlockSpec((tt, d // 2), lambda i: (i, 0)),
                   pl.BlockSpec((2, tt), lambda i: (0, i)),
                   pl.BlockSpec((tt, V7X_LANES), lambda i: (i, 0)),
                   pl.BlockSpec((CLASS_ROWS, V7X_LANES), lambda i: (0, 0))],
        out_shape=[jax.ShapeDtypeStruct((n_tok, d), F32),
                   jax.ShapeDtypeStruct((n_tok, d // 2), jnp.uint32),
                   jax.ShapeDtypeStruct((2, n_tok), jnp.int32),
                   jax.ShapeDtypeStruct((n_tok, V7X_LANES), F32),
                   jax.ShapeDtypeStruct((CLASS_ROWS, V7X_LANES), jnp.int32)],
        scratch_shapes=[pltpu.VMEM((CLASS_ROWS, V7X_LANES), F32)],
        compiler_params=pltpu.CompilerParams(
            dimension_semantics=("arbitrary",),
            vmem_limit_bytes=_vmem_limit(2 * int(wo.size) * 2 + 2 * tt * (k_in * 2 + 2 * d * 4) + 16 * tt * d * 4)),
        name="mixer_epilogue",
    )(act, wo, res, ln_g, ln_b, router_wt, router_bias)


def _dispatch_kernel(dest_ref, h_ref, xs_in_ref, xs_ref, sem):
    del xs_in_ref
    tt = h_ref.shape[0]

    def row_copy(t):
        return pltpu.make_async_copy(h_ref.at[pl.ds(t, 1)], xs_ref.at[pl.ds(dest_ref[0, 0, t], 1)], sem)

    def issue(t0, carry):
        for u in range(DMA_ISSUE_UNROLL):
            row_copy(t0 * DMA_ISSUE_UNROLL + u).start()
        return carry

    lax.fori_loop(0, tt // DMA_ISSUE_UNROLL, issue, 0)
    pltpu.make_async_copy(h_ref, xs_ref.at[pl.ds(0, tt)], sem).wait()


def _moe_dispatch(h, dest_tiles, n_rows):
    n_tok, d = h.shape
    tt = dest_tiles.shape[2]
    n_tiles = n_tok // tt
    zeros = jnp.zeros((n_rows, d), h.dtype)
    return pl.pallas_call(
        _dispatch_kernel,
        grid=(n_tiles,),
        in_specs=[pl.BlockSpec((1, 1, tt), lambda i: (i, 0, 0), memory_space=pltpu.SMEM),
                  pl.BlockSpec((tt, d), lambda i: (i, 0)),
                  pl.BlockSpec(memory_space=pl.ANY)],
        out_specs=pl.BlockSpec(memory_space=pl.ANY),
        out_shape=jax.ShapeDtypeStruct((n_rows, d), h.dtype),
        scratch_shapes=[pltpu.SemaphoreType.DMA(())],
        input_output_aliases={2: 0},
        compiler_params=pltpu.CompilerParams(dimension_semantics=("arbitrary",),
                                             vmem_limit_bytes=_vmem_limit(2 * tt * d * 4)),
        name="moe_dispatch",
    )(dest_tiles, h, zeros)


def _ffn_kernel(bea_ref, beb_ref, nblk_ref, x_ref, wga_ref, wua_ref, wda_ref, wgb_ref, wub_ref, wdb_ref,
                y_ref, wg_bf, wu_bf, wd_bf):
    j = pl.program_id(0)
    active = j < nblk_ref[0]
    prev = jnp.maximum(j - 1, 0)
    slots = ((bea_ref, wga_ref, wua_ref, wda_ref), (beb_ref, wgb_ref, wub_ref, wdb_ref))

    for s, (be_ref, wg_ref, wu_ref, wd_ref) in enumerate(slots):
        @pl.when(active & ((j == 0) | (be_ref[j] != be_ref[prev])))
        def _():
            wg_bf[s] = wg_ref[0, 0].astype(BF16)
            wu_bf[s] = wu_ref[0, 0].astype(BF16)
            wd_bf[s] = wd_ref[0, 0].astype(BF16)

    @pl.when(active)
    def _():
        half = x_ref.shape[1]
        mm = lambda a, b: jnp.dot(a, b, preferred_element_type=F32)
        x_lo, x_hi = (v.astype(BF16) for v in _unpack_halves(x_ref[...]))
        gate = [mm(x_lo, wg_bf[s, :half, :]) + mm(x_hi, wg_bf[s, half:, :]) for s in range(2)]
        up = [mm(x_lo, wu_bf[s, :half, :]) + mm(x_hi, wu_bf[s, half:, :]) for s in range(2)]
        hid = [((g * jax.nn.sigmoid(g)) * u).astype(BF16) for g, u in zip(gate, up)]
        for s in range(2):
            y_ref[:, s * half:(s + 1) * half] = _pack_halves(mm(hid[s], wd_bf[s]))

    @pl.when(jnp.logical_not(active))
    def _():
        y_ref[...] = jnp.zeros_like(y_ref)


def _moe_ffn(xs, block_ea, block_eb, n_used, layer, wg, wu, wd):
    n_rows, half = xs.shape
    d = 2 * half
    blk = MOE_BLOCK
    n_blocks = n_rows // blk
    de = wg.shape[3]
    expert_a = lambda j, ea, eb, nb: (layer, ea[j], 0, 0)
    expert_b = lambda j, ea, eb, nb: (layer, eb[j], 0, 0)
    rows = lambda j, ea, eb, nb: (j, 0)
    grid_spec = pltpu.PrefetchScalarGridSpec(
        num_scalar_prefetch=3,
        grid=(n_blocks,),
        in_specs=[pl.BlockSpec((blk, half), rows),
                  pl.BlockSpec((1, 1, d, de), expert_a), pl.BlockSpec((1, 1, d, de), expert_a),
                  pl.BlockSpec((1, 1, de, d), expert_a),
                  pl.BlockSpec((1, 1, d, de), expert_b), pl.BlockSpec((1, 1, d, de), expert_b),
                  pl.BlockSpec((1, 1, de, d), expert_b)],
        out_specs=pl.BlockSpec((blk, d), rows),
        scratch_shapes=[pltpu.VMEM((2, d, de), BF16), pltpu.VMEM((2, d, de), BF16), pltpu.VMEM((2, de, d), BF16)],
    )
    return pl.pallas_call(
        _ffn_kernel,
        grid_spec=grid_spec,
        out_shape=jax.ShapeDtypeStruct((n_rows, d), jnp.uint32),
        compiler_params=pltpu.CompilerParams(
            dimension_semantics=("arbitrary",),
            vmem_limit_bytes=_vmem_limit(2 * 3 * d * de * (2 * 4 + 2) + 6 * blk * d * 4 + 8 * blk * de * 4)),
        name="moe_ffn",
    )(block_ea, block_eb, n_used, xs, wg, wu, wd, wg, wu, wd)


def _combine_kernel(dest_ref, dest_next_ref, ys_ref, gcol_ref, res_ref, lng_ref, lnb_ref, o_ref, buf_ref, sem):
    i = pl.program_id(0)
    n = pl.num_programs(0)
    tt = res_ref.shape[0]
    cur = i % 2

    def gather(idx_ref, buf):
        def row_copy(t):
            return pltpu.make_async_copy(ys_ref.at[pl.ds(idx_ref[0, 0, t], 1)], buf_ref.at[buf, pl.ds(t, 1)],
                                         sem.at[buf])

        def issue(t0, carry):
            for u in range(DMA_ISSUE_UNROLL):
                row_copy(t0 * DMA_ISSUE_UNROLL + u).start()
            return carry

        lax.fori_loop(0, tt // DMA_ISSUE_UNROLL, issue, 0)

    @pl.when(i == 0)
    def _():
        gather(dest_ref, cur)

    @pl.when(i + 1 < n)
    def _():
        gather(dest_next_ref, 1 - cur)

    pltpu.make_async_copy(ys_ref.at[pl.ds(0, tt)], buf_ref.at[cur], sem.at[cur]).wait()

    gcol = gcol_ref[...]
    half = buf_ref.shape[2] // 2
    lo_a, hi_a = _unpack_halves(buf_ref[cur, :, :half])
    lo_b, hi_b = _unpack_halves(buf_ref[cur, :, half:])
    swap = gcol[:, 2:3] > 0.5
    g_a = jnp.where(swap, gcol[:, 1:2], gcol[:, 0:1])
    g_b = jnp.where(swap, gcol[:, 0:1], gcol[:, 1:2])
    ffn = jnp.concatenate([lo_a * g_a + lo_b * g_b, hi_a * g_a + hi_b * g_b], axis=1)
    o_ref[...] = _layer_norm_rows(ALPHA * res_ref[...] + ffn, lng_ref[...], lnb_ref[...])


def _moe_combine(ys, dest_tiles, gcol, res, ln_g, ln_b):
    n_tok, d = res.shape
    tt = dest_tiles.shape[2]
    n_tiles = n_tok // tt
    full = lambda arr: pl.BlockSpec(arr.shape, lambda i: (0,) * arr.ndim)
    return pl.pallas_call(
        _combine_kernel,
        grid=(n_tiles,),
        in_specs=[pl.BlockSpec((1, 1, tt), lambda i: (i, 0, 0), memory_space=pltpu.SMEM),
                  pl.BlockSpec((1, 1, tt), lambda i: (jnp.minimum(i + 1, n_tiles - 1), 0, 0),
                               memory_space=pltpu.SMEM),
                  pl.BlockSpec(memory_space=pl.ANY),
                  pl.BlockSpec((tt, V7X_LANES), lambda i: (i, 0)),
                  pl.BlockSpec((tt, d), lambda i: (i, 0)),
                  full(ln_g), full(ln_b)],
        out_specs=pl.BlockSpec((tt, d), lambda i: (i, 0)),
        out_shape=jax.ShapeDtypeStruct((n_tok, d), F32),
        scratch_shapes=[pltpu.VMEM((2, tt, ys.shape[1]), ys.dtype), pltpu.SemaphoreType.DMA((2,))],
        compiler_params=pltpu.CompilerParams(
            dimension_semantics=("arbitrary",),
            vmem_limit_bytes=_vmem_limit(4 * tt * d * 2 + 2 * 3 * tt * d * 4)),
        name="moe_combine",
    )(dest_tiles, dest_tiles, ys, gcol, res, ln_g, ln_b)


def _moe_layer(h, h_packed, route, counts, gcol, layer, wg, wu, wd, ln_g, ln_b):
    n_tok, d = h.shape
    blk = MOE_BLOCK
    n_rows = (n_tok + N_CLASSES * (blk - 1) + blk - 1) // blk * blk
    n_blocks = n_rows // blk
    cls, rank = route[0], route[1]
    counts = counts[:N_CLASSES, 0]
    padded = (counts + blk - 1) // blk * blk
    pend = jnp.cumsum(padded)
    pstart = pend - padded
    onehot = (cls[:, None] == jnp.arange(N_CLASSES, dtype=jnp.int32)).astype(jnp.int32)
    dest = rank + jnp.sum(onehot * pstart, axis=-1)
    tiles = lambda tt: dest.reshape(n_tok // tt, 1, tt)
    blk_start = jnp.arange(n_blocks, dtype=jnp.int32) * blk
    block_cls = jnp.minimum(jnp.sum((blk_start[:, None] >= pend[None, :]).astype(jnp.int32), axis=1), N_CLASSES - 1)
    n_used = (pend[-1:] // blk).astype(jnp.int32)
    blk_idx = jnp.arange(n_blocks, dtype=jnp.int32)
    block_cls = jnp.where(blk_idx < n_used[0], block_cls, block_cls[jnp.maximum(n_used[0] - 1, 0)])
    class_ids = jnp.arange(N_CLASSES, dtype=jnp.int32)
    first_expert = (class_ids // N_PAIRS) * EXPERTS_PER_GROUP
    expert_a = first_expert + jnp.array(_PAIR_SLOT_A, jnp.int32)[class_ids % N_PAIRS]
    expert_b = first_expert + jnp.array(_PAIR_SLOT_B, jnp.int32)[class_ids % N_PAIRS]
    xs = _moe_dispatch(h_packed, tiles(DISPATCH_TILE), n_rows)
    ys = _moe_ffn(xs, expert_a[block_cls].astype(jnp.int32), expert_b[block_cls].astype(jnp.int32), n_used,
                  layer, wg, wu, wd)
    return _moe_combine(ys, tiles(COMBINE_TILE), gcol, h, ln_g, ln_b)


def _fox_proj_kernel(x_ref, wq_ref, wk_ref, wv_ref, wf_ref, bf_ref, q_ref, k_ref, v_ref, c_ref, carry_ref):
    t = pl.program_id(1)

    @pl.when(t == 0)
    def _():
        carry_ref[...] = jnp.zeros_like(carry_ref)

    tt = PROJ_SUBTILE
    subs = [slice(u * tt, (u + 1) * tt) for u in range(x_ref.shape[0] // tt)]
    each = lambda fn, *lists: [fn(*args) for args in zip(*lists)]
    parts = each(lambda sl: _split3(x_ref[sl, :]), subs)
    wh, wm, wl = _split3(wf_ref[...])
    mm = lambda a, b: jnp.dot(a, b, preferred_element_type=F32)
    logit = each(lambda p: (mm(p[0], wh) + (mm(p[0], wm) + mm(p[1], wh))
                            + (mm(p[0], wl) + mm(p[1], wm) + mm(p[2], wh))) + bf_ref[...], parts)
    for sl, p in zip(subs, parts):
        q_ref[sl, :] = (mm(p[0], wq_ref[...]) * LOG2E).astype(q_ref.dtype)
    for sl, p in zip(subs, parts):
        k_ref[sl, :] = mm(p[0], wk_ref[...]).astype(k_ref.dtype)
    for sl, p in zip(subs, parts):
        v_ref[sl, :] = mm(p[0], wv_ref[...]).astype(v_ref.dtype)
    ti = lax.broadcasted_iota(jnp.int32, (tt, tt), 0)
    tj = lax.broadcasted_iota(jnp.int32, (tt, tt), 1)
    tril = (tj <= ti).astype(BF16)
    local = each(lambda lg: sum(mm(tril, part) for part in _split3(jax.nn.log_sigmoid(lg))), logit)
    carry = carry_ref[0:1, :]
    for sl, cs in zip(subs, local):
        c = cs + carry
        c_ref[sl, :] = c
        carry = c[tt - 1:tt, :]
    carry_ref[...] = jnp.broadcast_to(carry, carry_ref.shape)


def _fox_proj(x2, batch, seq_len, wq, wk, wv, wf, b_f):
    n_tok, d = x2.shape
    tt = PROJ_TILE
    nt = seq_len // tt
    tile = pl.BlockSpec((tt, d), lambda b, t: (b * nt + t, 0))
    full = lambda arr: pl.BlockSpec(arr.shape, lambda b, t: (0,) * arr.ndim)
    return pl.pallas_call(
        _fox_proj_kernel,
        grid=(batch, nt),
        in_specs=[tile, full(wq), full(wk), full(wv), full(wf), full(b_f)],
        out_specs=[tile, tile, tile, pl.BlockSpec((tt, N_HEADS), lambda b, t: (b * nt + t, 0))],
        out_shape=[jax.ShapeDtypeStruct((n_tok, d), BF16)] * 3 + [jax.ShapeDtypeStruct((n_tok, N_HEADS), F32)],
        scratch_shapes=[pltpu.VMEM((8, N_HEADS), F32)],
        compiler_params=pltpu.CompilerParams(
            dimension_semantics=("parallel", "arbitrary"),
            vmem_limit_bytes=_vmem_limit(2 * 3 * d * d * 2 + 2 * tt * d * (4 + 3 * 2) + 8 * tt * d * 4)),
        name="fox_proj",
    )(x2, wq, wk, wv, wf, b_f)


def _fox_attn_kernel(q_ref, k_ref, v_ref, ct_ref, o_ref):
    hp = pl.program_id(1)
    seq_len = q_ref.shape[0]
    tq = ATTN_TILE
    tk = tq
    nq = seq_len // tq
    N = HEAD_DIM
    nt = lambda a, b: lax.dot_general(a, b, (((1,), (1,)), ((), ())), preferred_element_type=F32)
    mm = lambda a, b: jnp.dot(a, b, preferred_element_type=F32)

    lane = lax.broadcasted_iota(jnp.int32, (tq, 2 * N), 1)
    vrow = lax.broadcasted_iota(jnp.int32, (2 * N, tk), 0)
    orow = lax.broadcasted_iota(jnp.int32, (2 * N, tq), 0)
    kpos = lax.broadcasted_iota(jnp.int32, (tk, tq), 0)
    qpos = lax.broadcasted_iota(jnp.int32, (tk, tq), 1)
    causal = kpos <= qpos
    ident_v = (lax.broadcasted_iota(jnp.int32, (2 * N, 2 * N), 0)
               == lax.broadcasted_iota(jnp.int32, (2 * N, 2 * N), 1)).astype(BF16)

    cq_row = [ct_ref[0, pl.ds(2 * hp + u, 1), :] * LOG2E for u in range(2)]

    def key_gates(j):
        return [jnp.broadcast_to(cq_row[u][:, j * tk:(j + 1) * tk], (tq, tk)).T for u in range(2)]

    m = [[None, None] for _ in range(nq)]
    l = [[None, None] for _ in range(nq)]
    acc = [None] * nq

    zero_q = jnp.zeros((tq, 2 * N), BF16)
    keys, values = {}, {}

    def scores(j, qi):
        if j not in keys:
            keys[j] = (k_ref[j * tk:(j + 1) * tk, :], key_gates(j))
        k_j, ck_b = keys[j]
        q = q_ref[qi * tq:(qi + 1) * tq, :]
        t = [nt(k_j, jnp.where((lane < N) == (u == 0), q, zero_q)) - ck_b[u] for u in range(2)]
        return [jnp.where(causal, tu, -jnp.inf) for tu in t] if qi == j else t

    def absorb(j, qi, t):
        if j not in values:
            v_t = nt(ident_v, v_ref[j * tk:(j + 1) * tk, :]).astype(BF16)
            values[j] = [jnp.where(vrow < N, v_t, jnp.zeros_like(v_t)), jnp.where(vrow < N, jnp.zeros_like(v_t), v_t)]
        v_heads = values[j]
        cq = [cq_row[u][:, qi * tq:(qi + 1) * tq] for u in range(2)]
        rmax = [jnp.max(t[u], axis=0, keepdims=True) + cq[u] for u in range(2)]
        m_new = rmax if j == 0 else [jnp.maximum(m[qi][u], rmax[u]) for u in range(2)]
        p = [jnp.exp2(t[u] + (cq[u] - m_new[u])) for u in range(2)]
        psum = [jnp.sum(p[u], axis=0, keepdims=True) for u in range(2)]
        pv = mm(v_heads[0], p[0].astype(BF16)) + mm(v_heads[1], p[1].astype(BF16))
        if j == 0:
            acc[qi] = pv
            l[qi] = psum
        else:
            alpha = [jnp.exp2(m[qi][u] - m_new[u]) for u in range(2)]
            acc[qi] = acc[qi] * jnp.where(orow < N, alpha[0], alpha[1]) + pv
            l[qi] = [alpha[u] * l[qi][u] + psum[u] for u in range(2)]
        m[qi] = m_new
        if j == qi:
            o_t = acc[qi] / jnp.where(orow < N, l[qi][0], l[qi][1])
            o_ref[qi * tq:(qi + 1) * tq, :] = o_t.T.astype(o_ref.dtype)

    pairs = [(j, qi) for j in range(nq) for qi in range(j, nq)]
    t_next = scores(*pairs[0])
    for n, (j, qi) in enumerate(pairs):
        t_cur = t_next
        if n + 1 < len(pairs):
            t_next = scores(*pairs[n + 1])
        absorb(j, qi, t_cur)


def _fox_attn(q, k, v, c_t, batch, seq_len):
    n_tok, d = q.shape
    pair = 2 * HEAD_DIM
    seq = pl.BlockSpec((seq_len, pair), lambda b, hp: (b, hp))
    return pl.pallas_call(
        _fox_attn_kernel,
        grid=(batch, d // pair),
        in_specs=[seq, seq, seq, pl.BlockSpec((1, N_HEADS, seq_len), lambda b, hp: (b, 0, 0))],
        out_specs=seq,
        out_shape=jax.ShapeDtypeStruct((n_tok, d), BF16),
        compiler_params=pltpu.CompilerParams(
            dimension_semantics=("parallel", "arbitrary"),
            vmem_limit_bytes=_vmem_limit(2 * 4 * seq_len * pair * 2 + 64 * ATTN_TILE * ATTN_TILE * 4)),
        name="fox_attn",
    )(q, k, v, c_t)


def kernel(x, rw_mix, rw_wr, rw_wk, rw_wv, rw_wo, rw_w0, rw_w1, rw_w2, rw_a0, rw_a1, rw_a2, rw_g1, rw_g2,
           rw_kk, rw_ka, rw_rk, rw_gn_g, rw_gn_b, fx_w_in, fx_b_f, fx_wo, router_w, router_bias,
           moe_w_gate, moe_w_up, moe_w_down, ln_g, ln_b):
    batch, seq_len, d = x.shape
    n_tok = batch * seq_len
    bf = lambda w: w.astype(BF16)
    row = lambda w: w.reshape(1, -1)
    router_wt = router_w.T
    router_b = router_bias.reshape(N_EXPERTS, 1)
    h = x.reshape(n_tok, d)

    for i in range(DEPTH):
        j = i // 2
        if i % 2 == 0:
            r, lw, k, v, a, g = _rwkv_proj(
                h, seq_len, rw_mix[j], bf(rw_wr[j]), bf(rw_wk[j]), bf(rw_wv[j]), bf(rw_w1[j]), bf(rw_w2[j]),
                bf(rw_a1[j]), bf(rw_a2[j]), bf(rw_g1[j]), bf(rw_g2[j]), row(rw_w0[j]), row(rw_a0[j]))
            act = _rwkv_recur(r, lw, k, v, a, g, batch, seq_len, row(rw_kk[j]), row(rw_ka[j]), row(rw_rk[j]),
                              row(rw_gn_g[j]), row(rw_gn_b[j]))
            wo = bf(rw_wo[j])
        else:
            w_in = fx_w_in[j]
            scale = HEAD_DIM ** -0.5
            q, k, v, c = _fox_proj(h, batch, seq_len, bf(w_in[:, :d] * scale), bf(w_in[:, d:2 * d]),
                                   bf(w_in[:, 2 * d:3 * d]), w_in[:, 3 * d:], row(fx_b_f[j]))
            c_t = c.reshape(batch, seq_len, N_HEADS).transpose(0, 2, 1)
            act = _fox_attn(q, k, v, c_t, batch, seq_len)
            wo = bf(fx_wo[j])
        h, h_packed, route, gcol, counts = _mixer_epilogue(act, wo, h, row(ln_g[i, 0]), row(ln_b[i, 0]),
                                                           router_wt, router_b)
        h = _moe_layer(h, h_packed, route, counts, gcol, i, moe_w_gate, moe_w_up, moe_w_down,
                       row(ln_g[i, 1]), row(ln_b[i, 1]))
    return h.reshape(batch, seq_len, d)
```

```python
import functools
import math

import jax
import jax.numpy as jnp
from jax import lax
from jax.experimental import pallas as pl
from jax.experimental.pallas import tpu as pltpu

D_MODEL = 1024
HEAD_DIM = 64
N_HEADS = D_MODEL // HEAD_DIM
N_EXPERTS = 16
N_GROUPS = 4
EXPERTS_PER_GROUP = N_EXPERTS // N_GROUPS
D_EXPERT = 512
N_PAIRS = 6
N_CLASSES = N_GROUPS * N_PAIRS
CLASS_ROWS = 32
_PAIR_SLOT_A = (0, 2, 2, 3, 3, 3)
_PAIR_SLOT_B = (1, 1, 0, 0, 1, 2)
GN_EPS = 64e-5
LN_EPS = 1e-5
DEPTH = 2
ALPHA = (2 * DEPTH) ** 0.25
LOG2E = math.log2(math.e)

V7X_LANES = 128
V7X_VMEM_BYTES = 64 * 2 ** 20

V7X_MXU_DIM = 256

RWKV_CHUNK = 64
RWKV_CHUNKS_PER_STEP = 4
RWKV_HEADS_PER_TILE = V7X_MXU_DIM // HEAD_DIM
PROJ_TILE = 512
PROJ_SUBTILE = 256
EPI_TILE = 1024
EPI_SUBTILE = 256
ATTN_TILE = 256
ATTN_PAIRS_PER_STEP = 1
ATTN_LOOKAHEAD = 2
MOE_BLOCK = 256
DISPATCH_TILE = 2048
COMBINE_TILE = 512
DMA_ISSUE_UNROLL = 8

F32 = jnp.float32
BF16 = jnp.bfloat16


def _vmem_limit(n_bytes):
    return int(min(n_bytes + 16 * 2 ** 20, V7X_VMEM_BYTES - 8 * 2 ** 20))


def _split3(x):
    hi = x.astype(BF16)
    r1 = x - hi.astype(F32)
    mid = r1.astype(BF16)
    lo = (r1 - mid.astype(F32)).astype(BF16)
    return hi, mid, lo


def _pack_halves(x):
    half = x.shape[1] // 2
    bits = lambda v: lax.bitcast_convert_type(v.astype(BF16).astype(F32), jnp.uint32)
    return (bits(x[:, :half]) >> 16) | (bits(x[:, half:]) & jnp.uint32(0xFFFF0000))


def _unpack_halves(w):
    lo = lax.bitcast_convert_type(w << 16, F32)
    hi = lax.bitcast_convert_type(w & jnp.uint32(0xFFFF0000), F32)
    return lo, hi


def _layer_norm_rows(x, g, b):
    mu = jnp.mean(x, axis=-1, keepdims=True)
    xc = x - mu
    var = jnp.mean(xc * xc, axis=-1, keepdims=True)
    return xc * lax.rsqrt(var + LN_EPS) * g + b


def _rwkv_proj_kernel(x_ref, xp_ref, mix_ref, wr_ref, wk_ref, wv_ref, w1_ref, w2_ref, a1_ref, a2_ref,
                      g1_ref, g2_ref, w0_ref, a0_ref,
                      r_ref, lw_ref, k_ref, v_ref, a_ref, g_ref, *, tiles_per_seq):
    i = pl.program_id(0)
    x = x_ref[...]
    tt = x.shape[0]
    first = (i % tiles_per_seq) == 0
    prev_row = jnp.where(first, 0.0, xp_ref[7:8, :])
    row = lax.broadcasted_iota(jnp.int32, (tt, 1), 0)
    xprev = jnp.where(row == 0, prev_row, pltpu.roll(x, 1, axis=0))
    xx = xprev - x
    mix = mix_ref[...]
    subs = [slice(u * PROJ_SUBTILE, (u + 1) * PROJ_SUBTILE) for u in range(tt // PROJ_SUBTILE)]
    each = lambda fn, *lists: [fn(*args) for args in zip(*lists)]
    mixed = lambda j: each(lambda sl: (x[sl] + xx[sl] * mix[j:j + 1, :]).astype(BF16), subs)
    xr, xw, xk, xv, xa, xg = (mixed(j) for j in range(6))
    mm = lambda a, w_ref: jnp.dot(a, w_ref[...], preferred_element_type=F32)
    w_mid = each(lambda a: mm(a, w1_ref), xw)
    a_mid = each(lambda a: mm(a, a1_ref), xa)
    g_mid = each(lambda a: mm(a, g1_ref), xg)
    for sl, a in zip(subs, xr):
        r_ref[sl, :] = mm(a, wr_ref).astype(r_ref.dtype)
    for sl, a in zip(subs, xk):
        k_ref[sl, :] = mm(a, wk_ref).astype(k_ref.dtype)
    for sl, a in zip(subs, xv):
        v_ref[sl, :] = mm(a, wv_ref).astype(v_ref.dtype)
    z = each(lambda t: w0_ref[...] + mm(jnp.tanh(t).astype(BF16), w2_ref), w_mid)
    for sl, t in zip(subs, a_mid):
        a_ref[sl, :] = jax.nn.sigmoid(a0_ref[...] + mm(t.astype(BF16), a2_ref)).astype(a_ref.dtype)
    for sl, t in zip(subs, g_mid):
        g_ref[sl, :] = mm(jax.nn.sigmoid(t).astype(BF16), g2_ref).astype(g_ref.dtype)
    for sl, zs in zip(subs, z):
        lw_ref[sl, :] = -jnp.exp(-jax.nn.softplus(-zs) - 0.5)


def _rwkv_proj(x2, seq_len, mix, wr, wk, wv, w1, w2, a1, a2, g1, g2, w0, a0):
    n_tok, d = x2.shape
    tt = PROJ_TILE
    n_tiles = n_tok // tt
    tiles_per_seq = seq_len // tt
    tile = pl.BlockSpec((tt, d), lambda i: (i, 0))
    prev = pl.BlockSpec((8, d), lambda i: (jnp.maximum(i * (tt // 8) - 1, 0), 0))
    full = lambda arr: pl.BlockSpec(arr.shape, lambda i: (0,) * arr.ndim)
    weights = (mix, wr, wk, wv, w1, w2, a1, a2, g1, g2, w0, a0)
    out_dtypes = (BF16, F32, BF16, BF16, BF16, BF16)
    w_bytes = sum(int(w.size) * w.dtype.itemsize for w in weights)
    return pl.pallas_call(
        functools.partial(_rwkv_proj_kernel, tiles_per_seq=tiles_per_seq),
        grid=(n_tiles,),
        in_specs=[tile, prev] + [full(w) for w in weights],
        out_specs=[tile] * 6,
        out_shape=[jax.ShapeDtypeStruct((n_tok, d), dt) for dt in out_dtypes],
        compiler_params=pltpu.CompilerParams(
            dimension_semantics=("parallel",),
            vmem_limit_bytes=_vmem_limit(2 * w_bytes + 2 * tt * d * (4 + 4 + 5 * 2) + 8 * tt * d * 4)),
        name="rwkv_proj",
    )(x2, x2, *weights)


def _rwkv_recur_kernel(r_ref, lw_ref, k_ref, v_ref, a_ref, g_ref, kk_ref, ka_ref, rk_ref, gng_ref, gnb_ref,
                       o_ref, s_ref):
    c = pl.program_id(1)
    C = RWKV_CHUNK
    N = HEAD_DIM

    @pl.when(c == 0)
    def _():
        s_ref[...] = jnp.zeros_like(s_ref)

    nt = lambda p, q: lax.dot_general(p, q, (((1,), (1,)), ((), ())), preferred_element_type=F32)
    mm = lambda p, q: jnp.dot(p, q, preferred_element_type=F32)
    each = lambda fn, *lists: [fn(*args) for args in zip(*lists)]
    bf = lambda x: x.astype(BF16)

    ti = lax.broadcasted_iota(jnp.int32, (C, C), 0)
    tj = lax.broadcasted_iota(jnp.int32, (C, C), 1)
    tril = (tj <= ti).astype(BF16)

    def chunk_terms(rows):
        lw = lw_ref[rows, :]
        cum = sum(mm(tril, part) for part in _split3(lw))
        rho = cum[C // 2 - 1:C // 2, :]
        last = cum[C - 1:C, :]
        r = r_ref[rows, :].astype(F32)
        k = k_ref[rows, :].astype(F32)
        a = a_ref[rows, :].astype(F32)
        k_mod = k * (1.0 + (a - 1.0) * ka_ref[...])
        return dict(e_q=jnp.exp(cum - rho), e_qx=jnp.exp(cum - lw - rho), e_k=jnp.exp(rho - cum),
                    e_end=jnp.exp(last - cum), e_rho=jnp.exp(rho), d_end=jnp.exp(last),
                    r=r, a=a, v=v_ref[rows, :].astype(F32), kk_raw=k * kk_ref[...], k_mod=k_mod,
                    rkk=r * k_mod * rk_ref[...])

    n_ch = r_ref.shape[0] // C
    terms = [chunk_terms(slice(ci * C, (ci + 1) * C)) for ci in range(n_ch)]

    G = RWKV_HEADS_PER_TILE
    R = G * C
    GW = G * N
    er = lax.broadcasted_iota(jnp.int32, (R, GW), 0)
    ec = lax.broadcasted_iota(jnp.int32, (R, GW), 1)
    blk = (er // C) == (ec // N)
    strict = (ec % C) < (er % C)
    incl = (ec % C) <= (er % C)
    eye = (er == ec).astype(F32)
    ones_blk = blk.astype(BF16)

    def expand(x):
        return jnp.where(blk, jnp.concatenate([x] * G, axis=0), 0.0).astype(BF16)

    expand_t = lambda x: bf(jnp.where(blk, jnp.concatenate([x] * G, axis=0), 0.0).T)

    def head_sums(xs):
        parts = _split3(jnp.concatenate(xs, axis=0))[:2]
        tot = mm(jnp.concatenate(parts, axis=0), ones_blk)
        n = len(xs) * C
        tot = tot[:n] + tot[n:]
        return [tot[u * C:(u + 1) * C] for u in range(len(xs))]

    n_grp = N_HEADS // G
    groups = [slice(gi * GW, (gi + 1) * GW) for gi in range(n_grp)]
    units = [(tm, sl) for tm in terms for sl in groups]

    def independent(us, out):
        u_tm = [tm for tm, _ in us]
        u_sl = [sl for _, sl in us]
        pre = head_sums([tm["kk_raw"][:, sl] * tm["kk_raw"][:, sl] for tm, sl in us] + [tm["rkk"][:, sl] for tm, sl in us])
        kk_ss, out["rkk_sum"] = pre[:len(us)], pre[len(us):]
        yield
        kk_n = each(lambda tm, sl, ss: tm["kk_raw"][:, sl] * lax.rsqrt(jnp.maximum(ss, 1e-24)), u_tm, u_sl, kk_ss)
        b_n = each(lambda tm, sl, kk_g: kk_g * tm["a"][:, sl], u_tm, u_sl, kk_n)
        kk_q = each(lambda tm, sl, kk_g: kk_g * tm["e_qx"][:, sl], u_tm, u_sl, kk_n)
        r_q = each(lambda tm, sl: tm["r"][:, sl] * tm["e_q"][:, sl], u_tm, u_sl)
        q2 = each(lambda x, y: jnp.concatenate([expand(x), expand(y)], axis=0), kk_q, r_q)
        a_k = each(lambda tm, sl, q: nt(q, expand(tm["k_mod"][:, sl] * tm["e_k"][:, sl])), u_tm, u_sl, q2)
        yield
        a_b = each(lambda tm, sl, q, b_g: nt(q, expand(b_g * tm["e_k"][:, sl])), u_tm, u_sl, q2, b_n)
        yield
        a_kk = each(lambda x: jnp.where(strict, x[:R], 0.0), a_k)
        a_rk = each(lambda x: jnp.where(incl, x[R:], 0.0), a_k)
        a_kb = each(lambda x: jnp.where(strict, x[:R], 0.0), a_b)
        out["a_rb"] = each(lambda x: jnp.where(incl, x[R:], 0.0), a_b)
        t_inv = each(lambda x: eye - x, a_kb)
        p = each(lambda x: mm(bf(-x), bf(-x)), a_kb)
        yield
        for _ in range(int(math.log2(C)) - 2):
            both = each(lambda pg, tg: mm(bf(pg), jnp.concatenate([bf(pg), bf(tg)], axis=1)), p, t_inv)
            p = each(lambda x: x[:, :R], both)
            t_inv = each(lambda tg, x: tg + x[:, R:], t_inv, both)
            yield
        out["t_inv"] = each(lambda pg, tg: tg + mm(bf(pg), bf(tg)), p, t_inv)
        yield
        out["kd_t"] = each(lambda tm, sl: expand_t(tm["k_mod"][:, sl] * tm["e_end"][:, sl]), u_tm, u_sl)
        out["bd_t"] = each(lambda tm, sl, b_g: expand_t(b_g * tm["e_end"][:, sl]), u_tm, u_sl, b_n)
        out["d_col"] = each(lambda tm, sl: jnp.broadcast_to(tm["d_end"][:, sl], (GW, GW)).T, u_tm, u_sl)
        out["v_e"] = each(lambda tm, sl: expand(tm["v"][:, sl]), u_tm, u_sl)
        out["av"] = each(lambda x, y, ve: mm(bf(jnp.concatenate([x, y], axis=0)), ve), a_kk, a_rk, out["v_e"])
        out["q2_abs"] = each(lambda tm, sl, x, y: jnp.concatenate([expand(x * tm["e_rho"][:, sl]),
                                                                   expand(y * tm["e_rho"][:, sl])], axis=0),
                             u_tm, u_sl, kk_q, r_q)
        yield

    state = {"st": [s_ref[gi] for gi in range(n_grp)]}
    ys = [None] * len(units)

    def dependent(ci, res, lo):
        span = slice(lo, lo + n_grp)
        qs = each(lambda q, s: mm(q, bf(s)), res["q2_abs"][span], state["st"])
        yield
        sa_e = each(lambda tg, q, x: bf(mm(bf(tg), bf(q[:R] + x[:R]))), res["t_inv"][span], qs, res["av"][span])
        yield
        y_e = each(lambda q, x, arb, sa: q[R:] + x[R:] - mm(bf(arb), sa), qs, res["av"][span], res["a_rb"][span], sa_e)
        ys[ci * n_grp:(ci + 1) * n_grp] = each(lambda x: sum(x[u * C:(u + 1) * C] for u in range(G)), y_e)
        yield
        upd = each(lambda kt, bt, ve, sa: mm(jnp.concatenate([kt, -bt], axis=1), jnp.concatenate([ve, sa], axis=0)),
                   res["kd_t"][span], res["bd_t"][span], res["v_e"][span], sa_e)
        state["st"] = each(lambda s, dc, up: s * dc + up, state["st"], res["d_col"][span], upd)
        yield

    half = (n_ch // 2) * n_grp
    first, second = {}, {}
    for _ in independent(units[:half], first):
        pass
    chain = [stage for ci in range(n_ch // 2) for stage in [dependent(ci, first, ci * n_grp)]]
    pending = iter(())
    todo = list(chain)

    def advance():
        nonlocal pending
        while True:
            try:
                next(pending)
                return True
            except StopIteration:
                if not todo:
                    return False
                pending = todo.pop(0)

    for _ in independent(units[half:], second):
        advance()
    while advance():
        pass
    for ci in range(n_ch // 2, n_ch):
        for _ in dependent(ci, second, (ci - n_ch // 2) * n_grp):
            pass
    for gi in range(n_grp):
        s_ref[gi] = state["st"][gi]
    rkk_sum = first["rkk_sum"] + second["rkk_sum"]

    inv_n = 1.0 / N
    ycs = [y - mu * inv_n for y, mu in zip(ys, head_sums(ys))]
    sqs = head_sums([yc * yc for yc in ycs])
    for ui, (tm, sl) in enumerate(units):
        rows = slice((ui // n_grp) * C, (ui // n_grp + 1) * C)
        yn = ycs[ui] * lax.rsqrt(sqs[ui] * inv_n + GN_EPS) * gng_ref[:, sl] + gnb_ref[:, sl]
        o_ref[rows, sl] = ((yn + rkk_sum[ui] * tm["v"][:, sl]) * g_ref[rows, sl]).astype(o_ref.dtype)


def _rwkv_recur(r, lw, k, v, a, g, batch, seq_len, k_k, k_a, r_k, gn_g, gn_b):
    n_tok, d = r.shape
    C = RWKV_CHUNK * RWKV_CHUNKS_PER_STEP
    nc = seq_len // C
    tile = pl.BlockSpec((C, d), lambda b, c: (b * nc + c, 0))
    vec = pl.BlockSpec((1, d), lambda b, c: (0, 0))
    return pl.pallas_call(
        _rwkv_recur_kernel,
        grid=(batch, nc),
        in_specs=[tile] * 6 + [vec] * 5,
        out_specs=tile,
        out_shape=jax.ShapeDtypeStruct((n_tok, d), BF16),
        scratch_shapes=[pltpu.VMEM((N_HEADS // RWKV_HEADS_PER_TILE, V7X_MXU_DIM, V7X_MXU_DIM), F32)],
        compiler_params=pltpu.CompilerParams(
            dimension_semantics=("parallel", "arbitrary"),
            vmem_limit_bytes=_vmem_limit(2 * 7 * C * d * 4 + 32 * C * d * 4 + 64 * V7X_MXU_DIM ** 2 * 4)),
        name="rwkv_recur",
    )(r, lw, k, v, a, g, k_k, k_a, r_k, gn_g, gn_b)


def _rank_among(vals, i):
    cnt = 0
    for j, vj in enumerate(vals):
        if j == i:
            continue
        before = (vj >= vals[i]) if j < i else (vj > vals[i])
        cnt = cnt + before.astype(jnp.int32)
    return cnt


def _pick(ranks, vals, want):
    out = vals[0]
    for rk, vl in zip(ranks[1:], vals[1:]):
        out = jnp.where(rk == want, vl, out)
    return out


def _epilogue_kernel(act_ref, wo_ref, res_ref, lng_ref, lnb_ref, rwt_ref, rb_ref,
                     h_ref, hp_ref, route_ref, gcol_ref, cnt_ref, base_ref):
    i = pl.program_id(0)

    @pl.when(i == 0)
    def _():
        base_ref[...] = jnp.zeros_like(base_ref)

    tt = EPI_SUBTILE
    subs = [slice(u * tt, (u + 1) * tt) for u in range(act_ref.shape[0] // tt)]
    each = lambda fn, *lists: [fn(*args) for args in zip(*lists)]
    nt = lambda a, b: lax.dot_general(a, b, (((1,), (1,)), ((), ())), preferred_element_type=F32)

    wh, wm, wl = _split3(rwt_ref[...])

    def router_logits(hs):
        hh, hm, hl = _split3(hs)
        return nt(wh, hh) + (nt(wh, hm) + nt(wm, hh)) + (nt(wh, hl) + nt(wm, hm) + nt(wl, hh))

    def project(sl):
        return jnp.dot(act_ref[sl, :], wo_ref[...], preferred_element_type=F32)

    def normalise(sl, mx):
        hs = _layer_norm_rows(ALPHA * res_ref[sl, :] + mx, lng_ref[...], lnb_ref[...])
        h_ref[sl, :] = hs
        hp_ref[sl, :] = _pack_halves(hs)
        return jax.nn.sigmoid(router_logits(hs))

    s = []
    mixed = project(subs[0])
    for u in range(len(subs)):
        nxt = project(subs[u + 1]) if u + 1 < len(subs) else None
        s.append(normalise(subs[u], mixed))
        mixed = nxt

    def select(sg):
        s_sel = sg + rb_ref[...]
        rows = [s_sel[e:e + 1, :] for e in range(N_EXPERTS)]
        grp_score, grp_i0, grp_i1 = [], [], []
        for gi in range(N_GROUPS):
            vals = rows[gi * EXPERTS_PER_GROUP:(gi + 1) * EXPERTS_PER_GROUP]
            ranks = [_rank_among(vals, q) for q in range(EXPERTS_PER_GROUP)]
            idx = [jnp.full_like(ranks[0], q) for q in range(EXPERTS_PER_GROUP)]
            grp_score.append(_pick(ranks, vals, 0) + _pick(ranks, vals, 1))
            grp_i0.append(_pick(ranks, idx, 0))
            grp_i1.append(_pick(ranks, idx, 1))
        g_ranks = [_rank_among(grp_score, q) for q in range(N_GROUPS)]
        gidx = [jnp.full_like(g_ranks[0], q) for q in range(N_GROUPS)]
        g_star = _pick(g_ranks, gidx, 0)
        e0 = g_star * EXPERTS_PER_GROUP + _pick(g_ranks, grp_i0, 0)
        e1 = g_star * EXPERTS_PER_GROUP + _pick(g_ranks, grp_i1, 0)
        return e0, e1

    picked = each(select, s)
    e_iota = lax.broadcasted_iota(jnp.int32, (N_EXPERTS, tt), 0)

    def gates(sg, pk):
        gate0 = jnp.sum(jnp.where(e_iota == pk[0], sg, 0.0), axis=0, keepdims=True)
        gate1 = jnp.sum(jnp.where(e_iota == pk[1], sg, 0.0), axis=0, keepdims=True)
        denom = gate0 + gate1
        return gate0 / denom, gate1 / denom

    gate = each(gates, s, picked)

    def classify(pk):
        e0, e1 = pk
        grp = lax.shift_right_logical(e0, 2)
        l0 = e0 - grp * EXPERTS_PER_GROUP
        l1 = e1 - grp * EXPERTS_PER_GROUP
        lo, hi = jnp.minimum(l0, l1), jnp.maximum(l0, l1)
        pair = jnp.where(hi == 1, 0, jnp.where(hi == 2, jnp.where(lo == 1, 1, 2),
                                               jnp.where(lo == 0, 3, jnp.where(lo == 1, 4, 5))))
        slot_a = jnp.where(pair == 0, 0, jnp.where(pair <= 2, 2, 3))
        return grp * N_PAIRS + pair, l0 != slot_a

    classes = each(classify, picked)
    c_iota = lax.broadcasted_iota(jnp.int32, (cnt_ref.shape[0], tt), 0)
    hit = each(lambda cl: c_iota == cl[0], classes)
    member = each(lambda hc: jnp.where(hc, 1.0, 0.0), hit)
    ui = lax.broadcasted_iota(jnp.int32, (tt, tt), 0)
    uj = lax.broadcasted_iota(jnp.int32, (tt, tt), 1)
    before = (ui < uj).astype(BF16)
    prefix = each(lambda mb: jnp.dot(mb.astype(BF16), before, preferred_element_type=F32), member)
    base = base_ref[:, 0:1]
    for u, sl in enumerate(subs):
        rank = jnp.sum(jnp.where(hit[u], prefix[u] + base, 0.0), axis=0, keepdims=True)
        route_ref[:, sl] = jnp.concatenate([classes[u][0], rank.astype(jnp.int32)], axis=0)
        base = base + jnp.sum(member[u], axis=1, keepdims=True)
    base_ref[...] = jnp.broadcast_to(base, base_ref.shape)
    cnt_ref[...] = jnp.broadcast_to(base, cnt_ref.shape).astype(jnp.int32)

    for sl, gt, cl in zip(subs, gate, classes):
        gpad = jnp.concatenate([gt[0], gt[1], jnp.where(cl[1], 1.0, 0.0), jnp.zeros((V7X_LANES - 3, tt), F32)], axis=0)
        gcol_ref[sl, :] = gpad.T


def _mixer_epilogue(act, wo, res, ln_g, ln_b, router_wt, router_bias):
    n_tok, d = res.shape
    k_in = act.shape[1]
    tt = EPI_TILE
    n_tiles = n_tok // tt
    full = lambda arr: pl.BlockSpec(arr.shape, lambda i: (0,) * arr.ndim)
    return pl.pallas_call(
        _epilogue_kernel,
        grid=(n_tiles,),
        in_specs=[pl.BlockSpec((tt, k_in), lambda i: (i, 0)), full(wo), pl.BlockSpec((tt, d), lambda i: (i, 0)),
                  full(ln_g), full(ln_b), full(router_wt), full(router_bias)],
        out_specs=[pl.BlockSpec((tt, d), lambda i: (i, 0)),
                   pl.BlockSpec((tt, d // 2), lambda i: (i, 0)),
                   pl.BlockSpec((2, tt), lambda i: (0, i)),
                   pl.BlockSpec((tt, V7X_LANES), lambda i: (i, 0)),
                   pl.BlockSpec((CLASS_ROWS, V7X_LANES), lambda i: (0, 0))],
        out_shape=[jax.ShapeDtypeStruct((n_tok, d), F32),
                   jax.ShapeDtypeStruct((n_tok, d // 2), jnp.uint32),
                   jax.ShapeDtypeStruct((2, n_tok), jnp.int32),
                   jax.ShapeDtypeStruct((n_tok, V7X_LANES), F32),
                   jax.ShapeDtypeStruct((CLASS_ROWS, V7X_LANES), jnp.int32)],
        scratch_shapes=[pltpu.VMEM((CLASS_ROWS, V7X_LANES), F32)],
        compiler_params=pltpu.CompilerParams(
            dimension_semantics=("arbitrary",),
            vmem_limit_bytes=_vmem_limit(2 * int(wo.size) * 2 + 2 * tt * (k_in * 2 + 2 * d * 4) + 16 * tt * d * 4)),
        name="mixer_epilogue",
    )(act, wo, res, ln_g, ln_b, router_wt, router_bias)


def _dispatch_kernel(dest_ref, h_ref, xs_in_ref, xs_ref, sem):
    del xs_in_ref
    tt = h_ref.shape[0]

    def row_copy(t):
        return pltpu.make_async_copy(h_ref.at[pl.ds(t, 1)], xs_ref.at[pl.ds(dest_ref[0, 0, t], 1)], sem)

    def issue(t0, carry):
        for u in range(DMA_ISSUE_UNROLL):
            row_copy(t0 * DMA_ISSUE_UNROLL + u).start()
        return carry

    lax.fori_loop(0, tt // DMA_ISSUE_UNROLL, issue, 0)
    pltpu.make_async_copy(h_ref, xs_ref.at[pl.ds(0, tt)], sem).wait()


def _moe_dispatch(h, dest_tiles, n_rows):
    n_tok, d = h.shape
    tt = dest_tiles.shape[2]
    n_tiles = n_tok // tt
    zeros = jnp.zeros((n_rows, d), h.dtype)
    return pl.pallas_call(
        _dispatch_kernel,
        grid=(n_tiles,),
        in_specs=[pl.BlockSpec((1, 1, tt), lambda i: (i, 0, 0), memory_space=pltpu.SMEM),
                  pl.BlockSpec((tt, d), lambda i: (i, 0)),
                  pl.BlockSpec(memory_space=pl.ANY)],
        out_specs=pl.BlockSpec(memory_space=pl.ANY),
        out_shape=jax.ShapeDtypeStruct((n_rows, d), h.dtype),
        scratch_shapes=[pltpu.SemaphoreType.DMA(())],
        input_output_aliases={2: 0},
        compiler_params=pltpu.CompilerParams(dimension_semantics=("arbitrary",),
                                             vmem_limit_bytes=_vmem_limit(2 * tt * d * 4)),
        name="moe_dispatch",
    )(dest_tiles, h, zeros)


def _ffn_kernel(bea_ref, beb_ref, nblk_ref, x_ref, wga_ref, wua_ref, wda_ref, wgb_ref, wub_ref, wdb_ref,
                y_ref, wg_bf, wu_bf, wd_bf):
    j = pl.program_id(0)
    active = j < nblk_ref[0]
    prev = jnp.maximum(j - 1, 0)
    slots = ((bea_ref, wga_ref, wua_ref, wda_ref), (beb_ref, wgb_ref, wub_ref, wdb_ref))

    for s, (be_ref, wg_ref, wu_ref, wd_ref) in enumerate(slots):
        @pl.when(active & ((j == 0) | (be_ref[j] != be_ref[prev])))
        def _():
            wg_bf[s] = wg_ref[0, 0].astype(BF16)
            wu_bf[s] = wu_ref[0, 0].astype(BF16)
            wd_bf[s] = wd_ref[0, 0].astype(BF16)

    @pl.when(active)
    def _():
        half = x_ref.shape[1]
        mm = lambda a, b: jnp.dot(a, b, preferred_element_type=F32)
        x_lo, x_hi = (v.astype(BF16) for v in _unpack_halves(x_ref[...]))
        gate = [mm(x_lo, wg_bf[s, :half, :]) + mm(x_hi, wg_bf[s, half:, :]) for s in range(2)]
        up = [mm(x_lo, wu_bf[s, :half, :]) + mm(x_hi, wu_bf[s, half:, :]) for s in range(2)]
        hid = [((g * jax.nn.sigmoid(g)) * u).astype(BF16) for g, u in zip(gate, up)]
        for s in range(2):
            y_ref[:, s * half:(s + 1) * half] = _pack_halves(mm(hid[s], wd_bf[s]))

    @pl.when(jnp.logical_not(active))
    def _():
        y_ref[...] = jnp.zeros_like(y_ref)


def _moe_ffn(xs, block_ea, block_eb, n_used, layer, wg, wu, wd):
    n_rows, half = xs.shape
    d = 2 * half
    blk = MOE_BLOCK
    n_blocks = n_rows // blk
    de = wg.shape[3]
    expert_a = lambda j, ea, eb, nb: (layer, ea[j], 0, 0)
    expert_b = lambda j, ea, eb, nb: (layer, eb[j], 0, 0)
    rows = lambda j, ea, eb, nb: (j, 0)
    grid_spec = pltpu.PrefetchScalarGridSpec(
        num_scalar_prefetch=3,
        grid=(n_blocks,),
        in_specs=[pl.BlockSpec((blk, half), rows),
                  pl.BlockSpec((1, 1, d, de), expert_a), pl.BlockSpec((1, 1, d, de), expert_a),
                  pl.BlockSpec((1, 1, de, d), expert_a),
                  pl.BlockSpec((1, 1, d, de), expert_b), pl.BlockSpec((1, 1, d, de), expert_b),
                  pl.BlockSpec((1, 1, de, d), expert_b)],
        out_specs=pl.BlockSpec((blk, d), rows),
        scratch_shapes=[pltpu.VMEM((2, d, de), BF16), pltpu.VMEM((2, d, de), BF16), pltpu.VMEM((2, de, d), BF16)],
    )
    return pl.pallas_call(
        _ffn_kernel,
        grid_spec=grid_spec,
        out_shape=jax.ShapeDtypeStruct((n_rows, d), jnp.uint32),
        compiler_params=pltpu.CompilerParams(
            dimension_semantics=("arbitrary",),
            vmem_limit_bytes=_vmem_limit(2 * 3 * d * de * (2 * 4 + 2) + 6 * blk * d * 4 + 8 * blk * de * 4)),
        name="moe_ffn",
    )(block_ea, block_eb, n_used, xs, wg, wu, wd, wg, wu, wd)


def _combine_kernel(dest_ref, dest_next_ref, ys_ref, gcol_ref, res_ref, lng_ref, lnb_ref, o_ref, buf_ref, sem):
    i = pl.program_id(0)
    n = pl.num_programs(0)
    tt = res_ref.shape[0]
    cur = i % 2

    def gather(idx_ref, buf):
        def row_copy(t):
            return pltpu.make_async_copy(ys_ref.at[pl.ds(idx_ref[0, 0, t], 1)], buf_ref.at[buf, pl.ds(t, 1)],
                                         sem.at[buf])

        def issue(t0, carry):
            for u in range(DMA_ISSUE_UNROLL):
                row_copy(t0 * DMA_ISSUE_UNROLL + u).start()
            return carry

        lax.fori_loop(0, tt // DMA_ISSUE_UNROLL, issue, 0)

    @pl.when(i == 0)
    def _():
        gather(dest_ref, cur)

    @pl.when(i + 1 < n)
    def _():
        gather(dest_next_ref, 1 - cur)

    pltpu.make_async_copy(ys_ref.at[pl.ds(0, tt)], buf_ref.at[cur], sem.at[cur]).wait()

    gcol = gcol_ref[...]
    half = buf_ref.shape[2] // 2
    lo_a, hi_a = _unpack_halves(buf_ref[cur, :, :half])
    lo_b, hi_b = _unpack_halves(buf_ref[cur, :, half:])
    swap = gcol[:, 2:3] > 0.5
    g_a = jnp.where(swap, gcol[:, 1:2], gcol[:, 0:1])
    g_b = jnp.where(swap, gcol[:, 0:1], gcol[:, 1:2])
    ffn = jnp.concatenate([lo_a * g_a + lo_b * g_b, hi_a * g_a + hi_b * g_b], axis=1)
    o_ref[...] = _layer_norm_rows(ALPHA * res_ref[...] + ffn, lng_ref[...], lnb_ref[...])


def _moe_combine(ys, dest_tiles, gcol, res, ln_g, ln_b):
    n_tok, d = res.shape
    tt = dest_tiles.shape[2]
    n_tiles = n_tok // tt
    full = lambda arr: pl.BlockSpec(arr.shape, lambda i: (0,) * arr.ndim)
    return pl.pallas_call(
        _combine_kernel,
        grid=(n_tiles,),
        in_specs=[pl.BlockSpec((1, 1, tt), lambda i: (i, 0, 0), memory_space=pltpu.SMEM),
                  pl.BlockSpec((1, 1, tt), lambda i: (jnp.minimum(i + 1, n_tiles - 1), 0, 0),
                               memory_space=pltpu.SMEM),
                  pl.BlockSpec(memory_space=pl.ANY),
                  pl.BlockSpec((tt, V7X_LANES), lambda i: (i, 0)),
                  pl.BlockSpec((tt, d), lambda i: (i, 0)),
                  full(ln_g), full(ln_b)],
        out_specs=pl.BlockSpec((tt, d), lambda i: (i, 0)),
        out_shape=jax.ShapeDtypeStruct((n_tok, d), F32),
        scratch_shapes=[pltpu.VMEM((2, tt, ys.shape[1]), ys.dtype), pltpu.SemaphoreType.DMA((2,))],
        compiler_params=pltpu.CompilerParams(
            dimension_semantics=("arbitrary",),
            vmem_limit_bytes=_vmem_limit(4 * tt * d * 2 + 2 * 3 * tt * d * 4)),
        name="moe_combine",
    )(dest_tiles, dest_tiles, ys, gcol, res, ln_g, ln_b)


def _moe_layer(h, h_packed, route, counts, gcol, layer, wg, wu, wd, ln_g, ln_b):
    n_tok, d = h.shape
    blk = MOE_BLOCK
    n_rows = (n_tok + N_CLASSES * (blk - 1) + blk - 1) // blk * blk
    n_blocks = n_rows // blk
    cls, rank = route[0], route[1]
    counts = counts[:N_CLASSES, 0]
    padded = (counts + blk - 1) // blk * blk
    pend = jnp.cumsum(padded)
    pstart = pend - padded
    onehot = (cls[:, None] == jnp.arange(N_CLASSES, dtype=jnp.int32)).astype(jnp.int32)
    dest = rank + jnp.sum(onehot * pstart, axis=-1)
    tiles = lambda tt: dest.reshape(n_tok // tt, 1, tt)
    blk_start = jnp.arange(n_blocks, dtype=jnp.int32) * blk
    block_cls = jnp.minimum(jnp.sum((blk_start[:, None] >= pend[None, :]).astype(jnp.int32), axis=1), N_CLASSES - 1)
    n_used = (pend[-1:] // blk).astype(jnp.int32)
    blk_idx = jnp.arange(n_blocks, dtype=jnp.int32)
    block_cls = jnp.where(blk_idx < n_used[0], block_cls, block_cls[jnp.maximum(n_used[0] - 1, 0)])
    class_ids = jnp.arange(N_CLASSES, dtype=jnp.int32)
    first_expert = (class_ids // N_PAIRS) * EXPERTS_PER_GROUP
    expert_a = first_expert + jnp.array(_PAIR_SLOT_A, jnp.int32)[class_ids % N_PAIRS]
    expert_b = first_expert + jnp.array(_PAIR_SLOT_B, jnp.int32)[class_ids % N_PAIRS]
    xs = _moe_dispatch(h_packed, tiles(DISPATCH_TILE), n_rows)
    ys = _moe_ffn(xs, expert_a[block_cls].astype(jnp.int32), expert_b[block_cls].astype(jnp.int32), n_used,
                  layer, wg, wu, wd)
    return _moe_combine(ys, tiles(COMBINE_TILE), gcol, h, ln_g, ln_b)


def _fox_proj_kernel(x_ref, wq_ref, wk_ref, wv_ref, wf_ref, bf_ref, q_ref, k_ref, v_ref, c_ref, carry_ref):
    t = pl.program_id(1)

    @pl.when(t == 0)
    def _():
        carry_ref[...] = jnp.zeros_like(carry_ref)

    tt = PROJ_SUBTILE
    subs = [slice(u * tt, (u + 1) * tt) for u in range(x_ref.shape[0] // tt)]
    each = lambda fn, *lists: [fn(*args) for args in zip(*lists)]
    parts = each(lambda sl: _split3(x_ref[sl, :]), subs)
    wh, wm, wl = _split3(wf_ref[...])
    mm = lambda a, b: jnp.dot(a, b, preferred_element_type=F32)
    logit = each(lambda p: (mm(p[0], wh) + (mm(p[0], wm) + mm(p[1], wh))
                            + (mm(p[0], wl) + mm(p[1], wm) + mm(p[2], wh))) + bf_ref[...], parts)
    for sl, p in zip(subs, parts):
        q_ref[sl, :] = (mm(p[0], wq_ref[...]) * LOG2E).astype(q_ref.dtype)
    for sl, p in zip(subs, parts):
        k_ref[sl, :] = mm(p[0], wk_ref[...]).astype(k_ref.dtype)
    for sl, p in zip(subs, parts):
        v_ref[sl, :] = mm(p[0], wv_ref[...]).astype(v_ref.dtype)
    ti = lax.broadcasted_iota(jnp.int32, (tt, tt), 0)
    tj = lax.broadcasted_iota(jnp.int32, (tt, tt), 1)
    tril = (tj <= ti).astype(BF16)
    local = each(lambda lg: sum(mm(tril, part) for part in _split3(jax.nn.log_sigmoid(lg))), logit)
    carry = carry_ref[0:1, :]
    for sl, cs in zip(subs, local):
        c = cs + carry
        c_ref[sl, :] = c
        carry = c[tt - 1:tt, :]
    carry_ref[...] = jnp.broadcast_to(carry, carry_ref.shape)


def _fox_proj(x2, batch, seq_len, wq, wk, wv, wf, b_f):
    n_tok, d = x2.shape
    tt = PROJ_TILE
    nt = seq_len // tt
    tile = pl.BlockSpec((tt, d), lambda b, t: (b * nt + t, 0))
    full = lambda arr: pl.BlockSpec(arr.shape, lambda b, t: (0,) * arr.ndim)
    return pl.pallas_call(
        _fox_proj_kernel,
        grid=(batch, nt),
        in_specs=[tile, full(wq), full(wk), full(wv), full(wf), full(b_f)],
        out_specs=[tile, tile, tile, pl.BlockSpec((tt, N_HEADS), lambda b, t: (b * nt + t, 0))],
        out_shape=[jax.ShapeDtypeStruct((n_tok, d), BF16)] * 3 + [jax.ShapeDtypeStruct((n_tok, N_HEADS), F32)],
        scratch_shapes=[pltpu.VMEM((8, N_HEADS), F32)],
        compiler_params=pltpu.CompilerParams(
            dimension_semantics=("parallel", "arbitrary"),
            vmem_limit_bytes=_vmem_limit(2 * 3 * d * d * 2 + 2 * tt * d * (4 + 3 * 2) + 8 * tt * d * 4)),
        name="fox_proj",
    )(x2, wq, wk, wv, wf, b_f)


def _fox_attn_kernel(q_ref, k_ref, v_ref, ct_ref, o_ref):
    hp = pl.program_id(1)
    seq_len = q_ref.shape[0]
    tq = ATTN_TILE
    tk = tq
    nq = seq_len // tq
    N = HEAD_DIM
    nt = lambda a, b: lax.dot_general(a, b, (((1,), (1,)), ((), ())), preferred_element_type=F32)
    mm = lambda a, b: jnp.dot(a, b, preferred_element_type=F32)

    lane = lax.broadcasted_iota(jnp.int32, (tq, 2 * N), 1)
    vrow = lax.broadcasted_iota(jnp.int32, (2 * N, tk), 0)
    orow = lax.broadcasted_iota(jnp.int32, (2 * N, tq), 0)
    kpos = lax.broadcasted_iota(jnp.int32, (tk, tq), 0)
    qpos = lax.broadcasted_iota(jnp.int32, (tk, tq), 1)
    causal = kpos <= qpos
    ident_v = (lax.broadcasted_iota(jnp.int32, (2 * N, 2 * N), 0)
               == lax.broadcasted_iota(jnp.int32, (2 * N, 2 * N), 1)).astype(BF16)

    zero_q = jnp.zeros((tq, 2 * N), BF16)

    def head_pair(hl):
        lanes = slice(hl * 2 * N, (hl + 1) * 2 * N)
        first_head = 2 * (hp * ATTN_PAIRS_PER_STEP + hl)
        c_parts = [[part.astype(F32) for part in _split3(ct_ref[0, pl.ds(first_head + u, 1), :] * LOG2E)]
                   for u in range(2)]
        ones = jnp.ones((3, tq), F32)
        pad = jnp.zeros((2 * N - 6, tq), F32)
        m = [[None, None] for _ in range(nq)]
        l = [[None, None] for _ in range(nq)]
        acc = [None] * nq
        keys, queries, values = {}, {}, {}

        def key_side(j):
            if j not in keys:
                ks = slice(j * tk, (j + 1) * tk)
                k_j = k_ref[ks, lanes]
                extra = [jnp.concatenate([-part[:, ks] for part in c_parts[u]] + [ones, pad], axis=0).T.astype(BF16)
                         for u in range(2)]
                keys[j] = [jnp.concatenate([k_j, extra[u]], axis=1) for u in range(2)]
            return keys[j]

        def query_side(qi):
            if qi not in queries:
                qs = slice(qi * tq, (qi + 1) * tq)
                q = q_ref[qs, lanes]
                extra = [jnp.concatenate([ones] + [part[:, qs] for part in c_parts[u]] + [pad], axis=0).T.astype(BF16)
                         for u in range(2)]
                queries[qi] = [jnp.concatenate([jnp.where((lane < N) == (u == 0), q, zero_q), extra[u]], axis=1)
                               for u in range(2)]
            return queries[qi]

        def scores(j, qi):
            k_aug, q_aug = key_side(j), query_side(qi)
            t = [nt(k_aug[u], q_aug[u]) for u in range(2)]
            return [jnp.where(causal, tu, -jnp.inf) for tu in t] if qi == j else t

        def absorb(j, qi, t):
            if j not in values:
                v_t = nt(ident_v, v_ref[j * tk:(j + 1) * tk, lanes]).astype(BF16)
                values[j] = [jnp.where(vrow < N, v_t, jnp.zeros_like(v_t)),
                             jnp.where(vrow < N, jnp.zeros_like(v_t), v_t)]
            v_heads = values[j]
            rmax = [jnp.max(t[u], axis=0, keepdims=True) for u in range(2)]
            m_new = rmax if j == 0 else [jnp.maximum(m[qi][u], rmax[u]) for u in range(2)]
            p = [jnp.exp2(t[u] - m_new[u]) for u in range(2)]
            psum = [jnp.sum(p[u], axis=0, keepdims=True) for u in range(2)]
            pv = mm(v_heads[0], p[0].astype(BF16)) + mm(v_heads[1], p[1].astype(BF16))
            if j == 0:
                acc[qi] = pv
                l[qi] = psum
            else:
                alpha = [jnp.exp2(m[qi][u] - m_new[u]) for u in range(2)]
                acc[qi] = acc[qi] * jnp.where(orow < N, alpha[0], alpha[1]) + pv
                l[qi] = [alpha[u] * l[qi][u] + psum[u] for u in range(2)]
            m[qi] = m_new
            if j == qi:
                o_t = acc[qi] / jnp.where(orow < N, l[qi][0], l[qi][1])
                o_ref[qi * tq:(qi + 1) * tq, lanes] = o_t.T.astype(o_ref.dtype)

        return scores, absorb

    pairs = [head_pair(hl) for hl in range(ATTN_PAIRS_PER_STEP)]
    items = [(hl, j, qi) for j in range(nq) for qi in range(j, nq) for hl in range(ATTN_PAIRS_PER_STEP)]
    score_of = lambda item: pairs[item[0]][0](*item[1:])
    queue = [score_of(item) for item in items[:ATTN_LOOKAHEAD]]
    for n, (hl, j, qi) in enumerate(items):
        t_cur = queue.pop(0)
        if n + ATTN_LOOKAHEAD < len(items):
            queue.append(score_of(items[n + ATTN_LOOKAHEAD]))
        pairs[hl][1](j, qi, t_cur)


def _fox_attn(q, k, v, c_t, batch, seq_len):
    n_tok, d = q.shape
    width = ATTN_PAIRS_PER_STEP * 2 * HEAD_DIM
    seq = pl.BlockSpec((seq_len, width), lambda b, hp: (b, hp))
    return pl.pallas_call(
        _fox_attn_kernel,
        grid=(batch, d // width),
        in_specs=[seq, seq, seq, pl.BlockSpec((1, N_HEADS, seq_len), lambda b, hp: (b, 0, 0))],
        out_specs=seq,
        out_shape=jax.ShapeDtypeStruct((n_tok, d), BF16),
        compiler_params=pltpu.CompilerParams(
            dimension_semantics=("parallel", "arbitrary"),
            vmem_limit_bytes=_vmem_limit(2 * 4 * seq_len * width * 2 + 128 * ATTN_TILE * ATTN_TILE * 4)),
        name="fox_attn",
    )(q, k, v, c_t)


def kernel(x, rw_mix, rw_wr, rw_wk, rw_wv, rw_wo, rw_w0, rw_w1, rw_w2, rw_a0, rw_a1, rw_a2, rw_g1, rw_g2,
           rw_kk, rw_ka, rw_rk, rw_gn_g, rw_gn_b, fx_w_in, fx_b_f, fx_wo, router_w, router_bias,
           moe_w_gate, moe_w_up, moe_w_down, ln_g, ln_b):
    batch, seq_len, d = x.shape
    n_tok = batch * seq_len
    bf = lambda w: w.astype(BF16)
    row = lambda w: w.reshape(1, -1)
    router_wt = router_w.T
    router_b = router_bias.reshape(N_EXPERTS, 1)
    h = x.reshape(n_tok, d)

    for i in range(DEPTH):
        j = i // 2
        if i % 2 == 0:
            r, lw, k, v, a, g = _rwkv_proj(
                h, seq_len, rw_mix[j], bf(rw_wr[j]), bf(rw_wk[j]), bf(rw_wv[j]), bf(rw_w1[j]), bf(rw_w2[j]),
                bf(rw_a1[j]), bf(rw_a2[j]), bf(rw_g1[j]), bf(rw_g2[j]), row(rw_w0[j]), row(rw_a0[j]))
            act = _rwkv_recur(r, lw, k, v, a, g, batch, seq_len, row(rw_kk[j]), row(rw_ka[j]), row(rw_rk[j]),
                              row(rw_gn_g[j]), row(rw_gn_b[j]))
            wo = bf(rw_wo[j])
        else:
            w_in = fx_w_in[j]
            scale = HEAD_DIM ** -0.5
            q, k, v, c = _fox_proj(h, batch, seq_len, bf(w_in[:, :d] * scale), bf(w_in[:, d:2 * d]),
                                   bf(w_in[:, 2 * d:3 * d]), w_in[:, 3 * d:], row(fx_b_f[j]))
            c_t = c.reshape(batch, seq_len, N_HEADS).transpose(0, 2, 1)
            act = _fox_attn(q, k, v, c_t, batch, seq_len)
            wo = bf(fx_wo[j])
        h, h_packed, route, gcol, counts = _mixer_epilogue(act, wo, h, row(ln_g[i, 0]), row(ln_b[i, 0]),
                                                           router_wt, router_b)
        h = _moe_layer(h, h_packed, route, counts, gcol, i, moe_w_gate, moe_w_up, moe_w_down,
                       row(ln_g[i, 1]), row(ln_b[i, 1]))
    return h.reshape(batch, seq_len, d)
```

```python
import functools
import math

import jax
import jax.numpy as jnp
from jax import lax
from jax.experimental import pallas as pl
from jax.experimental.pallas import tpu as pltpu

D_MODEL = 1024
HEAD_DIM = 64
N_HEADS = D_MODEL // HEAD_DIM
N_EXPERTS = 16
N_GROUPS = 4
EXPERTS_PER_GROUP = N_EXPERTS // N_GROUPS
D_EXPERT = 512
N_PAIRS = 6
N_CLASSES = N_GROUPS * N_PAIRS
CLASS_ROWS = 32
_PAIR_SLOT_A = (0, 2, 2, 3, 3, 3)
_PAIR_SLOT_B = (1, 1, 0, 0, 1, 2)
GN_EPS = 64e-5
LN_EPS = 1e-5
DEPTH = 2
ALPHA = (2 * DEPTH) ** 0.25
LOG2E = math.log2(math.e)

V7X_LANES = 128
V7X_VMEM_BYTES = 64 * 2 ** 20

V7X_MXU_DIM = 256

RWKV_CHUNK = 64
RWKV_CHUNKS_PER_STEP = 4
RWKV_HEADS_PER_TILE = V7X_MXU_DIM // HEAD_DIM
PROJ_TILE = 512
PROJ_SUBTILE = 256
EPI_TILE = 1024
EPI_SUBTILE = 256
ATTN_TILE = 256
ATTN_PAIRS_PER_STEP = 1
ATTN_LOOKAHEAD = 2
MOE_BLOCK = 256
DISPATCH_TILE = 2048
COMBINE_TILE = 512
DMA_ISSUE_UNROLL = 8

F32 = jnp.float32
BF16 = jnp.bfloat16


def _vmem_limit(n_bytes):
    return int(min(n_bytes + 16 * 2 ** 20, V7X_VMEM_BYTES - 8 * 2 ** 20))


def _split3(x):
    hi = x.astype(BF16)
    r1 = x - hi.astype(F32)
    mid = r1.astype(BF16)
    lo = (r1 - mid.astype(F32)).astype(BF16)
    return hi, mid, lo


def _pack_halves(x):
    half = x.shape[1] // 2
    bits = lambda v: lax.bitcast_convert_type(v.astype(BF16).astype(F32), jnp.uint32)
    return (bits(x[:, :half]) >> 16) | (bits(x[:, half:]) & jnp.uint32(0xFFFF0000))


def _unpack_halves(w):
    lo = lax.bitcast_convert_type(w << 16, F32)
    hi = lax.bitcast_convert_type(w & jnp.uint32(0xFFFF0000), F32)
    return lo, hi


def _layer_norm_rows(x, g, b):
    mu = jnp.mean(x, axis=-1, keepdims=True)
    xc = x - mu
    var = jnp.mean(xc * xc, axis=-1, keepdims=True)
    return xc * lax.rsqrt(var + LN_EPS) * g + b


def _rwkv_proj_kernel(x_ref, xp_ref, mix_ref, wr_ref, wk_ref, wv_ref, w1_ref, w2_ref, a1_ref, a2_ref,
                      g1_ref, g2_ref, w0_ref, a0_ref,
                      r_ref, lw_ref, k_ref, v_ref, a_ref, g_ref, *, tiles_per_seq):
    i = pl.program_id(0)
    x = x_ref[...]
    tt = x.shape[0]
    first = (i % tiles_per_seq) == 0
    prev_row = jnp.where(first, 0.0, xp_ref[7:8, :])
    row = lax.broadcasted_iota(jnp.int32, (tt, 1), 0)
    xprev = jnp.where(row == 0, prev_row, pltpu.roll(x, 1, axis=0))
    xx = xprev - x
    mix = mix_ref[...]
    subs = [slice(u * PROJ_SUBTILE, (u + 1) * PROJ_SUBTILE) for u in range(tt // PROJ_SUBTILE)]
    each = lambda fn, *lists: [fn(*args) for args in zip(*lists)]
    mixed = lambda j: each(lambda sl: (x[sl] + xx[sl] * mix[j:j + 1, :]).astype(BF16), subs)
    xr, xw, xk, xv, xa, xg = (mixed(j) for j in range(6))
    mm = lambda a, w_ref: jnp.dot(a, w_ref[...], preferred_element_type=F32)
    w_mid = each(lambda a: mm(a, w1_ref), xw)
    a_mid = each(lambda a: mm(a, a1_ref), xa)
    g_mid = each(lambda a: mm(a, g1_ref), xg)
    for sl, a in zip(subs, xr):
        r_ref[sl, :] = mm(a, wr_ref).astype(r_ref.dtype)
    for sl, a in zip(subs, xk):
        k_ref[sl, :] = mm(a, wk_ref).astype(k_ref.dtype)
    for sl, a in zip(subs, xv):
        v_ref[sl, :] = mm(a, wv_ref).astype(v_ref.dtype)
    z = each(lambda t: w0_ref[...] + mm(jnp.tanh(t).astype(BF16), w2_ref), w_mid)
    for sl, t in zip(subs, a_mid):
        a_ref[sl, :] = jax.nn.sigmoid(a0_ref[...] + mm(t.astype(BF16), a2_ref)).astype(a_ref.dtype)
    for sl, t in zip(subs, g_mid):
        g_ref[sl, :] = mm(jax.nn.sigmoid(t).astype(BF16), g2_ref).astype(g_ref.dtype)
    for sl, zs in zip(subs, z):
        lw_ref[sl, :] = -jnp.exp(-jax.nn.softplus(-zs) - 0.5)


def _rwkv_proj(x2, seq_len, mix, wr, wk, wv, w1, w2, a1, a2, g1, g2, w0, a0):
    n_tok, d = x2.shape
    tt = PROJ_TILE
    n_tiles = n_tok // tt
    tiles_per_seq = seq_len // tt
    tile = pl.BlockSpec((tt, d), lambda i: (i, 0))
    prev = pl.BlockSpec((8, d), lambda i: (jnp.maximum(i * (tt // 8) - 1, 0), 0))
    full = lambda arr: pl.BlockSpec(arr.shape, lambda i: (0,) * arr.ndim)
    weights = (mix, wr, wk, wv, w1, w2, a1, a2, g1, g2, w0, a0)
    out_dtypes = (BF16, F32, BF16, BF16, BF16, BF16)
    w_bytes = sum(int(w.size) * w.dtype.itemsize for w in weights)
    return pl.pallas_call(
        functools.partial(_rwkv_proj_kernel, tiles_per_seq=tiles_per_seq),
        grid=(n_tiles,),
        in_specs=[tile, prev] + [full(w) for w in weights],
        out_specs=[tile] * 6,
        out_shape=[jax.ShapeDtypeStruct((n_tok, d), dt) for dt in out_dtypes],
        compiler_params=pltpu.CompilerParams(
            dimension_semantics=("parallel",),
            vmem_limit_bytes=_vmem_limit(2 * w_bytes + 2 * tt * d * (4 + 4 + 5 * 2) + 8 * tt * d * 4)),
        name="rwkv_proj",
    )(x2, x2, *weights)


def _rwkv_recur_kernel(r_ref, lw_ref, k_ref, v_ref, a_ref, g_ref, kk_ref, ka_ref, rk_ref, gng_ref, gnb_ref,
                       o_ref, s_ref):
    c = pl.program_id(1)
    C = RWKV_CHUNK
    N = HEAD_DIM

    @pl.when(c == 0)
    def _():
        s_ref[...] = jnp.zeros_like(s_ref)

    nt = lambda p, q: lax.dot_general(p, q, (((1,), (1,)), ((), ())), preferred_element_type=F32)
    mm = lambda p, q: jnp.dot(p, q, preferred_element_type=F32)
    each = lambda fn, *lists: [fn(*args) for args in zip(*lists)]
    bf = lambda x: x.astype(BF16)

    ti = lax.broadcasted_iota(jnp.int32, (C, C), 0)
    tj = lax.broadcasted_iota(jnp.int32, (C, C), 1)
    tril = (tj <= ti).astype(BF16)

    def chunk_terms(rows):
        lw = lw_ref[rows, :]
        cum = sum(mm(tril, part) for part in _split3(lw))
        rho = cum[C // 2 - 1:C // 2, :]
        last = cum[C - 1:C, :]
        r = r_ref[rows, :].astype(F32)
        k = k_ref[rows, :].astype(F32)
        a = a_ref[rows, :].astype(F32)
        k_mod = k * (1.0 + (a - 1.0) * ka_ref[...])
        return dict(e_q=jnp.exp(cum - rho), e_qx=jnp.exp(cum - lw - rho), e_k=jnp.exp(rho - cum),
                    e_end=jnp.exp(last - cum), e_rho=jnp.exp(rho), d_end=jnp.exp(last),
                    r=r, a=a, v=v_ref[rows, :].astype(F32), kk_raw=k * kk_ref[...], k_mod=k_mod,
                    rkk=r * k_mod * rk_ref[...])

    n_ch = r_ref.shape[0] // C
    terms = [chunk_terms(slice(ci * C, (ci + 1) * C)) for ci in range(n_ch)]

    G = RWKV_HEADS_PER_TILE
    R = G * C
    GW = G * N
    er = lax.broadcasted_iota(jnp.int32, (R, GW), 0)
    ec = lax.broadcasted_iota(jnp.int32, (R, GW), 1)
    blk = (er // C) == (ec // N)
    strict = (ec % C) < (er % C)
    incl = (ec % C) <= (er % C)
    eye = (er == ec).astype(F32)
    ones_blk = blk.astype(BF16)

    def expand(x):
        return jnp.where(blk, jnp.concatenate([x] * G, axis=0), 0.0).astype(BF16)

    expand_t = lambda x: bf(jnp.where(blk, jnp.concatenate([x] * G, axis=0), 0.0).T)

    def head_sums(xs):
        parts = _split3(jnp.concatenate(xs, axis=0))[:2]
        tot = mm(jnp.concatenate(parts, axis=0), ones_blk)
        n = len(xs) * C
        tot = tot[:n] + tot[n:]
        return [tot[u * C:(u + 1) * C] for u in range(len(xs))]

    n_grp = N_HEADS // G
    groups = [slice(gi * GW, (gi + 1) * GW) for gi in range(n_grp)]
    units = [(tm, sl) for tm in terms for sl in groups]

    def independent(us, out):
        u_tm = [tm for tm, _ in us]
        u_sl = [sl for _, sl in us]
        pre = head_sums([tm["kk_raw"][:, sl] * tm["kk_raw"][:, sl] for tm, sl in us] + [tm["rkk"][:, sl] for tm, sl in us])
        kk_ss, out["rkk_sum"] = pre[:len(us)], pre[len(us):]
        yield
        kk_n = each(lambda tm, sl, ss: tm["kk_raw"][:, sl] * lax.rsqrt(jnp.maximum(ss, 1e-24)), u_tm, u_sl, kk_ss)
        b_n = each(lambda tm, sl, kk_g: kk_g * tm["a"][:, sl], u_tm, u_sl, kk_n)
        kk_q = each(lambda tm, sl, kk_g: kk_g * tm["e_qx"][:, sl], u_tm, u_sl, kk_n)
        r_q = each(lambda tm, sl: tm["r"][:, sl] * tm["e_q"][:, sl], u_tm, u_sl)
        q2 = each(lambda x, y: jnp.concatenate([expand(x), expand(y)], axis=0), kk_q, r_q)
        a_k = each(lambda tm, sl, q: nt(q, expand(tm["k_mod"][:, sl] * tm["e_k"][:, sl])), u_tm, u_sl, q2)
        yield
        a_b = each(lambda tm, sl, q, b_g: nt(q, expand(b_g * tm["e_k"][:, sl])), u_tm, u_sl, q2, b_n)
        yield
        a_kk = each(lambda x: jnp.where(strict, x[:R], 0.0), a_k)
        a_rk = each(lambda x: jnp.where(incl, x[R:], 0.0), a_k)
        a_kb = each(lambda x: jnp.where(strict, x[:R], 0.0), a_b)
        out["a_rb"] = each(lambda x: bf(jnp.where(incl, x[R:], 0.0)), a_b)
        t_inv = each(lambda x: eye - x, a_kb)
        p = each(lambda x: mm(bf(-x), bf(-x)), a_kb)
        yield
        for _ in range(int(math.log2(C)) - 2):
            both = each(lambda pg, tg: mm(bf(pg), jnp.concatenate([bf(pg), bf(tg)], axis=1)), p, t_inv)
            p = each(lambda x: x[:, :R], both)
            t_inv = each(lambda tg, x: tg + x[:, R:], t_inv, both)
            yield
        out["t_inv"] = each(lambda pg, tg: bf(tg + mm(bf(pg), bf(tg))), p, t_inv)
        yield
        out["kd_t"] = each(lambda tm, sl: expand_t(tm["k_mod"][:, sl] * tm["e_end"][:, sl]), u_tm, u_sl)
        out["bd_t"] = each(lambda tm, sl, b_g: expand_t(b_g * tm["e_end"][:, sl]), u_tm, u_sl, b_n)
        out["v_e"] = each(lambda tm, sl: expand(tm["v"][:, sl]), u_tm, u_sl)
        out["av"] = each(lambda x, y, ve: mm(bf(jnp.concatenate([x, y], axis=0)), ve), a_kk, a_rk, out["v_e"])
        out["q2_abs"] = each(lambda tm, sl, x, y: jnp.concatenate([expand(x * tm["e_rho"][:, sl]),
                                                                   expand(y * tm["e_rho"][:, sl])], axis=0),
                             u_tm, u_sl, kk_q, r_q)
        yield

    state = {"st": [s_ref[gi] for gi in range(n_grp)]}
    ys = [None] * len(units)

    def dependent(ci, res, lo):
        span = slice(lo, lo + n_grp)
        qs = each(lambda q, s: mm(q, bf(s)), res["q2_abs"][span], state["st"])
        yield
        sa_e = each(lambda tg, q, x: bf(mm(tg, bf(q[:R] + x[:R]))), res["t_inv"][span], qs, res["av"][span])
        yield
        y_e = each(lambda q, x, arb, sa: q[R:] + x[R:] - mm(arb, sa), qs, res["av"][span], res["a_rb"][span], sa_e)
        ys[ci * n_grp:(ci + 1) * n_grp] = each(lambda x: sum(x[u * C:(u + 1) * C] for u in range(G)), y_e)
        yield
        upd = each(lambda kt, bt, ve, sa: mm(jnp.concatenate([kt, -bt], axis=1), jnp.concatenate([ve, sa], axis=0)),
                   res["kd_t"][span], res["bd_t"][span], res["v_e"][span], sa_e)
        d_col = [jnp.broadcast_to(terms[ci]["d_end"][:, sl], (GW, GW)).T for sl in groups]
        state["st"] = each(lambda s, dc, up: s * dc + up, state["st"], d_col, upd)
        yield

    half = (n_ch // 2) * n_grp
    first, second = {}, {}
    for _ in independent(units[:half], first):
        pass
    chain = [stage for ci in range(n_ch // 2) for stage in [dependent(ci, first, ci * n_grp)]]
    pending = iter(())
    todo = list(chain)

    def advance():
        nonlocal pending
        while True:
            try:
                next(pending)
                return True
            except StopIteration:
                if not todo:
                    return False
                pending = todo.pop(0)

    for _ in independent(units[half:], second):
        advance()
    while advance():
        pass
    for ci in range(n_ch // 2, n_ch):
        for _ in dependent(ci, second, (ci - n_ch // 2) * n_grp):
            pass
    for gi in range(n_grp):
        s_ref[gi] = state["st"][gi]
    rkk_sum = first["rkk_sum"] + second["rkk_sum"]

    inv_n = 1.0 / N
    ycs = [y - mu * inv_n for y, mu in zip(ys, head_sums(ys))]
    sqs = head_sums([yc * yc for yc in ycs])
    for ui, (tm, sl) in enumerate(units):
        rows = slice((ui // n_grp) * C, (ui // n_grp + 1) * C)
        yn = ycs[ui] * lax.rsqrt(sqs[ui] * inv_n + GN_EPS) * gng_ref[:, sl] + gnb_ref[:, sl]
        o_ref[rows, sl] = ((yn + rkk_sum[ui] * tm["v"][:, sl]) * g_ref[rows, sl]).astype(o_ref.dtype)


def _rwkv_recur(r, lw, k, v, a, g, batch, seq_len, k_k, k_a, r_k, gn_g, gn_b):
    n_tok, d = r.shape
    C = RWKV_CHUNK * RWKV_CHUNKS_PER_STEP
    nc = seq_len // C
    tile = pl.BlockSpec((C, d), lambda b, c: (b * nc + c, 0))
    vec = pl.BlockSpec((1, d), lambda b, c: (0, 0))
    return pl.pallas_call(
        _rwkv_recur_kernel,
        grid=(batch, nc),
        in_specs=[tile] * 6 + [vec] * 5,
        out_specs=tile,
        out_shape=jax.ShapeDtypeStruct((n_tok, d), BF16),
        scratch_shapes=[pltpu.VMEM((N_HEADS // RWKV_HEADS_PER_TILE, V7X_MXU_DIM, V7X_MXU_DIM), F32)],
        compiler_params=pltpu.CompilerParams(
            dimension_semantics=("parallel", "arbitrary"),
            vmem_limit_bytes=_vmem_limit(2 * 7 * C * d * 4 + 32 * C * d * 4 + 64 * V7X_MXU_DIM ** 2 * 4)),
        name="rwkv_recur",
    )(r, lw, k, v, a, g, k_k, k_a, r_k, gn_g, gn_b)


def _rank_among(vals, i):
    cnt = 0
    for j, vj in enumerate(vals):
        if j == i:
            continue
        before = (vj >= vals[i]) if j < i else (vj > vals[i])
        cnt = cnt + before.astype(jnp.int32)
    return cnt


def _pick(ranks, vals, want):
    out = vals[0]
    for rk, vl in zip(ranks[1:], vals[1:]):
        out = jnp.where(rk == want, vl, out)
    return out


def _epilogue_kernel(act_ref, wo_ref, res_ref, lng_ref, lnb_ref, rwt_ref, rb_ref,
                     h_ref, hp_ref, route_ref, cnt_ref, base_ref):
    i = pl.program_id(0)

    @pl.when(i == 0)
    def _():
        base_ref[...] = jnp.zeros_like(base_ref)

    tt = EPI_SUBTILE
    subs = [slice(u * tt, (u + 1) * tt) for u in range(act_ref.shape[0] // tt)]
    each = lambda fn, *lists: [fn(*args) for args in zip(*lists)]
    nt = lambda a, b: lax.dot_general(a, b, (((1,), (1,)), ((), ())), preferred_element_type=F32)

    wh, wm, wl = _split3(rwt_ref[...])

    def router_logits(hs):
        hh, hm, hl = _split3(hs)
        return nt(wh, hh) + (nt(wh, hm) + nt(wm, hh)) + (nt(wh, hl) + nt(wm, hm) + nt(wl, hh))

    def project(sl):
        return jnp.dot(act_ref[sl, :], wo_ref[...], preferred_element_type=F32)

    def normalise(sl, mx):
        hs = _layer_norm_rows(ALPHA * res_ref[sl, :] + mx, lng_ref[...], lnb_ref[...])
        h_ref[sl, :] = hs
        hp_ref[sl, :hs.shape[1] // 2] = _pack_halves(hs)
        return jax.nn.sigmoid(router_logits(hs))

    s = []
    mixed = project(subs[0])
    for u in range(len(subs)):
        nxt = project(subs[u + 1]) if u + 1 < len(subs) else None
        s.append(normalise(subs[u], mixed))
        mixed = nxt

    def select(sg):
        s_sel = sg + rb_ref[...]
        rows = [s_sel[e:e + 1, :] for e in range(N_EXPERTS)]
        grp_score, grp_i0, grp_i1 = [], [], []
        for gi in range(N_GROUPS):
            vals = rows[gi * EXPERTS_PER_GROUP:(gi + 1) * EXPERTS_PER_GROUP]
            ranks = [_rank_among(vals, q) for q in range(EXPERTS_PER_GROUP)]
            idx = [jnp.full_like(ranks[0], q) for q in range(EXPERTS_PER_GROUP)]
            grp_score.append(_pick(ranks, vals, 0) + _pick(ranks, vals, 1))
            grp_i0.append(_pick(ranks, idx, 0))
            grp_i1.append(_pick(ranks, idx, 1))
        g_ranks = [_rank_among(grp_score, q) for q in range(N_GROUPS)]
        gidx = [jnp.full_like(g_ranks[0], q) for q in range(N_GROUPS)]
        g_star = _pick(g_ranks, gidx, 0)
        e0 = g_star * EXPERTS_PER_GROUP + _pick(g_ranks, grp_i0, 0)
        e1 = g_star * EXPERTS_PER_GROUP + _pick(g_ranks, grp_i1, 0)
        return e0, e1

    picked = each(select, s)
    e_iota = lax.broadcasted_iota(jnp.int32, (N_EXPERTS, tt), 0)

    def gates(sg, pk):
        gate0 = jnp.sum(jnp.where(e_iota == pk[0], sg, 0.0), axis=0, keepdims=True)
        gate1 = jnp.sum(jnp.where(e_iota == pk[1], sg, 0.0), axis=0, keepdims=True)
        denom = gate0 + gate1
        return gate0 / denom, gate1 / denom

    gate = each(gates, s, picked)

    def classify(pk):
        e0, e1 = pk
        grp = lax.shift_right_logical(e0, 2)
        l0 = e0 - grp * EXPERTS_PER_GROUP
        l1 = e1 - grp * EXPERTS_PER_GROUP
        lo, hi = jnp.minimum(l0, l1), jnp.maximum(l0, l1)
        pair = jnp.where(hi == 1, 0, jnp.where(hi == 2, jnp.where(lo == 1, 1, 2),
                                               jnp.where(lo == 0, 3, jnp.where(lo == 1, 4, 5))))
        slot_a = jnp.where(pair == 0, 0, jnp.where(pair <= 2, 2, 3))
        return grp * N_PAIRS + pair, l0 != slot_a

    classes = each(classify, picked)
    c_iota = lax.broadcasted_iota(jnp.int32, (cnt_ref.shape[0], tt), 0)
    hit = each(lambda cl: c_iota == cl[0], classes)
    member = each(lambda hc: jnp.where(hc, 1.0, 0.0), hit)
    ui = lax.broadcasted_iota(jnp.int32, (tt, tt), 0)
    uj = lax.broadcasted_iota(jnp.int32, (tt, tt), 1)
    before = (ui < uj).astype(BF16)
    prefix = each(lambda mb: jnp.dot(mb.astype(BF16), before, preferred_element_type=F32), member)
    base = base_ref[:, 0:1]
    for u, sl in enumerate(subs):
        rank = jnp.sum(jnp.where(hit[u], prefix[u] + base, 0.0), axis=0, keepdims=True)
        route_ref[:, sl] = jnp.concatenate([classes[u][0], rank.astype(jnp.int32)], axis=0)
        base = base + jnp.sum(member[u], axis=1, keepdims=True)
    base_ref[...] = jnp.broadcast_to(base, base_ref.shape)
    cnt_ref[...] = jnp.broadcast_to(base, cnt_ref.shape).astype(jnp.int32)

    half = hp_ref.shape[1] - V7X_LANES
    for sl, gt, cl in zip(subs, gate, classes):
        g_a = jnp.where(cl[1], gt[1], gt[0])
        g_b = jnp.where(cl[1], gt[0], gt[1])
        gpad = jnp.concatenate([g_a, g_b, jnp.zeros((V7X_LANES - 2, tt), F32)], axis=0)
        hp_ref[sl, half:] = lax.bitcast_convert_type(gpad.T, jnp.uint32)


def _mixer_epilogue(act, wo, res, ln_g, ln_b, router_wt, router_bias):
    n_tok, d = res.shape
    k_in = act.shape[1]
    tt = EPI_TILE
    n_tiles = n_tok // tt
    full = lambda arr: pl.BlockSpec(arr.shape, lambda i: (0,) * arr.ndim)
    return pl.pallas_call(
        _epilogue_kernel,
        grid=(n_tiles,),
        in_specs=[pl.BlockSpec((tt, k_in), lambda i: (i, 0)), full(wo), pl.BlockSpec((tt, d), lambda i: (i, 0)),
                  full(ln_g), full(ln_b), full(router_wt), full(router_bias)],
        out_specs=[pl.BlockSpec((tt, d), lambda i: (i, 0)),
                   pl.BlockSpec((tt, d // 2 + V7X_LANES), lambda i: (i, 0)),
                   pl.BlockSpec((2, tt), lambda i: (0, i)),
                   pl.BlockSpec((CLASS_ROWS, V7X_LANES), lambda i: (0, 0))],
        out_shape=[jax.ShapeDtypeStruct((n_tok, d), F32),
                   jax.ShapeDtypeStruct((n_tok, d // 2 + V7X_LANES), jnp.uint32),
                   jax.ShapeDtypeStruct((2, n_tok), jnp.int32),
                   jax.ShapeDtypeStruct((CLASS_ROWS, V7X_LANES), jnp.int32)],
        scratch_shapes=[pltpu.VMEM((CLASS_ROWS, V7X_LANES), F32)],
        compiler_params=pltpu.CompilerParams(
            dimension_semantics=("arbitrary",),
            vmem_limit_bytes=_vmem_limit(2 * int(wo.size) * 2 + 2 * tt * (k_in * 2 + 2 * d * 4) + 16 * tt * d * 4)),
        name="mixer_epilogue",
    )(act, wo, res, ln_g, ln_b, router_wt, router_bias)


def _dispatch_kernel(dest_ref, h_ref, xs_in_ref, xs_ref, sem):
    del xs_in_ref
    tt = h_ref.shape[0]

    def row_copy(t):
        return pltpu.make_async_copy(h_ref.at[pl.ds(t, 1)], xs_ref.at[pl.ds(dest_ref[0, 0, t], 1)], sem)

    def issue(t0, carry):
        for u in range(DMA_ISSUE_UNROLL):
            row_copy(t0 * DMA_ISSUE_UNROLL + u).start()
        return carry

    lax.fori_loop(0, tt // DMA_ISSUE_UNROLL, issue, 0)
    pltpu.make_async_copy(h_ref, xs_ref.at[pl.ds(0, tt)], sem).wait()


def _moe_dispatch(h, dest_tiles, n_rows):
    n_tok, d = h.shape
    tt = dest_tiles.shape[2]
    n_tiles = n_tok // tt
    zeros = jnp.zeros((n_rows, d), h.dtype)
    return pl.pallas_call(
        _dispatch_kernel,
        grid=(n_tiles,),
        in_specs=[pl.BlockSpec((1, 1, tt), lambda i: (i, 0, 0), memory_space=pltpu.SMEM),
                  pl.BlockSpec((tt, d), lambda i: (i, 0)),
                  pl.BlockSpec(memory_space=pl.ANY)],
        out_specs=pl.BlockSpec(memory_space=pl.ANY),
        out_shape=jax.ShapeDtypeStruct((n_rows, d), h.dtype),
        scratch_shapes=[pltpu.SemaphoreType.DMA(())],
        input_output_aliases={2: 0},
        compiler_params=pltpu.CompilerParams(dimension_semantics=("arbitrary",),
                                             vmem_limit_bytes=_vmem_limit(2 * tt * d * 4)),
        name="moe_dispatch",
    )(dest_tiles, h, zeros)


def _ffn_kernel(bea_ref, beb_ref, nblk_ref, x_ref, wga_ref, wua_ref, wda_ref, wgb_ref, wub_ref, wdb_ref,
                y_ref, wg_bf, wu_bf, wd_bf):
    j = pl.program_id(0)
    active = j < nblk_ref[0]
    prev = jnp.maximum(j - 1, 0)
    slots = ((bea_ref, wga_ref, wua_ref, wda_ref), (beb_ref, wgb_ref, wub_ref, wdb_ref))

    for s, (be_ref, wg_ref, wu_ref, wd_ref) in enumerate(slots):
        @pl.when(active & ((j == 0) | (be_ref[j] != be_ref[prev])))
        def _():
            wg_bf[s] = wg_ref[0, 0].astype(BF16)
            wu_bf[s] = wu_ref[0, 0].astype(BF16)
            wd_bf[s] = wd_ref[0, 0].astype(BF16)

    @pl.when(active)
    def _():
        half = y_ref.shape[1]
        mm = lambda a, b: jnp.dot(a, b, preferred_element_type=F32)
        x_lo, x_hi = (v.astype(BF16) for v in _unpack_halves(x_ref[:, :half]))
        gates = lax.bitcast_convert_type(x_ref[:, half:], F32)
        gate = [mm(x_lo, wg_bf[s, :half, :]) + mm(x_hi, wg_bf[s, half:, :]) for s in range(2)]
        up = [mm(x_lo, wu_bf[s, :half, :]) + mm(x_hi, wu_bf[s, half:, :]) for s in range(2)]
        hid = [((g * jax.nn.sigmoid(g)) * u).astype(BF16) for g, u in zip(gate, up)]
        y = [mm(hid[s], wd_bf[s]) for s in range(2)]
        y_ref[...] = _pack_halves(y[0] * gates[:, 0:1] + y[1] * gates[:, 1:2])

    @pl.when(jnp.logical_not(active))
    def _():
        y_ref[...] = jnp.zeros_like(y_ref)


def _moe_ffn(xs, block_ea, block_eb, n_used, layer, wg, wu, wd):
    n_rows, width = xs.shape
    half = width - V7X_LANES
    d = 2 * half
    blk = MOE_BLOCK
    n_blocks = n_rows // blk
    de = wg.shape[3]
    expert_a = lambda j, ea, eb, nb: (layer, ea[j], 0, 0)
    expert_b = lambda j, ea, eb, nb: (layer, eb[j], 0, 0)
    rows = lambda j, ea, eb, nb: (j, 0)
    grid_spec = pltpu.PrefetchScalarGridSpec(
        num_scalar_prefetch=3,
        grid=(n_blocks,),
        in_specs=[pl.BlockSpec((blk, width), rows),
                  pl.BlockSpec((1, 1, d, de), expert_a), pl.BlockSpec((1, 1, d, de), expert_a),
                  pl.BlockSpec((1, 1, de, d), expert_a),
                  pl.BlockSpec((1, 1, d, de), expert_b), pl.BlockSpec((1, 1, d, de), expert_b),
                  pl.BlockSpec((1, 1, de, d), expert_b)],
        out_specs=pl.BlockSpec((blk, half), rows),
        scratch_shapes=[pltpu.VMEM((2, d, de), BF16), pltpu.VMEM((2, d, de), BF16), pltpu.VMEM((2, de, d), BF16)],
    )
    return pl.pallas_call(
        _ffn_kernel,
        grid_spec=grid_spec,
        out_shape=jax.ShapeDtypeStruct((n_rows, half), jnp.uint32),
        compiler_params=pltpu.CompilerParams(
            dimension_semantics=("arbitrary",),
            vmem_limit_bytes=_vmem_limit(2 * 3 * d * de * (2 * 4 + 2) + 6 * blk * d * 4 + 8 * blk * de * 4)),
        name="moe_ffn",
    )(block_ea, block_eb, n_used, xs, wg, wu, wd, wg, wu, wd)


def _combine_kernel(dest_ref, dest_next_ref, ys_ref, res_ref, lng_ref, lnb_ref, o_ref, buf_ref, sem):
    i = pl.program_id(0)
    n = pl.num_programs(0)
    tt = res_ref.shape[0]
    cur = i % 2

    def gather(idx_ref, buf):
        def row_copy(t):
            return pltpu.make_async_copy(ys_ref.at[pl.ds(idx_ref[0, 0, t], 1)], buf_ref.at[buf, pl.ds(t, 1)],
                                         sem.at[buf])

        def issue(t0, carry):
            for u in range(DMA_ISSUE_UNROLL):
                row_copy(t0 * DMA_ISSUE_UNROLL + u).start()
            return carry

        lax.fori_loop(0, tt // DMA_ISSUE_UNROLL, issue, 0)

    @pl.when(i == 0)
    def _():
        gather(dest_ref, cur)

    @pl.when(i + 1 < n)
    def _():
        gather(dest_next_ref, 1 - cur)

    pltpu.make_async_copy(ys_ref.at[pl.ds(0, tt)], buf_ref.at[cur], sem.at[cur]).wait()

    ffn = jnp.concatenate(_unpack_halves(buf_ref[cur]), axis=1)
    o_ref[...] = _layer_norm_rows(ALPHA * res_ref[...] + ffn, lng_ref[...], lnb_ref[...])


def _moe_combine(ys, dest_tiles, res, ln_g, ln_b):
    n_tok, d = res.shape
    tt = dest_tiles.shape[2]
    n_tiles = n_tok // tt
    full = lambda arr: pl.BlockSpec(arr.shape, lambda i: (0,) * arr.ndim)
    return pl.pallas_call(
        _combine_kernel,
        grid=(n_tiles,),
        in_specs=[pl.BlockSpec((1, 1, tt), lambda i: (i, 0, 0), memory_space=pltpu.SMEM),
                  pl.BlockSpec((1, 1, tt), lambda i: (jnp.minimum(i + 1, n_tiles - 1), 0, 0),
                               memory_space=pltpu.SMEM),
                  pl.BlockSpec(memory_space=pl.ANY),
                  pl.BlockSpec((tt, d), lambda i: (i, 0)),
                  full(ln_g), full(ln_b)],
        out_specs=pl.BlockSpec((tt, d), lambda i: (i, 0)),
        out_shape=jax.ShapeDtypeStruct((n_tok, d), F32),
        scratch_shapes=[pltpu.VMEM((2, tt, ys.shape[1]), ys.dtype), pltpu.SemaphoreType.DMA((2,))],
        compiler_params=pltpu.CompilerParams(
            dimension_semantics=("arbitrary",),
            vmem_limit_bytes=_vmem_limit(4 * tt * d * 2 + 2 * 3 * tt * d * 4)),
        name="moe_combine",
    )(dest_tiles, dest_tiles, ys, res, ln_g, ln_b)


def _moe_layer(h, h_packed, route, counts, layer, wg, wu, wd, ln_g, ln_b):
    n_tok, d = h.shape
    blk = MOE_BLOCK
    n_rows = (n_tok + N_CLASSES * (blk - 1) + blk - 1) // blk * blk
    n_blocks = n_rows // blk
    cls, rank = route[0], route[1]
    counts = counts[:N_CLASSES, 0]
    padded = (counts + blk - 1) // blk * blk
    pend = jnp.cumsum(padded)
    pstart = pend - padded
    onehot = (cls[:, None] == jnp.arange(N_CLASSES, dtype=jnp.int32)).astype(jnp.int32)
    dest = rank + jnp.sum(onehot * pstart, axis=-1)
    tiles = lambda tt: dest.reshape(n_tok // tt, 1, tt)
    blk_start = jnp.arange(n_blocks, dtype=jnp.int32) * blk
    block_cls = jnp.minimum(jnp.sum((blk_start[:, None] >= pend[None, :]).astype(jnp.int32), axis=1), N_CLASSES - 1)
    n_used = (pend[-1:] // blk).astype(jnp.int32)
    blk_idx = jnp.arange(n_blocks, dtype=jnp.int32)
    block_cls = jnp.where(blk_idx < n_used[0], block_cls, block_cls[jnp.maximum(n_used[0] - 1, 0)])
    class_ids = jnp.arange(N_CLASSES, dtype=jnp.int32)
    first_expert = (class_ids // N_PAIRS) * EXPERTS_PER_GROUP
    expert_a = first_expert + jnp.array(_PAIR_SLOT_A, jnp.int32)[class_ids % N_PAIRS]
    expert_b = first_expert + jnp.array(_PAIR_SLOT_B, jnp.int32)[class_ids % N_PAIRS]
    xs = _moe_dispatch(h_packed, tiles(DISPATCH_TILE), n_rows)
    ys = _moe_ffn(xs, expert_a[block_cls].astype(jnp.int32), expert_b[block_cls].astype(jnp.int32), n_used,
                  layer, wg, wu, wd)
    return _moe_combine(ys, tiles(COMBINE_TILE), h, ln_g, ln_b)


def _fox_proj_kernel(x_ref, wq_ref, wk_ref, wv_ref, wf_ref, bf_ref, q_ref, k_ref, v_ref, c_ref, carry_ref):
    t = pl.program_id(1)

    @pl.when(t == 0)
    def _():
        carry_ref[...] = jnp.zeros_like(carry_ref)

    tt = PROJ_SUBTILE
    subs = [slice(u * tt, (u + 1) * tt) for u in range(x_ref.shape[0] // tt)]
    each = lambda fn, *lists: [fn(*args) for args in zip(*lists)]
    parts = each(lambda sl: _split3(x_ref[sl, :]), subs)
    wh, wm, wl = _split3(wf_ref[...])
    mm = lambda a, b: jnp.dot(a, b, preferred_element_type=F32)
    logit = each(lambda p: (mm(p[0], wh) + (mm(p[0], wm) + mm(p[1], wh))
                            + (mm(p[0], wl) + mm(p[1], wm) + mm(p[2], wh))) + bf_ref[...], parts)
    for sl, p in zip(subs, parts):
        q_ref[sl, :] = (mm(p[0], wq_ref[...]) * LOG2E).astype(q_ref.dtype)
    for sl, p in zip(subs, parts):
        k_ref[sl, :] = mm(p[0], wk_ref[...]).astype(k_ref.dtype)
    for sl, p in zip(subs, parts):
        v_ref[sl, :] = mm(p[0], wv_ref[...]).astype(v_ref.dtype)
    ti = lax.broadcasted_iota(jnp.int32, (tt, tt), 0)
    tj = lax.broadcasted_iota(jnp.int32, (tt, tt), 1)
    tril = (tj <= ti).astype(BF16)
    local = each(lambda lg: sum(mm(tril, part) for part in _split3(jax.nn.log_sigmoid(lg))), logit)
    carry = carry_ref[0:1, :]
    for sl, cs in zip(subs, local):
        c = cs + carry
        c_ref[sl, :] = c
        carry = c[tt - 1:tt, :]
    carry_ref[...] = jnp.broadcast_to(carry, carry_ref.shape)


def _fox_proj(x2, batch, seq_len, wq, wk, wv, wf, b_f):
    n_tok, d = x2.shape
    tt = PROJ_TILE
    nt = seq_len // tt
    tile = pl.BlockSpec((tt, d), lambda b, t: (b * nt + t, 0))
    full = lambda arr: pl.BlockSpec(arr.shape, lambda b, t: (0,) * arr.ndim)
    return pl.pallas_call(
        _fox_proj_kernel,
        grid=(batch, nt),
        in_specs=[tile, full(wq), full(wk), full(wv), full(wf), full(b_f)],
        out_specs=[tile, tile, tile, pl.BlockSpec((tt, N_HEADS), lambda b, t: (b * nt + t, 0))],
        out_shape=[jax.ShapeDtypeStruct((n_tok, d), BF16)] * 3 + [jax.ShapeDtypeStruct((n_tok, N_HEADS), F32)],
        scratch_shapes=[pltpu.VMEM((8, N_HEADS), F32)],
        compiler_params=pltpu.CompilerParams(
            dimension_semantics=("parallel", "arbitrary"),
            vmem_limit_bytes=_vmem_limit(2 * 3 * d * d * 2 + 2 * tt * d * (4 + 3 * 2) + 8 * tt * d * 4)),
        name="fox_proj",
    )(x2, wq, wk, wv, wf, b_f)


def _fox_attn_kernel(q_ref, k_ref, v_ref, ct_ref, o_ref):
    hp = pl.program_id(1)
    seq_len = q_ref.shape[0]
    tq = ATTN_TILE
    tk = tq
    nq = seq_len // tq
    N = HEAD_DIM
    nt = lambda a, b: lax.dot_general(a, b, (((1,), (1,)), ((), ())), preferred_element_type=F32)
    mm = lambda a, b: jnp.dot(a, b, preferred_element_type=F32)

    lane = lax.broadcasted_iota(jnp.int32, (tq, 2 * N), 1)
    vrow = lax.broadcasted_iota(jnp.int32, (2 * N, tk), 0)
    orow = lax.broadcasted_iota(jnp.int32, (2 * N, tq), 0)
    kpos = lax.broadcasted_iota(jnp.int32, (tk, tq), 0)
    qpos = lax.broadcasted_iota(jnp.int32, (tk, tq), 1)
    causal = kpos <= qpos
    ident_v = (lax.broadcasted_iota(jnp.int32, (2 * N, 2 * N), 0)
               == lax.broadcasted_iota(jnp.int32, (2 * N, 2 * N), 1)).astype(BF16)

    zero_q = jnp.zeros((tq, 2 * N), BF16)

    def head_pair(hl):
        lanes = slice(hl * 2 * N, (hl + 1) * 2 * N)
        first_head = 2 * (hp * ATTN_PAIRS_PER_STEP + hl)
        c_parts = [[part.astype(F32) for part in _split3(ct_ref[0, pl.ds(first_head + u, 1), :] * LOG2E)]
                   for u in range(2)]
        ones = jnp.ones((3, tq), F32)
        pad = jnp.zeros((2 * N - 6, tq), F32)
        m = [[None, None] for _ in range(nq)]
        l = [[None, None] for _ in range(nq)]
        acc = [None] * nq
        keys, queries, values = {}, {}, {}

        def key_side(j):
            if j not in keys:
                ks = slice(j * tk, (j + 1) * tk)
                k_j = k_ref[ks, lanes]
                extra = [jnp.concatenate([-part[:, ks] for part in c_parts[u]] + [ones, pad], axis=0).T.astype(BF16)
                         for u in range(2)]
                keys[j] = [jnp.concatenate([k_j, extra[u]], axis=1) for u in range(2)]
            return keys[j]

        def query_side(qi):
            if qi not in queries:
                qs = slice(qi * tq, (qi + 1) * tq)
                q = q_ref[qs, lanes]
                extra = [jnp.concatenate([ones] + [part[:, qs] for part in c_parts[u]] + [pad], axis=0).T.astype(BF16)
                         for u in range(2)]
                queries[qi] = [jnp.concatenate([jnp.where((lane < N) == (u == 0), q, zero_q), extra[u]], axis=1)
                               for u in range(2)]
            return queries[qi]

        def scores(j, qi):
            k_aug, q_aug = key_side(j), query_side(qi)
            t = [nt(k_aug[u], q_aug[u]) for u in range(2)]
            return [jnp.where(causal, tu, -jnp.inf) for tu in t] if qi == j else t

        def absorb(j, qi, t):
            if j not in values:
                v_t = nt(ident_v, v_ref[j * tk:(j + 1) * tk, lanes]).astype(BF16)
                values[j] = [jnp.where(vrow < N, v_t, jnp.zeros_like(v_t)),
                             jnp.where(vrow < N, jnp.zeros_like(v_t), v_t)]
            v_heads = values[j]
            rmax = [jnp.max(t[u], axis=0, keepdims=True) for u in range(2)]
            m_new = rmax if j == 0 else [jnp.maximum(m[qi][u], rmax[u]) for u in range(2)]
            p = [jnp.exp2(t[u] - m_new[u]) for u in range(2)]
            psum = [jnp.sum(p[u], axis=0, keepdims=True) for u in range(2)]
            pv = mm(v_heads[0], p[0].astype(BF16)) + mm(v_heads[1], p[1].astype(BF16))
            if j == 0:
                acc[qi] = pv
                l[qi] = psum
            else:
                alpha = [jnp.exp2(m[qi][u] - m_new[u]) for u in range(2)]
                acc[qi] = acc[qi] * jnp.where(orow < N, alpha[0], alpha[1]) + pv
                l[qi] = [alpha[u] * l[qi][u] + psum[u] for u in range(2)]
            m[qi] = m_new
            if j == qi:
                o_t = acc[qi] / jnp.where(orow < N, l[qi][0], l[qi][1])
                o_ref[qi * tq:(qi + 1) * tq, lanes] = o_t.T.astype(o_ref.dtype)

        return scores, absorb

    pairs = [head_pair(hl) for hl in range(ATTN_PAIRS_PER_STEP)]
    items = [(hl, j, qi) for j in range(nq) for qi in range(j, nq) for hl in range(ATTN_PAIRS_PER_STEP)]
    score_of = lambda item: pairs[item[0]][0](*item[1:])
    queue = [score_of(item) for item in items[:ATTN_LOOKAHEAD]]
    for n, (hl, j, qi) in enumerate(items):
        t_cur = queue.pop(0)
        if n + ATTN_LOOKAHEAD < len(items):
            queue.append(score_of(items[n + ATTN_LOOKAHEAD]))
        pairs[hl][1](j, qi, t_cur)


def _fox_attn(q, k, v, c_t, batch, seq_len):
    n_tok, d = q.shape
    width = ATTN_PAIRS_PER_STEP * 2 * HEAD_DIM
    seq = pl.BlockSpec((seq_len, width), lambda b, hp: (b, hp))
    return pl.pallas_call(
        _fox_attn_kernel,
        grid=(batch, d // width),
        in_specs=[seq, seq, seq, pl.BlockSpec((1, N_HEADS, seq_len), lambda b, hp: (b, 0, 0))],
        out_specs=seq,
        out_shape=jax.ShapeDtypeStruct((n_tok, d), BF16),
        compiler_params=pltpu.CompilerParams(
            dimension_semantics=("parallel", "arbitrary"),
            vmem_limit_bytes=_vmem_limit(2 * 4 * seq_len * width * 2 + 128 * ATTN_TILE * ATTN_TILE * 4)),
        name="fox_attn",
    )(q, k, v, c_t)


def kernel(x, rw_mix, rw_wr, rw_wk, rw_wv, rw_wo, rw_w0, rw_w1, rw_w2, rw_a0, rw_a1, rw_a2, rw_g1, rw_g2,
           rw_kk, rw_ka, rw_rk, rw_gn_g, rw_gn_b, fx_w_in, fx_b_f, fx_wo, router_w, router_bias,
           moe_w_gate, moe_w_up, moe_w_down, ln_g, ln_b):
    batch, seq_len, d = x.shape
    n_tok = batch * seq_len
    bf = lambda w: w.astype(BF16)
    row = lambda w: w.reshape(1, -1)
    router_wt = router_w.T
    router_b = router_bias.reshape(N_EXPERTS, 1)
    h = x.reshape(n_tok, d)

    for i in range(DEPTH):
        j = i // 2
        if i % 2 == 0:
            r, lw, k, v, a, g = _rwkv_proj(
                h, seq_len, rw_mix[j], bf(rw_wr[j]), bf(rw_wk[j]), bf(rw_wv[j]), bf(rw_w1[j]), bf(rw_w2[j]),
                bf(rw_a1[j]), bf(rw_a2[j]), bf(rw_g1[j]), bf(rw_g2[j]), row(rw_w0[j]), row(rw_a0[j]))
            act = _rwkv_recur(r, lw, k, v, a, g, batch, seq_len, row(rw_kk[j]), row(rw_ka[j]), row(rw_rk[j]),
                              row(rw_gn_g[j]), row(rw_gn_b[j]))
            wo = bf(rw_wo[j])
        else:
            w_in = fx_w_in[j]
            scale = HEAD_DIM ** -0.5
            q, k, v, c = _fox_proj(h, batch, seq_len, bf(w_in[:, :d] * scale), bf(w_in[:, d:2 * d]),
                                   bf(w_in[:, 2 * d:3 * d]), w_in[:, 3 * d:], row(fx_b_f[j]))
            c_t = c.reshape(batch, seq_len, N_HEADS).transpose(0, 2, 1)
            act = _fox_attn(q, k, v, c_t, batch, seq_len)
            wo = bf(fx_wo[j])
        h, h_packed, route, counts = _mixer_epilogue(act, wo, h, row(ln_g[i, 0]), row(ln_b[i, 0]),
                                                     router_wt, router_b)
        h = _moe_layer(h, h_packed, route, counts, i, moe_w_gate, moe_w_up, moe_w_down,
                       row(ln_g[i, 1]), row(ln_b[i, 1]))
    return h.reshape(batch, seq_len, d)
```

```python
import functools
import math

import jax
import jax.numpy as jnp
from jax import lax
from jax.experimental import pallas as pl
from jax.experimental.pallas import tpu as pltpu

D_MODEL = 1024
HEAD_DIM = 64
N_HEADS = D_MODEL // HEAD_DIM
N_EXPERTS = 16
N_GROUPS = 4
EXPERTS_PER_GROUP = N_EXPERTS // N_GROUPS
D_EXPERT = 512
N_PAIRS = 6
N_CLASSES = N_GROUPS * N_PAIRS
CLASS_ROWS = 32
_PAIR_SLOT_A = (0, 2, 2, 3, 3, 3)
_PAIR_SLOT_B = (1, 1, 0, 0, 1, 2)
GN_EPS = 64e-5
LN_EPS = 1e-5
DEPTH = 2
ALPHA = (2 * DEPTH) ** 0.25
LOG2E = math.log2(math.e)

V7X_LANES = 128
V7X_VMEM_BYTES = 64 * 2 ** 20

V7X_MXU_DIM = 256

RWKV_CHUNK = 64
RWKV_CHUNKS_PER_STEP = 4
RWKV_HEADS_PER_TILE = V7X_MXU_DIM // HEAD_DIM
PROJ_TILE = 512
PROJ_SUBTILE = 256
EPI_TILE = 1024
EPI_SUBTILE = 256
ATTN_TILE = 256
ATTN_PAIRS_PER_STEP = 1
ATTN_LOOKAHEAD = 2
MOE_BLOCK = 256
DISPATCH_TILE = 4096
COMBINE_TILE = 1024
DMA_ISSUE_UNROLL = 8

F32 = jnp.float32
BF16 = jnp.bfloat16


def _vmem_limit(n_bytes):
    return int(min(n_bytes + 16 * 2 ** 20, V7X_VMEM_BYTES - 8 * 2 ** 20))


def _split3(x):
    hi = x.astype(BF16)
    r1 = x - hi.astype(F32)
    mid = r1.astype(BF16)
    lo = (r1 - mid.astype(F32)).astype(BF16)
    return hi, mid, lo


def _pack_halves(x):
    half = x.shape[1] // 2
    bits = lambda v: lax.bitcast_convert_type(v.astype(BF16).astype(F32), jnp.uint32)
    return (bits(x[:, :half]) >> 16) | (bits(x[:, half:]) & jnp.uint32(0xFFFF0000))


def _unpack_halves(w):
    lo = lax.bitcast_convert_type(w << 16, F32)
    hi = lax.bitcast_convert_type(w & jnp.uint32(0xFFFF0000), F32)
    return lo, hi


def _layer_norm_rows(x, g, b):
    mu = jnp.mean(x, axis=-1, keepdims=True)
    xc = x - mu
    var = jnp.mean(xc * xc, axis=-1, keepdims=True)
    return xc * lax.rsqrt(var + LN_EPS) * g + b


def _rwkv_proj_kernel(x_ref, xp_ref, mix_ref, wr_ref, wk_ref, wv_ref, w1_ref, w2_ref, a1_ref, a2_ref,
                      g1_ref, g2_ref, w0_ref, a0_ref,
                      r_ref, lw_ref, k_ref, v_ref, a_ref, g_ref, *, tiles_per_seq):
    i = pl.program_id(0)
    x = x_ref[...]
    tt = x.shape[0]
    first = (i % tiles_per_seq) == 0
    prev_row = jnp.where(first, 0.0, xp_ref[7:8, :])
    row = lax.broadcasted_iota(jnp.int32, (tt, 1), 0)
    xprev = jnp.where(row == 0, prev_row, pltpu.roll(x, 1, axis=0))
    xx = xprev - x
    mix = mix_ref[...]
    subs = [slice(u * PROJ_SUBTILE, (u + 1) * PROJ_SUBTILE) for u in range(tt // PROJ_SUBTILE)]
    each = lambda fn, *lists: [fn(*args) for args in zip(*lists)]
    mixed = lambda j: each(lambda sl: (x[sl] + xx[sl] * mix[j:j + 1, :]).astype(BF16), subs)
    xr, xw, xk, xv, xa, xg = (mixed(j) for j in range(6))
    mm = lambda a, w_ref: jnp.dot(a, w_ref[...], preferred_element_type=F32)
    w_mid = each(lambda a: mm(a, w1_ref), xw)
    a_mid = each(lambda a: mm(a, a1_ref), xa)
    g_mid = each(lambda a: mm(a, g1_ref), xg)
    for sl, a in zip(subs, xr):
        r_ref[sl, :] = mm(a, wr_ref).astype(r_ref.dtype)
    for sl, a in zip(subs, xk):
        k_ref[sl, :] = mm(a, wk_ref).astype(k_ref.dtype)
    for sl, a in zip(subs, xv):
        v_ref[sl, :] = mm(a, wv_ref).astype(v_ref.dtype)
    z = each(lambda t: w0_ref[...] + mm(jnp.tanh(t).astype(BF16), w2_ref), w_mid)
    for sl, t in zip(subs, a_mid):
        a_ref[sl, :] = jax.nn.sigmoid(a0_ref[...] + mm(t.astype(BF16), a2_ref)).astype(a_ref.dtype)
    for sl, t in zip(subs, g_mid):
        g_ref[sl, :] = mm(jax.nn.sigmoid(t).astype(BF16), g2_ref).astype(g_ref.dtype)
    for sl, zs in zip(subs, z):
        lw_ref[sl, :] = -jnp.exp(-jax.nn.softplus(-zs) - 0.5)


def _rwkv_proj(x2, seq_len, mix, wr, wk, wv, w1, w2, a1, a2, g1, g2, w0, a0):
    n_tok, d = x2.shape
    tt = PROJ_TILE
    n_tiles = n_tok // tt
    tiles_per_seq = seq_len // tt
    tile = pl.BlockSpec((tt, d), lambda i: (i, 0))
    prev = pl.BlockSpec((8, d), lambda i: (jnp.maximum(i * (tt // 8) - 1, 0), 0))
    full = lambda arr: pl.BlockSpec(arr.shape, lambda i: (0,) * arr.ndim)
    weights = (mix, wr, wk, wv, w1, w2, a1, a2, g1, g2, w0, a0)
    out_dtypes = (BF16, F32, BF16, BF16, BF16, BF16)
    w_bytes = sum(int(w.size) * w.dtype.itemsize for w in weights)
    return pl.pallas_call(
        functools.partial(_rwkv_proj_kernel, tiles_per_seq=tiles_per_seq),
        grid=(n_tiles,),
        in_specs=[tile, prev] + [full(w) for w in weights],
        out_specs=[tile] * 6,
        out_shape=[jax.ShapeDtypeStruct((n_tok, d), dt) for dt in out_dtypes],
        compiler_params=pltpu.CompilerParams(
            dimension_semantics=("parallel",),
            vmem_limit_bytes=_vmem_limit(2 * w_bytes + 2 * tt * d * (4 + 4 + 5 * 2) + 8 * tt * d * 4)),
        name="rwkv_proj",
    )(x2, x2, *weights)


def _rwkv_recur_kernel(r_ref, lw_ref, k_ref, v_ref, a_ref, g_ref, kk_ref, ka_ref, rk_ref, gng_ref, gnb_ref,
                       o_ref, s_ref):
    c = pl.program_id(1)
    C = RWKV_CHUNK
    N = HEAD_DIM

    @pl.when(c == 0)
    def _():
        s_ref[...] = jnp.zeros_like(s_ref)

    nt = lambda p, q: lax.dot_general(p, q, (((1,), (1,)), ((), ())), preferred_element_type=F32)
    mm = lambda p, q: jnp.dot(p, q, preferred_element_type=F32)
    each = lambda fn, *lists: [fn(*args) for args in zip(*lists)]
    bf = lambda x: x.astype(BF16)

    ti = lax.broadcasted_iota(jnp.int32, (C, C), 0)
    tj = lax.broadcasted_iota(jnp.int32, (C, C), 1)
    tril = (tj <= ti).astype(BF16)

    def chunk_terms(rows):
        lw = lw_ref[rows, :]
        cum = sum(mm(tril, part) for part in _split3(lw))
        rho = cum[C // 2 - 1:C // 2, :]
        last = cum[C - 1:C, :]
        r = r_ref[rows, :].astype(F32)
        k = k_ref[rows, :].astype(F32)
        a = a_ref[rows, :].astype(F32)
        k_mod = k * (1.0 + (a - 1.0) * ka_ref[...])
        return dict(e_q=jnp.exp(cum - rho), e_qx=jnp.exp(cum - lw - rho), e_k=jnp.exp(rho - cum),
                    e_end=jnp.exp(last - cum), e_rho=jnp.exp(rho), d_end=jnp.exp(last),
                    r=r, a=a, v=v_ref[rows, :].astype(F32), kk_raw=k * kk_ref[...], k_mod=k_mod,
                    rkk=r * k_mod * rk_ref[...])

    n_ch = r_ref.shape[0] // C
    terms = [chunk_terms(slice(ci * C, (ci + 1) * C)) for ci in range(n_ch)]

    G = RWKV_HEADS_PER_TILE
    R = G * C
    GW = G * N
    er = lax.broadcasted_iota(jnp.int32, (R, GW), 0)
    ec = lax.broadcasted_iota(jnp.int32, (R, GW), 1)
    blk = (er // C) == (ec // N)
    strict = (ec % C) < (er % C)
    incl = (ec % C) <= (er % C)
    eye = (er == ec).astype(F32)
    ones_blk = blk.astype(BF16)

    def expand(x):
        return jnp.where(blk, jnp.concatenate([x] * G, axis=0), 0.0).astype(BF16)

    expand_t = lambda x: bf(jnp.where(blk, jnp.concatenate([x] * G, axis=0), 0.0).T)

    def head_sums(xs):
        parts = _split3(jnp.concatenate(xs, axis=0))[:2]
        tot = mm(jnp.concatenate(parts, axis=0), ones_blk)
        n = len(xs) * C
        tot = tot[:n] + tot[n:]
        return [tot[u * C:(u + 1) * C] for u in range(len(xs))]

    n_grp = N_HEADS // G
    groups = [slice(gi * GW, (gi + 1) * GW) for gi in range(n_grp)]
    units = [(tm, sl) for tm in terms for sl in groups]

    def independent(us, out):
        u_tm = [tm for tm, _ in us]
        u_sl = [sl for _, sl in us]
        pre = head_sums([tm["kk_raw"][:, sl] * tm["kk_raw"][:, sl] for tm, sl in us] + [tm["rkk"][:, sl] for tm, sl in us])
        kk_ss, out["rkk_sum"] = pre[:len(us)], pre[len(us):]
        yield
        kk_n = each(lambda tm, sl, ss: tm["kk_raw"][:, sl] * lax.rsqrt(jnp.maximum(ss, 1e-24)), u_tm, u_sl, kk_ss)
        b_n = each(lambda tm, sl, kk_g: kk_g * tm["a"][:, sl], u_tm, u_sl, kk_n)
        kk_q = each(lambda tm, sl, kk_g: kk_g * tm["e_qx"][:, sl], u_tm, u_sl, kk_n)
        r_q = each(lambda tm, sl: tm["r"][:, sl] * tm["e_q"][:, sl], u_tm, u_sl)
        q2 = each(lambda x, y: jnp.concatenate([expand(x), expand(y)], axis=0), kk_q, r_q)
        a_k = each(lambda tm, sl, q: nt(q, expand(tm["k_mod"][:, sl] * tm["e_k"][:, sl])), u_tm, u_sl, q2)
        yield
        a_b = each(lambda tm, sl, q, b_g: nt(q, expand(b_g * tm["e_k"][:, sl])), u_tm, u_sl, q2, b_n)
        yield
        a_kk = each(lambda x: jnp.where(strict, x[:R], 0.0), a_k)
        a_rk = each(lambda x: jnp.where(incl, x[R:], 0.0), a_k)
        a_kb = each(lambda x: jnp.where(strict, x[:R], 0.0), a_b)
        out["a_rb"] = each(lambda x: bf(jnp.where(incl, x[R:], 0.0)), a_b)
        t_inv = each(lambda x: eye - x, a_kb)
        p = each(lambda x: mm(bf(-x), bf(-x)), a_kb)
        yield
        for _ in range(int(math.log2(C)) - 2):
            both = each(lambda pg, tg: mm(bf(pg), jnp.concatenate([bf(pg), bf(tg)], axis=1)), p, t_inv)
            p = each(lambda x: x[:, :R], both)
            t_inv = each(lambda tg, x: tg + x[:, R:], t_inv, both)
            yield
        out["t_inv"] = each(lambda pg, tg: bf(tg + mm(bf(pg), bf(tg))), p, t_inv)
        yield
        out["kd_t"] = each(lambda tm, sl: expand_t(tm["k_mod"][:, sl] * tm["e_end"][:, sl]), u_tm, u_sl)
        out["bd_t"] = each(lambda tm, sl, b_g: expand_t(b_g * tm["e_end"][:, sl]), u_tm, u_sl, b_n)
        out["v_e"] = each(lambda tm, sl: expand(tm["v"][:, sl]), u_tm, u_sl)
        out["av"] = each(lambda x, y, ve: mm(bf(jnp.concatenate([x, y], axis=0)), ve), a_kk, a_rk, out["v_e"])
        out["q2_abs"] = each(lambda tm, sl, x, y: jnp.concatenate([expand(x * tm["e_rho"][:, sl]),
                                                                   expand(y * tm["e_rho"][:, sl])], axis=0),
                             u_tm, u_sl, kk_q, r_q)
        yield

    state = {"st": [s_ref[gi] for gi in range(n_grp)]}
    ys = [None] * len(units)

    def dependent(ci, res, lo):
        span = slice(lo, lo + n_grp)
        qs = each(lambda q, s: mm(q, bf(s)), res["q2_abs"][span], state["st"])
        yield
        sa_e = each(lambda tg, q, x: bf(mm(tg, bf(q[:R] + x[:R]))), res["t_inv"][span], qs, res["av"][span])
        yield
        y_e = each(lambda q, x, arb, sa: q[R:] + x[R:] - mm(arb, sa), qs, res["av"][span], res["a_rb"][span], sa_e)
        ys[ci * n_grp:(ci + 1) * n_grp] = each(lambda x: sum(x[u * C:(u + 1) * C] for u in range(G)), y_e)
        yield
        upd = each(lambda kt, bt, ve, sa: mm(jnp.concatenate([kt, -bt], axis=1), jnp.concatenate([ve, sa], axis=0)),
                   res["kd_t"][span], res["bd_t"][span], res["v_e"][span], sa_e)
        d_col = [jnp.broadcast_to(terms[ci]["d_end"][:, sl], (GW, GW)).T for sl in groups]
        state["st"] = each(lambda s, dc, up: s * dc + up, state["st"], d_col, upd)
        yield

    half = (n_ch // 2) * n_grp
    first, second = {}, {}
    for _ in independent(units[:half], first):
        pass
    chain = [stage for ci in range(n_ch // 2) for stage in [dependent(ci, first, ci * n_grp)]]
    pending = iter(())
    todo = list(chain)

    def advance():
        nonlocal pending
        while True:
            try:
                next(pending)
                return True
            except StopIteration:
                if not todo:
                    return False
                pending = todo.pop(0)

    for _ in independent(units[half:], second):
        advance()
    while advance():
        pass
    for ci in range(n_ch // 2, n_ch):
        for _ in dependent(ci, second, (ci - n_ch // 2) * n_grp):
            pass
    for gi in range(n_grp):
        s_ref[gi] = state["st"][gi]
    rkk_sum = first["rkk_sum"] + second["rkk_sum"]

    inv_n = 1.0 / N
    ycs = [y - mu * inv_n for y, mu in zip(ys, head_sums(ys))]
    sqs = head_sums([yc * yc for yc in ycs])
    for ui, (tm, sl) in enumerate(units):
        rows = slice((ui // n_grp) * C, (ui // n_grp + 1) * C)
        yn = ycs[ui] * lax.rsqrt(sqs[ui] * inv_n + GN_EPS) * gng_ref[:, sl] + gnb_ref[:, sl]
        o_ref[rows, sl] = ((yn + rkk_sum[ui] * tm["v"][:, sl]) * g_ref[rows, sl]).astype(o_ref.dtype)


def _rwkv_recur(r, lw, k, v, a, g, batch, seq_len, k_k, k_a, r_k, gn_g, gn_b):
    n_tok, d = r.shape
    C = RWKV_CHUNK * RWKV_CHUNKS_PER_STEP
    nc = seq_len // C
    tile = pl.BlockSpec((C, d), lambda b, c: (b * nc + c, 0))
    vec = pl.BlockSpec((1, d), lambda b, c: (0, 0))
    return pl.pallas_call(
        _rwkv_recur_kernel,
        grid=(batch, nc),
        in_specs=[tile] * 6 + [vec] * 5,
        out_specs=tile,
        out_shape=jax.ShapeDtypeStruct((n_tok, d), BF16),
        scratch_shapes=[pltpu.VMEM((N_HEADS // RWKV_HEADS_PER_TILE, V7X_MXU_DIM, V7X_MXU_DIM), F32)],
        compiler_params=pltpu.CompilerParams(
            dimension_semantics=("parallel", "arbitrary"),
            vmem_limit_bytes=_vmem_limit(2 * 7 * C * d * 4 + 32 * C * d * 4 + 64 * V7X_MXU_DIM ** 2 * 4)),
        name="rwkv_recur",
    )(r, lw, k, v, a, g, k_k, k_a, r_k, gn_g, gn_b)


def _rank_among(vals, i):
    cnt = 0
    for j, vj in enumerate(vals):
        if j == i:
            continue
        before = (vj >= vals[i]) if j < i else (vj > vals[i])
        cnt = cnt + before.astype(jnp.int32)
    return cnt


def _pick(ranks, vals, want):
    out = vals[0]
    for rk, vl in zip(ranks[1:], vals[1:]):
        out = jnp.where(rk == want, vl, out)
    return out


def _epilogue_kernel(act_ref, wo_ref, res_ref, lng_ref, lnb_ref, rwt_ref, rb_ref,
                     h_ref, hp_ref, route_ref, cnt_ref, base_ref):
    i = pl.program_id(0)

    @pl.when(i == 0)
    def _():
        base_ref[...] = jnp.zeros_like(base_ref)

    tt = EPI_SUBTILE
    subs = [slice(u * tt, (u + 1) * tt) for u in range(act_ref.shape[0] // tt)]
    each = lambda fn, *lists: [fn(*args) for args in zip(*lists)]
    nt = lambda a, b: lax.dot_general(a, b, (((1,), (1,)), ((), ())), preferred_element_type=F32)

    wh, wm, wl = _split3(rwt_ref[...])

    def router_logits(hs):
        hh, hm, hl = _split3(hs)
        return nt(wh, hh) + (nt(wh, hm) + nt(wm, hh)) + (nt(wh, hl) + nt(wm, hm) + nt(wl, hh))

    def project(sl):
        return jnp.dot(act_ref[sl, :], wo_ref[...], preferred_element_type=F32)

    def normalise(sl, mx):
        hs = _layer_norm_rows(ALPHA * res_ref[sl, :] + mx, lng_ref[...], lnb_ref[...])
        h_ref[sl, :] = hs
        hp_ref[sl, :hs.shape[1] // 2] = _pack_halves(hs)
        return jax.nn.sigmoid(router_logits(hs))

    s = []
    mixed = project(subs[0])
    for u in range(len(subs)):
        nxt = project(subs[u + 1]) if u + 1 < len(subs) else None
        s.append(normalise(subs[u], mixed))
        mixed = nxt

    def select(sg):
        s_sel = sg + rb_ref[...]
        rows = [s_sel[e:e + 1, :] for e in range(N_EXPERTS)]
        grp_score, grp_i0, grp_i1 = [], [], []
        for gi in range(N_GROUPS):
            vals = rows[gi * EXPERTS_PER_GROUP:(gi + 1) * EXPERTS_PER_GROUP]
            ranks = [_rank_among(vals, q) for q in range(EXPERTS_PER_GROUP)]
            idx = [jnp.full_like(ranks[0], q) for q in range(EXPERTS_PER_GROUP)]
            grp_score.append(_pick(ranks, vals, 0) + _pick(ranks, vals, 1))
            grp_i0.append(_pick(ranks, idx, 0))
            grp_i1.append(_pick(ranks, idx, 1))
        g_ranks = [_rank_among(grp_score, q) for q in range(N_GROUPS)]
        gidx = [jnp.full_like(g_ranks[0], q) for q in range(N_GROUPS)]
        g_star = _pick(g_ranks, gidx, 0)
        e0 = g_star * EXPERTS_PER_GROUP + _pick(g_ranks, grp_i0, 0)
        e1 = g_star * EXPERTS_PER_GROUP + _pick(g_ranks, grp_i1, 0)
        return e0, e1

    picked = each(select, s)
    e_iota = lax.broadcasted_iota(jnp.int32, (N_EXPERTS, tt), 0)

    def gates(sg, pk):
        gate0 = jnp.sum(jnp.where(e_iota == pk[0], sg, 0.0), axis=0, keepdims=True)
        gate1 = jnp.sum(jnp.where(e_iota == pk[1], sg, 0.0), axis=0, keepdims=True)
        denom = gate0 + gate1
        return gate0 / denom, gate1 / denom

    gate = each(gates, s, picked)

    def classify(pk):
        e0, e1 = pk
        grp = lax.shift_right_logical(e0, 2)
        l0 = e0 - grp * EXPERTS_PER_GROUP
        l1 = e1 - grp * EXPERTS_PER_GROUP
        lo, hi = jnp.minimum(l0, l1), jnp.maximum(l0, l1)
        pair = jnp.where(hi == 1, 0, jnp.where(hi == 2, jnp.where(lo == 1, 1, 2),
                                               jnp.where(lo == 0, 3, jnp.where(lo == 1, 4, 5))))
        slot_a = jnp.where(pair == 0, 0, jnp.where(pair <= 2, 2, 3))
        return grp * N_PAIRS + pair, l0 != slot_a

    classes = each(classify, picked)
    c_iota = lax.broadcasted_iota(jnp.int32, (cnt_ref.shape[0], tt), 0)
    hit = each(lambda cl: c_iota == cl[0], classes)
    member = each(lambda hc: jnp.where(hc, 1.0, 0.0), hit)
    ui = lax.broadcasted_iota(jnp.int32, (tt, tt), 0)
    uj = lax.broadcasted_iota(jnp.int32, (tt, tt), 1)
    before = (ui < uj).astype(BF16)
    prefix = each(lambda mb: jnp.dot(mb.astype(BF16), before, preferred_element_type=F32), member)
    base = base_ref[:, 0:1]
    for u, sl in enumerate(subs):
        rank = jnp.sum(jnp.where(hit[u], prefix[u] + base, 0.0), axis=0, keepdims=True)
        route_ref[:, sl] = jnp.concatenate([classes[u][0], rank.astype(jnp.int32)], axis=0)
        base = base + jnp.sum(member[u], axis=1, keepdims=True)
    base_ref[...] = jnp.broadcast_to(base, base_ref.shape)
    cnt_ref[...] = jnp.broadcast_to(base, cnt_ref.shape).astype(jnp.int32)

    half = hp_ref.shape[1] - V7X_LANES
    for sl, gt, cl in zip(subs, gate, classes):
        g_a = jnp.where(cl[1], gt[1], gt[0])
        g_b = jnp.where(cl[1], gt[0], gt[1])
        gpad = jnp.concatenate([g_a, g_b, jnp.zeros((V7X_LANES - 2, tt), F32)], axis=0)
        hp_ref[sl, half:] = lax.bitcast_convert_type(gpad.T, jnp.uint32)


def _mixer_epilogue(act, wo, res, ln_g, ln_b, router_wt, router_bias):
    n_tok, d = res.shape
    k_in = act.shape[1]
    tt = EPI_TILE
    n_tiles = n_tok // tt
    full = lambda arr: pl.BlockSpec(arr.shape, lambda i: (0,) * arr.ndim)
    return pl.pallas_call(
        _epilogue_kernel,
        grid=(n_tiles,),
        in_specs=[pl.BlockSpec((tt, k_in), lambda i: (i, 0)), full(wo), pl.BlockSpec((tt, d), lambda i: (i, 0)),
                  full(ln_g), full(ln_b), full(router_wt), full(router_bias)],
        out_specs=[pl.BlockSpec((tt, d), lambda i: (i, 0)),
                   pl.BlockSpec((tt, d // 2 + V7X_LANES), lambda i: (i, 0)),
                   pl.BlockSpec((2, tt), lambda i: (0, i)),
                   pl.BlockSpec((CLASS_ROWS, V7X_LANES), lambda i: (0, 0))],
        out_shape=[jax.ShapeDtypeStruct((n_tok, d), F32),
                   jax.ShapeDtypeStruct((n_tok, d // 2 + V7X_LANES), jnp.uint32),
                   jax.ShapeDtypeStruct((2, n_tok), jnp.int32),
                   jax.ShapeDtypeStruct((CLASS_ROWS, V7X_LANES), jnp.int32)],
        scratch_shapes=[pltpu.VMEM((CLASS_ROWS, V7X_LANES), F32)],
        compiler_params=pltpu.CompilerParams(
            dimension_semantics=("arbitrary",),
            vmem_limit_bytes=_vmem_limit(2 * int(wo.size) * 2 + 2 * tt * (k_in * 2 + 2 * d * 4) + 16 * tt * d * 4)),
        name="mixer_epilogue",
    )(act, wo, res, ln_g, ln_b, router_wt, router_bias)


def _dispatch_kernel(dest_ref, h_ref, xs_in_ref, xs_ref, sem):
    del xs_in_ref
    tt = h_ref.shape[0]

    def row_copy(t):
        return pltpu.make_async_copy(h_ref.at[pl.ds(t, 1)], xs_ref.at[pl.ds(dest_ref[0, 0, t], 1)], sem)

    def issue(t0, carry):
        for u in range(DMA_ISSUE_UNROLL):
            row_copy(t0 * DMA_ISSUE_UNROLL + u).start()
        return carry

    lax.fori_loop(0, tt // DMA_ISSUE_UNROLL, issue, 0)
    pltpu.make_async_copy(h_ref, xs_ref.at[pl.ds(0, tt)], sem).wait()


def _moe_dispatch(h, dest_tiles, n_rows, initialised=None):
    n_tok, d = h.shape
    tt = dest_tiles.shape[2]
    n_tiles = n_tok // tt
    zeros = jnp.zeros((n_rows, d), h.dtype) if initialised is None else initialised
    return pl.pallas_call(
        _dispatch_kernel,
        grid=(n_tiles,),
        in_specs=[pl.BlockSpec((1, 1, tt), lambda i: (i, 0, 0), memory_space=pltpu.SMEM),
                  pl.BlockSpec((tt, d), lambda i: (i, 0)),
                  pl.BlockSpec(memory_space=pl.ANY)],
        out_specs=pl.BlockSpec(memory_space=pl.ANY),
        out_shape=jax.ShapeDtypeStruct((n_rows, d), h.dtype),
        scratch_shapes=[pltpu.SemaphoreType.DMA(())],
        input_output_aliases={2: 0},
        compiler_params=pltpu.CompilerParams(dimension_semantics=("arbitrary",),
                                             vmem_limit_bytes=_vmem_limit(2 * tt * d * 4)),
        name="moe_dispatch",
    )(dest_tiles, h, zeros)


def _ffn_kernel(bea_ref, beb_ref, nblk_ref, x_ref, wga_ref, wua_ref, wda_ref, wgb_ref, wub_ref, wdb_ref,
                y_ref, wg_bf, wu_bf, wd_bf):
    j = pl.program_id(0)
    active = j < nblk_ref[0]
    prev = jnp.maximum(j - 1, 0)
    slots = ((bea_ref, wga_ref, wua_ref, wda_ref), (beb_ref, wgb_ref, wub_ref, wdb_ref))

    for s, (be_ref, wg_ref, wu_ref, wd_ref) in enumerate(slots):
        @pl.when(active & ((j == 0) | (be_ref[j] != be_ref[prev])))
        def _():
            wg_bf[s] = wg_ref[0, 0].astype(BF16)
            wu_bf[s] = wu_ref[0, 0].astype(BF16)
            wd_bf[s] = wd_ref[0, 0].astype(BF16)

    @pl.when(active)
    def _():
        half = y_ref.shape[1]
        mm = lambda a, b: jnp.dot(a, b, preferred_element_type=F32)
        x_lo, x_hi = (v.astype(BF16) for v in _unpack_halves(x_ref[:, :half]))
        gates = lax.bitcast_convert_type(x_ref[:, half:], F32)
        gate = [mm(x_lo, wg_bf[s, :half, :]) + mm(x_hi, wg_bf[s, half:, :]) for s in range(2)]
        up = [mm(x_lo, wu_bf[s, :half, :]) + mm(x_hi, wu_bf[s, half:, :]) for s in range(2)]
        hid = [((g * jax.nn.sigmoid(g)) * u).astype(BF16) for g, u in zip(gate, up)]
        y = [mm(hid[s], wd_bf[s]) for s in range(2)]
        y_ref[...] = _pack_halves(y[0] * gates[:, 0:1] + y[1] * gates[:, 1:2])

    @pl.when(jnp.logical_not(active))
    def _():
        y_ref[...] = jnp.zeros_like(y_ref)


def _moe_ffn(xs, block_ea, block_eb, n_used, layer, wg, wu, wd):
    n_rows, width = xs.shape
    half = width - V7X_LANES
    d = 2 * half
    blk = MOE_BLOCK
    n_blocks = n_rows // blk
    de = wg.shape[3]
    expert_a = lambda j, ea, eb, nb: (layer, ea[j], 0, 0)
    expert_b = lambda j, ea, eb, nb: (layer, eb[j], 0, 0)
    rows = lambda j, ea, eb, nb: (j, 0)
    grid_spec = pltpu.PrefetchScalarGridSpec(
        num_scalar_prefetch=3,
        grid=(n_blocks,),
        in_specs=[pl.BlockSpec((blk, width), rows),
                  pl.BlockSpec((1, 1, d, de), expert_a), pl.BlockSpec((1, 1, d, de), expert_a),
                  pl.BlockSpec((1, 1, de, d), expert_a),
                  pl.BlockSpec((1, 1, d, de), expert_b), pl.BlockSpec((1, 1, d, de), expert_b),
                  pl.BlockSpec((1, 1, de, d), expert_b)],
        out_specs=pl.BlockSpec((blk, half), rows),
        scratch_shapes=[pltpu.VMEM((2, d, de), BF16), pltpu.VMEM((2, d, de), BF16), pltpu.VMEM((2, de, d), BF16)],
    )
    return pl.pallas_call(
        _ffn_kernel,
        grid_spec=grid_spec,
        out_shape=jax.ShapeDtypeStruct((n_rows, half), jnp.uint32),
        compiler_params=pltpu.CompilerParams(
            dimension_semantics=("arbitrary",),
            vmem_limit_bytes=_vmem_limit(2 * 3 * d * de * (2 * 4 + 2) + 6 * blk * d * 4 + 8 * blk * de * 4)),
        name="moe_ffn",
    )(block_ea, block_eb, n_used, xs, wg, wu, wd, wg, wu, wd)


def _combine_kernel(dest_ref, dest_next_ref, ys_ref, res_ref, lng_ref, lnb_ref, o_ref, buf_ref, sem):
    i = pl.program_id(0)
    n = pl.num_programs(0)
    tt = res_ref.shape[0]
    cur = i % 2

    def gather(idx_ref, buf):
        def row_copy(t):
            return pltpu.make_async_copy(ys_ref.at[pl.ds(idx_ref[0, 0, t], 1)], buf_ref.at[buf, pl.ds(t, 1)],
                                         sem.at[buf])

        def issue(t0, carry):
            for u in range(DMA_ISSUE_UNROLL):
                row_copy(t0 * DMA_ISSUE_UNROLL + u).start()
            return carry

        lax.fori_loop(0, tt // DMA_ISSUE_UNROLL, issue, 0)

    @pl.when(i == 0)
    def _():
        gather(dest_ref, cur)

    @pl.when(i + 1 < n)
    def _():
        gather(dest_next_ref, 1 - cur)

    pltpu.make_async_copy(ys_ref.at[pl.ds(0, tt)], buf_ref.at[cur], sem.at[cur]).wait()

    ffn = jnp.concatenate(_unpack_halves(buf_ref[cur]), axis=1)
    o_ref[...] = _layer_norm_rows(ALPHA * res_ref[...] + ffn, lng_ref[...], lnb_ref[...])


def _moe_combine(ys, dest_tiles, res, ln_g, ln_b):
    n_tok, d = res.shape
    tt = dest_tiles.shape[2]
    n_tiles = n_tok // tt
    full = lambda arr: pl.BlockSpec(arr.shape, lambda i: (0,) * arr.ndim)
    return pl.pallas_call(
        _combine_kernel,
        grid=(n_tiles,),
        in_specs=[pl.BlockSpec((1, 1, tt), lambda i: (i, 0, 0), memory_space=pltpu.SMEM),
                  pl.BlockSpec((1, 1, tt), lambda i: (jnp.minimum(i + 1, n_tiles - 1), 0, 0),
                               memory_space=pltpu.SMEM),
                  pl.BlockSpec(memory_space=pl.ANY),
                  pl.BlockSpec((tt, d), lambda i: (i, 0)),
                  full(ln_g), full(ln_b)],
        out_specs=pl.BlockSpec((tt, d), lambda i: (i, 0)),
        out_shape=jax.ShapeDtypeStruct((n_tok, d), F32),
        scratch_shapes=[pltpu.VMEM((2, tt, ys.shape[1]), ys.dtype), pltpu.SemaphoreType.DMA((2,))],
        compiler_params=pltpu.CompilerParams(
            dimension_semantics=("arbitrary",),
            vmem_limit_bytes=_vmem_limit(4 * tt * d * 2 + 2 * 3 * tt * d * 4)),
        name="moe_combine",
    )(dest_tiles, dest_tiles, ys, res, ln_g, ln_b)


def _moe_layer(h, h_packed, route, counts, layer, wg, wu, wd, ln_g, ln_b, sorted_rows=None):
    n_tok, d = h.shape
    blk = MOE_BLOCK
    n_rows = (n_tok + N_CLASSES * (blk - 1) + blk - 1) // blk * blk
    n_blocks = n_rows // blk
    cls, rank = route[0], route[1]
    counts = counts[:N_CLASSES, 0]
    padded = (counts + blk - 1) // blk * blk
    pend = jnp.cumsum(padded)
    pstart = pend - padded
    onehot = (cls[:, None] == jnp.arange(N_CLASSES, dtype=jnp.int32)).astype(jnp.int32)
    dest = rank + jnp.sum(onehot * pstart, axis=-1)
    tiles = lambda tt: dest.reshape(n_tok // tt, 1, tt)
    blk_start = jnp.arange(n_blocks, dtype=jnp.int32) * blk
    block_cls = jnp.minimum(jnp.sum((blk_start[:, None] >= pend[None, :]).astype(jnp.int32), axis=1), N_CLASSES - 1)
    n_used = (pend[-1:] // blk).astype(jnp.int32)
    blk_idx = jnp.arange(n_blocks, dtype=jnp.int32)
    block_cls = jnp.where(blk_idx < n_used[0], block_cls, block_cls[jnp.maximum(n_used[0] - 1, 0)])
    class_ids = jnp.arange(N_CLASSES, dtype=jnp.int32)
    first_expert = (class_ids // N_PAIRS) * EXPERTS_PER_GROUP
    expert_a = first_expert + jnp.array(_PAIR_SLOT_A, jnp.int32)[class_ids % N_PAIRS]
    expert_b = first_expert + jnp.array(_PAIR_SLOT_B, jnp.int32)[class_ids % N_PAIRS]
    xs = _moe_dispatch(h_packed, tiles(DISPATCH_TILE), n_rows, sorted_rows)
    ys = _moe_ffn(xs, expert_a[block_cls].astype(jnp.int32), expert_b[block_cls].astype(jnp.int32), n_used,
                  layer, wg, wu, wd)
    return _moe_combine(ys, tiles(COMBINE_TILE), h, ln_g, ln_b), xs


def _fox_proj_kernel(x_ref, wq_ref, wk_ref, wv_ref, wf_ref, bf_ref, q_ref, k_ref, v_ref, c_ref, carry_ref):
    t = pl.program_id(1)

    @pl.when(t == 0)
    def _():
        carry_ref[...] = jnp.zeros_like(carry_ref)

    tt = PROJ_SUBTILE
    subs = [slice(u * tt, (u + 1) * tt) for u in range(x_ref.shape[0] // tt)]
    each = lambda fn, *lists: [fn(*args) for args in zip(*lists)]
    parts = each(lambda sl: _split3(x_ref[sl, :]), subs)
    wh, wm, wl = _split3(wf_ref[...])
    mm = lambda a, b: jnp.dot(a, b, preferred_element_type=F32)
    logit = each(lambda p: (mm(p[0], wh) + (mm(p[0], wm) + mm(p[1], wh))
                            + (mm(p[0], wl) + mm(p[1], wm) + mm(p[2], wh))) + bf_ref[...], parts)
    for sl, p in zip(subs, parts):
        q_ref[sl, :] = (mm(p[0], wq_ref[...]) * LOG2E).astype(q_ref.dtype)
    for sl, p in zip(subs, parts):
        k_ref[sl, :] = mm(p[0], wk_ref[...]).astype(k_ref.dtype)
    for sl, p in zip(subs, parts):
        v_ref[sl, :] = mm(p[0], wv_ref[...]).astype(v_ref.dtype)
    ti = lax.broadcasted_iota(jnp.int32, (tt, tt), 0)
    tj = lax.broadcasted_iota(jnp.int32, (tt, tt), 1)
    tril = (tj <= ti).astype(BF16)
    local = each(lambda lg: sum(mm(tril, part) for part in _split3(jax.nn.log_sigmoid(lg))), logit)
    carry = carry_ref[0:1, :]
    for sl, cs in zip(subs, local):
        c = cs + carry
        c_ref[sl, :] = c
        carry = c[tt - 1:tt, :]
    carry_ref[...] = jnp.broadcast_to(carry, carry_ref.shape)


def _fox_proj(x2, batch, seq_len, wq, wk, wv, wf, b_f):
    n_tok, d = x2.shape
    tt = PROJ_TILE
    nt = seq_len // tt
    tile = pl.BlockSpec((tt, d), lambda b, t: (b * nt + t, 0))
    full = lambda arr: pl.BlockSpec(arr.shape, lambda b, t: (0,) * arr.ndim)
    return pl.pallas_call(
        _fox_proj_kernel,
        grid=(batch, nt),
        in_specs=[tile, full(wq), full(wk), full(wv), full(wf), full(b_f)],
        out_specs=[tile, tile, tile, pl.BlockSpec((tt, N_HEADS), lambda b, t: (b * nt + t, 0))],
        out_shape=[jax.ShapeDtypeStruct((n_tok, d), BF16)] * 3 + [jax.ShapeDtypeStruct((n_tok, N_HEADS), F32)],
        scratch_shapes=[pltpu.VMEM((8, N_HEADS), F32)],
        compiler_params=pltpu.CompilerParams(
            dimension_semantics=("parallel", "arbitrary"),
            vmem_limit_bytes=_vmem_limit(2 * 3 * d * d * 2 + 2 * tt * d * (4 + 3 * 2) + 8 * tt * d * 4)),
        name="fox_proj",
    )(x2, wq, wk, wv, wf, b_f)


def _fox_attn_kernel(q_ref, k_ref, v_ref, ct_ref, o_ref):
    hp = pl.program_id(1)
    seq_len = q_ref.shape[0]
    tq = ATTN_TILE
    tk = tq
    nq = seq_len // tq
    N = HEAD_DIM
    nt = lambda a, b: lax.dot_general(a, b, (((1,), (1,)), ((), ())), preferred_element_type=F32)
    mm = lambda a, b: jnp.dot(a, b, preferred_element_type=F32)

    lane = lax.broadcasted_iota(jnp.int32, (tq, 2 * N), 1)
    vrow = lax.broadcasted_iota(jnp.int32, (2 * N, tk), 0)
    orow = lax.broadcasted_iota(jnp.int32, (2 * N, tq), 0)
    kpos = lax.broadcasted_iota(jnp.int32, (tk, tq), 0)
    qpos = lax.broadcasted_iota(jnp.int32, (tk, tq), 1)
    causal = kpos <= qpos
    ident_v = (lax.broadcasted_iota(jnp.int32, (2 * N, 2 * N), 0)
               == lax.broadcasted_iota(jnp.int32, (2 * N, 2 * N), 1)).astype(BF16)

    zero_q = jnp.zeros((tq, 2 * N), BF16)

    def head_pair(hl):
        lanes = slice(hl * 2 * N, (hl + 1) * 2 * N)
        first_head = 2 * (hp * ATTN_PAIRS_PER_STEP + hl)
        c_parts = [[part.astype(F32) for part in _split3(ct_ref[0, pl.ds(first_head + u, 1), :] * LOG2E)]
                   for u in range(2)]
        ones = jnp.ones((3, tq), F32)
        pad = jnp.zeros((2 * N - 6, tq), F32)
        m = [[None, None] for _ in range(nq)]
        l = [[None, None] for _ in range(nq)]
        acc = [None] * nq
        keys, queries, values = {}, {}, {}

        def key_side(j):
            if j not in keys:
                ks = slice(j * tk, (j + 1) * tk)
                k_j = k_ref[ks, lanes]
                extra = [jnp.concatenate([-part[:, ks] for part in c_parts[u]] + [ones, pad], axis=0).T.astype(BF16)
                         for u in range(2)]
                keys[j] = [jnp.concatenate([k_j, extra[u]], axis=1) for u in range(2)]
            return keys[j]

        def query_side(qi):
            if qi not in queries:
                qs = slice(qi * tq, (qi + 1) * tq)
                q = q_ref[qs, lanes]
                extra = [jnp.concatenate([ones] + [part[:, qs] for part in c_parts[u]] + [pad], axis=0).T.astype(BF16)
                         for u in range(2)]
                queries[qi] = [jnp.concatenate([jnp.where((lane < N) == (u == 0), q, zero_q), extra[u]], axis=1)
                               for u in range(2)]
            return queries[qi]

        def scores(j, qi):
            k_aug, q_aug = key_side(j), query_side(qi)
            t = [nt(k_aug[u], q_aug[u]) for u in range(2)]
            return [jnp.where(causal, tu, -jnp.inf) for tu in t] if qi == j else t

        def absorb(j, qi, t):
            if j not in values:
                v_t = nt(ident_v, v_ref[j * tk:(j + 1) * tk, lanes]).astype(BF16)
                values[j] = [jnp.where(vrow < N, v_t, jnp.zeros_like(v_t)),
                             jnp.where(vrow < N, jnp.zeros_like(v_t), v_t)]
            v_heads = values[j]
            rmax = [jnp.max(t[u], axis=0, keepdims=True) for u in range(2)]
            m_new = rmax if j == 0 else [jnp.maximum(m[qi][u], rmax[u]) for u in range(2)]
            p = [jnp.exp2(t[u] - m_new[u]) for u in range(2)]
            psum = [jnp.sum(p[u], axis=0, keepdims=True) for u in range(2)]
            pv = mm(v_heads[0], p[0].astype(BF16)) + mm(v_heads[1], p[1].astype(BF16))
            if j == 0:
                acc[qi] = pv
                l[qi] = psum
            else:
                alpha = [jnp.exp2(m[qi][u] - m_new[u]) for u in range(2)]
                acc[qi] = acc[qi] * jnp.where(orow < N, alpha[0], alpha[1]) + pv
                l[qi] = [alpha[u] * l[qi][u] + psum[u] for u in range(2)]
            m[qi] = m_new
            if j == qi:
                o_t = acc[qi] / jnp.where(orow < N, l[qi][0], l[qi][1])
                o_ref[qi * tq:(qi + 1) * tq, lanes] = o_t.T.astype(o_ref.dtype)

        return scores, absorb

    pairs = [head_pair(hl) for hl in range(ATTN_PAIRS_PER_STEP)]
    items = [(hl, j, qi) for j in range(nq) for qi in range(j, nq) for hl in range(ATTN_PAIRS_PER_STEP)]
    score_of = lambda item: pairs[item[0]][0](*item[1:])
    queue = [score_of(item) for item in items[:ATTN_LOOKAHEAD]]
    for n, (hl, j, qi) in enumerate(items):
        t_cur = queue.pop(0)
        if n + ATTN_LOOKAHEAD < len(items):
            queue.append(score_of(items[n + ATTN_LOOKAHEAD]))
        pairs[hl][1](j, qi, t_cur)


def _fox_attn(q, k, v, c_t, batch, seq_len):
    n_tok, d = q.shape
    width = ATTN_PAIRS_PER_STEP * 2 * HEAD_DIM
    seq = pl.BlockSpec((seq_len, width), lambda b, hp: (b, hp))
    return pl.pallas_call(
        _fox_attn_kernel,
        grid=(batch, d // width),
        in_specs=[seq, seq, seq, pl.BlockSpec((1, N_HEADS, seq_len), lambda b, hp: (b, 0, 0))],
        out_specs=seq,
        out_shape=jax.ShapeDtypeStruct((n_tok, d), BF16),
        compiler_params=pltpu.CompilerParams(
            dimension_semantics=("parallel", "arbitrary"),
            vmem_limit_bytes=_vmem_limit(2 * 4 * seq_len * width * 2 + 128 * ATTN_TILE * ATTN_TILE * 4)),
        name="fox_attn",
    )(q, k, v, c_t)


def kernel(x, rw_mix, rw_wr, rw_wk, rw_wv, rw_wo, rw_w0, rw_w1, rw_w2, rw_a0, rw_a1, rw_a2, rw_g1, rw_g2,
           rw_kk, rw_ka, rw_rk, rw_gn_g, rw_gn_b, fx_w_in, fx_b_f, fx_wo, router_w, router_bias,
           moe_w_gate, moe_w_up, moe_w_down, ln_g, ln_b):
    batch, seq_len, d = x.shape
    n_tok = batch * seq_len
    bf = lambda w: w.astype(BF16)
    row = lambda w: w.reshape(1, -1)
    router_wt = router_w.T
    router_b = router_bias.reshape(N_EXPERTS, 1)
    h = x.reshape(n_tok, d)
    sorted_rows = None

    for i in range(DEPTH):
        j = i // 2
        if i % 2 == 0:
            r, lw, k, v, a, g = _rwkv_proj(
                h, seq_len, rw_mix[j], bf(rw_wr[j]), bf(rw_wk[j]), bf(rw_wv[j]), bf(rw_w1[j]), bf(rw_w2[j]),
                bf(rw_a1[j]), bf(rw_a2[j]), bf(rw_g1[j]), bf(rw_g2[j]), row(rw_w0[j]), row(rw_a0[j]))
            act = _rwkv_recur(r, lw, k, v, a, g, batch, seq_len, row(rw_kk[j]), row(rw_ka[j]), row(rw_rk[j]),
                              row(rw_gn_g[j]), row(rw_gn_b[j]))
            wo = bf(rw_wo[j])
        else:
            w_in = fx_w_in[j]
            scale = HEAD_DIM ** -0.5
            q, k, v, c = _fox_proj(h, batch, seq_len, bf(w_in[:, :d] * scale), bf(w_in[:, d:2 * d]),
                                   bf(w_in[:, 2 * d:3 * d]), w_in[:, 3 * d:], row(fx_b_f[j]))
            c_t = c.reshape(batch, seq_len, N_HEADS).transpose(0, 2, 1)
            act = _fox_attn(q, k, v, c_t, batch, seq_len)
            wo = bf(fx_wo[j])
        h, h_packed, route, counts = _mixer_epilogue(act, wo, h, row(ln_g[i, 0]), row(ln_b[i, 0]),
                                                     router_wt, router_b)
        h, sorted_rows = _moe_layer(h, h_packed, route, counts, i, moe_w_gate, moe_w_up, moe_w_down,
                                    row(ln_g[i, 1]), row(ln_b[i, 1]), sorted_rows)
    return h.reshape(batch, seq_len, d)
```

```python
import functools
import math

import jax
import jax.numpy as jnp
from jax import lax
from jax.experimental import pallas as pl
from jax.experimental.pallas import tpu as pltpu

D_MODEL = 1024
HEAD_DIM = 64
N_HEADS = D_MODEL // HEAD_DIM
N_EXPERTS = 16
N_GROUPS = 4
EXPERTS_PER_GROUP = N_EXPERTS // N_GROUPS
N_PAIRS = 6
N_CLASSES = N_GROUPS * N_PAIRS
CLASS_ROWS = 32
_PAIR_SLOT_A = (0, 2, 2, 3, 3, 3)
_PAIR_SLOT_B = (1, 1, 0, 0, 1, 2)
GN_EPS = 64e-5
LN_EPS = 1e-5
DEPTH = 2
ALPHA = (2 * DEPTH) ** 0.25
LOG2E = math.log2(math.e)

V7X_LANES = 128
V7X_VMEM_BYTES = 64 * 2 ** 20

V7X_MXU_DIM = 256

RWKV_CHUNK = 64
RWKV_CHUNKS_PER_STEP = 4
RWKV_HEADS_PER_TILE = V7X_MXU_DIM // HEAD_DIM
PROJ_TILE = 512
PROJ_SUBTILE = 256
EPI_TILE = 1024
EPI_SUBTILE = 256
ATTN_TILE = 256
ATTN_PAIRS_PER_STEP = 1
ATTN_LOOKAHEAD = 2
MOE_BLOCK = 256
DISPATCH_TILE = 4096
COMBINE_TILE = 1024
DMA_ISSUE_UNROLL = 8

F32 = jnp.float32
BF16 = jnp.bfloat16


def _vmem_limit(n_bytes):
    return int(min(n_bytes + 16 * 2 ** 20, V7X_VMEM_BYTES - 8 * 2 ** 20))


def _split3(x):
    hi = x.astype(BF16)
    r1 = x - hi.astype(F32)
    mid = r1.astype(BF16)
    lo = (r1 - mid.astype(F32)).astype(BF16)
    return hi, mid, lo


def _pack_halves(x):
    half = x.shape[1] // 2
    bits = lambda v: lax.bitcast_convert_type(v.astype(BF16).astype(F32), jnp.uint32)
    return (bits(x[:, :half]) >> 16) | (bits(x[:, half:]) & jnp.uint32(0xFFFF0000))


def _unpack_halves(w):
    lo = lax.bitcast_convert_type(w << 16, F32)
    hi = lax.bitcast_convert_type(w & jnp.uint32(0xFFFF0000), F32)
    return lo, hi


def _layer_norm_rows(x, g, b):
    mu = jnp.mean(x, axis=-1, keepdims=True)
    xc = x - mu
    var = jnp.mean(xc * xc, axis=-1, keepdims=True)
    return xc * lax.rsqrt(var + LN_EPS) * g + b


def _rwkv_proj_kernel(x_ref, xp_ref, mix_ref, wr_ref, wk_ref, wv_ref, w1_ref, w2_ref, a1_ref, a2_ref,
                      g1_ref, g2_ref, w0_ref, a0_ref,
                      r_ref, lw_ref, k_ref, v_ref, a_ref, g_ref, *, tiles_per_seq):
    i = pl.program_id(0)
    x = x_ref[...]
    tt = x.shape[0]
    first = (i % tiles_per_seq) == 0
    prev_row = jnp.where(first, 0.0, xp_ref[7:8, :])
    row = lax.broadcasted_iota(jnp.int32, (tt, 1), 0)
    xprev = jnp.where(row == 0, prev_row, pltpu.roll(x, 1, axis=0))
    xx = xprev - x
    mix = mix_ref[...]
    subs = [slice(u * PROJ_SUBTILE, (u + 1) * PROJ_SUBTILE) for u in range(tt // PROJ_SUBTILE)]
    each = lambda fn, *lists: [fn(*args) for args in zip(*lists)]
    mixed = lambda j: each(lambda sl: (x[sl] + xx[sl] * mix[j:j + 1, :]).astype(BF16), subs)
    xr, xw, xk, xv, xa, xg = (mixed(j) for j in range(6))
    mm = lambda a, w_ref: jnp.dot(a, w_ref[...], preferred_element_type=F32)
    w_mid = each(lambda a: mm(a, w1_ref), xw)
    a_mid = each(lambda a: mm(a, a1_ref), xa)
    g_mid = each(lambda a: mm(a, g1_ref), xg)
    for sl, a in zip(subs, xr):
        r_ref[sl, :] = mm(a, wr_ref).astype(r_ref.dtype)
    for sl, a in zip(subs, xk):
        k_ref[sl, :] = mm(a, wk_ref).astype(k_ref.dtype)
    for sl, a in zip(subs, xv):
        v_ref[sl, :] = mm(a, wv_ref).astype(v_ref.dtype)
    z = each(lambda t: w0_ref[...] + mm(jnp.tanh(t).astype(BF16), w2_ref), w_mid)
    for sl, t in zip(subs, a_mid):
        a_ref[sl, :] = jax.nn.sigmoid(a0_ref[...] + mm(t.astype(BF16), a2_ref)).astype(a_ref.dtype)
    for sl, t in zip(subs, g_mid):
        g_ref[sl, :] = mm(jax.nn.sigmoid(t).astype(BF16), g2_ref).astype(g_ref.dtype)
    for sl, zs in zip(subs, z):
        lw_ref[sl, :] = -jnp.exp(-jax.nn.softplus(-zs) - 0.5)


def _rwkv_proj(x2, seq_len, mix, wr, wk, wv, w1, w2, a1, a2, g1, g2, w0, a0):
    n_tok, d = x2.shape
    tt = PROJ_TILE
    n_tiles = n_tok // tt
    tiles_per_seq = seq_len // tt
    tile = pl.BlockSpec((tt, d), lambda i: (i, 0))
    prev = pl.BlockSpec((8, d), lambda i: (jnp.maximum(i * (tt // 8) - 1, 0), 0))
    full = lambda arr: pl.BlockSpec(arr.shape, lambda i: (0,) * arr.ndim)
    weights = (mix, wr, wk, wv, w1, w2, a1, a2, g1, g2, w0, a0)
    out_dtypes = (BF16, F32, BF16, BF16, BF16, BF16)
    w_bytes = sum(int(w.size) * w.dtype.itemsize for w in weights)
    return pl.pallas_call(
        functools.partial(_rwkv_proj_kernel, tiles_per_seq=tiles_per_seq),
        grid=(n_tiles,),
        in_specs=[tile, prev] + [full(w) for w in weights],
        out_specs=[tile] * 6,
        out_shape=[jax.ShapeDtypeStruct((n_tok, d), dt) for dt in out_dtypes],
        compiler_params=pltpu.CompilerParams(
            dimension_semantics=("parallel",),
            vmem_limit_bytes=_vmem_limit(2 * w_bytes + 2 * tt * d * (4 + 4 + 5 * 2) + 8 * tt * d * 4)),
        name="rwkv_proj",
    )(x2, x2, *weights)


def _rwkv_recur_kernel(r_ref, lw_ref, k_ref, v_ref, a_ref, g_ref, kk_ref, ka_ref, rk_ref, gng_ref, gnb_ref,
                       o_ref, s_ref):
    c = pl.program_id(1)
    C = RWKV_CHUNK
    N = HEAD_DIM

    @pl.when(c == 0)
    def _():
        s_ref[...] = jnp.zeros_like(s_ref)

    nt = lambda p, q: lax.dot_general(p, q, (((1,), (1,)), ((), ())), preferred_element_type=F32)
    mm = lambda p, q: jnp.dot(p, q, preferred_element_type=F32)
    each = lambda fn, *lists: [fn(*args) for args in zip(*lists)]
    bf = lambda x: x.astype(BF16)

    ti = lax.broadcasted_iota(jnp.int32, (C, C), 0)
    tj = lax.broadcasted_iota(jnp.int32, (C, C), 1)
    tril = (tj <= ti).astype(BF16)

    def chunk_terms(rows):
        lw = lw_ref[rows, :]
        cum = sum(mm(tril, part) for part in _split3(lw))
        rho = cum[C // 2 - 1:C // 2, :]
        last = cum[C - 1:C, :]
        r = r_ref[rows, :].astype(F32)
        k = k_ref[rows, :].astype(F32)
        a = a_ref[rows, :].astype(F32)
        k_mod = k * (1.0 + (a - 1.0) * ka_ref[...])
        return dict(e_q=jnp.exp(cum - rho), e_qx=jnp.exp(cum - lw - rho), e_k=jnp.exp(rho - cum),
                    e_end=jnp.exp(last - cum), e_rho=jnp.exp(rho), d_end=jnp.exp(last),
                    r=r, a=a, v=v_ref[rows, :].astype(F32), kk_raw=k * kk_ref[...], k_mod=k_mod,
                    rkk=r * k_mod * rk_ref[...])

    n_ch = r_ref.shape[0] // C
    terms = [chunk_terms(slice(ci * C, (ci + 1) * C)) for ci in range(n_ch)]

    G = RWKV_HEADS_PER_TILE
    R = G * C
    GW = G * N
    er = lax.broadcasted_iota(jnp.int32, (R, GW), 0)
    ec = lax.broadcasted_iota(jnp.int32, (R, GW), 1)
    blk = (er // C) == (ec // N)
    strict = (ec % C) < (er % C)
    incl = (ec % C) <= (er % C)
    eye = (er == ec).astype(F32)
    ones_blk = blk.astype(BF16)

    def expand(x):
        return jnp.where(blk, jnp.concatenate([x] * G, axis=0), 0.0).astype(BF16)

    expand_t = lambda x: bf(jnp.where(blk, jnp.concatenate([x] * G, axis=0), 0.0).T)

    def head_sums(xs):
        parts = _split3(jnp.concatenate(xs, axis=0))[:2]
        tot = mm(jnp.concatenate(parts, axis=0), ones_blk)
        n = len(xs) * C
        tot = tot[:n] + tot[n:]
        return [tot[u * C:(u + 1) * C] for u in range(len(xs))]

    n_grp = N_HEADS // G
    groups = [slice(gi * GW, (gi + 1) * GW) for gi in range(n_grp)]
    units = [(tm, sl) for tm in terms for sl in groups]

    def independent(us, out):
        u_tm = [tm for tm, _ in us]
        u_sl = [sl for _, sl in us]
        pre = head_sums([tm["kk_raw"][:, sl] * tm["kk_raw"][:, sl] for tm, sl in us] + [tm["rkk"][:, sl] for tm, sl in us])
        kk_ss, out["rkk_sum"] = pre[:len(us)], pre[len(us):]
        yield
        kk_n = each(lambda tm, sl, ss: tm["kk_raw"][:, sl] * lax.rsqrt(jnp.maximum(ss, 1e-24)), u_tm, u_sl, kk_ss)
        b_n = each(lambda tm, sl, kk_g: kk_g * tm["a"][:, sl], u_tm, u_sl, kk_n)
        kk_q = each(lambda tm, sl, kk_g: kk_g * tm["e_qx"][:, sl], u_tm, u_sl, kk_n)
        r_q = each(lambda tm, sl: tm["r"][:, sl] * tm["e_q"][:, sl], u_tm, u_sl)
        q2 = each(lambda x, y: jnp.concatenate([expand(x), expand(y)], axis=0), kk_q, r_q)
        a_k = each(lambda tm, sl, q: nt(q, expand(tm["k_mod"][:, sl] * tm["e_k"][:, sl])), u_tm, u_sl, q2)
        yield
        a_b = each(lambda tm, sl, q, b_g: nt(q, expand(b_g * tm["e_k"][:, sl])), u_tm, u_sl, q2, b_n)
        yield
        a_kk = each(lambda x: jnp.where(strict, x[:R], 0.0), a_k)
        a_rk = each(lambda x: jnp.where(incl, x[R:], 0.0), a_k)
        a_kb = each(lambda x: jnp.where(strict, x[:R], 0.0), a_b)
        out["a_rb"] = each(lambda x: bf(jnp.where(incl, x[R:], 0.0)), a_b)
        t_inv = each(lambda x: eye - x, a_kb)
        p = each(lambda x: mm(bf(-x), bf(-x)), a_kb)
        yield
        for _ in range(int(math.log2(C)) - 2):
            both = each(lambda pg, tg: mm(bf(pg), jnp.concatenate([bf(pg), bf(tg)], axis=1)), p, t_inv)
            p = each(lambda x: x[:, :R], both)
            t_inv = each(lambda tg, x: tg + x[:, R:], t_inv, both)
            yield
        out["t_inv"] = each(lambda pg, tg: bf(tg + mm(bf(pg), bf(tg))), p, t_inv)
        yield
        out["kd_t"] = each(lambda tm, sl: expand_t(tm["k_mod"][:, sl] * tm["e_end"][:, sl]), u_tm, u_sl)
        out["bd_t"] = each(lambda tm, sl, b_g: expand_t(b_g * tm["e_end"][:, sl]), u_tm, u_sl, b_n)
        out["v_e"] = each(lambda tm, sl: expand(tm["v"][:, sl]), u_tm, u_sl)
        out["av"] = each(lambda x, y, ve: mm(bf(jnp.concatenate([x, y], axis=0)), ve), a_kk, a_rk, out["v_e"])
        out["q2_abs"] = each(lambda tm, sl, x, y: jnp.concatenate([expand(x * tm["e_rho"][:, sl]),
                                                                   expand(y * tm["e_rho"][:, sl])], axis=0),
                             u_tm, u_sl, kk_q, r_q)
        yield

    state = {"st": [s_ref[gi] for gi in range(n_grp)]}
    ys = [None] * len(units)

    def dependent(ci, res, lo):
        span = slice(lo, lo + n_grp)
        qs = each(lambda q, s: mm(q, bf(s)), res["q2_abs"][span], state["st"])
        yield
        sa_e = each(lambda tg, q, x: bf(mm(tg, bf(q[:R] + x[:R]))), res["t_inv"][span], qs, res["av"][span])
        yield
        y_e = each(lambda q, x, arb, sa: q[R:] + x[R:] - mm(arb, sa), qs, res["av"][span], res["a_rb"][span], sa_e)
        ys[ci * n_grp:(ci + 1) * n_grp] = each(lambda x: sum(x[u * C:(u + 1) * C] for u in range(G)), y_e)
        yield
        upd = each(lambda kt, bt, ve, sa: mm(jnp.concatenate([kt, -bt], axis=1), jnp.concatenate([ve, sa], axis=0)),
                   res["kd_t"][span], res["bd_t"][span], res["v_e"][span], sa_e)
        d_col = [jnp.broadcast_to(terms[ci]["d_end"][:, sl], (GW, GW)).T for sl in groups]
        state["st"] = each(lambda s, dc, up: s * dc + up, state["st"], d_col, upd)
        yield

    half = (n_ch // 2) * n_grp
    first, second = {}, {}
    for _ in independent(units[:half], first):
        pass
    chain = [stage for ci in range(n_ch // 2) for stage in [dependent(ci, first, ci * n_grp)]]
    pending = iter(())
    todo = list(chain)

    def advance():
        nonlocal pending
        while True:
            try:
                next(pending)
                return True
            except StopIteration:
                if not todo:
                    return False
                pending = todo.pop(0)

    for _ in independent(units[half:], second):
        advance()
    while advance():
        pass
    for ci in range(n_ch // 2, n_ch):
        for _ in dependent(ci, second, (ci - n_ch // 2) * n_grp):
            pass
    for gi in range(n_grp):
        s_ref[gi] = state["st"][gi]
    rkk_sum = first["rkk_sum"] + second["rkk_sum"]

    inv_n = 1.0 / N
    ycs = [y - mu * inv_n for y, mu in zip(ys, head_sums(ys))]
    sqs = head_sums([yc * yc for yc in ycs])
    for ui, (tm, sl) in enumerate(units):
        rows = slice((ui // n_grp) * C, (ui // n_grp + 1) * C)
        yn = ycs[ui] * lax.rsqrt(sqs[ui] * inv_n + GN_EPS) * gng_ref[:, sl] + gnb_ref[:, sl]
        o_ref[rows, sl] = ((yn + rkk_sum[ui] * tm["v"][:, sl]) * g_ref[rows, sl]).astype(o_ref.dtype)


def _rwkv_recur(r, lw, k, v, a, g, batch, seq_len, k_k, k_a, r_k, gn_g, gn_b):
    n_tok, d = r.shape
    C = RWKV_CHUNK * RWKV_CHUNKS_PER_STEP
    nc = seq_len // C
    tile = pl.BlockSpec((C, d), lambda b, c: (b * nc + c, 0))
    vec = pl.BlockSpec((1, d), lambda b, c: (0, 0))
    return pl.pallas_call(
        _rwkv_recur_kernel,
        grid=(batch, nc),
        in_specs=[tile] * 6 + [vec] * 5,
        out_specs=tile,
        out_shape=jax.ShapeDtypeStruct((n_tok, d), BF16),
        scratch_shapes=[pltpu.VMEM((N_HEADS // RWKV_HEADS_PER_TILE, V7X_MXU_DIM, V7X_MXU_DIM), F32)],
        compiler_params=pltpu.CompilerParams(
            dimension_semantics=("parallel", "arbitrary"),
            vmem_limit_bytes=_vmem_limit(2 * 7 * C * d * 4 + 32 * C * d * 4 + 64 * V7X_MXU_DIM ** 2 * 4)),
        name="rwkv_recur",
    )(r, lw, k, v, a, g, k_k, k_a, r_k, gn_g, gn_b)


def _rank_among(vals, i):
    cnt = 0
    for j, vj in enumerate(vals):
        if j == i:
            continue
        before = (vj >= vals[i]) if j < i else (vj > vals[i])
        cnt = cnt + before.astype(jnp.int32)
    return cnt


def _pick(ranks, vals, want):
    out = vals[0]
    for rk, vl in zip(ranks[1:], vals[1:]):
        out = jnp.where(rk == want, vl, out)
    return out


def _epilogue_kernel(act_ref, wo_ref, res_ref, lng_ref, lnb_ref, rwt_ref, rb_ref,
                     h_ref, hp_ref, route_ref, cnt_ref, base_ref):
    i = pl.program_id(0)

    @pl.when(i == 0)
    def _():
        base_ref[...] = jnp.zeros_like(base_ref)

    tt = EPI_SUBTILE
    subs = [slice(u * tt, (u + 1) * tt) for u in range(act_ref.shape[0] // tt)]
    each = lambda fn, *lists: [fn(*args) for args in zip(*lists)]
    nt = lambda a, b: lax.dot_general(a, b, (((1,), (1,)), ((), ())), preferred_element_type=F32)

    wh, wm, _ = _split3(rwt_ref[...])

    def router_logits(hs):
        hh, hm, _ = _split3(hs)
        return nt(wh, hh) + (nt(wh, hm) + nt(wm, hh))

    def project(sl):
        return jnp.dot(act_ref[sl, :], wo_ref[...], preferred_element_type=F32)

    def normalise(sl, mx):
        hs = _layer_norm_rows(ALPHA * res_ref[sl, :] + mx, lng_ref[...], lnb_ref[...])
        h_ref[sl, :] = hs
        hp_ref[sl, :hs.shape[1] // 2] = _pack_halves(hs)
        return jax.nn.sigmoid(router_logits(hs))

    s = []
    mixed = project(subs[0])
    for u in range(len(subs)):
        nxt = project(subs[u + 1]) if u + 1 < len(subs) else None
        s.append(normalise(subs[u], mixed))
        mixed = nxt

    def select(sg):
        s_sel = sg + rb_ref[...]
        rows = [s_sel[e:e + 1, :] for e in range(N_EXPERTS)]
        grp_score, grp_i0, grp_i1 = [], [], []
        for gi in range(N_GROUPS):
            vals = rows[gi * EXPERTS_PER_GROUP:(gi + 1) * EXPERTS_PER_GROUP]
            ranks = [_rank_among(vals, q) for q in range(EXPERTS_PER_GROUP)]
            idx = [jnp.full_like(ranks[0], q) for q in range(EXPERTS_PER_GROUP)]
            grp_score.append(_pick(ranks, vals, 0) + _pick(ranks, vals, 1))
            grp_i0.append(_pick(ranks, idx, 0))
            grp_i1.append(_pick(ranks, idx, 1))
        g_ranks = [_rank_among(grp_score, q) for q in range(N_GROUPS)]
        gidx = [jnp.full_like(g_ranks[0], q) for q in range(N_GROUPS)]
        g_star = _pick(g_ranks, gidx, 0)
        e0 = g_star * EXPERTS_PER_GROUP + _pick(g_ranks, grp_i0, 0)
        e1 = g_star * EXPERTS_PER_GROUP + _pick(g_ranks, grp_i1, 0)
        return e0, e1

    picked = each(select, s)
    e_iota = lax.broadcasted_iota(jnp.int32, (N_EXPERTS, tt), 0)

    def gates(sg, pk):
        gate0 = jnp.sum(jnp.where(e_iota == pk[0], sg, 0.0), axis=0, keepdims=True)
        gate1 = jnp.sum(jnp.where(e_iota == pk[1], sg, 0.0), axis=0, keepdims=True)
        denom = gate0 + gate1
        return gate0 / denom, gate1 / denom

    gate = each(gates, s, picked)

    def classify(pk):
        e0, e1 = pk
        grp = lax.shift_right_logical(e0, 2)
        l0 = e0 - grp * EXPERTS_PER_GROUP
        l1 = e1 - grp * EXPERTS_PER_GROUP
        lo, hi = jnp.minimum(l0, l1), jnp.maximum(l0, l1)
        pair = jnp.where(hi == 1, 0, jnp.where(hi == 2, jnp.where(lo == 1, 1, 2),
                                               jnp.where(lo == 0, 3, jnp.where(lo == 1, 4, 5))))
        slot_a = jnp.where(pair == 0, 0, jnp.where(pair <= 2, 2, 3))
        return grp * N_PAIRS + pair, l0 != slot_a

    classes = each(classify, picked)
    c_iota = lax.broadcasted_iota(jnp.int32, (cnt_ref.shape[0], tt), 0)
    hit = each(lambda cl: c_iota == cl[0], classes)
    member = each(lambda hc: jnp.where(hc, 1.0, 0.0), hit)
    ui = lax.broadcasted_iota(jnp.int32, (tt, tt), 0)
    uj = lax.broadcasted_iota(jnp.int32, (tt, tt), 1)
    before = (ui < uj).astype(BF16)
    prefix = each(lambda mb: jnp.dot(mb.astype(BF16), before, preferred_element_type=F32), member)
    base = base_ref[:, 0:1]
    for u, sl in enumerate(subs):
        rank = jnp.sum(jnp.where(hit[u], prefix[u] + base, 0.0), axis=0, keepdims=True)
        route_ref[:, sl] = jnp.concatenate([classes[u][0], rank.astype(jnp.int32)], axis=0)
        base = base + jnp.sum(member[u], axis=1, keepdims=True)
    base_ref[...] = jnp.broadcast_to(base, base_ref.shape)
    cnt_ref[...] = jnp.broadcast_to(base, cnt_ref.shape).astype(jnp.int32)

    half = hp_ref.shape[1] - V7X_LANES
    for sl, gt, cl in zip(subs, gate, classes):
        g_a = jnp.where(cl[1], gt[1], gt[0])
        g_b = jnp.where(cl[1], gt[0], gt[1])
        gpad = jnp.concatenate([g_a, g_b, jnp.zeros((V7X_LANES - 2, tt), F32)], axis=0)
        hp_ref[sl, half:] = lax.bitcast_convert_type(gpad.T, jnp.uint32)


def _mixer_epilogue(act, wo, res, ln_g, ln_b, router_wt, router_bias):
    n_tok, d = res.shape
    k_in = act.shape[1]
    tt = EPI_TILE
    n_tiles = n_tok // tt
    full = lambda arr: pl.BlockSpec(arr.shape, lambda i: (0,) * arr.ndim)
    return pl.pallas_call(
        _epilogue_kernel,
        grid=(n_tiles,),
        in_specs=[pl.BlockSpec((tt, k_in), lambda i: (i, 0)), full(wo), pl.BlockSpec((tt, d), lambda i: (i, 0)),
                  full(ln_g), full(ln_b), full(router_wt), full(router_bias)],
        out_specs=[pl.BlockSpec((tt, d), lambda i: (i, 0)),
                   pl.BlockSpec((tt, d // 2 + V7X_LANES), lambda i: (i, 0)),
                   pl.BlockSpec((2, tt), lambda i: (0, i)),
                   pl.BlockSpec((CLASS_ROWS, V7X_LANES), lambda i: (0, 0))],
        out_shape=[jax.ShapeDtypeStruct((n_tok, d), F32),
                   jax.ShapeDtypeStruct((n_tok, d // 2 + V7X_LANES), jnp.uint32),
                   jax.ShapeDtypeStruct((2, n_tok), jnp.int32),
                   jax.ShapeDtypeStruct((CLASS_ROWS, V7X_LANES), jnp.int32)],
        scratch_shapes=[pltpu.VMEM((CLASS_ROWS, V7X_LANES), F32)],
        compiler_params=pltpu.CompilerParams(
            dimension_semantics=("arbitrary",),
            vmem_limit_bytes=_vmem_limit(2 * int(wo.size) * 2 + 2 * tt * (k_in * 2 + 2 * d * 4) + 16 * tt * d * 4)),
        name="mixer_epilogue",
    )(act, wo, res, ln_g, ln_b, router_wt, router_bias)


def _dispatch_kernel(dest_ref, h_ref, xs_in_ref, xs_ref, sem):
    del xs_in_ref
    tt = h_ref.shape[0]

    def row_copy(t):
        return pltpu.make_async_copy(h_ref.at[pl.ds(t, 1)], xs_ref.at[pl.ds(dest_ref[0, 0, t], 1)], sem)

    def issue(t0, carry):
        for u in range(DMA_ISSUE_UNROLL):
            row_copy(t0 * DMA_ISSUE_UNROLL + u).start()
        return carry

    lax.fori_loop(0, tt // DMA_ISSUE_UNROLL, issue, 0)
    pltpu.make_async_copy(h_ref, xs_ref.at[pl.ds(0, tt)], sem).wait()


def _moe_dispatch(h, dest_tiles, n_rows, initialised=None):
    n_tok, d = h.shape
    tt = dest_tiles.shape[2]
    n_tiles = n_tok // tt
    zeros = jnp.zeros((n_rows, d), h.dtype) if initialised is None else initialised
    return pl.pallas_call(
        _dispatch_kernel,
        grid=(n_tiles,),
        in_specs=[pl.BlockSpec((1, 1, tt), lambda i: (i, 0, 0), memory_space=pltpu.SMEM),
                  pl.BlockSpec((tt, d), lambda i: (i, 0)),
                  pl.BlockSpec(memory_space=pl.ANY)],
        out_specs=pl.BlockSpec(memory_space=pl.ANY),
        out_shape=jax.ShapeDtypeStruct((n_rows, d), h.dtype),
        scratch_shapes=[pltpu.SemaphoreType.DMA(())],
        input_output_aliases={2: 0},
        compiler_params=pltpu.CompilerParams(dimension_semantics=("arbitrary",),
                                             vmem_limit_bytes=_vmem_limit(2 * tt * d * 4)),
        name="moe_dispatch",
    )(dest_tiles, h, zeros)


def _ffn_kernel(bea_ref, beb_ref, nblk_ref, x_ref, wga_ref, wua_ref, wda_ref, wgb_ref, wub_ref, wdb_ref,
                y_ref, wg_bf, wu_bf, wd_bf):
    j = pl.program_id(0)
    active = j < nblk_ref[0]
    prev = jnp.maximum(j - 1, 0)
    slots = ((bea_ref, wga_ref, wua_ref, wda_ref), (beb_ref, wgb_ref, wub_ref, wdb_ref))

    for s, (be_ref, wg_ref, wu_ref, wd_ref) in enumerate(slots):
        @pl.when(active & ((j == 0) | (be_ref[j] != be_ref[prev])))
        def _():
            wg_bf[s] = wg_ref[0, 0].astype(BF16)
            wu_bf[s] = wu_ref[0, 0].astype(BF16)
            wd_bf[s] = wd_ref[0, 0].astype(BF16)

    @pl.when(active)
    def _():
        half = y_ref.shape[1]
        mm = lambda a, b: jnp.dot(a, b, preferred_element_type=F32)
        x_lo, x_hi = (v.astype(BF16) for v in _unpack_halves(x_ref[:, :half]))
        gates = lax.bitcast_convert_type(x_ref[:, half:], F32)
        gate = [mm(x_lo, wg_bf[s, :half, :]) + mm(x_hi, wg_bf[s, half:, :]) for s in range(2)]
        up = [mm(x_lo, wu_bf[s, :half, :]) + mm(x_hi, wu_bf[s, half:, :]) for s in range(2)]
        hid = [((g * jax.nn.sigmoid(g)) * u).astype(BF16) for g, u in zip(gate, up)]
        y = [mm(hid[s], wd_bf[s]) for s in range(2)]
        y_ref[...] = _pack_halves(y[0] * gates[:, 0:1] + y[1] * gates[:, 1:2])

    @pl.when(jnp.logical_not(active))
    def _():
        y_ref[...] = jnp.zeros_like(y_ref)


def _moe_ffn(xs, block_ea, block_eb, n_used, layer, wg, wu, wd):
    n_rows, width = xs.shape
    half = width - V7X_LANES
    d = 2 * half
    blk = MOE_BLOCK
    n_blocks = n_rows // blk
    de = wg.shape[3]
    expert_a = lambda j, ea, eb, nb: (layer, ea[j], 0, 0)
    expert_b = lambda j, ea, eb, nb: (layer, eb[j], 0, 0)
    rows = lambda j, ea, eb, nb: (j, 0)
    grid_spec = pltpu.PrefetchScalarGridSpec(
        num_scalar_prefetch=3,
        grid=(n_blocks,),
        in_specs=[pl.BlockSpec((blk, width), rows),
                  pl.BlockSpec((1, 1, d, de), expert_a), pl.BlockSpec((1, 1, d, de), expert_a),
                  pl.BlockSpec((1, 1, de, d), expert_a),
                  pl.BlockSpec((1, 1, d, de), expert_b), pl.BlockSpec((1, 1, d, de), expert_b),
                  pl.BlockSpec((1, 1, de, d), expert_b)],
        out_specs=pl.BlockSpec((blk, half), rows),
        scratch_shapes=[pltpu.VMEM((2, d, de), BF16), pltpu.VMEM((2, d, de), BF16), pltpu.VMEM((2, de, d), BF16)],
    )
    return pl.pallas_call(
        _ffn_kernel,
        grid_spec=grid_spec,
        out_shape=jax.ShapeDtypeStruct((n_rows, half), jnp.uint32),
        compiler_params=pltpu.CompilerParams(
            dimension_semantics=("arbitrary",),
            vmem_limit_bytes=_vmem_limit(2 * 3 * d * de * (2 * 4 + 2) + 6 * blk * d * 4 + 8 * blk * de * 4)),
        name="moe_ffn",
    )(block_ea, block_eb, n_used, xs, wg, wu, wd, wg, wu, wd)


def _combine_kernel(dest_ref, dest_next_ref, ys_ref, res_ref, lng_ref, lnb_ref, o_ref, buf_ref, sem):
    i = pl.program_id(0)
    n = pl.num_programs(0)
    tt = res_ref.shape[0]
    cur = i % 2

    def gather(idx_ref, buf):
        def row_copy(t):
            return pltpu.make_async_copy(ys_ref.at[pl.ds(idx_ref[0, 0, t], 1)], buf_ref.at[buf, pl.ds(t, 1)],
                                         sem.at[buf])

        def issue(t0, carry):
            for u in range(DMA_ISSUE_UNROLL):
                row_copy(t0 * DMA_ISSUE_UNROLL + u).start()
            return carry

        lax.fori_loop(0, tt // DMA_ISSUE_UNROLL, issue, 0)

    @pl.when(i == 0)
    def _():
        gather(dest_ref, cur)

    @pl.when(i + 1 < n)
    def _():
        gather(dest_next_ref, 1 - cur)

    pltpu.make_async_copy(ys_ref.at[pl.ds(0, tt)], buf_ref.at[cur], sem.at[cur]).wait()

    ffn = jnp.concatenate(_unpack_halves(buf_ref[cur]), axis=1)
    o_ref[...] = _layer_norm_rows(ALPHA * res_ref[...] + ffn, lng_ref[...], lnb_ref[...])


def _moe_combine(ys, dest_tiles, res, ln_g, ln_b):
    n_tok, d = res.shape
    tt = dest_tiles.shape[2]
    n_tiles = n_tok // tt
    full = lambda arr: pl.BlockSpec(arr.shape, lambda i: (0,) * arr.ndim)
    return pl.pallas_call(
        _combine_kernel,
        grid=(n_tiles,),
        in_specs=[pl.BlockSpec((1, 1, tt), lambda i: (i, 0, 0), memory_space=pltpu.SMEM),
                  pl.BlockSpec((1, 1, tt), lambda i: (jnp.minimum(i + 1, n_tiles - 1), 0, 0),
                               memory_space=pltpu.SMEM),
                  pl.BlockSpec(memory_space=pl.ANY),
                  pl.BlockSpec((tt, d), lambda i: (i, 0)),
                  full(ln_g), full(ln_b)],
        out_specs=pl.BlockSpec((tt, d), lambda i: (i, 0)),
        out_shape=jax.ShapeDtypeStruct((n_tok, d), F32),
        scratch_shapes=[pltpu.VMEM((2, tt, ys.shape[1]), ys.dtype), pltpu.SemaphoreType.DMA((2,))],
        compiler_params=pltpu.CompilerParams(
            dimension_semantics=("arbitrary",),
            vmem_limit_bytes=_vmem_limit(4 * tt * d * 2 + 2 * 3 * tt * d * 4)),
        name="moe_combine",
    )(dest_tiles, dest_tiles, ys, res, ln_g, ln_b)


def _moe_layer(h, h_packed, route, counts, layer, wg, wu, wd, ln_g, ln_b, sorted_rows=None):
    n_tok, d = h.shape
    blk = MOE_BLOCK
    n_rows = (n_tok + N_CLASSES * (blk - 1) + blk - 1) // blk * blk
    n_blocks = n_rows // blk
    cls, rank = route[0], route[1]
    counts = counts[:N_CLASSES, 0]
    padded = (counts + blk - 1) // blk * blk
    pend = jnp.cumsum(padded)
    pstart = pend - padded
    onehot = (cls[:, None] == jnp.arange(N_CLASSES, dtype=jnp.int32)).astype(jnp.int32)
    dest = rank + jnp.sum(onehot * pstart, axis=-1)
    tiles = lambda tt: dest.reshape(n_tok // tt, 1, tt)
    blk_start = jnp.arange(n_blocks, dtype=jnp.int32) * blk
    block_cls = jnp.minimum(jnp.sum((blk_start[:, None] >= pend[None, :]).astype(jnp.int32), axis=1), N_CLASSES - 1)
    n_used = (pend[-1:] // blk).astype(jnp.int32)
    blk_idx = jnp.arange(n_blocks, dtype=jnp.int32)
    block_cls = jnp.where(blk_idx < n_used[0], block_cls, block_cls[jnp.maximum(n_used[0] - 1, 0)])
    class_ids = jnp.arange(N_CLASSES, dtype=jnp.int32)
    first_expert = (class_ids // N_PAIRS) * EXPERTS_PER_GROUP
    expert_a = first_expert + jnp.array(_PAIR_SLOT_A, jnp.int32)[class_ids % N_PAIRS]
    expert_b = first_expert + jnp.array(_PAIR_SLOT_B, jnp.int32)[class_ids % N_PAIRS]
    xs = _moe_dispatch(h_packed, tiles(DISPATCH_TILE), n_rows, sorted_rows)
    ys = _moe_ffn(xs, expert_a[block_cls].astype(jnp.int32), expert_b[block_cls].astype(jnp.int32), n_used,
                  layer, wg, wu, wd)
    return _moe_combine(ys, tiles(COMBINE_TILE), h, ln_g, ln_b), xs


def _fox_proj_kernel(x_ref, wq_ref, wk_ref, wv_ref, wf_ref, bf_ref, q_ref, k_ref, v_ref, c_ref, carry_ref):
    t = pl.program_id(1)

    @pl.when(t == 0)
    def _():
        carry_ref[...] = jnp.zeros_like(carry_ref)

    tt = PROJ_SUBTILE
    subs = [slice(u * tt, (u + 1) * tt) for u in range(x_ref.shape[0] // tt)]
    each = lambda fn, *lists: [fn(*args) for args in zip(*lists)]
    parts = each(lambda sl: _split3(x_ref[sl, :])[:2], subs)
    wh, wm, _ = _split3(wf_ref[...])
    mm = lambda a, b: jnp.dot(a, b, preferred_element_type=F32)
    logit = each(lambda p: (mm(p[0], wh) + (mm(p[0], wm) + mm(p[1], wh))) + bf_ref[...], parts)
    for sl, p in zip(subs, parts):
        q_ref[sl, :] = (mm(p[0], wq_ref[...]) * LOG2E).astype(q_ref.dtype)
    for sl, p in zip(subs, parts):
        k_ref[sl, :] = mm(p[0], wk_ref[...]).astype(k_ref.dtype)
    for sl, p in zip(subs, parts):
        v_ref[sl, :] = mm(p[0], wv_ref[...]).astype(v_ref.dtype)
    ti = lax.broadcasted_iota(jnp.int32, (tt, tt), 0)
    tj = lax.broadcasted_iota(jnp.int32, (tt, tt), 1)
    tril = (tj <= ti).astype(BF16)
    local = each(lambda lg: sum(mm(tril, part) for part in _split3(jax.nn.log_sigmoid(lg))), logit)
    carry = carry_ref[0:1, :]
    for sl, cs in zip(subs, local):
        c = cs + carry
        c_ref[sl, :] = c
        carry = c[tt - 1:tt, :]
    carry_ref[...] = jnp.broadcast_to(carry, carry_ref.shape)


def _fox_proj(x2, batch, seq_len, wq, wk, wv, wf, b_f):
    n_tok, d = x2.shape
    tt = PROJ_TILE
    nt = seq_len // tt
    tile = pl.BlockSpec((tt, d), lambda b, t: (b * nt + t, 0))
    full = lambda arr: pl.BlockSpec(arr.shape, lambda b, t: (0,) * arr.ndim)
    return pl.pallas_call(
        _fox_proj_kernel,
        grid=(batch, nt),
        in_specs=[tile, full(wq), full(wk), full(wv), full(wf), full(b_f)],
        out_specs=[tile, tile, tile, pl.BlockSpec((tt, N_HEADS), lambda b, t: (b * nt + t, 0))],
        out_shape=[jax.ShapeDtypeStruct((n_tok, d), BF16)] * 3 + [jax.ShapeDtypeStruct((n_tok, N_HEADS), F32)],
        scratch_shapes=[pltpu.VMEM((8, N_HEADS), F32)],
        compiler_params=pltpu.CompilerParams(
            dimension_semantics=("parallel", "arbitrary"),
            vmem_limit_bytes=_vmem_limit(2 * 3 * d * d * 2 + 2 * tt * d * (4 + 3 * 2) + 8 * tt * d * 4)),
        name="fox_proj",
    )(x2, wq, wk, wv, wf, b_f)


def _fox_attn_kernel(q_ref, k_ref, v_ref, ct_ref, o_ref):
    hp = pl.program_id(1)
    seq_len = q_ref.shape[0]
    tq = ATTN_TILE
    tk = tq
    nq = seq_len // tq
    N = HEAD_DIM
    nt = lambda a, b: lax.dot_general(a, b, (((1,), (1,)), ((), ())), preferred_element_type=F32)
    mm = lambda a, b: jnp.dot(a, b, preferred_element_type=F32)

    lane = lax.broadcasted_iota(jnp.int32, (tq, 2 * N), 1)
    vrow = lax.broadcasted_iota(jnp.int32, (2 * N, tk), 0)
    orow = lax.broadcasted_iota(jnp.int32, (2 * N, tq), 0)
    kpos = lax.broadcasted_iota(jnp.int32, (tk, tq), 0)
    qpos = lax.broadcasted_iota(jnp.int32, (tk, tq), 1)
    causal = kpos <= qpos
    ident_v = (lax.broadcasted_iota(jnp.int32, (2 * N, 2 * N), 0)
               == lax.broadcasted_iota(jnp.int32, (2 * N, 2 * N), 1)).astype(BF16)

    zero_q = jnp.zeros((tq, 2 * N), BF16)

    def head_pair(hl):
        lanes = slice(hl * 2 * N, (hl + 1) * 2 * N)
        first_head = 2 * (hp * ATTN_PAIRS_PER_STEP + hl)
        c_parts = [[part.astype(F32) for part in _split3(ct_ref[0, pl.ds(first_head + u, 1), :] * LOG2E)]
                   for u in range(2)]
        ones = jnp.ones((3, tq), F32)
        pad = jnp.zeros((2 * N - 6, tq), F32)
        m = [[None, None] for _ in range(nq)]
        l = [[None, None] for _ in range(nq)]
        acc = [None] * nq
        keys, queries, values = {}, {}, {}

        def key_side(j):
            if j not in keys:
                ks = slice(j * tk, (j + 1) * tk)
                k_j = k_ref[ks, lanes]
                extra = [jnp.concatenate([-part[:, ks] for part in c_parts[u]] + [ones, pad], axis=0).T.astype(BF16)
                         for u in range(2)]
                keys[j] = [jnp.concatenate([k_j, extra[u]], axis=1) for u in range(2)]
            return keys[j]

        def query_side(qi):
            if qi not in queries:
                qs = slice(qi * tq, (qi + 1) * tq)
                q = q_ref[qs, lanes]
                extra = [jnp.concatenate([ones] + [part[:, qs] for part in c_parts[u]] + [pad], axis=0).T.astype(BF16)
                         for u in range(2)]
                queries[qi] = [jnp.concatenate([jnp.where((lane < N) == (u == 0), q, zero_q), extra[u]], axis=1)
                               for u in range(2)]
            return queries[qi]

        def scores(j, qi):
            k_aug, q_aug = key_side(j), query_side(qi)
            t = [nt(k_aug[u], q_aug[u]) for u in range(2)]
            return [jnp.where(causal, tu, -jnp.inf) for tu in t] if qi == j else t

        def absorb(j, qi, t):
            if j not in values:
                v_t = nt(ident_v, v_ref[j * tk:(j + 1) * tk, lanes]).astype(BF16)
                values[j] = [jnp.where(vrow < N, v_t, jnp.zeros_like(v_t)),
                             jnp.where(vrow < N, jnp.zeros_like(v_t), v_t)]
            v_heads = values[j]
            rmax = [jnp.max(t[u], axis=0, keepdims=True) for u in range(2)]
            m_new = rmax if j == 0 else [jnp.maximum(m[qi][u], rmax[u]) for u in range(2)]
            p = [jnp.exp2(t[u] - m_new[u]) for u in range(2)]
            psum = [jnp.sum(p[u], axis=0, keepdims=True) for u in range(2)]
            pv = mm(v_heads[0], p[0].astype(BF16)) + mm(v_heads[1], p[1].astype(BF16))
            if j == 0:
                acc[qi] = pv
                l[qi] = psum
            else:
                alpha = [jnp.exp2(m[qi][u] - m_new[u]) for u in range(2)]
                acc[qi] = acc[qi] * jnp.where(orow < N, alpha[0], alpha[1]) + pv
                l[qi] = [alpha[u] * l[qi][u] + psum[u] for u in range(2)]
            m[qi] = m_new
            if j == qi:
                o_t = acc[qi] / jnp.where(orow < N, l[qi][0], l[qi][1])
                o_ref[qi * tq:(qi + 1) * tq, lanes] = o_t.T.astype(o_ref.dtype)

        return scores, absorb

    pairs = [head_pair(hl) for hl in range(ATTN_PAIRS_PER_STEP)]
    items = [(hl, j, qi) for j in range(nq) for qi in range(j, nq) for hl in range(ATTN_PAIRS_PER_STEP)]
    score_of = lambda item: pairs[item[0]][0](*item[1:])
    queue = [score_of(item) for item in items[:ATTN_LOOKAHEAD]]
    for n, (hl, j, qi) in enumerate(items):
        t_cur = queue.pop(0)
        if n + ATTN_LOOKAHEAD < len(items):
            queue.append(score_of(items[n + ATTN_LOOKAHEAD]))
        pairs[hl][1](j, qi, t_cur)


def _fox_attn(q, k, v, c_t, batch, seq_len):
    n_tok, d = q.shape
    width = ATTN_PAIRS_PER_STEP * 2 * HEAD_DIM
    seq = pl.BlockSpec((seq_len, width), lambda b, hp: (b, hp))
    return pl.pallas_call(
        _fox_attn_kernel,
        grid=(batch, d // width),
        in_specs=[seq, seq, seq, pl.BlockSpec((1, N_HEADS, seq_len), lambda b, hp: (b, 0, 0))],
        out_specs=seq,
        out_shape=jax.ShapeDtypeStruct((n_tok, d), BF16),
        compiler_params=pltpu.CompilerParams(
            dimension_semantics=("parallel", "arbitrary"),
            vmem_limit_bytes=_vmem_limit(2 * 4 * seq_len * width * 2 + 128 * ATTN_TILE * ATTN_TILE * 4)),
        name="fox_attn",
    )(q, k, v, c_t)


def kernel(x, rw_mix, rw_wr, rw_wk, rw_wv, rw_wo, rw_w0, rw_w1, rw_w2, rw_a0, rw_a1, rw_a2, rw_g1, rw_g2,
           rw_kk, rw_ka, rw_rk, rw_gn_g, rw_gn_b, fx_w_in, fx_b_f, fx_wo, router_w, router_bias,
           moe_w_gate, moe_w_up, moe_w_down, ln_g, ln_b):
    batch, seq_len, d = x.shape
    n_tok = batch * seq_len
    bf = lambda w: w.astype(BF16)
    row = lambda w: w.reshape(1, -1)
    router_wt = router_w.T
    router_b = router_bias.reshape(N_EXPERTS, 1)
    h = x.reshape(n_tok, d)
    sorted_rows = None

    for i in range(DEPTH):
        j = i // 2
        if i % 2 == 0:
            r, lw, k, v, a, g = _rwkv_proj(
                h, seq_len, rw_mix[j], bf(rw_wr[j]), bf(rw_wk[j]), bf(rw_wv[j]), bf(rw_w1[j]), bf(rw_w2[j]),
                bf(rw_a1[j]), bf(rw_a2[j]), bf(rw_g1[j]), bf(rw_g2[j]), row(rw_w0[j]), row(rw_a0[j]))
            act = _rwkv_recur(r, lw, k, v, a, g, batch, seq_len, row(rw_kk[j]), row(rw_ka[j]), row(rw_rk[j]),
                              row(rw_gn_g[j]), row(rw_gn_b[j]))
            wo = bf(rw_wo[j])
        else:
            w_in = fx_w_in[j]
            scale = HEAD_DIM ** -0.5
            q, k, v, c = _fox_proj(h, batch, seq_len, bf(w_in[:, :d] * scale), bf(w_in[:, d:2 * d]),
                                   bf(w_in[:, 2 * d:3 * d]), w_in[:, 3 * d:], row(fx_b_f[j]))
            c_t = c.reshape(batch, seq_len, N_HEADS).transpose(0, 2, 1)
            act = _fox_attn(q, k, v, c_t, batch, seq_len)
            wo = bf(fx_wo[j])
        h, h_packed, route, counts = _mixer_epilogue(act, wo, h, row(ln_g[i, 0]), row(ln_b[i, 0]),
                                                     router_wt, router_b)
        h, sorted_rows = _moe_layer(h, h_packed, route, counts, i, moe_w_gate, moe_w_up, moe_w_down,
                                    row(ln_g[i, 1]), row(ln_b[i, 1]), sorted_rows)
    return h.reshape(batch, seq_len, d)
```

```python
import functools
import math

import jax
import jax.numpy as jnp
from jax import lax
from jax.experimental import pallas as pl
from jax.experimental.pallas import tpu as pltpu

D_MODEL = 1024
HEAD_DIM = 64
N_HEADS = D_MODEL // HEAD_DIM
N_EXPERTS = 16
N_GROUPS = 4
EXPERTS_PER_GROUP = N_EXPERTS // N_GROUPS
N_PAIRS = 6
N_CLASSES = N_GROUPS * N_PAIRS
CLASS_ROWS = 32
_PAIR_SLOT_A = (0, 2, 2, 3, 3, 3)
_PAIR_SLOT_B = (1, 1, 0, 0, 1, 2)
GN_EPS = 64e-5
LN_EPS = 1e-5
DEPTH = 2
ALPHA = (2 * DEPTH) ** 0.25
LOG2E = math.log2(math.e)

V7X_LANES = 128
V7X_VMEM_BYTES = 64 * 2 ** 20

V7X_MXU_DIM = 256

RWKV_CHUNK = 64
RWKV_CHUNKS_PER_STEP = 4
RWKV_HEADS_PER_TILE = V7X_MXU_DIM // HEAD_DIM
PROJ_TILE = 512
PROJ_SUBTILE = 256
EPI_TILE = 1024
EPI_SUBTILE = 256
ATTN_TILE = 256
ATTN_PAIRS_PER_STEP = 1
ATTN_LOOKAHEAD = 2
MOE_BLOCK = 256
DISPATCH_TILE = 4096
COMBINE_TILE = 1024
DMA_ISSUE_UNROLL = 8

F32 = jnp.float32
BF16 = jnp.bfloat16


def _vmem_limit(n_bytes):
    return int(min(n_bytes + 16 * 2 ** 20, V7X_VMEM_BYTES - 8 * 2 ** 20))


def _split3(x):
    hi = x.astype(BF16)
    r1 = x - hi.astype(F32)
    mid = r1.astype(BF16)
    lo = (r1 - mid.astype(F32)).astype(BF16)
    return hi, mid, lo


def _pack_halves(x):
    half = x.shape[1] // 2
    bits = lambda v: lax.bitcast_convert_type(v.astype(BF16).astype(F32), jnp.uint32)
    return (bits(x[:, :half]) >> 16) | (bits(x[:, half:]) & jnp.uint32(0xFFFF0000))


def _unpack_halves(w):
    lo = lax.bitcast_convert_type(w << 16, F32)
    hi = lax.bitcast_convert_type(w & jnp.uint32(0xFFFF0000), F32)
    return lo, hi


def _layer_norm_rows(x, g, b):
    mu = jnp.mean(x, axis=-1, keepdims=True)
    xc = x - mu
    var = jnp.mean(xc * xc, axis=-1, keepdims=True)
    return xc * lax.rsqrt(var + LN_EPS) * g + b


def _rwkv_proj_kernel(x_ref, xp_ref, mix_ref, wr_ref, wk_ref, wv_ref, w1_ref, w2_ref, a1_ref, a2_ref,
                      g1_ref, g2_ref, w0_ref, a0_ref,
                      r_ref, lw_ref, k_ref, v_ref, a_ref, g_ref, *, tiles_per_seq):
    i = pl.program_id(0)
    x = x_ref[...]
    tt = x.shape[0]
    first = (i % tiles_per_seq) == 0
    prev_row = jnp.where(first, 0.0, xp_ref[7:8, :])
    row = lax.broadcasted_iota(jnp.int32, (tt, 1), 0)
    xprev = jnp.where(row == 0, prev_row, pltpu.roll(x, 1, axis=0))
    xx = xprev - x
    mix = mix_ref[...]
    subs = [slice(u * PROJ_SUBTILE, (u + 1) * PROJ_SUBTILE) for u in range(tt // PROJ_SUBTILE)]
    each = lambda fn, *lists: [fn(*args) for args in zip(*lists)]
    mixed = lambda j: each(lambda sl: (x[sl] + xx[sl] * mix[j:j + 1, :]).astype(BF16), subs)
    xr, xw, xk, xv, xa, xg = (mixed(j) for j in range(6))
    mm = lambda a, w_ref: jnp.dot(a, w_ref[...], preferred_element_type=F32)
    w_mid = each(lambda a: mm(a, w1_ref), xw)
    a_mid = each(lambda a: mm(a, a1_ref), xa)
    g_mid = each(lambda a: mm(a, g1_ref), xg)
    for sl, a in zip(subs, xr):
        r_ref[sl, :] = mm(a, wr_ref).astype(r_ref.dtype)
    for sl, a in zip(subs, xk):
        k_ref[sl, :] = mm(a, wk_ref).astype(k_ref.dtype)
    for sl, a in zip(subs, xv):
        v_ref[sl, :] = mm(a, wv_ref).astype(v_ref.dtype)
    z = each(lambda t: w0_ref[...] + mm(jnp.tanh(t).astype(BF16), w2_ref), w_mid)
    for sl, t in zip(subs, a_mid):
        a_ref[sl, :] = jax.nn.sigmoid(a0_ref[...] + mm(t.astype(BF16), a2_ref)).astype(a_ref.dtype)
    for sl, t in zip(subs, g_mid):
        g_ref[sl, :] = mm(jax.nn.sigmoid(t).astype(BF16), g2_ref).astype(g_ref.dtype)
    for sl, zs in zip(subs, z):
        lw_ref[sl, :] = -jnp.exp(-jax.nn.softplus(-zs) - 0.5)


def _rwkv_proj(x2, seq_len, mix, wr, wk, wv, w1, w2, a1, a2, g1, g2, w0, a0):
    n_tok, d = x2.shape
    tt = PROJ_TILE
    n_tiles = n_tok // tt
    tiles_per_seq = seq_len // tt
    tile = pl.BlockSpec((tt, d), lambda i: (i, 0))
    prev = pl.BlockSpec((8, d), lambda i: (jnp.maximum(i * (tt // 8) - 1, 0), 0))
    full = lambda arr: pl.BlockSpec(arr.shape, lambda i: (0,) * arr.ndim)
    weights = (mix, wr, wk, wv, w1, w2, a1, a2, g1, g2, w0, a0)
    out_dtypes = (BF16, F32, BF16, BF16, BF16, BF16)
    w_bytes = sum(int(w.size) * w.dtype.itemsize for w in weights)
    return pl.pallas_call(
        functools.partial(_rwkv_proj_kernel, tiles_per_seq=tiles_per_seq),
        grid=(n_tiles,),
        in_specs=[tile, prev] + [full(w) for w in weights],
        out_specs=[tile] * 6,
        out_shape=[jax.ShapeDtypeStruct((n_tok, d), dt) for dt in out_dtypes],
        compiler_params=pltpu.CompilerParams(
            dimension_semantics=("parallel",),
            vmem_limit_bytes=_vmem_limit(2 * w_bytes + 2 * tt * d * (4 + 4 + 5 * 2) + 8 * tt * d * 4)),
        name="rwkv_proj",
    )(x2, x2, *weights)


def _rwkv_recur_kernel(r_ref, lw_ref, k_ref, v_ref, a_ref, g_ref, kk_ref, ka_ref, rk_ref, gng_ref, gnb_ref,
                       o_ref, s_ref):
    c = pl.program_id(1)
    C = RWKV_CHUNK
    N = HEAD_DIM

    @pl.when(c == 0)
    def _():
        s_ref[...] = jnp.zeros_like(s_ref)

    nt = lambda p, q: lax.dot_general(p, q, (((1,), (1,)), ((), ())), preferred_element_type=F32)
    mm = lambda p, q: jnp.dot(p, q, preferred_element_type=F32)
    each = lambda fn, *lists: [fn(*args) for args in zip(*lists)]
    bf = lambda x: x.astype(BF16)

    ti = lax.broadcasted_iota(jnp.int32, (C, C), 0)
    tj = lax.broadcasted_iota(jnp.int32, (C, C), 1)
    tril = (tj <= ti).astype(BF16)

    def chunk_terms(rows):
        lw = lw_ref[rows, :]
        cum = sum(mm(tril, part) for part in _split3(lw)[:2])
        rho = cum[C // 2 - 1:C // 2, :]
        last = cum[C - 1:C, :]
        r = r_ref[rows, :].astype(F32)
        k = k_ref[rows, :].astype(F32)
        a = a_ref[rows, :].astype(F32)
        k_mod = k * (1.0 + (a - 1.0) * ka_ref[...])
        return dict(e_q=jnp.exp(cum - rho), e_qx=jnp.exp(cum - lw - rho), e_k=jnp.exp(rho - cum),
                    e_end=jnp.exp(last - cum), e_rho=jnp.exp(rho), d_end=jnp.exp(last),
                    r=r, a=a, v=v_ref[rows, :].astype(F32), kk_raw=k * kk_ref[...], k_mod=k_mod,
                    rkk=r * k_mod * rk_ref[...])

    n_ch = r_ref.shape[0] // C
    terms = [chunk_terms(slice(ci * C, (ci + 1) * C)) for ci in range(n_ch)]

    G = RWKV_HEADS_PER_TILE
    R = G * C
    GW = G * N
    er = lax.broadcasted_iota(jnp.int32, (R, GW), 0)
    ec = lax.broadcasted_iota(jnp.int32, (R, GW), 1)
    blk = (er // C) == (ec // N)
    strict = (ec % C) < (er % C)
    incl = (ec % C) <= (er % C)
    eye = (er == ec).astype(F32)
    ones_blk = blk.astype(BF16)

    def expand(x):
        return jnp.where(blk, jnp.concatenate([x] * G, axis=0), 0.0).astype(BF16)

    expand_t = lambda x: bf(jnp.where(blk, jnp.concatenate([x] * G, axis=0), 0.0).T)

    def head_sums(xs):
        parts = _split3(jnp.concatenate(xs, axis=0))[:2]
        tot = mm(jnp.concatenate(parts, axis=0), ones_blk)
        n = len(xs) * C
        tot = tot[:n] + tot[n:]
        return [tot[u * C:(u + 1) * C] for u in range(len(xs))]

    n_grp = N_HEADS // G
    groups = [slice(gi * GW, (gi + 1) * GW) for gi in range(n_grp)]
    units = [(tm, sl) for tm in terms for sl in groups]

    def independent(us, out):
        u_tm = [tm for tm, _ in us]
        u_sl = [sl for _, sl in us]
        pre = head_sums([tm["kk_raw"][:, sl] * tm["kk_raw"][:, sl] for tm, sl in us] + [tm["rkk"][:, sl] for tm, sl in us])
        kk_ss, out["rkk_sum"] = pre[:len(us)], pre[len(us):]
        yield
        kk_n = each(lambda tm, sl, ss: tm["kk_raw"][:, sl] * lax.rsqrt(jnp.maximum(ss, 1e-24)), u_tm, u_sl, kk_ss)
        b_n = each(lambda tm, sl, kk_g: kk_g * tm["a"][:, sl], u_tm, u_sl, kk_n)
        kk_q = each(lambda tm, sl, kk_g: kk_g * tm["e_qx"][:, sl], u_tm, u_sl, kk_n)
        r_q = each(lambda tm, sl: tm["r"][:, sl] * tm["e_q"][:, sl], u_tm, u_sl)
        q2 = each(lambda x, y: jnp.concatenate([expand(x), expand(y)], axis=0), kk_q, r_q)
        a_k = each(lambda tm, sl, q: nt(q, expand(tm["k_mod"][:, sl] * tm["e_k"][:, sl])), u_tm, u_sl, q2)
        yield
        a_b = each(lambda tm, sl, q, b_g: nt(q, expand(b_g * tm["e_k"][:, sl])), u_tm, u_sl, q2, b_n)
        yield
        a_kk = each(lambda x: jnp.where(strict, x[:R], 0.0), a_k)
        a_rk = each(lambda x: jnp.where(incl, x[R:], 0.0), a_k)
        a_kb = each(lambda x: jnp.where(strict, x[:R], 0.0), a_b)
        out["a_rb"] = each(lambda x: bf(jnp.where(incl, x[R:], 0.0)), a_b)
        t_inv = each(lambda x: eye - x, a_kb)
        p = each(lambda x: mm(bf(-x), bf(-x)), a_kb)
        yield
        for _ in range(int(math.log2(C)) - 2):
            both = each(lambda pg, tg: mm(bf(pg), jnp.concatenate([bf(pg), bf(tg)], axis=1)), p, t_inv)
            p = each(lambda x: x[:, :R], both)
            t_inv = each(lambda tg, x: tg + x[:, R:], t_inv, both)
            yield
        out["t_inv"] = each(lambda pg, tg: bf(tg + mm(bf(pg), bf(tg))), p, t_inv)
        yield
        out["kd_t"] = each(lambda tm, sl: expand_t(tm["k_mod"][:, sl] * tm["e_end"][:, sl]), u_tm, u_sl)
        out["bd_t"] = each(lambda tm, sl, b_g: expand_t(b_g * tm["e_end"][:, sl]), u_tm, u_sl, b_n)
        out["v_e"] = each(lambda tm, sl: expand(tm["v"][:, sl]), u_tm, u_sl)
        out["av"] = each(lambda x, y, ve: mm(bf(jnp.concatenate([x, y], axis=0)), ve), a_kk, a_rk, out["v_e"])
        out["q2_abs"] = each(lambda tm, sl, x, y: jnp.concatenate([expand(x * tm["e_rho"][:, sl]),
                                                                   expand(y * tm["e_rho"][:, sl])], axis=0),
                             u_tm, u_sl, kk_q, r_q)
        yield

    state = {"st": [s_ref[gi] for gi in range(n_grp)]}
    ys = [None] * len(units)

    def dependent(ci, res, lo):
        span = slice(lo, lo + n_grp)
        qs = each(lambda q, s: mm(q, bf(s)), res["q2_abs"][span], state["st"])
        yield
        sa_e = each(lambda tg, q, x: bf(mm(tg, bf(q[:R] + x[:R]))), res["t_inv"][span], qs, res["av"][span])
        yield
        y_e = each(lambda q, x, arb, sa: q[R:] + x[R:] - mm(arb, sa), qs, res["av"][span], res["a_rb"][span], sa_e)
        ys[ci * n_grp:(ci + 1) * n_grp] = each(lambda x: sum(x[u * C:(u + 1) * C] for u in range(G)), y_e)
        yield
        upd = each(lambda kt, bt, ve, sa: mm(jnp.concatenate([kt, -bt], axis=1), jnp.concatenate([ve, sa], axis=0)),
                   res["kd_t"][span], res["bd_t"][span], res["v_e"][span], sa_e)
        d_col = [jnp.broadcast_to(terms[ci]["d_end"][:, sl], (GW, GW)).T for sl in groups]
        state["st"] = each(lambda s, dc, up: s * dc + up, state["st"], d_col, upd)
        yield

    half = (n_ch // 2) * n_grp
    first, second = {}, {}
    for _ in independent(units[:half], first):
        pass
    chain = [stage for ci in range(n_ch // 2) for stage in [dependent(ci, first, ci * n_grp)]]
    pending = iter(())
    todo = list(chain)

    def advance():
        nonlocal pending
        while True:
            try:
                next(pending)
                return True
            except StopIteration:
                if not todo:
                    return False
                pending = todo.pop(0)

    for _ in independent(units[half:], second):
        advance()
    while advance():
        pass
    for ci in range(n_ch // 2, n_ch):
        for _ in dependent(ci, second, (ci - n_ch // 2) * n_grp):
            pass
    for gi in range(n_grp):
        s_ref[gi] = state["st"][gi]
    rkk_sum = first["rkk_sum"] + second["rkk_sum"]

    inv_n = 1.0 / N
    ycs = [y - mu * inv_n for y, mu in zip(ys, head_sums(ys))]
    sqs = head_sums([yc * yc for yc in ycs])
    for ui, (tm, sl) in enumerate(units):
        rows = slice((ui // n_grp) * C, (ui // n_grp + 1) * C)
        yn = ycs[ui] * lax.rsqrt(sqs[ui] * inv_n + GN_EPS) * gng_ref[:, sl] + gnb_ref[:, sl]
        o_ref[rows, sl] = ((yn + rkk_sum[ui] * tm["v"][:, sl]) * g_ref[rows, sl]).astype(o_ref.dtype)


def _rwkv_recur(r, lw, k, v, a, g, batch, seq_len, k_k, k_a, r_k, gn_g, gn_b):
    n_tok, d = r.shape
    C = RWKV_CHUNK * RWKV_CHUNKS_PER_STEP
    nc = seq_len // C
    tile = pl.BlockSpec((C, d), lambda b, c: (b * nc + c, 0))
    vec = pl.BlockSpec((1, d), lambda b, c: (0, 0))
    return pl.pallas_call(
        _rwkv_recur_kernel,
        grid=(batch, nc),
        in_specs=[tile] * 6 + [vec] * 5,
        out_specs=tile,
        out_shape=jax.ShapeDtypeStruct((n_tok, d), BF16),
        scratch_shapes=[pltpu.VMEM((N_HEADS // RWKV_HEADS_PER_TILE, V7X_MXU_DIM, V7X_MXU_DIM), F32)],
        compiler_params=pltpu.CompilerParams(
            dimension_semantics=("parallel", "arbitrary"),
            vmem_limit_bytes=_vmem_limit(2 * 7 * C * d * 4 + 32 * C * d * 4 + 64 * V7X_MXU_DIM ** 2 * 4)),
        name="rwkv_recur",
    )(r, lw, k, v, a, g, k_k, k_a, r_k, gn_g, gn_b)


def _rank_among(vals, i):
    cnt = 0
    for j, vj in enumerate(vals):
        if j == i:
            continue
        before = (vj >= vals[i]) if j < i else (vj > vals[i])
        cnt = cnt + before.astype(jnp.int32)
    return cnt


def _pick(ranks, vals, want):
    out = vals[0]
    for rk, vl in zip(ranks[1:], vals[1:]):
        out = jnp.where(rk == want, vl, out)
    return out


def _epilogue_kernel(act_ref, wo_ref, res_ref, lng_ref, lnb_ref, rwt_ref, rb_ref,
                     h_ref, hp_ref, route_ref, cnt_ref, base_ref):
    i = pl.program_id(0)

    @pl.when(i == 0)
    def _():
        base_ref[...] = jnp.zeros_like(base_ref)

    tt = EPI_SUBTILE
    subs = [slice(u * tt, (u + 1) * tt) for u in range(act_ref.shape[0] // tt)]
    each = lambda fn, *lists: [fn(*args) for args in zip(*lists)]
    nt = lambda a, b: lax.dot_general(a, b, (((1,), (1,)), ((), ())), preferred_element_type=F32)

    wh, wm, _ = _split3(rwt_ref[...])

    def router_logits(hs):
        hh, hm, _ = _split3(hs)
        return nt(wh, hh) + (nt(wh, hm) + nt(wm, hh))

    def project(sl):
        return jnp.dot(act_ref[sl, :], wo_ref[...], preferred_element_type=F32)

    def normalise(sl, mx):
        hs = _layer_norm_rows(ALPHA * res_ref[sl, :] + mx, lng_ref[...], lnb_ref[...])
        h_ref[sl, :] = hs
        hp_ref[sl, :hs.shape[1] // 2] = _pack_halves(hs)
        return jax.nn.sigmoid(router_logits(hs))

    s = []
    mixed = project(subs[0])
    for u in range(len(subs)):
        nxt = project(subs[u + 1]) if u + 1 < len(subs) else None
        s.append(normalise(subs[u], mixed))
        mixed = nxt

    def select(sg):
        s_sel = sg + rb_ref[...]
        rows = [s_sel[e:e + 1, :] for e in range(N_EXPERTS)]
        grp_score, grp_i0, grp_i1 = [], [], []
        for gi in range(N_GROUPS):
            vals = rows[gi * EXPERTS_PER_GROUP:(gi + 1) * EXPERTS_PER_GROUP]
            ranks = [_rank_among(vals, q) for q in range(EXPERTS_PER_GROUP)]
            idx = [jnp.full_like(ranks[0], q) for q in range(EXPERTS_PER_GROUP)]
            grp_score.append(_pick(ranks, vals, 0) + _pick(ranks, vals, 1))
            grp_i0.append(_pick(ranks, idx, 0))
            grp_i1.append(_pick(ranks, idx, 1))
        g_ranks = [_rank_among(grp_score, q) for q in range(N_GROUPS)]
        gidx = [jnp.full_like(g_ranks[0], q) for q in range(N_GROUPS)]
        g_star = _pick(g_ranks, gidx, 0)
        e0 = g_star * EXPERTS_PER_GROUP + _pick(g_ranks, grp_i0, 0)
        e1 = g_star * EXPERTS_PER_GROUP + _pick(g_ranks, grp_i1, 0)
        return e0, e1

    picked = each(select, s)
    e_iota = lax.broadcasted_iota(jnp.int32, (N_EXPERTS, tt), 0)

    def gates(sg, pk):
        gate0 = jnp.sum(jnp.where(e_iota == pk[0], sg, 0.0), axis=0, keepdims=True)
        gate1 = jnp.sum(jnp.where(e_iota == pk[1], sg, 0.0), axis=0, keepdims=True)
        denom = gate0 + gate1
        return gate0 / denom, gate1 / denom

    gate = each(gates, s, picked)

    def classify(pk):
        e0, e1 = pk
        grp = lax.shift_right_logical(e0, 2)
        l0 = e0 - grp * EXPERTS_PER_GROUP
        l1 = e1 - grp * EXPERTS_PER_GROUP
        lo, hi = jnp.minimum(l0, l1), jnp.maximum(l0, l1)
        pair = jnp.where(hi == 1, 0, jnp.where(hi == 2, jnp.where(lo == 1, 1, 2),
                                               jnp.where(lo == 0, 3, jnp.where(lo == 1, 4, 5))))
        slot_a = jnp.where(pair == 0, 0, jnp.where(pair <= 2, 2, 3))
        return grp * N_PAIRS + pair, l0 != slot_a

    classes = each(classify, picked)
    c_iota = lax.broadcasted_iota(jnp.int32, (cnt_ref.shape[0], tt), 0)
    hit = each(lambda cl: c_iota == cl[0], classes)
    member = each(lambda hc: jnp.where(hc, 1.0, 0.0), hit)
    ui = lax.broadcasted_iota(jnp.int32, (tt, tt), 0)
    uj = lax.broadcasted_iota(jnp.int32, (tt, tt), 1)
    before = (ui < uj).astype(BF16)
    prefix = each(lambda mb: jnp.dot(mb.astype(BF16), before, preferred_element_type=F32), member)
    base = base_ref[:, 0:1]
    for u, sl in enumerate(subs):
        rank = jnp.sum(jnp.where(hit[u], prefix[u] + base, 0.0), axis=0, keepdims=True)
        route_ref[:, sl] = jnp.concatenate([classes[u][0], rank.astype(jnp.int32)], axis=0)
        base = base + jnp.sum(member[u], axis=1, keepdims=True)
    base_ref[...] = jnp.broadcast_to(base, base_ref.shape)
    cnt_ref[...] = jnp.broadcast_to(base, cnt_ref.shape).astype(jnp.int32)

    half = hp_ref.shape[1] - V7X_LANES
    for sl, gt, cl in zip(subs, gate, classes):
        g_a = jnp.where(cl[1], gt[1], gt[0])
        g_b = jnp.where(cl[1], gt[0], gt[1])
        gpad = jnp.concatenate([g_a, g_b, jnp.zeros((V7X_LANES - 2, tt), F32)], axis=0)
        hp_ref[sl, half:] = lax.bitcast_convert_type(gpad.T, jnp.uint32)


def _mixer_epilogue(act, wo, res, ln_g, ln_b, router_wt, router_bias):
    n_tok, d = res.shape
    k_in = act.shape[1]
    tt = EPI_TILE
    n_tiles = n_tok // tt
    full = lambda arr: pl.BlockSpec(arr.shape, lambda i: (0,) * arr.ndim)
    return pl.pallas_call(
        _epilogue_kernel,
        grid=(n_tiles,),
        in_specs=[pl.BlockSpec((tt, k_in), lambda i: (i, 0)), full(wo), pl.BlockSpec((tt, d), lambda i: (i, 0)),
                  full(ln_g), full(ln_b), full(router_wt), full(router_bias)],
        out_specs=[pl.BlockSpec((tt, d), lambda i: (i, 0)),
                   pl.BlockSpec((tt, d // 2 + V7X_LANES), lambda i: (i, 0)),
                   pl.BlockSpec((2, tt), lambda i: (0, i)),
                   pl.BlockSpec((CLASS_ROWS, V7X_LANES), lambda i: (0, 0))],
        out_shape=[jax.ShapeDtypeStruct((n_tok, d), F32),
                   jax.ShapeDtypeStruct((n_tok, d // 2 + V7X_LANES), jnp.uint32),
                   jax.ShapeDtypeStruct((2, n_tok), jnp.int32),
                   jax.ShapeDtypeStruct((CLASS_ROWS, V7X_LANES), jnp.int32)],
        scratch_shapes=[pltpu.VMEM((CLASS_ROWS, V7X_LANES), F32)],
        compiler_params=pltpu.CompilerParams(
            dimension_semantics=("arbitrary",),
            vmem_limit_bytes=_vmem_limit(2 * int(wo.size) * 2 + 2 * tt * (k_in * 2 + 2 * d * 4) + 16 * tt * d * 4)),
        name="mixer_epilogue",
    )(act, wo, res, ln_g, ln_b, router_wt, router_bias)


def _dispatch_kernel(dest_ref, h_ref, xs_in_ref, xs_ref, sem):
    del xs_in_ref
    tt = h_ref.shape[0]

    def row_copy(t):
        return pltpu.make_async_copy(h_ref.at[pl.ds(t, 1)], xs_ref.at[pl.ds(dest_ref[0, 0, t], 1)], sem)

    def issue(t0, carry):
        for u in range(DMA_ISSUE_UNROLL):
            row_copy(t0 * DMA_ISSUE_UNROLL + u).start()
        return carry

    lax.fori_loop(0, tt // DMA_ISSUE_UNROLL, issue, 0)
    pltpu.make_async_copy(h_ref, xs_ref.at[pl.ds(0, tt)], sem).wait()


def _moe_dispatch(h, dest_tiles, n_rows, initialised=None):
    n_tok, d = h.shape
    tt = dest_tiles.shape[2]
    n_tiles = n_tok // tt
    zeros = jnp.zeros((n_rows, d), h.dtype) if initialised is None else initialised
    return pl.pallas_call(
        _dispatch_kernel,
        grid=(n_tiles,),
        in_specs=[pl.BlockSpec((1, 1, tt), lambda i: (i, 0, 0), memory_space=pltpu.SMEM),
                  pl.BlockSpec((tt, d), lambda i: (i, 0)),
                  pl.BlockSpec(memory_space=pl.ANY)],
        out_specs=pl.BlockSpec(memory_space=pl.ANY),
        out_shape=jax.ShapeDtypeStruct((n_rows, d), h.dtype),
        scratch_shapes=[pltpu.SemaphoreType.DMA(())],
        input_output_aliases={2: 0},
        compiler_params=pltpu.CompilerParams(dimension_semantics=("arbitrary",),
                                             vmem_limit_bytes=_vmem_limit(2 * tt * d * 4)),
        name="moe_dispatch",
    )(dest_tiles, h, zeros)


def _ffn_kernel(bea_ref, beb_ref, nblk_ref, x_ref, wga_ref, wua_ref, wda_ref, wgb_ref, wub_ref, wdb_ref,
                y_ref, wg_bf, wu_bf, wd_bf):
    j = pl.program_id(0)
    active = j < nblk_ref[0]
    prev = jnp.maximum(j - 1, 0)
    slots = ((bea_ref, wga_ref, wua_ref, wda_ref), (beb_ref, wgb_ref, wub_ref, wdb_ref))

    for s, (be_ref, wg_ref, wu_ref, wd_ref) in enumerate(slots):
        @pl.when(active & ((j == 0) | (be_ref[j] != be_ref[prev])))
        def _():
            wg_bf[s] = wg_ref[0, 0].astype(BF16)
            wu_bf[s] = wu_ref[0, 0].astype(BF16)
            wd_bf[s] = wd_ref[0, 0].astype(BF16)

    @pl.when(active)
    def _():
        half = y_ref.shape[1]
        mm = lambda a, b: jnp.dot(a, b, preferred_element_type=F32)
        x_lo, x_hi = (v.astype(BF16) for v in _unpack_halves(x_ref[:, :half]))
        gates = lax.bitcast_convert_type(x_ref[:, half:], F32)
        gate = [mm(x_lo, wg_bf[s, :half, :]) + mm(x_hi, wg_bf[s, half:, :]) for s in range(2)]
        up = [mm(x_lo, wu_bf[s, :half, :]) + mm(x_hi, wu_bf[s, half:, :]) for s in range(2)]
        hid = [((g * jax.nn.sigmoid(g)) * u).astype(BF16) for g, u in zip(gate, up)]
        y = [mm(hid[s], wd_bf[s]) for s in range(2)]
        y_ref[...] = _pack_halves(y[0] * gates[:, 0:1] + y[1] * gates[:, 1:2])

    @pl.when(jnp.logical_not(active))
    def _():
        y_ref[...] = jnp.zeros_like(y_ref)


def _moe_ffn(xs, block_ea, block_eb, n_used, layer, wg, wu, wd):
    n_rows, width = xs.shape
    half = width - V7X_LANES
    d = 2 * half
    blk = MOE_BLOCK
    n_blocks = n_rows // blk
    de = wg.shape[3]
    expert_a = lambda j, ea, eb, nb: (layer, ea[j], 0, 0)
    expert_b = lambda j, ea, eb, nb: (layer, eb[j], 0, 0)
    rows = lambda j, ea, eb, nb: (j, 0)
    grid_spec = pltpu.PrefetchScalarGridSpec(
        num_scalar_prefetch=3,
        grid=(n_blocks,),
        in_specs=[pl.BlockSpec((blk, width), rows),
                  pl.BlockSpec((1, 1, d, de), expert_a), pl.BlockSpec((1, 1, d, de), expert_a),
                  pl.BlockSpec((1, 1, de, d), expert_a),
                  pl.BlockSpec((1, 1, d, de), expert_b), pl.BlockSpec((1, 1, d, de), expert_b),
                  pl.BlockSpec((1, 1, de, d), expert_b)],
        out_specs=pl.BlockSpec((blk, half), rows),
        scratch_shapes=[pltpu.VMEM((2, d, de), BF16), pltpu.VMEM((2, d, de), BF16), pltpu.VMEM((2, de, d), BF16)],
    )
    return pl.pallas_call(
        _ffn_kernel,
        grid_spec=grid_spec,
        out_shape=jax.ShapeDtypeStruct((n_rows, half), jnp.uint32),
        compiler_params=pltpu.CompilerParams(
            dimension_semantics=("arbitrary",),
            vmem_limit_bytes=_vmem_limit(2 * 3 * d * de * (2 * 4 + 2) + 6 * blk * d * 4 + 8 * blk * de * 4)),
        name="moe_ffn",
    )(block_ea, block_eb, n_used, xs, wg, wu, wd, wg, wu, wd)


def _combine_kernel(dest_ref, dest_next_ref, ys_ref, res_ref, lng_ref, lnb_ref, o_ref, buf_ref, sem):
    i = pl.program_id(0)
    n = pl.num_programs(0)
    tt = res_ref.shape[0]
    cur = i % 2

    def gather(idx_ref, buf):
        def row_copy(t):
            return pltpu.make_async_copy(ys_ref.at[pl.ds(idx_ref[0, 0, t], 1)], buf_ref.at[buf, pl.ds(t, 1)],
                                         sem.at[buf])

        def issue(t0, carry):
            for u in range(DMA_ISSUE_UNROLL):
                row_copy(t0 * DMA_ISSUE_UNROLL + u).start()
            return carry

        lax.fori_loop(0, tt // DMA_ISSUE_UNROLL, issue, 0)

    @pl.when(i == 0)
    def _():
        gather(dest_ref, cur)

    @pl.when(i + 1 < n)
    def _():
        gather(dest_next_ref, 1 - cur)

    pltpu.make_async_copy(ys_ref.at[pl.ds(0, tt)], buf_ref.at[cur], sem.at[cur]).wait()

    ffn = jnp.concatenate(_unpack_halves(buf_ref[cur]), axis=1)
    o_ref[...] = _layer_norm_rows(ALPHA * res_ref[...] + ffn, lng_ref[...], lnb_ref[...])


def _moe_combine(ys, dest_tiles, res, ln_g, ln_b):
    n_tok, d = res.shape
    tt = dest_tiles.shape[2]
    n_tiles = n_tok // tt
    full = lambda arr: pl.BlockSpec(arr.shape, lambda i: (0,) * arr.ndim)
    return pl.pallas_call(
        _combine_kernel,
        grid=(n_tiles,),
        in_specs=[pl.BlockSpec((1, 1, tt), lambda i: (i, 0, 0), memory_space=pltpu.SMEM),
                  pl.BlockSpec((1, 1, tt), lambda i: (jnp.minimum(i + 1, n_tiles - 1), 0, 0),
                               memory_space=pltpu.SMEM),
                  pl.BlockSpec(memory_space=pl.ANY),
                  pl.BlockSpec((tt, d), lambda i: (i, 0)),
                  full(ln_g), full(ln_b)],
        out_specs=pl.BlockSpec((tt, d), lambda i: (i, 0)),
        out_shape=jax.ShapeDtypeStruct((n_tok, d), F32),
        scratch_shapes=[pltpu.VMEM((2, tt, ys.shape[1]), ys.dtype), pltpu.SemaphoreType.DMA((2,))],
        compiler_params=pltpu.CompilerParams(
            dimension_semantics=("arbitrary",),
            vmem_limit_bytes=_vmem_limit(4 * tt * d * 2 + 2 * 3 * tt * d * 4)),
        name="moe_combine",
    )(dest_tiles, dest_tiles, ys, res, ln_g, ln_b)


def _moe_layer(h, h_packed, route, counts, layer, wg, wu, wd, ln_g, ln_b, sorted_rows=None):
    n_tok, d = h.shape
    blk = MOE_BLOCK
    n_rows = (n_tok + N_CLASSES * (blk - 1) + blk - 1) // blk * blk
    n_blocks = n_rows // blk
    cls, rank = route[0], route[1]
    counts = counts[:N_CLASSES, 0]
    padded = (counts + blk - 1) // blk * blk
    pend = jnp.cumsum(padded)
    pstart = pend - padded
    onehot = (cls[:, None] == jnp.arange(N_CLASSES, dtype=jnp.int32)).astype(jnp.int32)
    dest = rank + jnp.sum(onehot * pstart, axis=-1)
    tiles = lambda tt: dest.reshape(n_tok // tt, 1, tt)
    blk_start = jnp.arange(n_blocks, dtype=jnp.int32) * blk
    block_cls = jnp.minimum(jnp.sum((blk_start[:, None] >= pend[None, :]).astype(jnp.int32), axis=1), N_CLASSES - 1)
    n_used = (pend[-1:] // blk).astype(jnp.int32)
    blk_idx = jnp.arange(n_blocks, dtype=jnp.int32)
    block_cls = jnp.where(blk_idx < n_used[0], block_cls, block_cls[jnp.maximum(n_used[0] - 1, 0)])
    class_ids = jnp.arange(N_CLASSES, dtype=jnp.int32)
    first_expert = (class_ids // N_PAIRS) * EXPERTS_PER_GROUP
    expert_a = first_expert + jnp.array(_PAIR_SLOT_A, jnp.int32)[class_ids % N_PAIRS]
    expert_b = first_expert + jnp.array(_PAIR_SLOT_B, jnp.int32)[class_ids % N_PAIRS]
    xs = _moe_dispatch(h_packed, tiles(DISPATCH_TILE), n_rows, sorted_rows)
    ys = _moe_ffn(xs, expert_a[block_cls].astype(jnp.int32), expert_b[block_cls].astype(jnp.int32), n_used,
                  layer, wg, wu, wd)
    return _moe_combine(ys, tiles(COMBINE_TILE), h, ln_g, ln_b), xs


def _fox_proj_kernel(x_ref, wq_ref, wk_ref, wv_ref, wf_ref, bf_ref, q_ref, k_ref, v_ref, c_ref, carry_ref):
    t = pl.program_id(1)

    @pl.when(t == 0)
    def _():
        carry_ref[...] = jnp.zeros_like(carry_ref)

    tt = PROJ_SUBTILE
    subs = [slice(u * tt, (u + 1) * tt) for u in range(x_ref.shape[0] // tt)]
    each = lambda fn, *lists: [fn(*args) for args in zip(*lists)]
    parts = each(lambda sl: _split3(x_ref[sl, :])[:2], subs)
    wh, wm, _ = _split3(wf_ref[...])
    mm = lambda a, b: jnp.dot(a, b, preferred_element_type=F32)
    logit = each(lambda p: (mm(p[0], wh) + (mm(p[0], wm) + mm(p[1], wh))) + bf_ref[...], parts)
    for sl, p in zip(subs, parts):
        q_ref[sl, :] = (mm(p[0], wq_ref[...]) * LOG2E).astype(q_ref.dtype)
    for sl, p in zip(subs, parts):
        k_ref[sl, :] = mm(p[0], wk_ref[...]).astype(k_ref.dtype)
    for sl, p in zip(subs, parts):
        v_ref[sl, :] = mm(p[0], wv_ref[...]).astype(v_ref.dtype)
    ti = lax.broadcasted_iota(jnp.int32, (tt, tt), 0)
    tj = lax.broadcasted_iota(jnp.int32, (tt, tt), 1)
    tril = (tj <= ti).astype(BF16)
    local = each(lambda lg: sum(mm(tril, part) for part in _split3(jax.nn.log_sigmoid(lg))[:2]), logit)
    carry = carry_ref[0:1, :]
    for sl, cs in zip(subs, local):
        c = cs + carry
        c_ref[sl, :] = c
        carry = c[tt - 1:tt, :]
    carry_ref[...] = jnp.broadcast_to(carry, carry_ref.shape)


def _fox_proj(x2, batch, seq_len, wq, wk, wv, wf, b_f):
    n_tok, d = x2.shape
    tt = PROJ_TILE
    nt = seq_len // tt
    tile = pl.BlockSpec((tt, d), lambda b, t: (b * nt + t, 0))
    full = lambda arr: pl.BlockSpec(arr.shape, lambda b, t: (0,) * arr.ndim)
    return pl.pallas_call(
        _fox_proj_kernel,
        grid=(batch, nt),
        in_specs=[tile, full(wq), full(wk), full(wv), full(wf), full(b_f)],
        out_specs=[tile, tile, tile, pl.BlockSpec((tt, N_HEADS), lambda b, t: (b * nt + t, 0))],
        out_shape=[jax.ShapeDtypeStruct((n_tok, d), BF16)] * 3 + [jax.ShapeDtypeStruct((n_tok, N_HEADS), F32)],
        scratch_shapes=[pltpu.VMEM((8, N_HEADS), F32)],
        compiler_params=pltpu.CompilerParams(
            dimension_semantics=("parallel", "arbitrary"),
            vmem_limit_bytes=_vmem_limit(2 * 3 * d * d * 2 + 2 * tt * d * (4 + 3 * 2) + 8 * tt * d * 4)),
        name="fox_proj",
    )(x2, wq, wk, wv, wf, b_f)


def _fox_attn_kernel(q_ref, k_ref, v_ref, ct_ref, o_ref):
    hp = pl.program_id(1)
    seq_len = q_ref.shape[0]
    tq = ATTN_TILE
    tk = tq
    nq = seq_len // tq
    N = HEAD_DIM
    nt = lambda a, b: lax.dot_general(a, b, (((1,), (1,)), ((), ())), preferred_element_type=F32)
    mm = lambda a, b: jnp.dot(a, b, preferred_element_type=F32)

    lane = lax.broadcasted_iota(jnp.int32, (tq, 2 * N), 1)
    vrow = lax.broadcasted_iota(jnp.int32, (2 * N, tk), 0)
    orow = lax.broadcasted_iota(jnp.int32, (2 * N, tq), 0)
    kpos = lax.broadcasted_iota(jnp.int32, (tk, tq), 0)
    qpos = lax.broadcasted_iota(jnp.int32, (tk, tq), 1)
    causal = kpos <= qpos
    ident_v = (lax.broadcasted_iota(jnp.int32, (2 * N, 2 * N), 0)
               == lax.broadcasted_iota(jnp.int32, (2 * N, 2 * N), 1)).astype(BF16)

    zero_q = jnp.zeros((tq, 2 * N), BF16)

    def head_pair(hl):
        lanes = slice(hl * 2 * N, (hl + 1) * 2 * N)
        first_head = 2 * (hp * ATTN_PAIRS_PER_STEP + hl)
        c_parts = [[part.astype(F32) for part in _split3(ct_ref[0, pl.ds(first_head + u, 1), :] * LOG2E)]
                   for u in range(2)]
        ones = jnp.ones((3, tq), F32)
        pad = jnp.zeros((2 * N - 6, tq), F32)
        m = [[None, None] for _ in range(nq)]
        l = [[None, None] for _ in range(nq)]
        acc = [None] * nq
        keys, queries, values = {}, {}, {}

        def key_side(j):
            if j not in keys:
                ks = slice(j * tk, (j + 1) * tk)
                k_j = k_ref[ks, lanes]
                extra = [jnp.concatenate([-part[:, ks] for part in c_parts[u]] + [ones, pad], axis=0).T.astype(BF16)
                         for u in range(2)]
                keys[j] = [jnp.concatenate([k_j, extra[u]], axis=1) for u in range(2)]
            return keys[j]

        def query_side(qi):
            if qi not in queries:
                qs = slice(qi * tq, (qi + 1) * tq)
                q = q_ref[qs, lanes]
                extra = [jnp.concatenate([ones] + [part[:, qs] for part in c_parts[u]] + [pad], axis=0).T.astype(BF16)
                         for u in range(2)]
                queries[qi] = [jnp.concatenate([jnp.where((lane < N) == (u == 0), q, zero_q), extra[u]], axis=1)
                               for u in range(2)]
            return queries[qi]

        def scores(j, qi):
            k_aug, q_aug = key_side(j), query_side(qi)
            t = [nt(k_aug[u], q_aug[u]) for u in range(2)]
            return [jnp.where(causal, tu, -jnp.inf) for tu in t] if qi == j else t

        def absorb(j, qi, t):
            if j not in values:
                v_t = nt(ident_v, v_ref[j * tk:(j + 1) * tk, lanes]).astype(BF16)
                values[j] = [jnp.where(vrow < N, v_t, jnp.zeros_like(v_t)),
                             jnp.where(vrow < N, jnp.zeros_like(v_t), v_t)]
            v_heads = values[j]
            rmax = [jnp.max(t[u], axis=0, keepdims=True) for u in range(2)]
            m_new = rmax if j == 0 else [jnp.maximum(m[qi][u], rmax[u]) for u in range(2)]
            p = [jnp.exp2(t[u] - m_new[u]) for u in range(2)]
            psum = [jnp.sum(p[u], axis=0, keepdims=True) for u in range(2)]
            pv = mm(v_heads[0], p[0].astype(BF16)) + mm(v_heads[1], p[1].astype(BF16))
            if j == 0:
                acc[qi] = pv
                l[qi] = psum
            else:
                alpha = [jnp.exp2(m[qi][u] - m_new[u]) for u in range(2)]
                acc[qi] = acc[qi] * jnp.where(orow < N, alpha[0], alpha[1]) + pv
                l[qi] = [alpha[u] * l[qi][u] + psum[u] for u in range(2)]
            m[qi] = m_new
            if j == qi:
                o_t = acc[qi] / jnp.where(orow < N, l[qi][0], l[qi][1])
                o_ref[qi * tq:(qi + 1) * tq, lanes] = o_t.T.astype(o_ref.dtype)

        return scores, absorb

    pairs = [head_pair(hl) for hl in range(ATTN_PAIRS_PER_STEP)]
    items = [(hl, j, qi) for j in range(nq) for qi in range(j, nq) for hl in range(ATTN_PAIRS_PER_STEP)]
    score_of = lambda item: pairs[item[0]][0](*item[1:])
    queue = [score_of(item) for item in items[:ATTN_LOOKAHEAD]]
    for n, (hl, j, qi) in enumerate(items):
        t_cur = queue.pop(0)
        if n + ATTN_LOOKAHEAD < len(items):
            queue.append(score_of(items[n + ATTN_LOOKAHEAD]))
        pairs[hl][1](j, qi, t_cur)


def _fox_attn(q, k, v, c_t, batch, seq_len):
    n_tok, d = q.shape
    width = ATTN_PAIRS_PER_STEP * 2 * HEAD_DIM
    seq = pl.BlockSpec((seq_len, width), lambda b, hp: (b, hp))
    return pl.pallas_call(
        _fox_attn_kernel,
        grid=(batch, d // width),
        in_specs=[seq, seq, seq, pl.BlockSpec((1, N_HEADS, seq_len), lambda b, hp: (b, 0, 0))],
        out_specs=seq,
        out_shape=jax.ShapeDtypeStruct((n_tok, d), BF16),
        compiler_params=pltpu.CompilerParams(
            dimension_semantics=("parallel", "arbitrary"),
            vmem_limit_bytes=_vmem_limit(2 * 4 * seq_len * width * 2 + 128 * ATTN_TILE * ATTN_TILE * 4)),
        name="fox_attn",
    )(q, k, v, c_t)


def kernel(x, rw_mix, rw_wr, rw_wk, rw_wv, rw_wo, rw_w0, rw_w1, rw_w2, rw_a0, rw_a1, rw_a2, rw_g1, rw_g2,
           rw_kk, rw_ka, rw_rk, rw_gn_g, rw_gn_b, fx_w_in, fx_b_f, fx_wo, router_w, router_bias,
           moe_w_gate, moe_w_up, moe_w_down, ln_g, ln_b):
    batch, seq_len, d = x.shape
    n_tok = batch * seq_len
    bf = lambda w: w.astype(BF16)
    row = lambda w: w.reshape(1, -1)
    router_wt = router_w.T
    router_b = router_bias.reshape(N_EXPERTS, 1)
    h = x.reshape(n_tok, d)
    sorted_rows = None

    for i in range(DEPTH):
        j = i // 2
        if i % 2 == 0:
            r, lw, k, v, a, g = _rwkv_proj(
                h, seq_len, rw_mix[j], bf(rw_wr[j]), bf(rw_wk[j]), bf(rw_wv[j]), bf(rw_w1[j]), bf(rw_w2[j]),
                bf(rw_a1[j]), bf(rw_a2[j]), bf(rw_g1[j]), bf(rw_g2[j]), row(rw_w0[j]), row(rw_a0[j]))
            act = _rwkv_recur(r, lw, k, v, a, g, batch, seq_len, row(rw_kk[j]), row(rw_ka[j]), row(rw_rk[j]),
                              row(rw_gn_g[j]), row(rw_gn_b[j]))
            wo = bf(rw_wo[j])
        else:
            w_in = fx_w_in[j]
            scale = HEAD_DIM ** -0.5
            q, k, v, c = _fox_proj(h, batch, seq_len, bf(w_in[:, :d] * scale), bf(w_in[:, d:2 * d]),
                                   bf(w_in[:, 2 * d:3 * d]), w_in[:, 3 * d:], row(fx_b_f[j]))
            c_t = c.reshape(batch, seq_len, N_HEADS).transpose(0, 2, 1)
            act = _fox_attn(q, k, v, c_t, batch, seq_len)
            wo = bf(fx_wo[j])
        h, h_packed, route, counts = _mixer_epilogue(act, wo, h, row(ln_g[i, 0]), row(ln_b[i, 0]),
                                                     router_wt, router_b)
        h, sorted_rows = _moe_layer(h, h_packed, route, counts, i, moe_w_gate, moe_w_up, moe_w_down,
                                    row(ln_g[i, 1]), row(ln_b[i, 1]), sorted_rows)
    return h.reshape(batch, seq_len, d)
```
